```python
import jax, jax.numpy as jnp
from jax import lax
import numpy as np

D_MODEL = 2048
BATCH = 8
SEQ = 8192
DEPTH = 1

POOL_WINDOWS = (2, 4, 8, 16)
POOL_GROUPS = len(POOL_WINDOWS)
POOL_GROUP_WIDTH = D_MODEL // 8
POOL_WIDTH = POOL_GROUPS * POOL_GROUP_WIDTH
LRU_WIDTH = D_MODEL
LRU_BLOCK_WIDTH = 256
LRU_BLOCKS = LRU_WIDTH // LRU_BLOCK_WIDTH
LRU_CONV_WIDTH = 4
LRU_C = 8.0
LRU_A_MIN = 0.9
LRU_A_MAX = 0.999
N_BRANCHES = 2
IN_WIDTH = POOL_WIDTH + 2 * LRU_WIDTH + N_BRANCHES * D_MODEL
D_FF = 3 * D_MODEL
FFN_CONV_WIDTH = 3
EPS = 1e-6

kernel_name = "hybrid_pool_rglru_gated_block"


def rms_norm(x, g):
    xf = x.astype(jnp.float32)
    y = xf * lax.rsqrt(jnp.mean(xf * xf, axis=-1, keepdims=True) + EPS)
    return (y * g.astype(jnp.float32)).astype(x.dtype)


def causal_depthwise_conv(x, w, b):
    K = w.shape[0]
    S = x.shape[1]
    xp = jnp.pad(x, ((0, 0), (K - 1, 0), (0, 0)))
    out = b
    for k in range(K):
        out = out + xp[:, k:k + S] * w[k]
    return out


def pool_mixer(u, w_pool, pool_scale):
    B, S, _ = u.shape
    uf = u.astype(jnp.float32)
    c = jnp.cumsum(uf, axis=1)
    pos = jnp.arange(1, S + 1, dtype=jnp.float32)
    means = []
    for g, w in enumerate(POOL_WINDOWS):
        cg = c[..., g * POOL_GROUP_WIDTH:(g + 1) * POOL_GROUP_WIDTH]
        shifted = jnp.pad(cg, ((0, 0), (w, 0), (0, 0)))[:, :S]
        count = jnp.minimum(pos, float(w))[None, :, None]
        means.append((cg - shifted) / count)
    mean = jnp.stack(means, axis=2)
    d = (mean - uf.reshape(B, S, POOL_GROUPS, POOL_GROUP_WIDTH)).astype(u.dtype)
    y = jnp.einsum('bsgc,gcd->bsgd', d, w_pool).reshape(B, S, POOL_WIDTH)
    return y * pool_scale


def rg_lru(x, w_a, b_a, w_i, b_i, lam):
    B, S, R = x.shape
    xb = x.reshape(B, S, LRU_BLOCKS, LRU_BLOCK_WIDTH)
    r = jax.nn.sigmoid(jnp.einsum('bshc,hcd->bshd', xb, w_a).reshape(B, S, R) + b_a)
    i = jax.nn.sigmoid(jnp.einsum('bshc,hcd->bshd', xb, w_i).reshape(B, S, R) + b_i)
    log_a = -LRU_C * r.astype(jnp.float32) * jax.nn.softplus(-lam.astype(jnp.float32))
    a = jnp.exp(log_a)
    mult = jnp.sqrt(-jnp.expm1(2.0 * log_a))
    bx = mult * (i * x).astype(jnp.float32)

    def combine(left, right):
        a1, b1 = left
        a2, b2 = right
        return a1 * a2, a2 * b1 + b2

    _, h = lax.associative_scan(combine, (a, bx), axis=1)
    return h.astype(x.dtype)


def _fwd_setup_inputs(seed: int = 0) -> dict:
    key = jax.random.key(seed)
    ks = jax.random.split(key, 24)
    f32 = jnp.float32

    def nrm(k, shape, fan_in):
        return jax.random.normal(k, shape, f32) * (fan_in ** -0.5)

    def gain(k, shape):
        return 1.0 + 0.02 * jax.random.normal(k, shape, f32)

    def bias(k, shape):
        return 0.01 * jax.random.normal(k, shape, f32)

    L = DEPTH
    u = jax.random.uniform(ks[12], (L, LRU_WIDTH), f32, LRU_A_MIN, LRU_A_MAX)
    s = u ** (1.0 / LRU_C)
    lru_lambda = jnp.log(s) - jnp.log1p(-s)
    return {
        "x": jax.random.normal(ks[0], (BATCH, SEQ, D_MODEL), f32),
        "g_mix": gain(ks[1], (L, D_MODEL)),
        "w_in": nrm(ks[2], (L, D_MODEL, IN_WIDTH), D_MODEL),
        "b_gate": bias(ks[3], (L, N_BRANCHES * D_MODEL)),
        "w_pool": nrm(ks[4], (L, POOL_GROUPS, POOL_GROUP_WIDTH, POOL_GROUP_WIDTH), POOL_GROUP_WIDTH),
        "pool_scale": gain(ks[5], (L, POOL_WIDTH)),
        "lru_conv_w": nrm(ks[6], (L, LRU_CONV_WIDTH, LRU_WIDTH), LRU_CONV_WIDTH),
        "lru_conv_b": bias(ks[7], (L, LRU_WIDTH)),
        "w_a": nrm(ks[8], (L, LRU_BLOCKS, LRU_BLOCK_WIDTH, LRU_BLOCK_WIDTH), LRU_BLOCK_WIDTH),
        "b_a": bias(ks[9], (L, LRU_WIDTH)),
        "w_i": nrm(ks[10], (L, LRU_BLOCKS, LRU_BLOCK_WIDTH, LRU_BLOCK_WIDTH), LRU_BLOCK_WIDTH),
        "b_i": bias(ks[11], (L, LRU_WIDTH)),
        "lru_lambda": lru_lambda,
        "w_pool_proj": nrm(ks[13], (L, POOL_WIDTH, D_MODEL), POOL_WIDTH),
        "w_lru_proj": nrm(ks[14], (L, LRU_WIDTH, D_MODEL), LRU_WIDTH),
        "w_out": nrm(ks[15], (L, D_MODEL, D_MODEL), D_MODEL),
        "g_mlp": gain(ks[16], (L, D_MODEL)),
        "w_up": nrm(ks[17], (L, D_MODEL, 2 * D_FF), D_MODEL),
        "ffn_conv_w": nrm(ks[18], (L, FFN_CONV_WIDTH, D_FF), FFN_CONV_WIDTH),
        "ffn_conv_b": bias(ks[19], (L, D_FF)),
        "w_down": nrm(ks[20], (L, D_FF, D_MODEL), D_FF),
        "g_final": gain(ks[21], (D_MODEL,)),
    }


def _fwd_reference(x, g_mix, w_in, b_gate, w_pool, pool_scale, lru_conv_w, lru_conv_b,
              w_a, b_a, w_i, b_i, lru_lambda, w_pool_proj, w_lru_proj, w_out,
              g_mlp, w_up, ffn_conv_w, ffn_conv_b, w_down, g_final):
    B, S, D = x.shape
    for l in range(DEPTH):
        h = rms_norm(x, g_mix[l])
        proj = h @ w_in[l]
        p0 = POOL_WIDTH
        p1 = p0 + LRU_WIDTH
        p2 = p1 + LRU_WIDTH
        u_pool = proj[..., :p0]
        u_lru = proj[..., p0:p1]
        u_gelu = proj[..., p1:p2]
        gates = jax.nn.sigmoid(proj[..., p2:] + b_gate[l]).reshape(B, S, N_BRANCHES, D)

        y_pool = pool_mixer(u_pool, w_pool[l], pool_scale[l])
        v = causal_depthwise_conv(u_lru, lru_conv_w[l], lru_conv_b[l])
        y_lru = rg_lru(v, w_a[l], b_a[l], w_i[l], b_i[l], lru_lambda[l]) * jax.nn.gelu(u_gelu)

        merged = (gates[:, :, 0] * (y_pool @ w_pool_proj[l])
                  + gates[:, :, 1] * (y_lru @ w_lru_proj[l]))
        x = x + merged @ w_out[l]

        h2 = rms_norm(x, g_mlp[l])
        up = h2 @ w_up[l]
        gate_pre = up[..., :D_FF]
        val = up[..., D_FF:]
        gate = jax.nn.gelu(causal_depthwise_conv(gate_pre, ffn_conv_w[l], ffn_conv_b[l]))
        x = x + (gate * val) @ w_down[l]
    return rms_norm(x, g_final)


import jax as _jax
import jax.numpy as _jnp

TWIN_FORMAT = 'train_step'
FWD_PARAMS = ['x', 'g_mix', 'w_in', 'b_gate', 'w_pool', 'pool_scale', 'lru_conv_w', 'lru_conv_b', 'w_a', 'b_a', 'w_i', 'b_i', 'lru_lambda', 'w_pool_proj', 'w_lru_proj', 'w_out', 'g_mlp', 'w_up', 'ffn_conv_w', 'ffn_conv_b', 'w_down', 'g_final']
TWIN_WEIGHTS = ['g_mix', 'w_in', 'b_gate', 'w_pool', 'pool_scale', 'lru_conv_w', 'lru_conv_b', 'w_a', 'b_a', 'w_i', 'b_i', 'lru_lambda', 'w_pool_proj', 'w_lru_proj', 'w_out', 'g_mlp', 'w_up', 'ffn_conv_w', 'ffn_conv_b', 'w_down', 'g_final']
TWIN_DIFF_INPUT = 'x'
TWIN_INPUTS = ['x', 'g_mix', 'w_in', 'b_gate', 'w_pool', 'pool_scale', 'lru_conv_w', 'lru_conv_b', 'w_a', 'b_a', 'w_i', 'b_i', 'lru_lambda', 'w_pool_proj', 'w_lru_proj', 'w_out', 'g_mlp', 'w_up', 'ffn_conv_w', 'ffn_conv_b', 'w_down', 'g_final', 'loss_target', 'm_g_mix', 'm_w_in', 'm_b_gate', 'm_w_pool', 'm_pool_scale', 'm_lru_conv_w', 'm_lru_conv_b', 'm_w_a', 'm_b_a', 'm_w_i', 'm_b_i', 'm_lru_lambda', 'm_w_pool_proj', 'm_w_lru_proj', 'm_w_out', 'm_g_mlp', 'm_w_up', 'm_ffn_conv_w', 'm_ffn_conv_b', 'm_w_down', 'm_g_final', 'v_g_mix', 'v_w_in', 'v_b_gate', 'v_w_pool', 'v_pool_scale', 'v_lru_conv_w', 'v_lru_conv_b', 'v_w_a', 'v_b_a', 'v_w_i', 'v_b_i', 'v_lru_lambda', 'v_w_pool_proj', 'v_w_lru_proj', 'v_w_out', 'v_g_mlp', 'v_w_up', 'v_ffn_conv_w', 'v_ffn_conv_b', 'v_w_down', 'v_g_final']
TWIN_OUTPUTS = ['loss', 'grad_x', 'grad_g_mix', 'grad_w_in', 'grad_b_gate', 'grad_w_pool', 'grad_pool_scale', 'grad_lru_conv_w', 'grad_lru_conv_b', 'grad_w_a', 'grad_b_a', 'grad_w_i', 'grad_b_i', 'grad_lru_lambda', 'grad_w_pool_proj', 'grad_w_lru_proj', 'grad_w_out', 'grad_g_mlp', 'grad_w_up', 'grad_ffn_conv_w', 'grad_ffn_conv_b', 'grad_w_down', 'grad_g_final', 'delta_g_mix', 'delta_w_in', 'delta_b_gate', 'delta_w_pool', 'delta_pool_scale', 'delta_lru_conv_w', 'delta_lru_conv_b', 'delta_w_a', 'delta_b_a', 'delta_w_i', 'delta_b_i', 'delta_lru_lambda', 'delta_w_pool_proj', 'delta_w_lru_proj', 'delta_w_out', 'delta_g_mlp', 'delta_w_up', 'delta_ffn_conv_w', 'delta_ffn_conv_b', 'delta_w_down', 'delta_g_final', 'new_m_g_mix', 'new_m_w_in', 'new_m_b_gate', 'new_m_w_pool', 'new_m_pool_scale', 'new_m_lru_conv_w', 'new_m_lru_conv_b', 'new_m_w_a', 'new_m_b_a', 'new_m_w_i', 'new_m_b_i', 'new_m_lru_lambda', 'new_m_w_pool_proj', 'new_m_w_lru_proj', 'new_m_w_out', 'new_m_g_mlp', 'new_m_w_up', 'new_m_ffn_conv_w', 'new_m_ffn_conv_b', 'new_m_w_down', 'new_m_g_final', 'new_v_g_mix', 'new_v_w_in', 'new_v_b_gate', 'new_v_w_pool', 'new_v_pool_scale', 'new_v_lru_conv_w', 'new_v_lru_conv_b', 'new_v_w_a', 'new_v_b_a', 'new_v_w_i', 'new_v_b_i', 'new_v_lru_lambda', 'new_v_w_pool_proj', 'new_v_w_lru_proj', 'new_v_w_out', 'new_v_g_mlp', 'new_v_w_up', 'new_v_ffn_conv_w', 'new_v_ffn_conv_b', 'new_v_w_down', 'new_v_g_final']
TWIN_LEAF_KINDS = {'loss': 'loss', 'grad_x': 'grad_x', 'grad_g_mix': 'grad_w', 'grad_w_in': 'grad_w', 'grad_b_gate': 'grad_w', 'grad_w_pool': 'grad_w', 'grad_pool_scale': 'grad_w', 'grad_lru_conv_w': 'grad_w', 'grad_lru_conv_b': 'grad_w', 'grad_w_a': 'grad_w', 'grad_b_a': 'grad_w', 'grad_w_i': 'grad_w', 'grad_b_i': 'grad_w', 'grad_lru_lambda': 'grad_w', 'grad_w_pool_proj': 'grad_w', 'grad_w_lru_proj': 'grad_w', 'grad_w_out': 'grad_w', 'grad_g_mlp': 'grad_w', 'grad_w_up': 'grad_w', 'grad_ffn_conv_w': 'grad_w', 'grad_ffn_conv_b': 'grad_w', 'grad_w_down': 'grad_w', 'grad_g_final': 'grad_w', 'delta_g_mix': 'delta_w', 'delta_w_in': 'delta_w', 'delta_b_gate': 'delta_w', 'delta_w_pool': 'delta_w', 'delta_pool_scale': 'delta_w', 'delta_lru_conv_w': 'delta_w', 'delta_lru_conv_b': 'delta_w', 'delta_w_a': 'delta_w', 'delta_b_a': 'delta_w', 'delta_w_i': 'delta_w', 'delta_b_i': 'delta_w', 'delta_lru_lambda': 'delta_w', 'delta_w_pool_proj': 'delta_w', 'delta_w_lru_proj': 'delta_w', 'delta_w_out': 'delta_w', 'delta_g_mlp': 'delta_w', 'delta_w_up': 'delta_w', 'delta_ffn_conv_w': 'delta_w', 'delta_ffn_conv_b': 'delta_w', 'delta_w_down': 'delta_w', 'delta_g_final': 'delta_w', 'new_m_g_mix': 'new_m', 'new_m_w_in': 'new_m', 'new_m_b_gate': 'new_m', 'new_m_w_pool': 'new_m', 'new_m_pool_scale': 'new_m', 'new_m_lru_conv_w': 'new_m', 'new_m_lru_conv_b': 'new_m', 'new_m_w_a': 'new_m', 'new_m_b_a': 'new_m', 'new_m_w_i': 'new_m', 'new_m_b_i': 'new_m', 'new_m_lru_lambda': 'new_m', 'new_m_w_pool_proj': 'new_m', 'new_m_w_lru_proj': 'new_m', 'new_m_w_out': 'new_m', 'new_m_g_mlp': 'new_m', 'new_m_w_up': 'new_m', 'new_m_ffn_conv_w': 'new_m', 'new_m_ffn_conv_b': 'new_m', 'new_m_w_down': 'new_m', 'new_m_g_final': 'new_m', 'new_v_g_mix': 'new_v', 'new_v_w_in': 'new_v', 'new_v_b_gate': 'new_v', 'new_v_w_pool': 'new_v', 'new_v_pool_scale': 'new_v', 'new_v_lru_conv_w': 'new_v', 'new_v_lru_conv_b': 'new_v', 'new_v_w_a': 'new_v', 'new_v_b_a': 'new_v', 'new_v_w_i': 'new_v', 'new_v_b_i': 'new_v', 'new_v_lru_lambda': 'new_v', 'new_v_w_pool_proj': 'new_v', 'new_v_w_lru_proj': 'new_v', 'new_v_w_out': 'new_v', 'new_v_g_mlp': 'new_v', 'new_v_w_up': 'new_v', 'new_v_ffn_conv_w': 'new_v', 'new_v_ffn_conv_b': 'new_v', 'new_v_w_down': 'new_v', 'new_v_g_final': 'new_v'}


def _forward(args):
    return _fwd_reference(*[args[k] for k in FWD_PARAMS])


def _output_shape():
    def fwd():
        inp = _fwd_setup_inputs(0)
        return _fwd_reference(*[inp[k] for k in FWD_PARAMS])
    out = _jax.eval_shape(fwd)
    return out.shape, out.dtype

N_MICROBATCH = 1
ADAM_LR = 0.001
ADAM_B1 = 0.9
ADAM_B2 = 0.999
ADAM_EPS = 1e-08
ADAM_WD = 0.01
ADAM_STEP = 10
PER_EXAMPLE_BATCH_AXIS = {'x': 0, 'loss_target': 0}
SHARED_INPUTS = []
_WEIGHT_DTYPES = {'g_mix': _jnp.float32, 'w_in': _jnp.float32, 'b_gate': _jnp.float32, 'w_pool': _jnp.float32, 'pool_scale': _jnp.float32, 'lru_conv_w': _jnp.float32, 'lru_conv_b': _jnp.float32, 'w_a': _jnp.float32, 'b_a': _jnp.float32, 'w_i': _jnp.float32, 'b_i': _jnp.float32, 'lru_lambda': _jnp.float32, 'w_pool_proj': _jnp.float32, 'w_lru_proj': _jnp.float32, 'w_out': _jnp.float32, 'g_mlp': _jnp.float32, 'w_up': _jnp.float32, 'ffn_conv_w': _jnp.float32, 'ffn_conv_b': _jnp.float32, 'w_down': _jnp.float32, 'g_final': _jnp.float32}
MOMENT_SCALE = {'g_mix': 7.459960e-02, 'w_in': 3.604692e-02, 'b_gate': 1.740160e-02, 'w_pool': 8.455350e-02, 'pool_scale': 8.752988e-02, 'lru_conv_w': 2.900142e-02, 'lru_conv_b': 3.757559e-01, 'w_a': 8.374682e-03, 'b_a': 8.013369e-03, 'w_i': 1.492753e-02, 'b_i': 1.028078e-02, 'lru_lambda': 1.620979e-02, 'w_pool_proj': 5.992098e-02, 'w_lru_proj': 2.828954e-02, 'w_out': 6.608046e-02, 'g_mlp': 9.310476e-02, 'w_up': 3.699005e-02, 'ffn_conv_w': 3.818488e-02, 'ffn_conv_b': 3.642914e-02, 'w_down': 6.310329e-02, 'g_final': 3.197187e+01}


def _to_microbatches(a, axis):
    t = _jnp.moveaxis(a, axis, 0)
    t = t.reshape((N_MICROBATCH, t.shape[0] // N_MICROBATCH) + t.shape[1:])
    return _jnp.moveaxis(t, 1, axis + 1)


def setup_inputs(seed: int = 0) -> dict:
    inp = _fwd_setup_inputs(seed)
    key = _jax.random.fold_in(_jax.random.key(seed), 7919)
    shape, _ = _output_shape()
    out = dict(inp)
    out["loss_target"] = _jax.random.normal(_jax.random.fold_in(key, 0), shape, _jnp.float32)
    for i, name in enumerate(TWIN_WEIGHTS):
        w = inp[name].astype(_jnp.float32)
        if MOMENT_SCALE is None:
            s = _jnp.sqrt(_jnp.mean(_jnp.square(w)) + 1e-30)
        else:
            s = MOMENT_SCALE[name]
        km, kv = _jax.random.split(_jax.random.fold_in(key, i + 1))
        out[name] = w
        out["m_" + name] = s * _jax.random.normal(km, w.shape, _jnp.float32)
        out["v_" + name] = (s * s) * _jax.random.uniform(kv, w.shape, _jnp.float32, 0.5, 1.5)
    if N_MICROBATCH > 1:
        for name, axis in PER_EXAMPLE_BATCH_AXIS.items():
            out[name] = _to_microbatches(out[name], axis)
    return {'x': out['x'], 'g_mix': out['g_mix'], 'w_in': out['w_in'], 'b_gate': out['b_gate'], 'w_pool': out['w_pool'], 'pool_scale': out['pool_scale'], 'lru_conv_w': out['lru_conv_w'], 'lru_conv_b': out['lru_conv_b'], 'w_a': out['w_a'], 'b_a': out['b_a'], 'w_i': out['w_i'], 'b_i': out['b_i'], 'lru_lambda': out['lru_lambda'], 'w_pool_proj': out['w_pool_proj'], 'w_lru_proj': out['w_lru_proj'], 'w_out': out['w_out'], 'g_mlp': out['g_mlp'], 'w_up': out['w_up'], 'ffn_conv_w': out['ffn_conv_w'], 'ffn_conv_b': out['ffn_conv_b'], 'w_down': out['w_down'], 'g_final': out['g_final'], 'loss_target': out['loss_target'], 'm_g_mix': out['m_g_mix'], 'm_w_in': out['m_w_in'], 'm_b_gate': out['m_b_gate'], 'm_w_pool': out['m_w_pool'], 'm_pool_scale': out['m_pool_scale'], 'm_lru_conv_w': out['m_lru_conv_w'], 'm_lru_conv_b': out['m_lru_conv_b'], 'm_w_a': out['m_w_a'], 'm_b_a': out['m_b_a'], 'm_w_i': out['m_w_i'], 'm_b_i': out['m_b_i'], 'm_lru_lambda': out['m_lru_lambda'], 'm_w_pool_proj': out['m_w_pool_proj'], 'm_w_lru_proj': out['m_w_lru_proj'], 'm_w_out': out['m_w_out'], 'm_g_mlp': out['m_g_mlp'], 'm_w_up': out['m_w_up'], 'm_ffn_conv_w': out['m_ffn_conv_w'], 'm_ffn_conv_b': out['m_ffn_conv_b'], 'm_w_down': out['m_w_down'], 'm_g_final': out['m_g_final'], 'v_g_mix': out['v_g_mix'], 'v_w_in': out['v_w_in'], 'v_b_gate': out['v_b_gate'], 'v_w_pool': out['v_w_pool'], 'v_pool_scale': out['v_pool_scale'], 'v_lru_conv_w': out['v_lru_conv_w'], 'v_lru_conv_b': out['v_lru_conv_b'], 'v_w_a': out['v_w_a'], 'v_b_a': out['v_b_a'], 'v_w_i': out['v_w_i'], 'v_b_i': out['v_b_i'], 'v_lru_lambda': out['v_lru_lambda'], 'v_w_pool_proj': out['v_w_pool_proj'], 'v_w_lru_proj': out['v_w_lru_proj'], 'v_w_out': out['v_w_out'], 'v_g_mlp': out['v_g_mlp'], 'v_w_up': out['v_w_up'], 'v_ffn_conv_w': out['v_ffn_conv_w'], 'v_ffn_conv_b': out['v_ffn_conv_b'], 'v_w_down': out['v_w_down'], 'v_g_final': out['v_g_final']}


def _loss(weights, diff, rest, loss_target):
    with _jax.named_scope("forward"):
        args = {**rest, TWIN_DIFF_INPUT: diff, **{k: w.astype(_WEIGHT_DTYPES[k]) for k, w in weights.items()}}
        y = _forward(args)
    with _jax.named_scope("loss_head"):
        err = _jnp.square(y.astype(_jnp.float32) - loss_target)
        return 0.5 * _jnp.sum(_jnp.mean(err, axis=-1)) if err.ndim else 0.5 * err


def _adamw(w, g, m, v):
    m = ADAM_B1 * m + (1.0 - ADAM_B1) * g
    v = ADAM_B2 * v + (1.0 - ADAM_B2) * _jnp.square(g)
    m_hat = m / (1.0 - ADAM_B1 ** ADAM_STEP)
    v_hat = v / (1.0 - ADAM_B2 ** ADAM_STEP)
    delta = -ADAM_LR * (m_hat / (_jnp.sqrt(v_hat) + ADAM_EPS) + ADAM_WD * w)
    return delta, m, v


def reference(x, g_mix, w_in, b_gate, w_pool, pool_scale, lru_conv_w, lru_conv_b, w_a, b_a, w_i, b_i, lru_lambda, w_pool_proj, w_lru_proj, w_out, g_mlp, w_up, ffn_conv_w, ffn_conv_b, w_down, g_final, loss_target, m_g_mix, m_w_in, m_b_gate, m_w_pool, m_pool_scale, m_lru_conv_w, m_lru_conv_b, m_w_a, m_b_a, m_w_i, m_b_i, m_lru_lambda, m_w_pool_proj, m_w_lru_proj, m_w_out, m_g_mlp, m_w_up, m_ffn_conv_w, m_ffn_conv_b, m_w_down, m_g_final, v_g_mix, v_w_in, v_b_gate, v_w_pool, v_pool_scale, v_lru_conv_w, v_lru_conv_b, v_w_a, v_b_a, v_w_i, v_b_i, v_lru_lambda, v_w_pool_proj, v_w_lru_proj, v_w_out, v_g_mlp, v_w_up, v_ffn_conv_w, v_ffn_conv_b, v_w_down, v_g_final):
    given = dict(x=x, g_mix=g_mix, w_in=w_in, b_gate=b_gate, w_pool=w_pool, pool_scale=pool_scale, lru_conv_w=lru_conv_w, lru_conv_b=lru_conv_b, w_a=w_a, b_a=b_a, w_i=w_i, b_i=b_i, lru_lambda=lru_lambda, w_pool_proj=w_pool_proj, w_lru_proj=w_lru_proj, w_out=w_out, g_mlp=g_mlp, w_up=w_up, ffn_conv_w=ffn_conv_w, ffn_conv_b=ffn_conv_b, w_down=w_down, g_final=g_final, loss_target=loss_target, m_g_mix=m_g_mix, m_w_in=m_w_in, m_b_gate=m_b_gate, m_w_pool=m_w_pool, m_pool_scale=m_pool_scale, m_lru_conv_w=m_lru_conv_w, m_lru_conv_b=m_lru_conv_b, m_w_a=m_w_a, m_b_a=m_b_a, m_w_i=m_w_i, m_b_i=m_b_i, m_lru_lambda=m_lru_lambda, m_w_pool_proj=m_w_pool_proj, m_w_lru_proj=m_w_lru_proj, m_w_out=m_w_out, m_g_mlp=m_g_mlp, m_w_up=m_w_up, m_ffn_conv_w=m_ffn_conv_w, m_ffn_conv_b=m_ffn_conv_b, m_w_down=m_w_down, m_g_final=m_g_final, v_g_mix=v_g_mix, v_w_in=v_w_in, v_b_gate=v_b_gate, v_w_pool=v_w_pool, v_pool_scale=v_pool_scale, v_lru_conv_w=v_lru_conv_w, v_lru_conv_b=v_lru_conv_b, v_w_a=v_w_a, v_b_a=v_b_a, v_w_i=v_w_i, v_b_i=v_b_i, v_lru_lambda=v_lru_lambda, v_w_pool_proj=v_w_pool_proj, v_w_lru_proj=v_w_lru_proj, v_w_out=v_w_out, v_g_mlp=v_g_mlp, v_w_up=v_w_up, v_ffn_conv_w=v_ffn_conv_w, v_ffn_conv_b=v_ffn_conv_b, v_w_down=v_w_down, v_g_final=v_g_final)
    weights = {n: given[n] for n in TWIN_WEIGHTS}
    shared = {n: given[n] for n in SHARED_INPUTS}
    per_example = {n: given[n] for n in ['x']}
    grad_fn = _jax.value_and_grad(_loss, argnums=(0, 1))

    def one_microbatch(ex, loss_target):
        ex = dict(ex)
        diff = ex.pop(TWIN_DIFF_INPUT)
        return grad_fn(weights, diff, {**shared, **ex}, loss_target)

    if N_MICROBATCH == 1:
        loss, (grad_w, grad_x) = one_microbatch(per_example, given["loss_target"])
    else:
        def body(carry, xs):
            loss_sum, grad_sum = carry
            l_k, (gw_k, gx_k) = one_microbatch(xs[0], xs[1])
            with _jax.named_scope("update"):
                return (loss_sum + l_k, _jax.tree.map(_jnp.add, grad_sum, gw_k)), gx_k

        init = (_jnp.zeros((), _jnp.float32), _jax.tree.map(_jnp.zeros_like, weights))
        (loss, grad_w), grad_x = _jax.lax.scan(body, init, (per_example, given["loss_target"]))
    with _jax.named_scope("update"):
        delta_w, new_m, new_v = {}, {}, {}
        for n in TWIN_WEIGHTS:
            delta_w[n], new_m[n], new_v[n] = _adamw(weights[n], grad_w[n], given["m_" + n], given["v_" + n])
    return (loss, grad_x, *[grad_w[n] for n in TWIN_WEIGHTS], *[delta_w[n] for n in TWIN_WEIGHTS],
            *[new_m[n] for n in TWIN_WEIGHTS], *[new_v[n] for n in TWIN_WEIGHTS])
```

```python
import functools

import jax
import jax.numpy as jnp
from jax import lax
from jax.experimental import pallas as pl
from jax.experimental.pallas import tpu as pltpu

F32 = jnp.float32
BF16 = jnp.bfloat16
MESH = pl.DeviceIdType.MESH

VMEM_LIMIT_V7X = 48 * 1024 * 1024
LANES = 128
HALO = 16
N_SLOTS = 4
ADAMW_BLOCK_BYTES = 3 * 512 * 1024

EPS = 1e-6
LRU_C = 8.0
ADAM_LR = 0.001
ADAM_B1 = 0.9
ADAM_B2 = 0.999
ADAM_EPS = 1e-08
ADAM_WD = 0.01
ADAM_STEP = 10
GELU_C = 0.7978845608028654
GELU_A = 0.044715

SDS = jax.ShapeDtypeStruct
ANY = pl.BlockSpec(memory_space=pl.ANY)


def _pick(n, cands):
    for c in cands:
        if c <= n and n % c == 0:
            return c
    raise ValueError(f"no tile for {n} in {cands}")


def _cparams(**kw):
    return pltpu.CompilerParams(vmem_limit_bytes=VMEM_LIMIT_V7X, **kw)


def _sigmoid(x):
    return 1.0 / (1.0 + jnp.exp(-x))


def _gelu(x):
    return 0.5 * x * (1.0 + jnp.tanh(GELU_C * (x + GELU_A * x * x * x)))


def _gelu_and_grad(x):
    x2 = x * x
    th = jnp.tanh(GELU_C * x * (1.0 + GELU_A * x2))
    g = 0.5 * x * (1.0 + th)
    dg = 0.5 * (1.0 + th) + 0.5 * x * (1.0 - th * th) * (GELU_C * (1.0 + 3.0 * GELU_A * x2))
    return g, dg


def _lookback(ext, k):
    return pltpu.roll(ext, k, 0)[HALO:]


def _lookahead(ext, k, tt):
    return pltpu.roll(ext, ext.shape[0] - k, 0)[:tt]


def _colsum(v):
    return jnp.sum(v, axis=0, keepdims=True)


def _seg3(a):
    return a if a.ndim == 3 else a.reshape((1,) + a.shape)


def _seg_spec(shape3, br, bc, rowf, colf):
    assert shape3[1] % br == 0 and shape3[2] % bc == 0, (shape3, br, bc)
    nb = shape3[2] // bc

    def imap(i, j, k):
        cb = colf(i, j, k)
        return (cb // nb, rowf(i, j, k), cb % nb)

    return pl.BlockSpec((None, br, bc), imap)


_I = lambda i, j, k: i
_J = lambda i, j, k: j
_K = lambda i, j, k: k


def _matmul(name, mode, a, b, *, out_seg=1, out_dtype, tm, tn, tk, res=None):
    a3, b3 = _seg3(a), _seg3(b)
    if mode == "nn":
        M, K, N = a3.shape[1], a3.shape[0] * a3.shape[2], b3.shape[0] * b3.shape[2]
        assert b3.shape[1] == K
        a_spec = _seg_spec(a3.shape, tm, tk, _I, _K)
        b_spec = _seg_spec(b3.shape, tk, tn, _K, _J)
        dims = (((1,), (0,)), ((), ()))
    elif mode == "nt":
        M, K, N = a3.shape[1], a3.shape[0] * a3.shape[2], b3.shape[1]
        assert b3.shape[0] * b3.shape[2] == K
        a_spec = _seg_spec(a3.shape, tm, tk, _I, _K)
        b_spec = _seg_spec(b3.shape, tn, tk, _J, _K)
        dims = (((1,), (1,)), ((), ()))
    else:
        K, M, N = a3.shape[1], a3.shape[0] * a3.shape[2], b3.shape[0] * b3.shape[2]
        assert b3.shape[1] == K
        a_spec = _seg_spec(a3.shape, tk, tm, _K, _I)
        b_spec = _seg_spec(b3.shape, tk, tn, _K, _J)
        dims = (((0,), (0,)), ((), ()))
    out3 = (out_seg, M, N // out_seg)
    o_spec = _seg_spec(out3, tm, tn, _I, _J)
    grid = (M // tm, N // tn, K // tk)
    nk = grid[2]
    has_res = res is not None

    def body(*refs):
        a_ref, b_ref = refs[0], refs[1]
        r_ref = refs[2] if has_res else None
        o_ref = refs[3] if has_res else refs[2]
        prod = lax.dot_general(a_ref[...], b_ref[...], dims, preferred_element_type=F32)

        def finish(val):
            if has_res:
                val = val + r_ref[...]
            o_ref[...] = val.astype(o_ref.dtype)

        if nk == 1:
            finish(prod)
        else:
            acc_ref = refs[-1]
            k = pl.program_id(2)

            @pl.when(k == 0)
            def _():
                acc_ref[...] = prod

            @pl.when(k > 0)
            def _():
                acc_ref[...] += prod

            @pl.when(k == nk - 1)
            def _():
                finish(acc_ref[...])

    in_specs = [a_spec, b_spec]
    args = [a3, b3]
    if has_res:
        in_specs.append(pl.BlockSpec((tm, tn), lambda i, j, k: (i, j)))
        args.append(res)
    return pl.pallas_call(
        body, name=name, grid=grid, in_specs=in_specs, out_specs=o_spec,
        out_shape=SDS(out3, out_dtype),
        scratch_shapes=[pltpu.VMEM((tm, tn), F32)] if nk > 1 else [],
        compiler_params=_cparams(dimension_semantics=("parallel", "parallel", "arbitrary")),
    )(*args)


def _bd_fwd(name, a, ws, tm, scale=None):
    T = a.shape[0]
    H, w, _ = ws[0].shape
    nw = len(ws)
    has_scale = scale is not None

    def body(*refs):
        a_ref = refs[0]
        w_refs = refs[1:1 + nw]
        pos = 1 + nw
        s_ref = refs[pos] if has_scale else None
        o_refs = refs[pos + (1 if has_scale else 0):]
        av = a_ref[...]
        for l in range(nw):
            r = jnp.dot(av, w_refs[l][...], preferred_element_type=F32)
            o_refs[l][...] = r
            if has_scale and l == 0:
                o_refs[nw][...] = (r * s_ref[...]).astype(BF16)

    blk = pl.BlockSpec((tm, w), lambda h, i: (i, h))
    wspec = pl.BlockSpec((None, w, w), lambda h, i: (h, 0, 0))
    in_specs = [blk] + [wspec] * nw
    args = [a] + list(ws)
    out_shape = [SDS((T, H * w), F32)] * nw
    out_specs = [blk] * nw
    if has_scale:
        in_specs.append(pl.BlockSpec((1, w), lambda h, i: (0, h)))
        args.append(scale)
        out_shape = out_shape + [SDS((T, H * w), BF16)]
        out_specs = out_specs + [blk]
    return pl.pallas_call(
        body, name=name, grid=(H, T // tm), in_specs=in_specs, out_specs=out_specs, out_shape=out_shape,
        compiler_params=_cparams(dimension_semantics=("parallel", "parallel")),
    )(*args)


def _bd_bwd_x(name, dys, ws, tm, res=None):
    T = dys[0].shape[0]
    H, w, _ = ws[0].shape
    nw = len(ws)
    has_res = res is not None

    def body(*refs):
        d_refs = refs[:nw]
        w_refs = refs[nw:2 * nw]
        r_ref = refs[2 * nw] if has_res else None
        o_ref = refs[-1]
        acc = None
        for l in range(nw):
            p = lax.dot_general(d_refs[l][...], w_refs[l][...], (((1,), (1,)), ((), ())),
                                preferred_element_type=F32)
            acc = p if acc is None else acc + p
        if has_res:
            acc = acc + r_ref[...]
        o_ref[...] = acc

    blk = pl.BlockSpec((tm, w), lambda h, i: (i, h))
    wspec = pl.BlockSpec((None, w, w), lambda h, i: (h, 0, 0))
    in_specs = [blk] * nw + [wspec] * nw + ([blk] if has_res else [])
    args = list(dys) + list(ws) + ([res] if has_res else [])
    return pl.pallas_call(
        body, name=name, grid=(H, T // tm), in_specs=in_specs, out_specs=blk, out_shape=SDS((T, H * w), F32),
        compiler_params=_cparams(dimension_semantics=("parallel", "parallel")),
    )(*args)


def _bd_bwd_w(name, a, dys, w, tk):
    T, HW = a.shape
    H = HW // w
    nw = len(dys)

    def body(*refs):
        a_ref = refs[0]
        d_refs = refs[1:1 + nw]
        o_refs = refs[1 + nw:]
        k = pl.program_id(1)
        av = a_ref[...]
        for l in range(nw):
            p = lax.dot_general(av, d_refs[l][...], (((0,), (0,)), ((), ())), preferred_element_type=F32)

            @pl.when(k == 0)
            def _(p=p, l=l):
                o_refs[l][...] = p

            @pl.when(k > 0)
            def _(p=p, l=l):
                o_refs[l][...] += p

    blk = pl.BlockSpec((tk, w), lambda h, k: (k, h))
    ospec = pl.BlockSpec((None, w, w), lambda h, k: (h, 0, 0))
    return pl.pallas_call(
        body, name=name, grid=(H, T // tk), in_specs=[blk] * (1 + nw), out_specs=[ospec] * nw,
        out_shape=[SDS((H, w, w), F32)] * nw,
        compiler_params=_cparams(dimension_semantics=("parallel", "arbitrary")),
    )(a, *dys)


def _rms_fwd(name, x, g, tt):
    T, D = x.shape

    def body(x_ref, g_ref, h_ref):
        xv = x_ref[...]
        rstd = lax.rsqrt(jnp.mean(xv * xv, axis=-1, keepdims=True) + EPS)
        h_ref[...] = (xv * rstd * g_ref[...]).astype(BF16)

    row = pl.BlockSpec((tt, D), lambda i: (i, 0))
    vec = pl.BlockSpec((1, D), lambda i: (0, 0))
    return pl.pallas_call(
        body, name=name, grid=(T // tt,), in_specs=[row, vec], out_specs=row, out_shape=SDS((T, D), BF16),
        compiler_params=_cparams(dimension_semantics=("parallel",)),
    )(x, g)


def _rms_bwd(name, dh, x, g, dres, tt, want_bf16):
    T, D = x.shape

    def body(dh_ref, x_ref, g_ref, dres_ref, *outs):
        i = pl.program_id(0)
        dx_ref = outs[0]
        dg_ref = outs[-1]
        xv = x_ref[...]
        rstd = lax.rsqrt(jnp.mean(xv * xv, axis=-1, keepdims=True) + EPS)
        xn = xv * rstd
        dhv = dh_ref[...]
        dyg = dhv * g_ref[...]
        dx = dres_ref[...] + rstd * (dyg - xn * jnp.mean(dyg * xn, axis=-1, keepdims=True))
        dx_ref[...] = dx
        if want_bf16:
            outs[1][...] = dx.astype(BF16)

        @pl.when(i == 0)
        def _():
            dg_ref[...] = jnp.zeros_like(dg_ref)

        dg_ref[...] += _colsum(dhv * xn)

    row = pl.BlockSpec((tt, D), lambda i: (i, 0))
    vec = pl.BlockSpec((1, D), lambda i: (0, 0))
    out_shape = [SDS((T, D), F32)] + ([SDS((T, D), BF16)] if want_bf16 else []) + [SDS((1, D), F32)]
    out_specs = [row] + ([row] if want_bf16 else []) + [vec]
    return pl.pallas_call(
        body, name=name, grid=(T // tt,), in_specs=[row, row, vec, row], out_specs=out_specs, out_shape=out_shape,
        compiler_params=_cparams(dimension_semantics=("arbitrary",)),
    )(dh, x, g, dres)


def _final_loss_bwd(name, x2, tgt, g, tt):
    T, D = x2.shape

    def body(x_ref, t_ref, g_ref, dx_ref, dxb_ref, dg_ref, lv_ref):
        i = pl.program_id(0)
        xv = x_ref[...]
        gv = g_ref[...]
        rstd = lax.rsqrt(jnp.mean(xv * xv, axis=-1, keepdims=True) + EPS)
        xn = xv * rstd
        e = xn * gv - t_ref[...]
        dy = e * (1.0 / D)
        dyg = dy * gv
        dx = rstd * (dyg - xn * jnp.mean(dyg * xn, axis=-1, keepdims=True))
        dx_ref[...] = dx
        dxb_ref[...] = dx.astype(BF16)

        @pl.when(i == 0)
        def _():
            dg_ref[...] = jnp.zeros_like(dg_ref)
            lv_ref[...] = jnp.zeros_like(lv_ref)

        dg_ref[...] += _colsum(dy * xn)
        lv_ref[...] += _colsum(e * e)

    row = pl.BlockSpec((tt, D), lambda i: (i, 0))
    vec = pl.BlockSpec((1, D), lambda i: (0, 0))
    return pl.pallas_call(
        body, name=name, grid=(T // tt,), in_specs=[row, row, vec], out_specs=[row, row, vec, vec],
        out_shape=[SDS((T, D), F32), SDS((T, D), BF16), SDS((1, D), F32), SDS((1, D), F32)],
        compiler_params=_cparams(dimension_semantics=("arbitrary",)),
    )(x2, tgt, g)


def _blk(tt, tc, off=0):
    return pl.BlockSpec((tt, tc), lambda j, i: (i, j + off))


def _vec(r, tc, off=0):
    return pl.BlockSpec((r, tc), lambda j, i: (0, j + off))


def _prev_halo(tt, tc, off=0):
    r = tt // HALO
    return pl.BlockSpec((HALO, tc), lambda j, i: (jnp.maximum(i * r - 1, 0), j + off))


def _next_halo(tt, tc, T, off=0):
    r = tt // HALO
    last = T // HALO - 1
    return pl.BlockSpec((HALO, tc), lambda j, i: (jnp.minimum((i + 1) * r, last), j + off))


def _seq_params():
    return _cparams(dimension_semantics=("parallel", "arbitrary"))


def _pool_fwd(name, proj, Cp, gw, tt):
    T = proj.shape[0]
    G = Cp // gw
    assert G == 4

    def body(u_ref, halo_ref, d_ref):
        j, i = pl.program_id(0), pl.program_id(1)
        u = u_ref[...]
        halo = jnp.where(i > 0, halo_ref[...], 0.0)
        ext = jnp.concatenate([halo, u], axis=0)
        e2 = ext + pltpu.roll(ext, 1, 0)
        e4 = e2 + pltpu.roll(e2, 2, 0)
        e8 = e4 + pltpu.roll(e4, 4, 0)
        e16 = e8 + pltpu.roll(e8, 8, 0)
        s = jnp.where(j == 0, e2, jnp.where(j == 1, e4, jnp.where(j == 2, e8, e16)))[HALO:]
        t1 = (i * tt + 1 + lax.broadcasted_iota(jnp.int32, (tt, 1), 0)).astype(F32)
        win = lax.shift_left(jnp.int32(2), j).astype(F32)
        d_ref[...] = (s / jnp.minimum(t1, win) - u).astype(BF16)

    return pl.pallas_call(
        body, name=name, grid=(G, T // tt), in_specs=[_blk(tt, gw), _prev_halo(tt, gw)], out_specs=_blk(tt, gw),
        out_shape=SDS((T, Cp), BF16), compiler_params=_seq_params(),
    )(proj, proj)


def _pool_bwd_a(name, dy, z, scale, tt):
    T, Cp = dy.shape

    def body(dy_ref, z_ref, s_ref, dz_ref, ds_ref):
        i = pl.program_id(0)
        dyv = dy_ref[...]
        dz_ref[...] = (dyv * s_ref[...]).astype(BF16)

        @pl.when(i == 0)
        def _():
            ds_ref[...] = jnp.zeros_like(ds_ref)

        ds_ref[...] += _colsum(dyv * z_ref[...])

    row = pl.BlockSpec((tt, Cp), lambda i: (i, 0))
    vec = pl.BlockSpec((1, Cp), lambda i: (0, 0))
    return pl.pallas_call(
        body, name=name, grid=(T // tt,), in_specs=[row, row, vec], out_specs=[row, vec],
        out_shape=[SDS((T, Cp), BF16), SDS((1, Cp), F32)],
        compiler_params=_cparams(dimension_semantics=("arbitrary",)),
    )(dy, z, scale)


def _pool_bwd_b(name, dd, gw, tt):
    T, Cp = dd.shape
    G = Cp // gw
    nT = T // tt

    def body(d_ref, nxt_ref, du_ref):
        j, i = pl.program_id(0), pl.program_id(1)
        win = lax.shift_left(jnp.int32(2), j).astype(F32)
        dv = d_ref[...]
        t1 = (i * tt + 1 + lax.broadcasted_iota(jnp.int32, (tt + HALO, 1), 0)).astype(F32)
        nxt = jnp.where(i < nT - 1, nxt_ref[...], 0.0)
        ext = jnp.concatenate([dv, nxt], axis=0) / jnp.minimum(t1, win)
        n = tt + HALO
        f2 = ext + pltpu.roll(ext, n - 1, 0)
        f4 = f2 + pltpu.roll(f2, n - 2, 0)
        f8 = f4 + pltpu.roll(f4, n - 4, 0)
        f16 = f8 + pltpu.roll(f8, n - 8, 0)
        s = jnp.where(j == 0, f2, jnp.where(j == 1, f4, jnp.where(j == 2, f8, f16)))[:tt]
        du_ref[...] = (s - dv).astype(BF16)

    return pl.pallas_call(
        body, name=name, grid=(G, nT), in_specs=[_blk(tt, gw), _next_halo(tt, gw, T)], out_specs=_blk(tt, gw),
        out_shape=SDS((T, Cp), BF16), compiler_params=_seq_params(),
    )(dd, dd)


def _conv_fwd(name, src, off_cols, w, b, tt, tc):
    T = src.shape[0]
    K, C = w.shape
    off = off_cols // tc

    def body(u_ref, halo_ref, w_ref, b_ref, v_ref, vb_ref):
        i = pl.program_id(1)
        u = u_ref[...]
        halo = jnp.where(i > 0, halo_ref[...], 0.0)
        ext = jnp.concatenate([halo, u], axis=0)
        wv = w_ref[...]
        acc = b_ref[...] + u * wv[K - 1:K]
        for s in range(1, K):
            acc = acc + _lookback(ext, s) * wv[K - 1 - s:K - s]
        v_ref[...] = acc
        vb_ref[...] = acc.astype(BF16)

    return pl.pallas_call(
        body, name=name, grid=(C // tc, T // tt),
        in_specs=[_blk(tt, tc, off), _prev_halo(tt, tc, off), _vec(K, tc), _vec(1, tc)],
        out_specs=[_blk(tt, tc), _blk(tt, tc)], out_shape=[SDS((T, C), F32), SDS((T, C), BF16)],
        compiler_params=_seq_params(),
    )(src, src, w, b)


def _conv_bwd(name, dv, src, off_cols, w, tt, tc):
    T, C = dv.shape
    K = w.shape[0]
    off = off_cols // tc
    nT = T // tt

    def body(dv_ref, nxt_ref, u_ref, halo_ref, w_ref, du_ref, dw_ref, db_ref):
        i = pl.program_id(1)
        d = dv_ref[...]
        nxt = jnp.where(i < nT - 1, nxt_ref[...], 0.0)
        dext = jnp.concatenate([d, nxt], axis=0)
        u = u_ref[...]
        halo = jnp.where(i > 0, halo_ref[...], 0.0)
        uext = jnp.concatenate([halo, u], axis=0)
        wv = w_ref[...]
        du = d * wv[K - 1:K]
        rows = [_colsum(d * u)]
        for s in range(1, K):
            du = du + _lookahead(dext, s, tt) * wv[K - 1 - s:K - s]
            rows.append(_colsum(d * _lookback(uext, s)))
        du_ref[...] = du.astype(BF16)
        dw = jnp.concatenate(rows[::-1] + [jnp.zeros((8 - K, tc), F32)], axis=0)

        @pl.when(i == 0)
        def _():
            dw_ref[...] = jnp.zeros_like(dw_ref)
            db_ref[...] = jnp.zeros_like(db_ref)

        dw_ref[...] += dw
        db_ref[...] += _colsum(d)

    return pl.pallas_call(
        body, name=name, grid=(C // tc, nT),
        in_specs=[_blk(tt, tc), _next_halo(tt, tc, T), _blk(tt, tc, off), _prev_halo(tt, tc, off), _vec(K, tc)],
        out_specs=[_blk(tt, tc), _vec(8, tc), _vec(1, tc)],
        out_shape=[SDS((T, C), BF16), SDS((8, C), F32), SDS((1, C), F32)],
        compiler_params=_seq_params(),
    )(dv, dv, src, src, w)


def _lru_gates(ra, ia, v, ba, bi, lam):
    r = _sigmoid(ra + ba)
    ig = _sigmoid(ia + bi)
    nl = -lam
    ex = jnp.exp(-jnp.abs(nl))
    one_p = 1.0 + ex
    l1p = jnp.where(one_p == 1.0, ex, jnp.log(one_p) * ex / (one_p - 1.0))
    sp = jnp.maximum(nl, 0.0) + l1p
    a = jnp.exp(-LRU_C * r * sp)
    mult = jnp.sqrt(1.0 - a * a)
    return r, ig, sp, a, mult


def _lru_fwd(name, ra_pre, ia_pre, v, proj, off_cols, ba, bi, lam, tt, tc):
    T, R = v.shape
    off = off_cols // tc

    def body(ra_ref, ia_ref, v_ref, ug_ref, ba_ref, bi_ref, lam_ref, h_ref, y_ref, carry_ref):
        i = pl.program_id(1)

        @pl.when(i == 0)
        def _():
            carry_ref[...] = jnp.zeros_like(carry_ref)

        vv = v_ref[...]
        _, ig, _, a, mult = _lru_gates(ra_ref[...], ia_ref[...], vv, ba_ref[...], bi_ref[...], lam_ref[...])
        A = a
        B = mult * (ig * vv)
        row = lax.broadcasted_iota(jnp.int32, (tt, 1), 0)
        s = 1
        while s < tt:
            m = row >= s
            B = jnp.where(m, A * pltpu.roll(B, s, 0) + B, B)
            A = jnp.where(m, A * pltpu.roll(A, s, 0), A)
            s *= 2
        h = A * carry_ref[7:8, :] + B
        h_ref[...] = h
        carry_ref[...] = h[tt - 8:, :]
        y_ref[...] = (h * _gelu(ug_ref[...])).astype(BF16)

    return pl.pallas_call(
        body, name=name, grid=(R // tc, T // tt),
        in_specs=[_blk(tt, tc), _blk(tt, tc), _blk(tt, tc), _blk(tt, tc, off), _vec(1, tc), _vec(1, tc), _vec(1, tc)],
        out_specs=[_blk(tt, tc), _blk(tt, tc)], out_shape=[SDS((T, R), F32), SDS((T, R), BF16)],
        scratch_shapes=[pltpu.VMEM((8, tc), F32)], compiler_params=_seq_params(),
    )(ra_pre, ia_pre, v, proj, ba, bi, lam)


def _lru_bwd(name, dy, hs, proj, off_cols, ra_pre, ia_pre, v, ba, bi, lam, tt, tc):
    T, R = v.shape
    off = off_cols // tc
    nT = T // tt
    r16 = tt // HALO

    def rblk(o=0):
        return pl.BlockSpec((tt, tc), lambda j, i: (nT - 1 - i, j + o))

    hprev = pl.BlockSpec((HALO, tc), lambda j, i: (jnp.maximum((nT - 1 - i) * r16 - 1, 0), j))

    def body(dy_ref, h_ref, hp_ref, ug_ref, ra_ref, ia_ref, v_ref, ba_ref, bi_ref, lam_ref,
             dra_ref, dia_ref, dv_ref, dug_ref, dba_ref, dbi_ref, dlam_ref, dh_carry, a_carry):
        i = pl.program_id(1)

        @pl.when(i == 0)
        def _():
            dh_carry[...] = jnp.zeros_like(dh_carry)
            a_carry[...] = jnp.zeros_like(a_carry)
            dba_ref[...] = jnp.zeros_like(dba_ref)
            dbi_ref[...] = jnp.zeros_like(dbi_ref)
            dlam_ref[...] = jnp.zeros_like(dlam_ref)

        vv = v_ref[...]
        lamv = lam_ref[...]
        r, ig, sp, a, mult = _lru_gates(ra_ref[...], ia_ref[...], vv, ba_ref[...], bi_ref[...], lamv)
        h = h_ref[...]
        gel, dgel = _gelu_and_grad(ug_ref[...])
        dyv = dy_ref[...]
        dug_ref[...] = (dyv * h * dgel).astype(BF16)
        dhs = dyv * gel

        a_ext = jnp.concatenate([a, jnp.broadcast_to(a_carry[0:1, :], (HALO, tc))], axis=0)
        A = _lookahead(a_ext, 1, tt)
        B = dhs
        row = lax.broadcasted_iota(jnp.int32, (tt, 1), 0)
        s = 1
        while s < tt:
            m = row < tt - s
            B = jnp.where(m, B + A * pltpu.roll(B, tt - s, 0), B)
            A = jnp.where(m, A * pltpu.roll(A, tt - s, 0), A)
            s *= 2
        dH = B + A * dh_carry[0:1, :]
        dh_carry[...] = dH[:8, :]
        a_carry[...] = a[:8, :]

        hp = jnp.where(i < nT - 1, hp_ref[...], 0.0)
        h_prev = _lookback(jnp.concatenate([hp, h], axis=0), 1)
        da = dH * h_prev
        dmult = dH * (ig * vv)
        div = dH * mult
        dv_ref[...] = div * ig
        dig = div * vv
        dlog_a = da * a - dmult * (a * a) / mult
        dr = dlog_a * (-LRU_C * sp)
        dra = dr * r * (1.0 - r)
        dia = dig * ig * (1.0 - ig)
        dra_ref[...] = dra.astype(BF16)
        dia_ref[...] = dia.astype(BF16)
        dba_ref[...] += _colsum(dra)
        dbi_ref[...] += _colsum(dia)
        dlam_ref[...] += _colsum(dlog_a * (-LRU_C * r))

        @pl.when(i == nT - 1)
        def _():
            dlam_ref[...] = dlam_ref[...] * (-_sigmoid(-lamv))

    return pl.pallas_call(
        body, name=name, grid=(R // tc, nT),
        in_specs=[rblk(), rblk(), hprev, rblk(off), rblk(), rblk(), rblk(), _vec(1, tc), _vec(1, tc), _vec(1, tc)],
        out_specs=[rblk(), rblk(), rblk(), rblk(), _vec(1, tc), _vec(1, tc), _vec(1, tc)],
        out_shape=[SDS((T, R), BF16), SDS((T, R), BF16), SDS((T, R), F32), SDS((T, R), BF16),
                   SDS((1, R), F32), SDS((1, R), F32), SDS((1, R), F32)],
        scratch_shapes=[pltpu.VMEM((8, tc), F32), pltpu.VMEM((8, tc), F32)], compiler_params=_seq_params(),
    )(dy, hs, hs, proj, ra_pre, ia_pre, v, ba, bi, lam)


def _merge_fwd(name, proj, off_cols, b_gate, P, Q, tt, tc):
    T, D = P.shape
    o0 = off_cols // tc
    o1 = o0 + D // tc

    def body(g0_ref, g1_ref, b0_ref, b1_ref, p_ref, q_ref, m_ref):
        g0 = _sigmoid(g0_ref[...] + b0_ref[...])
        g1 = _sigmoid(g1_ref[...] + b1_ref[...])
        m_ref[...] = (g0 * p_ref[...] + g1 * q_ref[...]).astype(BF16)

    return pl.pallas_call(
        body, name=name, grid=(D // tc, T // tt),
        in_specs=[_blk(tt, tc, o0), _blk(tt, tc, o1), _vec(1, tc), _vec(1, tc, D // tc), _blk(tt, tc), _blk(tt, tc)],
        out_specs=_blk(tt, tc), out_shape=SDS((T, D), BF16), compiler_params=_seq_params(),
    )(proj, proj, b_gate, b_gate, P, Q)


def _merge_bwd(name, dm, proj, off_cols, b_gate, P, Q, tt, tc):
    T, D = P.shape
    o0 = off_cols // tc
    o1 = o0 + D // tc

    def body(dm_ref, g0_ref, g1_ref, b0_ref, b1_ref, p_ref, q_ref, dp_ref, dq_ref, dl0_ref, dl1_ref, db0_ref, db1_ref):
        i = pl.program_id(1)
        g0 = _sigmoid(g0_ref[...] + b0_ref[...])
        g1 = _sigmoid(g1_ref[...] + b1_ref[...])
        d = dm_ref[...].astype(F32)
        dp_ref[...] = (d * g0).astype(BF16)
        dq_ref[...] = (d * g1).astype(BF16)
        dl0 = d * p_ref[...] * g0 * (1.0 - g0)
        dl1 = d * q_ref[...] * g1 * (1.0 - g1)
        dl0_ref[...] = dl0.astype(BF16)
        dl1_ref[...] = dl1.astype(BF16)

        @pl.when(i == 0)
        def _():
            db0_ref[...] = jnp.zeros_like(db0_ref)
            db1_ref[...] = jnp.zeros_like(db1_ref)

        db0_ref[...] += _colsum(dl0)
        db1_ref[...] += _colsum(dl1)

    b = _blk(tt, tc)
    return pl.pallas_call(
        body, name=name, grid=(D // tc, T // tt),
        in_specs=[b, _blk(tt, tc, o0), _blk(tt, tc, o1), _vec(1, tc), _vec(1, tc, D // tc), b, b],
        out_specs=[b, b, b, b, _vec(1, tc), _vec(1, tc)],
        out_shape=[SDS((T, D), BF16)] * 4 + [SDS((1, D), F32)] * 2, compiler_params=_seq_params(),
    )(dm, proj, proj, b_gate, b_gate, P, Q)


def _ffn_act_fwd(name, up, w, b, tt, tc):
    T = up.shape[0]
    K, F = w.shape
    nF = F // tc

    def body(gp_ref, halo_ref, val_ref, w_ref, b_ref, z_ref):
        i = pl.program_id(1)
        gp = gp_ref[...]
        halo = jnp.where(i > 0, halo_ref[...], 0.0)
        ext = jnp.concatenate([halo, gp], axis=0)
        wv = w_ref[...]
        c = b_ref[...] + gp * wv[K - 1:K]
        for s in range(1, K):
            c = c + _lookback(ext, s) * wv[K - 1 - s:K - s]
        z_ref[...] = (_gelu(c) * val_ref[...]).astype(BF16)

    return pl.pallas_call(
        body, name=name, grid=(nF, T // tt),
        in_specs=[_blk(tt, tc), _prev_halo(tt, tc), _blk(tt, tc, nF), _vec(K, tc), _vec(1, tc)],
        out_specs=_blk(tt, tc), out_shape=SDS((T, F), BF16), compiler_params=_seq_params(),
    )(up, up, up, w, b)


def _ffn_act_bwd(name, up, dz, w, b, tt, tc):
    T = up.shape[0]
    K, F = w.shape
    nF = F // tc
    nT = T // tt

    def body(gp_ref, gph_ref, gpn_ref, val_ref, valn_ref, dz_ref, dzn_ref, w_ref, b_ref, dup_ref, dw_ref, db_ref):
        i = pl.program_id(1)
        wv = w_ref[...]
        gp = gp_ref[...]
        halo = jnp.where(i > 0, gph_ref[...], 0.0)
        full = jnp.concatenate([halo, gp, gpn_ref[...]], axis=0)
        c = b_ref[...] + full * wv[K - 1:K]
        for s in range(1, K):
            c = c + pltpu.roll(full, s, 0) * wv[K - 1 - s:K - s]
        c = c[HALO:]
        gate, dgate = _gelu_and_grad(c)
        dz_e = jnp.concatenate([dz_ref[...], dzn_ref[...]], axis=0).astype(F32)
        val_e = jnp.concatenate([val_ref[...], valn_ref[...]], axis=0)
        dc_e = dz_e * val_e * dgate
        row = lax.broadcasted_iota(jnp.int32, (tt + HALO, 1), 0)
        dc_e = jnp.where((row < tt) | (i < nT - 1), dc_e, 0.0)
        dc = dc_e[:tt]
        dgp = dc * wv[K - 1:K]
        uext = full[:tt + HALO]
        rows = [_colsum(dc * gp)]
        for s in range(1, K):
            dgp = dgp + _lookahead(dc_e, s, tt) * wv[K - 1 - s:K - s]
            rows.append(_colsum(dc * _lookback(uext, s)))
        dup_ref[0] = dgp.astype(BF16)
        dup_ref[1] = (dz_e[:tt] * gate[:tt]).astype(BF16)
        dw = jnp.concatenate(rows[::-1] + [jnp.zeros((8 - K, tc), F32)], axis=0)

        @pl.when(i == 0)
        def _():
            dw_ref[...] = jnp.zeros_like(dw_ref)
            db_ref[...] = jnp.zeros_like(db_ref)

        dw_ref[...] += dw
        db_ref[...] += _colsum(dc)

    return pl.pallas_call(
        body, name=name, grid=(nF, nT),
        in_specs=[_blk(tt, tc), _prev_halo(tt, tc), _next_halo(tt, tc, T), _blk(tt, tc, nF), _next_halo(tt, tc, T, nF),
                  _blk(tt, tc), _next_halo(tt, tc, T), _vec(K, tc), _vec(1, tc)],
        out_specs=[pl.BlockSpec((2, tt, tc), lambda j, i: (0, i, j)), _vec(8, tc), _vec(1, tc)],
        out_shape=[SDS((2, T, F), BF16), SDS((8, F), F32), SDS((1, F), F32)],
        compiler_params=_seq_params(),
    )(up, up, up, up, up, dz, dz, w, b)


def _adamw(name, w, g, m, v):
    R, C = w.shape
    tr = R
    if R * C * 4 > ADAMW_BLOCK_BYTES:
        tr = _pick(R, [t for t in (256, 128, 64, 32, 16, 8) if t * C * 4 <= ADAMW_BLOCK_BYTES])
    c1 = 1.0 - ADAM_B1 ** ADAM_STEP
    c2 = 1.0 - ADAM_B2 ** ADAM_STEP

    def body(w_ref, g_ref, m_ref, v_ref, d_ref, nm_ref, nv_ref):
        gv = g_ref[...]
        nm = ADAM_B1 * m_ref[...] + (1.0 - ADAM_B1) * gv
        nv = ADAM_B2 * v_ref[...] + (1.0 - ADAM_B2) * (gv * gv)
        nm_ref[...] = nm
        nv_ref[...] = nv
        d_ref[...] = -ADAM_LR * ((nm / c1) / (jnp.sqrt(nv / c2) + ADAM_EPS) + ADAM_WD * w_ref[...])

    spec = pl.BlockSpec((tr, C), lambda i: (i, 0))
    return pl.pallas_call(
        body, name=name, grid=(R // tr,), in_specs=[spec] * 4, out_specs=[spec] * 3,
        out_shape=[SDS((R, C), F32)] * 3, compiler_params=_cparams(dimension_semantics=("parallel",)),
    )(w, g, m, v)


def _add_halves(name, part, rcv, pos):
    S, R, C = part.shape
    h = R // 2
    tr = _pick(h, (512, 256, 128, 64, 32, 16))
    nb = h // tr

    def body(pos_ref, p_ref, r_ref, o_ref):
        o_ref[...] = (p_ref[...].astype(F32) + r_ref[...].astype(F32)).astype(BF16)

    return pl.pallas_call(
        body, name=name,
        grid_spec=pltpu.PrefetchScalarGridSpec(
            num_scalar_prefetch=1, grid=(S, nb),
            in_specs=[pl.BlockSpec((None, tr, C), lambda s, r, pos: (s, pos[0] * nb + r, 0)),
                      pl.BlockSpec((None, tr, C), lambda s, r, pos: (s, r, 0))],
            out_specs=pl.BlockSpec((None, tr, C), lambda s, r, pos: (s, r, 0))),
        out_shape=SDS((S, h, C), BF16), compiler_params=_cparams(dimension_semantics=("parallel", "parallel")),
    )(pos, part, rcv)


def _add_chips(name, chipsum, rcv, pos):
    S, h, C = chipsum.shape
    tr = _pick(h, (512, 256, 128, 64, 32, 16))

    def body(pos_ref, own_ref, r_ref, o_ref):
        o_ref[...] = ((own_ref[...].astype(F32) + r_ref[0].astype(F32)) + r_ref[1].astype(F32)) + r_ref[2].astype(F32)

    return pl.pallas_call(
        body, name=name,
        grid_spec=pltpu.PrefetchScalarGridSpec(
            num_scalar_prefetch=1, grid=(h // tr,),
            in_specs=[pl.BlockSpec((None, tr, C), lambda r, pos: (pos[1], r, 0)),
                      pl.BlockSpec((3, tr, C), lambda r, pos: (0, r, 0))],
            out_specs=pl.BlockSpec((None, tr, C), lambda r, pos: (pos[0], r, 0))),
        out_shape=SDS((2, h, C), F32), compiler_params=_cparams(dimension_semantics=("parallel",)),
    )(pos, chipsum, rcv)


def _place():
    x, y, c = lax.axis_index("x"), lax.axis_index("y"), lax.axis_index("c")
    chips = [(1 - x, y), (x, 1 - y), (1 - x, 1 - y)]
    slots = [2 * cx + cy for cx, cy in chips]
    return x, y, c, chips, slots


def _gather_weights(shards, split):
    n = len(shards)

    def body(*refs):
        ins, outs = refs[:n], refs[n:2 * n]
        send_sems, recv_sems, local_sems = refs[2 * n:]
        x, y, c, chips, slots = _place()
        me = 2 * x + y
        sibling = (x, y, 1 - c)

        def rows(a, half):
            hrows = ins[a].shape[0] // 2
            return pl.ds(half * hrows, hrows)

        def ici(a, j):
            if split[a]:
                src, dst = ins[a].at[rows(a, c)], outs[a].at[me, rows(a, c)]
            else:
                src, dst = ins[a], outs[a].at[me]
            return pltpu.make_async_remote_copy(src_ref=src, dst_ref=dst, send_sem=send_sems.at[a, j],
                                                recv_sem=recv_sems.at[a, j], device_id=(*chips[j], c), device_id_type=MESH)

        def landed(a, j):
            dst = outs[a].at[slots[j], rows(a, c)] if split[a] else outs[a].at[slots[j]]
            return pltpu.make_async_remote_copy(src_ref=dst, dst_ref=dst, send_sem=send_sems.at[a, j],
                                                recv_sem=recv_sems.at[a, j], device_id=(*chips[j], c), device_id_type=MESH)

        def d2d(a, j, half):
            blk = outs[a].at[slots[j], rows(a, half)]
            return pltpu.make_async_remote_copy(src_ref=blk, dst_ref=blk, send_sem=send_sems.at[a, 3 + j],
                                                recv_sem=recv_sems.at[a, 3 + j], device_id=sibling, device_id_type=MESH)

        local = [pltpu.make_async_copy(ins[a], outs[a].at[me], local_sems.at[a]) for a in range(n)]
        for cp in local:
            cp.start()
        for a in range(n):
            for j in range(3):
                ici(a, j).start()
        for a in range(n):
            for j in range(3):
                landed(a, j).wait_recv()
                if split[a]:
                    d2d(a, j, c).start()
        for a in range(n):
            if split[a]:
                for j in range(3):
                    d2d(a, j, 1 - c).wait_recv()
        for a in range(n):
            for j in range(3):
                ici(a, j).wait_send()
                if split[a]:
                    d2d(a, j, c).wait_send()
        for cp in local:
            cp.wait()

    return pl.pallas_call(
        body, name="gather_weights", in_specs=[ANY] * n, out_specs=[ANY] * n,
        out_shape=[SDS((N_SLOTS,) + s.shape, s.dtype) for s in shards],
        scratch_shapes=[pltpu.SemaphoreType.DMA((n, 6)), pltpu.SemaphoreType.DMA((n, 6)), pltpu.SemaphoreType.DMA((n,))],
        compiler_params=_cparams(has_side_effects=True),
    )(*shards)


def _swap_partial_halves(parts):
    n = len(parts)

    def body(*refs):
        ins, outs = refs[:n], refs[n:2 * n]
        send_sems, recv_sems = refs[2 * n:]
        x, y, c, _, _ = _place()
        sibling = (x, y, 1 - c)
        cps = []
        for a in range(n):
            h = ins[a].shape[1] // 2
            cps.append(pltpu.make_async_remote_copy(
                src_ref=ins[a].at[:, pl.ds((1 - c) * h, h)], dst_ref=outs[a], send_sem=send_sems.at[a],
                recv_sem=recv_sems.at[a], device_id=sibling, device_id_type=MESH))
        for cp in cps:
            cp.start()
        for cp in cps:
            cp.wait()

    return pl.pallas_call(
        body, name="swap_partial_halves", in_specs=[ANY] * n, out_specs=[ANY] * n,
        out_shape=[SDS((p.shape[0], p.shape[1] // 2, p.shape[2]), p.dtype) for p in parts],
        scratch_shapes=[pltpu.SemaphoreType.DMA((n,)), pltpu.SemaphoreType.DMA((n,))],
        compiler_params=_cparams(has_side_effects=True),
    )(*parts)


def _exchange_chip_sums(sums):
    n = len(sums)

    def body(*refs):
        ins, outs = refs[:n], refs[n:2 * n]
        send_sems, recv_sems = refs[2 * n:]
        x, y, c, chips, slots = _place()
        cps = []
        for a in range(n):
            for j in range(3):
                cps.append(pltpu.make_async_remote_copy(
                    src_ref=ins[a].at[slots[j]], dst_ref=outs[a].at[j], send_sem=send_sems.at[a, j],
                    recv_sem=recv_sems.at[a, j], device_id=(*chips[j], c), device_id_type=MESH))
        for cp in cps:
            cp.start()
        for cp in cps:
            cp.wait()

    return pl.pallas_call(
        body, name="exchange_chip_sums", in_specs=[ANY] * n, out_specs=[ANY] * n,
        out_shape=[SDS((3,) + s.shape[1:], s.dtype) for s in sums],
        scratch_shapes=[pltpu.SemaphoreType.DMA((n, 3)), pltpu.SemaphoreType.DMA((n, 3))],
        compiler_params=_cparams(has_side_effects=True),
    )(*sums)


def _swap_reduced_halves(totals):
    n = len(totals)

    def body(*refs):
        ins, outs = refs[:n], refs[n:2 * n]
        send_sems, recv_sems = refs[2 * n:]
        x, y, c, _, _ = _place()
        sibling = (x, y, 1 - c)
        sends = [pltpu.make_async_remote_copy(src_ref=outs[a].at[c], dst_ref=outs[a].at[c], send_sem=send_sems.at[a],
                                              recv_sem=recv_sems.at[a], device_id=sibling, device_id_type=MESH)
                 for a in range(n)]
        for cp in sends:
            cp.start()
        for a in range(n):
            got = outs[a].at[1 - c]
            pltpu.make_async_remote_copy(src_ref=got, dst_ref=got, send_sem=send_sems.at[a], recv_sem=recv_sems.at[a],
                                         device_id=sibling, device_id_type=MESH).wait_recv()
        for cp in sends:
            cp.wait_send()

    return pl.pallas_call(
        body, name="swap_reduced_halves", in_specs=[ANY] * n, out_specs=[ANY] * n,
        out_shape=[SDS(t.shape, t.dtype) for t in totals], input_output_aliases={a: a for a in range(n)},
        scratch_shapes=[pltpu.SemaphoreType.DMA((n,)), pltpu.SemaphoreType.DMA((n,))],
        compiler_params=_cparams(has_side_effects=True),
    )(*totals)


def _allreduce_small(pack):
    rows = pack.shape[0]

    def body(in_ref, out_ref, buf, send_sems, recv_sems):
        x, y, c = lax.axis_index("x"), lax.axis_index("y"), lax.axis_index("c")
        me = 4 * x + 2 * y + c
        buf[me] = in_ref[...]
        cps = []
        for k in range(1, 8):
            kx, ky, kc = (k >> 2) & 1, (k >> 1) & 1, k & 1
            peer = (x ^ kx, y ^ ky, c ^ kc)
            cps.append(pltpu.make_async_remote_copy(src_ref=in_ref, dst_ref=buf.at[me], send_sem=send_sems.at[k - 1],
                                                    recv_sem=recv_sems.at[k - 1], device_id=peer, device_id_type=MESH))
        for cp in cps:
            cp.start()
        for k in range(1, 8):
            got = buf.at[me ^ k]
            pltpu.make_async_remote_copy(src_ref=got, dst_ref=got, send_sem=send_sems.at[k - 1],
                                         recv_sem=recv_sems.at[k - 1], device_id=(x, y, c), device_id_type=MESH).wait_recv()
        for cp in cps:
            cp.wait_send()
        acc = buf[0]
        for d in range(1, 8):
            acc = acc + buf[d]
        out_ref[...] = acc

    vm = pl.BlockSpec(memory_space=pltpu.VMEM)
    return pl.pallas_call(
        body, name="allreduce_small", in_specs=[vm], out_specs=vm, out_shape=SDS(pack.shape, F32),
        scratch_shapes=[pltpu.VMEM((8, rows, LANES), F32), pltpu.SemaphoreType.DMA((7,)), pltpu.SemaphoreType.DMA((7,))],
        compiler_params=_cparams(has_side_effects=True),
    )(pack)


def _rows2d(a):
    return a.reshape((-1, a.shape[-1]))


def _from_slots_blockdiag(g, H, w):
    q = w // N_SLOTS
    return g.reshape(N_SLOTS, H, q, w).transpose(1, 0, 2, 3).reshape(H, w, w)


def _to_slots_blockdiag(d, H, w):
    q = w // N_SLOTS
    return d.reshape(H, N_SLOTS, q, w).transpose(1, 0, 2, 3).reshape(N_SLOTS, H * q, w)


def kernel(x, g_mix, w_in, b_gate, w_pool, pool_scale, lru_conv_w, lru_conv_b, w_a, b_a, w_i, b_i, lru_lambda, w_pool_proj, w_lru_proj, w_out, g_mlp, w_up, ffn_conv_w, ffn_conv_b, w_down, g_final, loss_target, m_g_mix, m_w_in, m_b_gate, m_w_pool, m_pool_scale, m_lru_conv_w, m_lru_conv_b, m_w_a, m_b_a, m_w_i, m_b_i, m_lru_lambda, m_w_pool_proj, m_w_lru_proj, m_w_out, m_g_mlp, m_w_up, m_ffn_conv_w, m_ffn_conv_b, m_w_down, m_g_final, v_g_mix, v_w_in, v_b_gate, v_w_pool, v_pool_scale, v_lru_conv_w, v_lru_conv_b, v_w_a, v_b_a, v_w_i, v_b_i, v_lru_lambda, v_w_pool_proj, v_w_lru_proj, v_w_out, v_g_mlp, v_w_up, v_ffn_conv_w, v_ffn_conv_b, v_w_down, v_g_final):
    weights = dict(g_mix=g_mix, w_in=w_in, b_gate=b_gate, w_pool=w_pool, pool_scale=pool_scale, lru_conv_w=lru_conv_w,
                   lru_conv_b=lru_conv_b, w_a=w_a, b_a=b_a, w_i=w_i, b_i=b_i, lru_lambda=lru_lambda,
                   w_pool_proj=w_pool_proj, w_lru_proj=w_lru_proj, w_out=w_out, g_mlp=g_mlp, w_up=w_up,
                   ffn_conv_w=ffn_conv_w, ffn_conv_b=ffn_conv_b, w_down=w_down, g_final=g_final)
    mom_m = dict(g_mix=m_g_mix, w_in=m_w_in, b_gate=m_b_gate, w_pool=m_w_pool, pool_scale=m_pool_scale,
                 lru_conv_w=m_lru_conv_w, lru_conv_b=m_lru_conv_b, w_a=m_w_a, b_a=m_b_a, w_i=m_w_i, b_i=m_b_i,
                 lru_lambda=m_lru_lambda, w_pool_proj=m_w_pool_proj, w_lru_proj=m_w_lru_proj, w_out=m_w_out,
                 g_mlp=m_g_mlp, w_up=m_w_up, ffn_conv_w=m_ffn_conv_w, ffn_conv_b=m_ffn_conv_b, w_down=m_w_down,
                 g_final=m_g_final)
    mom_v = dict(g_mix=v_g_mix, w_in=v_w_in, b_gate=v_b_gate, w_pool=v_w_pool, pool_scale=v_pool_scale,
                 lru_conv_w=v_lru_conv_w, lru_conv_b=v_lru_conv_b, w_a=v_w_a, b_a=v_b_a, w_i=v_w_i, b_i=v_b_i,
                 lru_lambda=v_lru_lambda, w_pool_proj=v_w_pool_proj, w_lru_proj=v_w_lru_proj, w_out=v_w_out,
                 g_mlp=v_g_mlp, w_up=v_w_up, ffn_conv_w=v_ffn_conv_w, ffn_conv_b=v_ffn_conv_b, w_down=v_w_down,
                 g_final=v_g_final)
    order = list(weights)

    T, D = x.shape[1], x.shape[2]
    Cp = pool_scale.shape[1]
    G, gw = w_pool.shape[1], w_pool.shape[3]
    H, hw = w_a.shape[1], w_a.shape[3]
    R = b_a.shape[1]
    F = ffn_conv_b.shape[1]
    KC, KF = lru_conv_w.shape[1], ffn_conv_w.shape[1]
    p0, p1, p2 = Cp, Cp + R, Cp + 2 * R
    xs = x.reshape(T, D)
    tgt = loss_target.reshape(T, D)

    my_x, my_y, my_c = lax.axis_index("x"), lax.axis_index("y"), lax.axis_index("c")
    slot = 2 * my_x + my_y
    pos = jnp.stack([my_c, slot]).astype(jnp.int32)

    big = ["w_in", "w_pool", "w_a", "w_i", "w_pool_proj", "w_lru_proj", "w_out", "w_up", "w_down"]
    shards = [_rows2d(weights[n][0]).astype(BF16) for n in big] + [lru_conv_w[0], ffn_conv_w[0]]
    gathered = _gather_weights(shards, [True] * len(big) + [False, False])
    gw_ = dict(zip(big + ["lru_conv_w", "ffn_conv_w"], gathered))
    W_in = gw_["w_in"]
    W_pool = _from_slots_blockdiag(gw_["w_pool"], G, gw)
    W_a = _from_slots_blockdiag(gw_["w_a"], H, hw)
    W_i = _from_slots_blockdiag(gw_["w_i"], H, hw)
    W_pp = gw_["w_pool_proj"]
    W_lp = gw_["w_lru_proj"].reshape(R, D)
    W_out = gw_["w_out"].reshape(D, D)
    W_up = gw_["w_up"]
    W_down = gw_["w_down"].reshape(F, D)
    cw_lru = gw_["lru_conv_w"].transpose(1, 0, 2).reshape(KC, R)
    cw_ffn = gw_["ffn_conv_w"].transpose(1, 0, 2).reshape(KF, F)

    tt_row = _pick(T, (256, 128))
    tt_seq = _pick(T, (256, 128))
    tc_seq = _pick(R, (512, 256, 128))
    tm = _pick(T, (512, 256))
    n_in = W_in.shape[2]
    n_up = W_up.shape[2]
    n_pp = W_pp.shape[2]
    tn_in = _pick(n_in, (768, 1152, 384, 128))
    tn_up = _pick(n_up, (1024, 768, 512, 128))
    tn_pp = _pick(n_pp, (512, 256, 128))
    tD = _pick(D, (1024, 512))
    tDw = _pick(D, (2048, 1024, 512))
    tkT = _pick(T, (512, 256))

    h = _rms_fwd("rms1_fwd", xs, g_mix, tt_row)
    proj = _matmul("proj_fwd", "nn", h, W_in, out_dtype=F32, tm=tm, tn=tn_in, tk=D)[0]
    d_pool = _pool_fwd("pool_fwd", proj, Cp, gw, tt_seq)
    z_pool, y_pool = _bd_fwd("pool_mix_fwd", d_pool, [W_pool], tm, scale=pool_scale)
    v_f, v_b = _conv_fwd("lru_conv_fwd", proj, p0, cw_lru, lru_conv_b, tt_seq, tc_seq)
    ra_pre, ia_pre = _bd_fwd("lru_gate_fwd", v_b, [W_a, W_i], tm)
    hs, y_lru = _lru_fwd("lru_scan_fwd", ra_pre, ia_pre, v_f, proj, p1, b_a, b_i, lru_lambda, tt_seq, tc_seq)
    P = _matmul("pool_proj_fwd", "nn", y_pool, W_pp, out_dtype=F32, tm=tm, tn=tn_pp, tk=Cp)[0]
    Q = _matmul("lru_proj_fwd", "nn", y_lru, W_lp, out_dtype=F32, tm=tm, tn=tD, tk=R)[0]
    merged = _merge_fwd("merge_fwd", proj, p2, b_gate, P, Q, tt_seq, tc_seq)
    x1 = _matmul("out_fwd", "nn", merged, W_out, out_dtype=F32, tm=tm, tn=tD, tk=D, res=xs)[0]
    h2 = _rms_fwd("rms2_fwd", x1, g_mlp, tt_row)
    up = _matmul("up_fwd", "nn", h2, W_up, out_dtype=F32, tm=tm, tn=tn_up, tk=D)[0]
    z = _ffn_act_fwd("ffn_act_fwd", up, cw_ffn, ffn_conv_b, tt_seq, tc_seq)
    x2 = _matmul("down_fwd", "nn", z, W_down, out_dtype=F32, tm=tm, tn=tD, tk=_pick(F, (2048, 1536, 1024, 512)), res=x1)[0]

    dx2, dx2b, d_g_final, lossvec = _final_loss_bwd("final_loss_bwd", x2, tgt, g_final.reshape(1, D), tt_row)
    dz = _matmul("down_bwd_x", "nt", dx2b, W_down, out_dtype=BF16, tm=tm, tn=_pick(F, (1536, 1024, 512)), tk=D)[0]
    dW_down = _matmul("down_bwd_w", "tn", z, dx2b, out_dtype=BF16, tm=_pick(F, (1536, 1024, 512)), tn=tDw, tk=tkT)
    dup, d_cw_ffn, d_ffn_b = _ffn_act_bwd("ffn_act_bwd", up, dz, cw_ffn, ffn_conv_b, tt_seq, tc_seq)
    dh2 = _matmul("up_bwd_x", "nt", dup, W_up, out_dtype=F32, tm=tm, tn=D, tk=tn_up)[0]
    dW_up = _matmul("up_bwd_w", "tn", h2, dup, out_seg=N_SLOTS, out_dtype=BF16, tm=tDw, tn=tn_up, tk=tkT)
    dx1, dx1b, d_g_mlp = _rms_bwd("rms2_bwd", dh2, x1, g_mlp, dx2, tt_row, True)
    dmerged = _matmul("out_bwd_x", "nt", dx1b, W_out, out_dtype=BF16, tm=tm, tn=tD, tk=D)[0]
    dW_out = _matmul("out_bwd_w", "tn", merged, dx1b, out_dtype=BF16, tm=tDw, tn=tD, tk=tkT)
    dP, dQ, dgl0, dgl1, d_bg0, d_bg1 = _merge_bwd("merge_bwd", dmerged, proj, p2, b_gate, P, Q, tt_seq, tc_seq)
    dy_pool = _matmul("pool_proj_bwd_x", "nt", dP, W_pp, out_dtype=F32, tm=tm, tn=Cp, tk=tn_pp)[0]
    dW_pp = _matmul("pool_proj_bwd_w", "tn", y_pool, dP, out_seg=N_SLOTS, out_dtype=BF16, tm=Cp, tn=tn_pp, tk=tkT)
    dy_lru = _matmul("lru_proj_bwd_x", "nt", dQ, W_lp, out_dtype=F32, tm=tm, tn=tD, tk=D)[0]
    dW_lp = _matmul("lru_proj_bwd_w", "tn", y_lru, dQ, out_dtype=BF16, tm=_pick(R, (2048, 1024, 512)), tn=tD, tk=tkT)
    dra, dia, dv1, du_gelu, d_b_a, d_b_i, d_lam = _lru_bwd(
        "lru_scan_bwd", dy_lru, hs, proj, p1, ra_pre, ia_pre, v_f, b_a, b_i, lru_lambda, tt_seq, tc_seq)
    dv = _bd_bwd_x("lru_gate_bwd_x", [dra, dia], [W_a, W_i], tm, res=dv1)
    dW_a, dW_i = _bd_bwd_w("lru_gate_bwd_w", v_b, [dra, dia], hw, tkT)
    du_lru, d_cw_lru, d_lru_b = _conv_bwd("lru_conv_bwd", dv, proj, p0, cw_lru, tt_seq, tc_seq)
    dzp, d_pool_scale = _pool_bwd_a("pool_scale_bwd", dy_pool, z_pool, pool_scale, tt_row)
    dd = _bd_bwd_x("pool_mix_bwd_x", [dzp], [W_pool], tm)
    (dW_pool,) = _bd_bwd_w("pool_mix_bwd_w", d_pool, [dzp], gw, tkT)
    du_pool = _pool_bwd_b("pool_bwd", dd, gw, tt_seq)
    dproj = jnp.concatenate([du_pool, du_lru, du_gelu, dgl0, dgl1], axis=1)
    dh = _matmul("proj_bwd_x", "nt", dproj, W_in, out_dtype=F32, tm=tm, tn=D, tk=tn_in)[0]
    dW_in = _matmul("proj_bwd_w", "tn", h, dproj, out_seg=N_SLOTS, out_dtype=BF16, tm=tDw, tn=tn_in, tk=tkT)
    grad_x, d_g_mix = _rms_bwd("rms1_bwd", dh, xs, g_mix, dx1, tt_row, False)

    q_rows = lambda a, rows: a.reshape(N_SLOTS, rows // N_SLOTS, a.shape[-1])
    parts = [dW_in,
             _to_slots_blockdiag(dW_pool.astype(BF16), G, gw),
             _to_slots_blockdiag(dW_a.astype(BF16), H, hw),
             _to_slots_blockdiag(dW_i.astype(BF16), H, hw),
             dW_pp, q_rows(dW_lp, R), q_rows(dW_out, D), dW_up, q_rows(dW_down, F)]
    from_sibling = _swap_partial_halves(parts)
    chip_sums = [_add_halves(f"add_halves_{n}", p, r, pos) for n, p, r in zip(big, parts, from_sibling)]
    from_chips = _exchange_chip_sums(chip_sums)
    totals = [_add_chips(f"add_chips_{n}", s, r, pos) for n, s, r in zip(big, chip_sums, from_chips)]
    full = _swap_reduced_halves(totals)
    grads = {n: f.reshape(weights[n].shape) for n, f in zip(big, full)}

    small = ["g_mix", "b_gate", "pool_scale", "lru_conv_b", "b_a", "b_i", "lru_lambda", "g_mlp", "ffn_conv_b", "g_final"]
    small_g = [d_g_mix, jnp.concatenate([d_bg0, d_bg1], axis=1), d_pool_scale, d_lru_b, d_b_a, d_b_i, d_lam, d_g_mlp,
               d_ffn_b, d_g_final]
    pieces = [g.reshape(-1) for g in small_g] + [d_cw_lru[:KC].reshape(-1), d_cw_ffn[:KF].reshape(-1), lossvec.reshape(-1)]
    sizes = [p.shape[0] for p in pieces]
    total = sum(sizes)
    rows = -(-total // (8 * LANES)) * 8
    pack = jnp.concatenate(pieces + [jnp.zeros((rows * LANES - total,), F32)]).reshape(rows, LANES)
    summed = _allreduce_small(pack).reshape(-1)
    offs = [0]
    for s in sizes:
        offs.append(offs[-1] + s)
    for k, n in enumerate(small):
        grads[n] = summed[offs[k]:offs[k + 1]].reshape(weights[n].shape)
    ns = len(small)
    g_cw_lru = summed[offs[ns]:offs[ns + 1]].reshape(KC, R)
    g_cw_ffn = summed[offs[ns + 1]:offs[ns + 2]].reshape(KF, F)
    grads["lru_conv_w"] = lax.dynamic_slice_in_dim(g_cw_lru, slot * (R // N_SLOTS), R // N_SLOTS, axis=1)[None]
    grads["ffn_conv_w"] = lax.dynamic_slice_in_dim(g_cw_ffn, slot * (F // N_SLOTS), F // N_SLOTS, axis=1)[None]
    loss = jnp.sum(summed[offs[ns + 2]:offs[ns + 3]]) * (0.5 / D)

    delta, new_m, new_v = {}, {}, {}
    for n in big:
        d_, m_, v_ = _adamw(f"adamw_{n}", _rows2d(weights[n][0]), _rows2d(grads[n][0]), _rows2d(mom_m[n][0]),
                            _rows2d(mom_v[n][0]))
        delta[n], new_m[n], new_v[n] = (t.reshape(weights[n].shape) for t in (d_, m_, v_))
    rest = small + ["lru_conv_w", "ffn_conv_w"]
    rsizes = [weights[n].size for n in rest]
    rtotal = sum(rsizes)
    rrows = -(-rtotal // (8 * LANES)) * 8

    def packed(tree):
        flat = [tree[n].reshape(-1) for n in rest] + [jnp.zeros((rrows * LANES - rtotal,), F32)]
        return jnp.concatenate(flat).reshape(rrows, LANES)

    d_, m_, v_ = _adamw("adamw_small", packed(weights), packed(grads), packed(mom_m), packed(mom_v))
    o = 0
    for n, s in zip(rest, rsizes):
        for tree, flat in ((delta, d_), (new_m, m_), (new_v, v_)):
            tree[n] = flat.reshape(-1)[o:o + s].reshape(weights[n].shape)
        o += s

    return (loss, grad_x.reshape(x.shape), *[grads[n] for n in order], *[delta[n] for n in order],
            *[new_m[n] for n in order], *[new_v[n] for n in order])
```

```python
import functools

import jax
import jax.numpy as jnp
from jax import lax
from jax.experimental import pallas as pl
from jax.experimental.pallas import tpu as pltpu

F32 = jnp.float32
BF16 = jnp.bfloat16
MESH = pl.DeviceIdType.MESH

VMEM_LIMIT_V7X = 56 * 1024 * 1024
LANES = 128
HALO = 16
N_SLOTS = 4
ADAMW_BLOCK_BYTES = 3 * 512 * 1024

EPS = 1e-6
LRU_C = 8.0
ADAM_LR = 0.001
ADAM_B1 = 0.9
ADAM_B2 = 0.999
ADAM_EPS = 1e-08
ADAM_WD = 0.01
ADAM_STEP = 10
GELU_C = 0.7978845608028654
GELU_A = 0.044715

SDS = jax.ShapeDtypeStruct
ANY = pl.BlockSpec(memory_space=pl.ANY)


def _pick(n, cands):
    for c in cands:
        if c <= n and n % c == 0:
            return c
    raise ValueError(f"no tile for {n} in {cands}")


def _cparams(**kw):
    return pltpu.CompilerParams(vmem_limit_bytes=VMEM_LIMIT_V7X, **kw)


def _sigmoid(x):
    return 1.0 / (1.0 + jnp.exp(-x))


def _gelu(x):
    return 0.5 * x * (1.0 + jnp.tanh(GELU_C * (x + GELU_A * x * x * x)))


def _gelu_and_grad(x):
    x2 = x * x
    th = jnp.tanh(GELU_C * x * (1.0 + GELU_A * x2))
    g = 0.5 * x * (1.0 + th)
    dg = 0.5 * (1.0 + th) + 0.5 * x * (1.0 - th * th) * (GELU_C * (1.0 + 3.0 * GELU_A * x2))
    return g, dg


def _lookback(ext, k):
    return pltpu.roll(ext, k, 0)[HALO:]


def _lookahead(ext, k, tt):
    return pltpu.roll(ext, ext.shape[0] - k, 0)[:tt]


def _colsum(v):
    return jnp.sum(v, axis=0, keepdims=True)


def _seg3(a):
    return a if a.ndim == 3 else a.reshape((1,) + a.shape)


def _seg_spec(shape3, br, bc, rowf, colf):
    assert shape3[1] % br == 0 and shape3[2] % bc == 0, (shape3, br, bc)
    nb = shape3[2] // bc

    def imap(i, j, k):
        cb = colf(i, j, k)
        return (cb // nb, rowf(i, j, k), cb % nb)

    return pl.BlockSpec((None, br, bc), imap)


_I = lambda i, j, k: i
_J = lambda i, j, k: j
_K = lambda i, j, k: k


def _matmul(name, mode, a, b, *, out_seg=1, out_dtype, tm, tn, tk, res=None):
    a3, b3 = _seg3(a), _seg3(b)
    if mode == "nn":
        M, K, N = a3.shape[1], a3.shape[0] * a3.shape[2], b3.shape[0] * b3.shape[2]
        assert b3.shape[1] == K
        a_spec = _seg_spec(a3.shape, tm, tk, _I, _K)
        b_spec = _seg_spec(b3.shape, tk, tn, _K, _J)
        dims = (((1,), (0,)), ((), ()))
    elif mode == "nt":
        M, K, N = a3.shape[1], a3.shape[0] * a3.shape[2], b3.shape[1]
        assert b3.shape[0] * b3.shape[2] == K
        a_spec = _seg_spec(a3.shape, tm, tk, _I, _K)
        b_spec = _seg_spec(b3.shape, tn, tk, _J, _K)
        dims = (((1,), (1,)), ((), ()))
    else:
        K, M, N = a3.shape[1], a3.shape[0] * a3.shape[2], b3.shape[0] * b3.shape[2]
        assert b3.shape[1] == K
        a_spec = _seg_spec(a3.shape, tk, tm, _K, _I)
        b_spec = _seg_spec(b3.shape, tk, tn, _K, _J)
        dims = (((0,), (0,)), ((), ()))
    out3 = (out_seg, M, N // out_seg)
    o_spec = _seg_spec(out3, tm, tn, _I, _J)
    grid = (M // tm, N // tn, K // tk)
    nk = grid[2]
    has_res = res is not None

    def body(*refs):
        a_ref, b_ref = refs[0], refs[1]
        r_ref = refs[2] if has_res else None
        o_ref = refs[3] if has_res else refs[2]
        prod = lax.dot_general(a_ref[...], b_ref[...], dims, preferred_element_type=F32)

        def finish(val):
            if has_res:
                val = val + r_ref[...]
            o_ref[...] = val.astype(o_ref.dtype)

        if nk == 1:
            finish(prod)
        else:
            acc_ref = refs[-1]
            k = pl.program_id(2)

            @pl.when(k == 0)
            def _():
                acc_ref[...] = prod

            @pl.when(k > 0)
            def _():
                acc_ref[...] += prod

            @pl.when(k == nk - 1)
            def _():
                finish(acc_ref[...])

    in_specs = [a_spec, b_spec]
    args = [a3, b3]
    if has_res:
        in_specs.append(pl.BlockSpec((tm, tn), lambda i, j, k: (i, j)))
        args.append(res)
    return pl.pallas_call(
        body, name=name, grid=grid, in_specs=in_specs, out_specs=o_spec,
        out_shape=SDS(out3, out_dtype),
        scratch_shapes=[pltpu.VMEM((tm, tn), F32)] if nk > 1 else [],
        compiler_params=_cparams(dimension_semantics=("parallel", "parallel", "arbitrary")),
    )(*args)


def _bd_fwd(name, a, ws, tm, scale=None):
    T = a.shape[0]
    H, w, _ = ws[0].shape
    nw = len(ws)
    has_scale = scale is not None

    def body(*refs):
        a_ref = refs[0]
        w_refs = refs[1:1 + nw]
        pos = 1 + nw
        s_ref = refs[pos] if has_scale else None
        o_refs = refs[pos + (1 if has_scale else 0):]
        av = a_ref[...]
        for l in range(nw):
            r = jnp.dot(av, w_refs[l][...], preferred_element_type=F32)
            o_refs[l][...] = r
            if has_scale and l == 0:
                o_refs[nw][...] = (r * s_ref[...]).astype(BF16)

    blk = pl.BlockSpec((tm, w), lambda h, i: (i, h))
    wspec = pl.BlockSpec((None, w, w), lambda h, i: (h, 0, 0))
    in_specs = [blk] + [wspec] * nw
    args = [a] + list(ws)
    out_shape = [SDS((T, H * w), F32)] * nw
    out_specs = [blk] * nw
    if has_scale:
        in_specs.append(pl.BlockSpec((1, w), lambda h, i: (0, h)))
        args.append(scale)
        out_shape = out_shape + [SDS((T, H * w), BF16)]
        out_specs = out_specs + [blk]
    return pl.pallas_call(
        body, name=name, grid=(H, T // tm), in_specs=in_specs, out_specs=out_specs, out_shape=out_shape,
        compiler_params=_cparams(dimension_semantics=("parallel", "parallel")),
    )(*args)


def _bd_bwd_x(name, dys, ws, tm, res=None):
    T = dys[0].shape[0]
    H, w, _ = ws[0].shape
    nw = len(ws)
    has_res = res is not None

    def body(*refs):
        d_refs = refs[:nw]
        w_refs = refs[nw:2 * nw]
        r_ref = refs[2 * nw] if has_res else None
        o_ref = refs[-1]
        acc = None
        for l in range(nw):
            p = lax.dot_general(d_refs[l][...], w_refs[l][...], (((1,), (1,)), ((), ())),
                                preferred_element_type=F32)
            acc = p if acc is None else acc + p
        if has_res:
            acc = acc + r_ref[...]
        o_ref[...] = acc

    blk = pl.BlockSpec((tm, w), lambda h, i: (i, h))
    wspec = pl.BlockSpec((None, w, w), lambda h, i: (h, 0, 0))
    in_specs = [blk] * nw + [wspec] * nw + ([blk] if has_res else [])
    args = list(dys) + list(ws) + ([res] if has_res else [])
    return pl.pallas_call(
        body, name=name, grid=(H, T // tm), in_specs=in_specs, out_specs=blk, out_shape=SDS((T, H * w), F32),
        compiler_params=_cparams(dimension_semantics=("parallel", "parallel")),
    )(*args)


def _bd_bwd_w(name, a, dys, w, tk):
    T, HW = a.shape
    H = HW // w
    nw = len(dys)

    def body(*refs):
        a_ref = refs[0]
        d_refs = refs[1:1 + nw]
        o_refs = refs[1 + nw:]
        k = pl.program_id(1)
        av = a_ref[...]
        for l in range(nw):
            p = lax.dot_general(av, d_refs[l][...], (((0,), (0,)), ((), ())), preferred_element_type=F32)

            @pl.when(k == 0)
            def _(p=p, l=l):
                o_refs[l][...] = p

            @pl.when(k > 0)
            def _(p=p, l=l):
                o_refs[l][...] += p

    blk = pl.BlockSpec((tk, w), lambda h, k: (k, h))
    ospec = pl.BlockSpec((None, w, w), lambda h, k: (h, 0, 0))
    return pl.pallas_call(
        body, name=name, grid=(H, T // tk), in_specs=[blk] * (1 + nw), out_specs=[ospec] * nw,
        out_shape=[SDS((H, w, w), F32)] * nw,
        compiler_params=_cparams(dimension_semantics=("parallel", "arbitrary")),
    )(a, *dys)


def _rms_fwd(name, x, g, tt):
    T, D = x.shape

    def body(x_ref, g_ref, h_ref):
        xv = x_ref[...]
        rstd = lax.rsqrt(jnp.mean(xv * xv, axis=-1, keepdims=True) + EPS)
        h_ref[...] = (xv * rstd * g_ref[...]).astype(BF16)

    row = pl.BlockSpec((tt, D), lambda i: (i, 0))
    vec = pl.BlockSpec((1, D), lambda i: (0, 0))
    return pl.pallas_call(
        body, name=name, grid=(T // tt,), in_specs=[row, vec], out_specs=row, out_shape=SDS((T, D), BF16),
        compiler_params=_cparams(dimension_semantics=("parallel",)),
    )(x, g)


def _rms_bwd(name, dh, x, g, dres, tt, want_bf16):
    T, D = x.shape

    def body(dh_ref, x_ref, g_ref, dres_ref, *outs):
        i = pl.program_id(0)
        dx_ref = outs[0]
        dg_ref = outs[-1]
        xv = x_ref[...]
        rstd = lax.rsqrt(jnp.mean(xv * xv, axis=-1, keepdims=True) + EPS)
        xn = xv * rstd
        dhv = dh_ref[...]
        dyg = dhv * g_ref[...]
        dx = dres_ref[...] + rstd * (dyg - xn * jnp.mean(dyg * xn, axis=-1, keepdims=True))
        dx_ref[...] = dx
        if want_bf16:
            outs[1][...] = dx.astype(BF16)

        @pl.when(i == 0)
        def _():
            dg_ref[...] = jnp.zeros_like(dg_ref)

        dg_ref[...] += _colsum(dhv * xn)

    row = pl.BlockSpec((tt, D), lambda i: (i, 0))
    vec = pl.BlockSpec((1, D), lambda i: (0, 0))
    out_shape = [SDS((T, D), F32)] + ([SDS((T, D), BF16)] if want_bf16 else []) + [SDS((1, D), F32)]
    out_specs = [row] + ([row] if want_bf16 else []) + [vec]
    return pl.pallas_call(
        body, name=name, grid=(T // tt,), in_specs=[row, row, vec, row], out_specs=out_specs, out_shape=out_shape,
        compiler_params=_cparams(dimension_semantics=("arbitrary",)),
    )(dh, x, g, dres)


def _final_loss_bwd(name, x2, tgt, g, tt):
    T, D = x2.shape

    def body(x_ref, t_ref, g_ref, dx_ref, dxb_ref, dg_ref, lv_ref):
        i = pl.program_id(0)
        xv = x_ref[...]
        gv = g_ref[...]
        rstd = lax.rsqrt(jnp.mean(xv * xv, axis=-1, keepdims=True) + EPS)
        xn = xv * rstd
        e = xn * gv - t_ref[...]
        dy = e * (1.0 / D)
        dyg = dy * gv
        dx = rstd * (dyg - xn * jnp.mean(dyg * xn, axis=-1, keepdims=True))
        dx_ref[...] = dx
        dxb_ref[...] = dx.astype(BF16)

        @pl.when(i == 0)
        def _():
            dg_ref[...] = jnp.zeros_like(dg_ref)
            lv_ref[...] = jnp.zeros_like(lv_ref)

        dg_ref[...] += _colsum(dy * xn)
        lv_ref[...] += _colsum(e * e)

    row = pl.BlockSpec((tt, D), lambda i: (i, 0))
    vec = pl.BlockSpec((1, D), lambda i: (0, 0))
    return pl.pallas_call(
        body, name=name, grid=(T // tt,), in_specs=[row, row, vec], out_specs=[row, row, vec, vec],
        out_shape=[SDS((T, D), F32), SDS((T, D), BF16), SDS((1, D), F32), SDS((1, D), F32)],
        compiler_params=_cparams(dimension_semantics=("arbitrary",)),
    )(x2, tgt, g)


def _blk(tt, tc, off=0):
    return pl.BlockSpec((tt, tc), lambda j, i: (i, j + off))


def _vec(r, tc, off=0):
    return pl.BlockSpec((r, tc), lambda j, i: (0, j + off))


def _prev_halo(tt, tc, off=0):
    r = tt // HALO
    return pl.BlockSpec((HALO, tc), lambda j, i: (jnp.maximum(i * r - 1, 0), j + off))


def _next_halo(tt, tc, T, off=0):
    r = tt // HALO
    last = T // HALO - 1
    return pl.BlockSpec((HALO, tc), lambda j, i: (jnp.minimum((i + 1) * r, last), j + off))


def _seq_params():
    return _cparams(dimension_semantics=("parallel", "arbitrary"))


def _pool_fwd(name, proj, Cp, gw, tt):
    T = proj.shape[0]
    G = Cp // gw
    assert G == 4

    def body(u_ref, halo_ref, d_ref):
        j, i = pl.program_id(0), pl.program_id(1)
        u = u_ref[...]
        halo = jnp.where(i > 0, halo_ref[...], 0.0)
        ext = jnp.concatenate([halo, u], axis=0)
        e2 = ext + pltpu.roll(ext, 1, 0)
        e4 = e2 + pltpu.roll(e2, 2, 0)
        e8 = e4 + pltpu.roll(e4, 4, 0)
        e16 = e8 + pltpu.roll(e8, 8, 0)
        s = jnp.where(j == 0, e2, jnp.where(j == 1, e4, jnp.where(j == 2, e8, e16)))[HALO:]
        t1 = (i * tt + 1 + lax.broadcasted_iota(jnp.int32, (tt, 1), 0)).astype(F32)
        win = lax.shift_left(jnp.int32(2), j).astype(F32)
        d_ref[...] = (s / jnp.minimum(t1, win) - u).astype(BF16)

    return pl.pallas_call(
        body, name=name, grid=(G, T // tt), in_specs=[_blk(tt, gw), _prev_halo(tt, gw)], out_specs=_blk(tt, gw),
        out_shape=SDS((T, Cp), BF16), compiler_params=_seq_params(),
    )(proj, proj)


def _pool_bwd_a(name, dy, z, scale, tt):
    T, Cp = dy.shape

    def body(dy_ref, z_ref, s_ref, dz_ref, ds_ref):
        i = pl.program_id(0)
        dyv = dy_ref[...]
        dz_ref[...] = (dyv * s_ref[...]).astype(BF16)

        @pl.when(i == 0)
        def _():
            ds_ref[...] = jnp.zeros_like(ds_ref)

        ds_ref[...] += _colsum(dyv * z_ref[...])

    row = pl.BlockSpec((tt, Cp), lambda i: (i, 0))
    vec = pl.BlockSpec((1, Cp), lambda i: (0, 0))
    return pl.pallas_call(
        body, name=name, grid=(T // tt,), in_specs=[row, row, vec], out_specs=[row, vec],
        out_shape=[SDS((T, Cp), BF16), SDS((1, Cp), F32)],
        compiler_params=_cparams(dimension_semantics=("arbitrary",)),
    )(dy, z, scale)


def _pool_bwd_b(name, dd, gw, tt):
    T, Cp = dd.shape
    G = Cp // gw
    nT = T // tt

    def body(d_ref, nxt_ref, du_ref):
        j, i = pl.program_id(0), pl.program_id(1)
        win = lax.shift_left(jnp.int32(2), j).astype(F32)
        dv = d_ref[...]
        t1 = (i * tt + 1 + lax.broadcasted_iota(jnp.int32, (tt + HALO, 1), 0)).astype(F32)
        nxt = jnp.where(i < nT - 1, nxt_ref[...], 0.0)
        ext = jnp.concatenate([dv, nxt], axis=0) / jnp.minimum(t1, win)
        n = tt + HALO
        f2 = ext + pltpu.roll(ext, n - 1, 0)
        f4 = f2 + pltpu.roll(f2, n - 2, 0)
        f8 = f4 + pltpu.roll(f4, n - 4, 0)
        f16 = f8 + pltpu.roll(f8, n - 8, 0)
        s = jnp.where(j == 0, f2, jnp.where(j == 1, f4, jnp.where(j == 2, f8, f16)))[:tt]
        du_ref[...] = (s - dv).astype(BF16)

    return pl.pallas_call(
        body, name=name, grid=(G, nT), in_specs=[_blk(tt, gw), _next_halo(tt, gw, T)], out_specs=_blk(tt, gw),
        out_shape=SDS((T, Cp), BF16), compiler_params=_seq_params(),
    )(dd, dd)


def _conv_fwd(name, src, off_cols, w, b, tt, tc):
    T = src.shape[0]
    K, C = w.shape
    off = off_cols // tc

    def body(u_ref, halo_ref, w_ref, b_ref, v_ref, vb_ref):
        i = pl.program_id(1)
        u = u_ref[...]
        halo = jnp.where(i > 0, halo_ref[...], 0.0)
        ext = jnp.concatenate([halo, u], axis=0)
        wv = w_ref[...]
        acc = b_ref[...] + u * wv[K - 1:K]
        for s in range(1, K):
            acc = acc + _lookback(ext, s) * wv[K - 1 - s:K - s]
        v_ref[...] = acc
        vb_ref[...] = acc.astype(BF16)

    return pl.pallas_call(
        body, name=name, grid=(C // tc, T // tt),
        in_specs=[_blk(tt, tc, off), _prev_halo(tt, tc, off), _vec(K, tc), _vec(1, tc)],
        out_specs=[_blk(tt, tc), _blk(tt, tc)], out_shape=[SDS((T, C), F32), SDS((T, C), BF16)],
        compiler_params=_seq_params(),
    )(src, src, w, b)


def _conv_bwd(name, dv, src, off_cols, w, tt, tc):
    T, C = dv.shape
    K = w.shape[0]
    off = off_cols // tc
    nT = T // tt

    def body(dv_ref, nxt_ref, u_ref, halo_ref, w_ref, du_ref, dw_ref, db_ref):
        i = pl.program_id(1)
        d = dv_ref[...]
        nxt = jnp.where(i < nT - 1, nxt_ref[...], 0.0)
        dext = jnp.concatenate([d, nxt], axis=0)
        u = u_ref[...]
        halo = jnp.where(i > 0, halo_ref[...], 0.0)
        uext = jnp.concatenate([halo, u], axis=0)
        wv = w_ref[...]
        du = d * wv[K - 1:K]
        rows = [_colsum(d * u)]
        for s in range(1, K):
            du = du + _lookahead(dext, s, tt) * wv[K - 1 - s:K - s]
            rows.append(_colsum(d * _lookback(uext, s)))
        du_ref[...] = du.astype(BF16)
        dw = jnp.concatenate(rows[::-1] + [jnp.zeros((8 - K, tc), F32)], axis=0)

        @pl.when(i == 0)
        def _():
            dw_ref[...] = jnp.zeros_like(dw_ref)
            db_ref[...] = jnp.zeros_like(db_ref)

        dw_ref[...] += dw
        db_ref[...] += _colsum(d)

    return pl.pallas_call(
        body, name=name, grid=(C // tc, nT),
        in_specs=[_blk(tt, tc), _next_halo(tt, tc, T), _blk(tt, tc, off), _prev_halo(tt, tc, off), _vec(K, tc)],
        out_specs=[_blk(tt, tc), _vec(8, tc), _vec(1, tc)],
        out_shape=[SDS((T, C), BF16), SDS((8, C), F32), SDS((1, C), F32)],
        compiler_params=_seq_params(),
    )(dv, dv, src, src, w)


def _lru_gates(ra, ia, v, ba, bi, lam):
    r = _sigmoid(ra + ba)
    ig = _sigmoid(ia + bi)
    nl = -lam
    ex = jnp.exp(-jnp.abs(nl))
    one_p = 1.0 + ex
    l1p = jnp.where(one_p == 1.0, ex, jnp.log(one_p) * ex / (one_p - 1.0))
    sp = jnp.maximum(nl, 0.0) + l1p
    a = jnp.exp(-LRU_C * r * sp)
    mult = jnp.sqrt(1.0 - a * a)
    return r, ig, sp, a, mult


def _lru_fwd(name, ra_pre, ia_pre, v, proj, off_cols, ba, bi, lam, tt, tc):
    T, R = v.shape
    off = off_cols // tc

    def body(ra_ref, ia_ref, v_ref, ug_ref, ba_ref, bi_ref, lam_ref, h_ref, y_ref, carry_ref):
        i = pl.program_id(1)

        @pl.when(i == 0)
        def _():
            carry_ref[...] = jnp.zeros_like(carry_ref)

        vv = v_ref[...]
        _, ig, _, a, mult = _lru_gates(ra_ref[...], ia_ref[...], vv, ba_ref[...], bi_ref[...], lam_ref[...])
        A = a
        B = mult * (ig * vv)
        row = lax.broadcasted_iota(jnp.int32, (tt, 1), 0)
        s = 1
        while s < tt:
            m = row >= s
            B = jnp.where(m, A * pltpu.roll(B, s, 0) + B, B)
            A = jnp.where(m, A * pltpu.roll(A, s, 0), A)
            s *= 2
        h = A * carry_ref[7:8, :] + B
        h_ref[...] = h
        carry_ref[...] = h[tt - 8:, :]
        y_ref[...] = (h * _gelu(ug_ref[...])).astype(BF16)

    return pl.pallas_call(
        body, name=name, grid=(R // tc, T // tt),
        in_specs=[_blk(tt, tc), _blk(tt, tc), _blk(tt, tc), _blk(tt, tc, off), _vec(1, tc), _vec(1, tc), _vec(1, tc)],
        out_specs=[_blk(tt, tc), _blk(tt, tc)], out_shape=[SDS((T, R), F32), SDS((T, R), BF16)],
        scratch_shapes=[pltpu.VMEM((8, tc), F32)], compiler_params=_seq_params(),
    )(ra_pre, ia_pre, v, proj, ba, bi, lam)


def _lru_bwd(name, dy, hs, proj, off_cols, ra_pre, ia_pre, v, ba, bi, lam, tt, tc):
    T, R = v.shape
    off = off_cols // tc
    nT = T // tt
    r16 = tt // HALO

    def rblk(o=0):
        return pl.BlockSpec((tt, tc), lambda j, i: (nT - 1 - i, j + o))

    hprev = pl.BlockSpec((HALO, tc), lambda j, i: (jnp.maximum((nT - 1 - i) * r16 - 1, 0), j))

    def body(dy_ref, h_ref, hp_ref, ug_ref, ra_ref, ia_ref, v_ref, ba_ref, bi_ref, lam_ref,
             dra_ref, dia_ref, dv_ref, dug_ref, dba_ref, dbi_ref, dlam_ref, dh_carry, a_carry):
        i = pl.program_id(1)

        @pl.when(i == 0)
        def _():
            dh_carry[...] = jnp.zeros_like(dh_carry)
            a_carry[...] = jnp.zeros_like(a_carry)
            dba_ref[...] = jnp.zeros_like(dba_ref)
            dbi_ref[...] = jnp.zeros_like(dbi_ref)
            dlam_ref[...] = jnp.zeros_like(dlam_ref)

        vv = v_ref[...]
        lamv = lam_ref[...]
        r, ig, sp, a, mult = _lru_gates(ra_ref[...], ia_ref[...], vv, ba_ref[...], bi_ref[...], lamv)
        h = h_ref[...]
        gel, dgel = _gelu_and_grad(ug_ref[...])
        dyv = dy_ref[...]
        dug_ref[...] = (dyv * h * dgel).astype(BF16)
        dhs = dyv * gel

        a_ext = jnp.concatenate([a, jnp.broadcast_to(a_carry[0:1, :], (HALO, tc))], axis=0)
        A = _lookahead(a_ext, 1, tt)
        B = dhs
        row = lax.broadcasted_iota(jnp.int32, (tt, 1), 0)
        s = 1
        while s < tt:
            m = row < tt - s
            B = jnp.where(m, B + A * pltpu.roll(B, tt - s, 0), B)
            A = jnp.where(m, A * pltpu.roll(A, tt - s, 0), A)
            s *= 2
        dH = B + A * dh_carry[0:1, :]
        dh_carry[...] = dH[:8, :]
        a_carry[...] = a[:8, :]

        hp = jnp.where(i < nT - 1, hp_ref[...], 0.0)
        h_prev = _lookback(jnp.concatenate([hp, h], axis=0), 1)
        da = dH * h_prev
        dmult = dH * (ig * vv)
        div = dH * mult
        dv_ref[...] = div * ig
        dig = div * vv
        dlog_a = da * a - dmult * (a * a) / mult
        dr = dlog_a * (-LRU_C * sp)
        dra = dr * r * (1.0 - r)
        dia = dig * ig * (1.0 - ig)
        dra_ref[...] = dra.astype(BF16)
        dia_ref[...] = dia.astype(BF16)
        dba_ref[...] += _colsum(dra)
        dbi_ref[...] += _colsum(dia)
        dlam_ref[...] += _colsum(dlog_a * (-LRU_C * r))

        @pl.when(i == nT - 1)
        def _():
            dlam_ref[...] = dlam_ref[...] * (-_sigmoid(-lamv))

    return pl.pallas_call(
        body, name=name, grid=(R // tc, nT),
        in_specs=[rblk(), rblk(), hprev, rblk(off), rblk(), rblk(), rblk(), _vec(1, tc), _vec(1, tc), _vec(1, tc)],
        out_specs=[rblk(), rblk(), rblk(), rblk(), _vec(1, tc), _vec(1, tc), _vec(1, tc)],
        out_shape=[SDS((T, R), BF16), SDS((T, R), BF16), SDS((T, R), F32), SDS((T, R), BF16),
                   SDS((1, R), F32), SDS((1, R), F32), SDS((1, R), F32)],
        scratch_shapes=[pltpu.VMEM((8, tc), F32), pltpu.VMEM((8, tc), F32)], compiler_params=_seq_params(),
    )(dy, hs, hs, proj, ra_pre, ia_pre, v, ba, bi, lam)


def _merge_fwd(name, proj, off_cols, b_gate, P, Q, tt, tc):
    T, D = P.shape
    o0 = off_cols // tc
    o1 = o0 + D // tc

    def body(g0_ref, g1_ref, b0_ref, b1_ref, p_ref, q_ref, m_ref):
        g0 = _sigmoid(g0_ref[...] + b0_ref[...])
        g1 = _sigmoid(g1_ref[...] + b1_ref[...])
        m_ref[...] = (g0 * p_ref[...] + g1 * q_ref[...]).astype(BF16)

    return pl.pallas_call(
        body, name=name, grid=(D // tc, T // tt),
        in_specs=[_blk(tt, tc, o0), _blk(tt, tc, o1), _vec(1, tc), _vec(1, tc, D // tc), _blk(tt, tc), _blk(tt, tc)],
        out_specs=_blk(tt, tc), out_shape=SDS((T, D), BF16), compiler_params=_seq_params(),
    )(proj, proj, b_gate, b_gate, P, Q)


def _merge_bwd(name, dm, proj, off_cols, b_gate, P, Q, tt, tc):
    T, D = P.shape
    o0 = off_cols // tc
    o1 = o0 + D // tc

    def body(dm_ref, g0_ref, g1_ref, b0_ref, b1_ref, p_ref, q_ref, dp_ref, dq_ref, dl0_ref, dl1_ref, db0_ref, db1_ref):
        i = pl.program_id(1)
        g0 = _sigmoid(g0_ref[...] + b0_ref[...])
        g1 = _sigmoid(g1_ref[...] + b1_ref[...])
        d = dm_ref[...].astype(F32)
        dp_ref[...] = (d * g0).astype(BF16)
        dq_ref[...] = (d * g1).astype(BF16)
        dl0 = d * p_ref[...] * g0 * (1.0 - g0)
        dl1 = d * q_ref[...] * g1 * (1.0 - g1)
        dl0_ref[...] = dl0.astype(BF16)
        dl1_ref[...] = dl1.astype(BF16)

        @pl.when(i == 0)
        def _():
            db0_ref[...] = jnp.zeros_like(db0_ref)
            db1_ref[...] = jnp.zeros_like(db1_ref)

        db0_ref[...] += _colsum(dl0)
        db1_ref[...] += _colsum(dl1)

    b = _blk(tt, tc)
    return pl.pallas_call(
        body, name=name, grid=(D // tc, T // tt),
        in_specs=[b, _blk(tt, tc, o0), _blk(tt, tc, o1), _vec(1, tc), _vec(1, tc, D // tc), b, b],
        out_specs=[b, b, b, b, _vec(1, tc), _vec(1, tc)],
        out_shape=[SDS((T, D), BF16)] * 4 + [SDS((1, D), F32)] * 2, compiler_params=_seq_params(),
    )(dm, proj, proj, b_gate, b_gate, P, Q)


def _ffn_act_fwd(name, up, w, b, tt, tc):
    T = up.shape[0]
    K, F = w.shape
    nF = F // tc

    def body(gp_ref, halo_ref, val_ref, w_ref, b_ref, z_ref):
        i = pl.program_id(1)
        gp = gp_ref[...].astype(F32)
        halo = jnp.where(i > 0, halo_ref[...].astype(F32), 0.0)
        ext = jnp.concatenate([halo, gp], axis=0)
        wv = w_ref[...]
        c = b_ref[...] + gp * wv[K - 1:K]
        for s in range(1, K):
            c = c + _lookback(ext, s) * wv[K - 1 - s:K - s]
        z_ref[...] = (_gelu(c) * val_ref[...].astype(F32)).astype(BF16)

    return pl.pallas_call(
        body, name=name, grid=(nF, T // tt),
        in_specs=[_blk(tt, tc), _prev_halo(tt, tc), _blk(tt, tc, nF), _vec(K, tc), _vec(1, tc)],
        out_specs=_blk(tt, tc), out_shape=SDS((T, F), BF16), compiler_params=_seq_params(),
    )(up, up, up, w, b)


def _ffn_act_bwd(name, up, dz, w, b, tt, tc):
    T = up.shape[0]
    K, F = w.shape
    nF = F // tc
    nT = T // tt

    def body(gp_ref, gph_ref, gpn_ref, val_ref, valn_ref, dz_ref, dzn_ref, w_ref, b_ref, dup_ref, dw_ref, db_ref):
        i = pl.program_id(1)
        wv = w_ref[...]
        gp = gp_ref[...].astype(F32)
        halo = jnp.where(i > 0, gph_ref[...].astype(F32), 0.0)
        full = jnp.concatenate([halo, gp, gpn_ref[...].astype(F32)], axis=0)
        c = b_ref[...] + full * wv[K - 1:K]
        for s in range(1, K):
            c = c + pltpu.roll(full, s, 0) * wv[K - 1 - s:K - s]
        c = c[HALO:]
        gate, dgate = _gelu_and_grad(c)
        dz_e = jnp.concatenate([dz_ref[...], dzn_ref[...]], axis=0).astype(F32)
        val_e = jnp.concatenate([val_ref[...], valn_ref[...]], axis=0).astype(F32)
        dc_e = dz_e * val_e * dgate
        row = lax.broadcasted_iota(jnp.int32, (tt + HALO, 1), 0)
        dc_e = jnp.where((row < tt) | (i < nT - 1), dc_e, 0.0)
        dc = dc_e[:tt]
        dgp = dc * wv[K - 1:K]
        uext = full[:tt + HALO]
        rows = [_colsum(dc * gp)]
        for s in range(1, K):
            dgp = dgp + _lookahead(dc_e, s, tt) * wv[K - 1 - s:K - s]
            rows.append(_colsum(dc * _lookback(uext, s)))
        dup_ref[0] = dgp.astype(BF16)
        dup_ref[1] = (dz_e[:tt] * gate[:tt]).astype(BF16)
        dw = jnp.concatenate(rows[::-1] + [jnp.zeros((8 - K, tc), F32)], axis=0)

        @pl.when(i == 0)
        def _():
            dw_ref[...] = jnp.zeros_like(dw_ref)
            db_ref[...] = jnp.zeros_like(db_ref)

        dw_ref[...] += dw
        db_ref[...] += _colsum(dc)

    return pl.pallas_call(
        body, name=name, grid=(nF, nT),
        in_specs=[_blk(tt, tc), _prev_halo(tt, tc), _next_halo(tt, tc, T), _blk(tt, tc, nF), _next_halo(tt, tc, T, nF),
                  _blk(tt, tc), _next_halo(tt, tc, T), _vec(K, tc), _vec(1, tc)],
        out_specs=[pl.BlockSpec((2, tt, tc), lambda j, i: (0, i, j)), _vec(8, tc), _vec(1, tc)],
        out_shape=[SDS((2, T, F), BF16), SDS((8, F), F32), SDS((1, F), F32)],
        compiler_params=_seq_params(),
    )(up, up, up, up, up, dz, dz, w, b)


def _adamw(name, w, g, m, v):
    R, C = w.shape
    tr = R
    if R * C * 4 > ADAMW_BLOCK_BYTES:
        tr = _pick(R, [t for t in (256, 128, 64, 32, 16, 8) if t * C * 4 <= ADAMW_BLOCK_BYTES])
    c1 = 1.0 - ADAM_B1 ** ADAM_STEP
    c2 = 1.0 - ADAM_B2 ** ADAM_STEP

    def body(w_ref, g_ref, m_ref, v_ref, d_ref, nm_ref, nv_ref):
        gv = g_ref[...]
        nm = ADAM_B1 * m_ref[...] + (1.0 - ADAM_B1) * gv
        nv = ADAM_B2 * v_ref[...] + (1.0 - ADAM_B2) * (gv * gv)
        nm_ref[...] = nm
        nv_ref[...] = nv
        d_ref[...] = -ADAM_LR * ((nm / c1) / (jnp.sqrt(nv / c2) + ADAM_EPS) + ADAM_WD * w_ref[...])

    spec = pl.BlockSpec((tr, C), lambda i: (i, 0))
    return pl.pallas_call(
        body, name=name, grid=(R // tr,), in_specs=[spec] * 4, out_specs=[spec] * 3,
        out_shape=[SDS((R, C), F32)] * 3, compiler_params=_cparams(dimension_semantics=("parallel",)),
    )(w, g, m, v)


def _add_halves(name, part, rcv, pos):
    S, R, C = part.shape
    h = R // 2
    tr = _pick(h, (512, 256, 128, 64, 32, 16))
    nb = h // tr

    def body(pos_ref, p_ref, r_ref, o_ref):
        o_ref[...] = (p_ref[...].astype(F32) + r_ref[...].astype(F32)).astype(BF16)

    return pl.pallas_call(
        body, name=name,
        grid_spec=pltpu.PrefetchScalarGridSpec(
            num_scalar_prefetch=1, grid=(S, nb),
            in_specs=[pl.BlockSpec((None, tr, C), lambda s, r, pos: (s, pos[0] * nb + r, 0)),
                      pl.BlockSpec((None, tr, C), lambda s, r, pos: (s, r, 0))],
            out_specs=pl.BlockSpec((None, tr, C), lambda s, r, pos: (s, r, 0))),
        out_shape=SDS((S, h, C), BF16), compiler_params=_cparams(dimension_semantics=("parallel", "parallel")),
    )(pos, part, rcv)


def _add_chips(name, chipsum, rcv, pos):
    S, h, C = chipsum.shape
    tr = _pick(h, (512, 256, 128, 64, 32, 16))

    def body(pos_ref, own_ref, r_ref, o_ref):
        o_ref[...] = ((own_ref[...].astype(F32) + r_ref[0].astype(F32)) + r_ref[1].astype(F32)) + r_ref[2].astype(F32)

    return pl.pallas_call(
        body, name=name,
        grid_spec=pltpu.PrefetchScalarGridSpec(
            num_scalar_prefetch=1, grid=(h // tr,),
            in_specs=[pl.BlockSpec((None, tr, C), lambda r, pos: (pos[1], r, 0)),
                      pl.BlockSpec((3, tr, C), lambda r, pos: (0, r, 0))],
            out_specs=pl.BlockSpec((None, tr, C), lambda r, pos: (pos[0], r, 0))),
        out_shape=SDS((2, h, C), F32), compiler_params=_cparams(dimension_semantics=("parallel",)),
    )(pos, chipsum, rcv)


def _place():
    x, y, c = lax.axis_index("x"), lax.axis_index("y"), lax.axis_index("c")
    chips = [(1 - x, y), (x, 1 - y), (1 - x, 1 - y)]
    slots = [2 * cx + cy for cx, cy in chips]
    return x, y, c, chips, slots


def _gather_weights(shards, split):
    n = len(shards)

    def body(*refs):
        ins, outs = refs[:n], refs[n:2 * n]
        send_sems, recv_sems, local_sems = refs[2 * n:]
        x, y, c, chips, slots = _place()
        me = 2 * x + y
        sibling = (x, y, 1 - c)

        def rows(a, half):
            hrows = ins[a].shape[0] // 2
            return pl.ds(half * hrows, hrows)

        def ici(a, j):
            if split[a]:
                src, dst = ins[a].at[rows(a, c)], outs[a].at[me, rows(a, c)]
            else:
                src, dst = ins[a], outs[a].at[me]
            return pltpu.make_async_remote_copy(src_ref=src, dst_ref=dst, send_sem=send_sems.at[a, j],
                                                recv_sem=recv_sems.at[a, j], device_id=(*chips[j], c), device_id_type=MESH)

        def landed(a, j):
            dst = outs[a].at[slots[j], rows(a, c)] if split[a] else outs[a].at[slots[j]]
            return pltpu.make_async_remote_copy(src_ref=dst, dst_ref=dst, send_sem=send_sems.at[a, j],
                                                recv_sem=recv_sems.at[a, j], device_id=(*chips[j], c), device_id_type=MESH)

        def d2d(a, j, half):
            blk = outs[a].at[slots[j], rows(a, half)]
            return pltpu.make_async_remote_copy(src_ref=blk, dst_ref=blk, send_sem=send_sems.at[a, 3 + j],
                                                recv_sem=recv_sems.at[a, 3 + j], device_id=sibling, device_id_type=MESH)

        local = [pltpu.make_async_copy(ins[a], outs[a].at[me], local_sems.at[a]) for a in range(n)]
        for cp in local:
            cp.start()
        for a in range(n):
            for j in range(3):
                ici(a, j).start()
        for a in range(n):
            for j in range(3):
                landed(a, j).wait_recv()
                if split[a]:
                    d2d(a, j, c).start()
        for a in range(n):
            if split[a]:
                for j in range(3):
                    d2d(a, j, 1 - c).wait_recv()
        for a in range(n):
            for j in range(3):
                ici(a, j).wait_send()
                if split[a]:
                    d2d(a, j, c).wait_send()
        for cp in local:
            cp.wait()

    return pl.pallas_call(
        body, name="gather_weights", in_specs=[ANY] * n, out_specs=[ANY] * n,
        out_shape=[SDS((N_SLOTS,) + s.shape, s.dtype) for s in shards],
        scratch_shapes=[pltpu.SemaphoreType.DMA((n, 6)), pltpu.SemaphoreType.DMA((n, 6)), pltpu.SemaphoreType.DMA((n,))],
        compiler_params=_cparams(has_side_effects=True),
    )(*shards)


def _swap_partial_halves(parts):
    n = len(parts)

    def body(*refs):
        ins, outs = refs[:n], refs[n:2 * n]
        send_sems, recv_sems = refs[2 * n:]
        x, y, c, _, _ = _place()
        sibling = (x, y, 1 - c)
        cps = []
        for a in range(n):
            h = ins[a].shape[1] // 2
            cps.append(pltpu.make_async_remote_copy(
                src_ref=ins[a].at[:, pl.ds((1 - c) * h, h)], dst_ref=outs[a], send_sem=send_sems.at[a],
                recv_sem=recv_sems.at[a], device_id=sibling, device_id_type=MESH))
        for cp in cps:
            cp.start()
        for cp in cps:
            cp.wait()

    return pl.pallas_call(
        body, name="swap_partial_halves", in_specs=[ANY] * n, out_specs=[ANY] * n,
        out_shape=[SDS((p.shape[0], p.shape[1] // 2, p.shape[2]), p.dtype) for p in parts],
        scratch_shapes=[pltpu.SemaphoreType.DMA((n,)), pltpu.SemaphoreType.DMA((n,))],
        compiler_params=_cparams(has_side_effects=True),
    )(*parts)


def _exchange_chip_sums(sums):
    n = len(sums)

    def body(*refs):
        ins, outs = refs[:n], refs[n:2 * n]
        send_sems, recv_sems = refs[2 * n:]
        x, y, c, chips, slots = _place()
        cps = []
        for a in range(n):
            for j in range(3):
                cps.append(pltpu.make_async_remote_copy(
                    src_ref=ins[a].at[slots[j]], dst_ref=outs[a].at[j], send_sem=send_sems.at[a, j],
                    recv_sem=recv_sems.at[a, j], device_id=(*chips[j], c), device_id_type=MESH))
        for cp in cps:
            cp.start()
        for cp in cps:
            cp.wait()

    return pl.pallas_call(
        body, name="exchange_chip_sums", in_specs=[ANY] * n, out_specs=[ANY] * n,
        out_shape=[SDS((3,) + s.shape[1:], s.dtype) for s in sums],
        scratch_shapes=[pltpu.SemaphoreType.DMA((n, 3)), pltpu.SemaphoreType.DMA((n, 3))],
        compiler_params=_cparams(has_side_effects=True),
    )(*sums)


def _swap_reduced_halves(totals):
    n = len(totals)

    def body(*refs):
        ins, outs = refs[:n], refs[n:2 * n]
        send_sems, recv_sems = refs[2 * n:]
        x, y, c, _, _ = _place()
        sibling = (x, y, 1 - c)
        sends = [pltpu.make_async_remote_copy(src_ref=outs[a].at[c], dst_ref=outs[a].at[c], send_sem=send_sems.at[a],
                                              recv_sem=recv_sems.at[a], device_id=sibling, device_id_type=MESH)
                 for a in range(n)]
        for cp in sends:
            cp.start()
        for a in range(n):
            got = outs[a].at[1 - c]
            pltpu.make_async_remote_copy(src_ref=got, dst_ref=got, send_sem=send_sems.at[a], recv_sem=recv_sems.at[a],
                                         device_id=sibling, device_id_type=MESH).wait_recv()
        for cp in sends:
            cp.wait_send()

    return pl.pallas_call(
        body, name="swap_reduced_halves", in_specs=[ANY] * n, out_specs=[ANY] * n,
        out_shape=[SDS(t.shape, t.dtype) for t in totals], input_output_aliases={a: a for a in range(n)},
        scratch_shapes=[pltpu.SemaphoreType.DMA((n,)), pltpu.SemaphoreType.DMA((n,))],
        compiler_params=_cparams(has_side_effects=True),
    )(*totals)


def _allreduce_small(pack):
    rows = pack.shape[0]

    def body(in_ref, out_ref, buf, send_sems, recv_sems):
        x, y, c = lax.axis_index("x"), lax.axis_index("y"), lax.axis_index("c")
        me = 4 * x + 2 * y + c
        buf[me] = in_ref[...]
        cps = []
        for k in range(1, 8):
            kx, ky, kc = (k >> 2) & 1, (k >> 1) & 1, k & 1
            peer = (x ^ kx, y ^ ky, c ^ kc)
            cps.append(pltpu.make_async_remote_copy(src_ref=in_ref, dst_ref=buf.at[me], send_sem=send_sems.at[k - 1],
                                                    recv_sem=recv_sems.at[k - 1], device_id=peer, device_id_type=MESH))
        for cp in cps:
            cp.start()
        for k in range(1, 8):
            got = buf.at[me ^ k]
            pltpu.make_async_remote_copy(src_ref=got, dst_ref=got, send_sem=send_sems.at[k - 1],
                                         recv_sem=recv_sems.at[k - 1], device_id=(x, y, c), device_id_type=MESH).wait_recv()
        for cp in cps:
            cp.wait_send()
        acc = buf[0]
        for d in range(1, 8):
            acc = acc + buf[d]
        out_ref[...] = acc

    vm = pl.BlockSpec(memory_space=pltpu.VMEM)
    return pl.pallas_call(
        body, name="allreduce_small", in_specs=[vm], out_specs=vm, out_shape=SDS(pack.shape, F32),
        scratch_shapes=[pltpu.VMEM((8, rows, LANES), F32), pltpu.SemaphoreType.DMA((7,)), pltpu.SemaphoreType.DMA((7,))],
        compiler_params=_cparams(has_side_effects=True),
    )(pack)


def _rows2d(a):
    return a.reshape((-1, a.shape[-1]))


def _from_slots_blockdiag(g, H, w):
    q = w // N_SLOTS
    return g.reshape(N_SLOTS, H, q, w).transpose(1, 0, 2, 3).reshape(H, w, w)


def _to_slots_blockdiag(d, H, w):
    q = w // N_SLOTS
    return d.reshape(H, N_SLOTS, q, w).transpose(1, 0, 2, 3).reshape(N_SLOTS, H * q, w)


def kernel(x, g_mix, w_in, b_gate, w_pool, pool_scale, lru_conv_w, lru_conv_b, w_a, b_a, w_i, b_i, lru_lambda, w_pool_proj, w_lru_proj, w_out, g_mlp, w_up, ffn_conv_w, ffn_conv_b, w_down, g_final, loss_target, m_g_mix, m_w_in, m_b_gate, m_w_pool, m_pool_scale, m_lru_conv_w, m_lru_conv_b, m_w_a, m_b_a, m_w_i, m_b_i, m_lru_lambda, m_w_pool_proj, m_w_lru_proj, m_w_out, m_g_mlp, m_w_up, m_ffn_conv_w, m_ffn_conv_b, m_w_down, m_g_final, v_g_mix, v_w_in, v_b_gate, v_w_pool, v_pool_scale, v_lru_conv_w, v_lru_conv_b, v_w_a, v_b_a, v_w_i, v_b_i, v_lru_lambda, v_w_pool_proj, v_w_lru_proj, v_w_out, v_g_mlp, v_w_up, v_ffn_conv_w, v_ffn_conv_b, v_w_down, v_g_final):
    weights = dict(g_mix=g_mix, w_in=w_in, b_gate=b_gate, w_pool=w_pool, pool_scale=pool_scale, lru_conv_w=lru_conv_w,
                   lru_conv_b=lru_conv_b, w_a=w_a, b_a=b_a, w_i=w_i, b_i=b_i, lru_lambda=lru_lambda,
                   w_pool_proj=w_pool_proj, w_lru_proj=w_lru_proj, w_out=w_out, g_mlp=g_mlp, w_up=w_up,
                   ffn_conv_w=ffn_conv_w, ffn_conv_b=ffn_conv_b, w_down=w_down, g_final=g_final)
    mom_m = dict(g_mix=m_g_mix, w_in=m_w_in, b_gate=m_b_gate, w_pool=m_w_pool, pool_scale=m_pool_scale,
                 lru_conv_w=m_lru_conv_w, lru_conv_b=m_lru_conv_b, w_a=m_w_a, b_a=m_b_a, w_i=m_w_i, b_i=m_b_i,
                 lru_lambda=m_lru_lambda, w_pool_proj=m_w_pool_proj, w_lru_proj=m_w_lru_proj, w_out=m_w_out,
                 g_mlp=m_g_mlp, w_up=m_w_up, ffn_conv_w=m_ffn_conv_w, ffn_conv_b=m_ffn_conv_b, w_down=m_w_down,
                 g_final=m_g_final)
    mom_v = dict(g_mix=v_g_mix, w_in=v_w_in, b_gate=v_b_gate, w_pool=v_w_pool, pool_scale=v_pool_scale,
                 lru_conv_w=v_lru_conv_w, lru_conv_b=v_lru_conv_b, w_a=v_w_a, b_a=v_b_a, w_i=v_w_i, b_i=v_b_i,
                 lru_lambda=v_lru_lambda, w_pool_proj=v_w_pool_proj, w_lru_proj=v_w_lru_proj, w_out=v_w_out,
                 g_mlp=v_g_mlp, w_up=v_w_up, ffn_conv_w=v_ffn_conv_w, ffn_conv_b=v_ffn_conv_b, w_down=v_w_down,
                 g_final=v_g_final)
    order = list(weights)

    T, D = x.shape[1], x.shape[2]
    Cp = pool_scale.shape[1]
    G, gw = w_pool.shape[1], w_pool.shape[3]
    H, hw = w_a.shape[1], w_a.shape[3]
    R = b_a.shape[1]
    F = ffn_conv_b.shape[1]
    KC, KF = lru_conv_w.shape[1], ffn_conv_w.shape[1]
    p0, p1, p2 = Cp, Cp + R, Cp + 2 * R
    xs = x.reshape(T, D)
    tgt = loss_target.reshape(T, D)

    my_x, my_y, my_c = lax.axis_index("x"), lax.axis_index("y"), lax.axis_index("c")
    slot = 2 * my_x + my_y
    pos = jnp.stack([my_c, slot]).astype(jnp.int32)

    big = ["w_in", "w_pool", "w_a", "w_i", "w_pool_proj", "w_lru_proj", "w_out", "w_up", "w_down"]
    shards = [_rows2d(weights[n][0]).astype(BF16) for n in big] + [lru_conv_w[0], ffn_conv_w[0]]
    gathered = _gather_weights(shards, [True] * len(big) + [False, False])
    gw_ = dict(zip(big + ["lru_conv_w", "ffn_conv_w"], gathered))
    W_in = gw_["w_in"]
    W_pool = _from_slots_blockdiag(gw_["w_pool"], G, gw)
    W_a = _from_slots_blockdiag(gw_["w_a"], H, hw)
    W_i = _from_slots_blockdiag(gw_["w_i"], H, hw)
    W_pp = gw_["w_pool_proj"]
    W_lp = gw_["w_lru_proj"].reshape(R, D)
    W_out = gw_["w_out"].reshape(D, D)
    W_up = gw_["w_up"]
    W_down = gw_["w_down"].reshape(F, D)
    cw_lru = gw_["lru_conv_w"].transpose(1, 0, 2).reshape(KC, R)
    cw_ffn = gw_["ffn_conv_w"].transpose(1, 0, 2).reshape(KF, F)

    tt_row = _pick(T, (256, 128))
    tt_seq = _pick(T, (512, 256, 128))
    tt_pool = _pick(T, (1024, 512, 256, 128))
    tc_seq = _pick(R, (512, 256, 128))
    tm1 = _pick(T, (1024, 512, 256))
    tm2 = _pick(T, (2048, 1024, 512, 256))
    tkT = _pick(T, (1024, 512, 256))
    tkT2 = _pick(T, (4096, 2048, 1024, 512, 256))
    n_in, n_up, n_pp = W_in.shape[2], W_up.shape[2], W_pp.shape[2]
    tn_in = _pick(n_in, (768, 1152, 384, 128))
    tn_up = _pick(n_up, (1536, 1024, 768, 512, 128))
    tk_up = _pick(n_up, (1024, 768, 512, 128))
    tn_pp = _pick(n_pp, (512, 256, 128))
    tD1 = _pick(D, (1024, 512))
    tD2 = _pick(D, (2048, 1024, 512))
    tF = _pick(F, (1536, 1024, 512))
    tkF = _pick(F, (2048, 1536, 1024, 512))

    h = _rms_fwd("rms1_fwd", xs, g_mix, tt_row)
    proj = _matmul("proj_fwd", "nn", h, W_in, out_dtype=F32, tm=tm2, tn=tn_in, tk=D)[0]
    d_pool = _pool_fwd("pool_fwd", proj, Cp, gw, tt_pool)
    z_pool, y_pool = _bd_fwd("pool_mix_fwd", d_pool, [W_pool], tm1, scale=pool_scale)
    v_f, v_b = _conv_fwd("lru_conv_fwd", proj, p0, cw_lru, lru_conv_b, tt_seq, tc_seq)
    ra_pre, ia_pre = _bd_fwd("lru_gate_fwd", v_b, [W_a, W_i], tm1)
    hs, y_lru = _lru_fwd("lru_scan_fwd", ra_pre, ia_pre, v_f, proj, p1, b_a, b_i, lru_lambda, tt_seq, tc_seq)
    P = _matmul("pool_proj_fwd", "nn", y_pool, W_pp, out_dtype=F32, tm=tm2, tn=tn_pp, tk=Cp)[0]
    Q = _matmul("lru_proj_fwd", "nn", y_lru, W_lp, out_dtype=F32, tm=tm1, tn=tD1, tk=R)[0]
    merged = _merge_fwd("merge_fwd", proj, p2, b_gate, P, Q, tt_seq, tc_seq)
    x1 = _matmul("out_fwd", "nn", merged, W_out, out_dtype=F32, tm=tm1, tn=tD1, tk=D, res=xs)[0]
    h2 = _rms_fwd("rms2_fwd", x1, g_mlp, tt_row)
    up = _matmul("up_fwd", "nn", h2, W_up, out_dtype=BF16, tm=tm1, tn=tn_up, tk=D)[0]
    z = _ffn_act_fwd("ffn_act_fwd", up, cw_ffn, ffn_conv_b, tt_seq, tc_seq)
    x2 = _matmul("down_fwd", "nn", z, W_down, out_dtype=F32, tm=tm1, tn=tD1, tk=tkF, res=x1)[0]

    dx2, dx2b, d_g_final, lossvec = _final_loss_bwd("final_loss_bwd", x2, tgt, g_final.reshape(1, D), tt_row)
    dz = _matmul("down_bwd_x", "nt", dx2b, W_down, out_dtype=BF16, tm=tm1, tn=tF, tk=D)[0]
    dW_down = _matmul("down_bwd_w", "tn", z, dx2b, out_dtype=BF16, tm=_pick(F, (1024, 512)), tn=tD1, tk=tkT2)
    dup, d_cw_ffn, d_ffn_b = _ffn_act_bwd("ffn_act_bwd", up, dz, cw_ffn, ffn_conv_b, tt_seq, tc_seq)
    dh2 = _matmul("up_bwd_x", "nt", dup, W_up, out_dtype=F32, tm=tm1, tn=tD1, tk=n_up)[0]
    dW_up = _matmul("up_bwd_w", "tn", h2, dup, out_seg=N_SLOTS, out_dtype=BF16, tm=tD1, tn=tk_up, tk=tkT2)
    dx1, dx1b, d_g_mlp = _rms_bwd("rms2_bwd", dh2, x1, g_mlp, dx2, tt_row, True)
    dmerged = _matmul("out_bwd_x", "nt", dx1b, W_out, out_dtype=BF16, tm=tm1, tn=tD2, tk=D)[0]
    dW_out = _matmul("out_bwd_w", "tn", merged, dx1b, out_dtype=BF16, tm=tD1, tn=tD1, tk=tkT2)
    dP, dQ, dgl0, dgl1, d_bg0, d_bg1 = _merge_bwd("merge_bwd", dmerged, proj, p2, b_gate, P, Q, tt_seq, tc_seq)
    dy_pool = _matmul("pool_proj_bwd_x", "nt", dP, W_pp, out_dtype=F32, tm=tm2, tn=Cp, tk=tn_pp)[0]
    dW_pp = _matmul("pool_proj_bwd_w", "tn", y_pool, dP, out_seg=N_SLOTS, out_dtype=BF16, tm=Cp, tn=tn_pp,
                    tk=_pick(T, (2048, 1024, 512, 256)))
    dy_lru = _matmul("lru_proj_bwd_x", "nt", dQ, W_lp, out_dtype=F32, tm=tm1, tn=tD1, tk=D)[0]
    dW_lp = _matmul("lru_proj_bwd_w", "tn", y_lru, dQ, out_dtype=BF16, tm=_pick(R, (1024, 512)), tn=tD1, tk=tkT2)
    dra, dia, dv1, du_gelu, d_b_a, d_b_i, d_lam = _lru_bwd(
        "lru_scan_bwd", dy_lru, hs, proj, p1, ra_pre, ia_pre, v_f, b_a, b_i, lru_lambda, tt_seq, tc_seq)
    dv = _bd_bwd_x("lru_gate_bwd_x", [dra, dia], [W_a, W_i], tm1, res=dv1)
    dW_a, dW_i = _bd_bwd_w("lru_gate_bwd_w", v_b, [dra, dia], hw, tkT)
    du_lru, d_cw_lru, d_lru_b = _conv_bwd("lru_conv_bwd", dv, proj, p0, cw_lru, tt_seq, tc_seq)
    dzp, d_pool_scale = _pool_bwd_a("pool_scale_bwd", dy_pool, z_pool, pool_scale, tt_row)
    dd = _bd_bwd_x("pool_mix_bwd_x", [dzp], [W_pool], tm1)
    (dW_pool,) = _bd_bwd_w("pool_mix_bwd_w", d_pool, [dzp], gw, tkT)
    du_pool = _pool_bwd_b("pool_bwd", dd, gw, tt_pool)
    dproj = jnp.concatenate([du_pool, du_lru, du_gelu, dgl0, dgl1], axis=1)
    dh = _matmul("proj_bwd_x", "nt", dproj, W_in, out_dtype=F32, tm=tm1, tn=tD1, tk=n_in)[0]
    dW_in = _matmul("proj_bwd_w", "tn", h, dproj, out_seg=N_SLOTS, out_dtype=BF16, tm=tD1, tn=tn_in, tk=tkT2)
    grad_x, d_g_mix = _rms_bwd("rms1_bwd", dh, xs, g_mix, dx1, tt_row, False)

    q_rows = lambda a, rows: a.reshape(N_SLOTS, rows // N_SLOTS, a.shape[-1])
    parts = [dW_in,
             _to_slots_blockdiag(dW_pool.astype(BF16), G, gw),
             _to_slots_blockdiag(dW_a.astype(BF16), H, hw),
             _to_slots_blockdiag(dW_i.astype(BF16), H, hw),
             dW_pp, q_rows(dW_lp, R), q_rows(dW_out, D), dW_up, q_rows(dW_down, F)]
    from_sibling = _swap_partial_halves(parts)
    chip_sums = [_add_halves(f"add_halves_{n}", p, r, pos) for n, p, r in zip(big, parts, from_sibling)]
    from_chips = _exchange_chip_sums(chip_sums)
    totals = [_add_chips(f"add_chips_{n}", s, r, pos) for n, s, r in zip(big, chip_sums, from_chips)]
    full = _swap_reduced_halves(totals)
    grads = {n: f.reshape(weights[n].shape) for n, f in zip(big, full)}

    small = ["g_mix", "b_gate", "pool_scale", "lru_conv_b", "b_a", "b_i", "lru_lambda", "g_mlp", "ffn_conv_b", "g_final"]
    small_g = [d_g_mix, jnp.concatenate([d_bg0, d_bg1], axis=1), d_pool_scale, d_lru_b, d_b_a, d_b_i, d_lam, d_g_mlp,
               d_ffn_b, d_g_final]
    pieces = [g.reshape(-1) for g in small_g] + [d_cw_lru[:KC].reshape(-1), d_cw_ffn[:KF].reshape(-1), lossvec.reshape(-1)]
    sizes = [p.shape[0] for p in pieces]
    total = sum(sizes)
    rows = -(-total // (8 * LANES)) * 8
    pack = jnp.concatenate(pieces + [jnp.zeros((rows * LANES - total,), F32)]).reshape(rows, LANES)
    summed = _allreduce_small(pack).reshape(-1)
    offs = [0]
    for s in sizes:
        offs.append(offs[-1] + s)
    for k, n in enumerate(small):
        grads[n] = summed[offs[k]:offs[k + 1]].reshape(weights[n].shape)
    ns = len(small)
    g_cw_lru = summed[offs[ns]:offs[ns + 1]].reshape(KC, R)
    g_cw_ffn = summed[offs[ns + 1]:offs[ns + 2]].reshape(KF, F)
    grads["lru_conv_w"] = lax.dynamic_slice_in_dim(g_cw_lru, slot * (R // N_SLOTS), R // N_SLOTS, axis=1)[None]
    grads["ffn_conv_w"] = lax.dynamic_slice_in_dim(g_cw_ffn, slot * (F // N_SLOTS), F // N_SLOTS, axis=1)[None]
    loss = jnp.sum(summed[offs[ns + 2]:offs[ns + 3]]) * (0.5 / D)

    delta, new_m, new_v = {}, {}, {}
    for n in big:
        d_, m_, v_ = _adamw(f"adamw_{n}", _rows2d(weights[n][0]), _rows2d(grads[n][0]), _rows2d(mom_m[n][0]),
                            _rows2d(mom_v[n][0]))
        delta[n], new_m[n], new_v[n] = (t.reshape(weights[n].shape) for t in (d_, m_, v_))
    rest = small + ["lru_conv_w", "ffn_conv_w"]
    rsizes = [weights[n].size for n in rest]
    rtotal = sum(rsizes)
    rrows = -(-rtotal // (8 * LANES)) * 8

    def packed(tree):
        flat = [tree[n].reshape(-1) for n in rest] + [jnp.zeros((rrows * LANES - rtotal,), F32)]
        return jnp.concatenate(flat).reshape(rrows, LANES)

    d_, m_, v_ = _adamw("adamw_small", packed(weights), packed(grads), packed(mom_m), packed(mom_v))
    o = 0
    for n, s in zip(rest, rsizes):
        for tree, flat in ((delta, d_), (new_m, m_), (new_v, v_)):
            tree[n] = flat.reshape(-1)[o:o + s].reshape(weights[n].shape)
        o += s

    return (loss, grad_x.reshape(x.shape), *[grads[n] for n in order], *[delta[n] for n in order],
            *[new_m[n] for n in order], *[new_v[n] for n in order])
```

```python
import functools

import jax
import jax.numpy as jnp
from jax import lax
from jax.experimental import pallas as pl
from jax.experimental.pallas import tpu as pltpu

F32 = jnp.float32
BF16 = jnp.bfloat16
MESH = pl.DeviceIdType.MESH

VMEM_LIMIT_V7X = 56 * 1024 * 1024
LANES = 128
HALO = 16
N_SLOTS = 4
ADAMW_BLOCK_BYTES = 3 * 512 * 1024

EPS = 1e-6
LRU_C = 8.0
ADAM_LR = 0.001
ADAM_B1 = 0.9
ADAM_B2 = 0.999
ADAM_EPS = 1e-08
ADAM_WD = 0.01
ADAM_STEP = 10
GELU_C = 0.7978845608028654
GELU_A = 0.044715

SDS = jax.ShapeDtypeStruct
ANY = pl.BlockSpec(memory_space=pl.ANY)


def _pick(n, cands):
    for c in cands:
        if c <= n and n % c == 0:
            return c
    raise ValueError(f"no tile for {n} in {cands}")


def _cparams(**kw):
    return pltpu.CompilerParams(vmem_limit_bytes=VMEM_LIMIT_V7X, **kw)


def _sigmoid(x):
    return 1.0 / (1.0 + jnp.exp(-x))


def _gelu(x):
    return 0.5 * x * (1.0 + jnp.tanh(GELU_C * (x + GELU_A * x * x * x)))


def _gelu_and_grad(x):
    x2 = x * x
    th = jnp.tanh(GELU_C * x * (1.0 + GELU_A * x2))
    g = 0.5 * x * (1.0 + th)
    dg = 0.5 * (1.0 + th) + 0.5 * x * (1.0 - th * th) * (GELU_C * (1.0 + 3.0 * GELU_A * x2))
    return g, dg


def _lookback(ext, k):
    return pltpu.roll(ext, k, 0)[HALO:]


def _lookahead(ext, k, tt):
    return pltpu.roll(ext, ext.shape[0] - k, 0)[:tt]


def _colsum(v):
    return jnp.sum(v, axis=0, keepdims=True)


def _seg3(a):
    return a if a.ndim == 3 else a.reshape((1,) + a.shape)


def _seg_spec(shape3, br, bc, rowf, colf):
    assert shape3[1] % br == 0 and shape3[2] % bc == 0, (shape3, br, bc)
    nb = shape3[2] // bc

    def imap(i, j, k):
        cb = colf(i, j, k)
        return (cb // nb, rowf(i, j, k), cb % nb)

    return pl.BlockSpec((None, br, bc), imap)


_I = lambda i, j, k: i
_J = lambda i, j, k: j
_K = lambda i, j, k: k


class _Side:
    def __init__(self, ins, out_shapes, sem_shapes, start, mid, finish):
        self.ins, self.out_shapes, self.sem_shapes = list(ins), list(out_shapes), list(sem_shapes)
        self.start, self.mid, self.finish = start, mid, finish


def _matmul(name, mode, a, b, *, out_seg=1, out_dtype, tm, tn, tk, res=None, side=None):
    a3, b3 = _seg3(a), _seg3(b)
    if mode == "nn":
        M, K, N = a3.shape[1], a3.shape[0] * a3.shape[2], b3.shape[0] * b3.shape[2]
        assert b3.shape[1] == K
        a_spec = _seg_spec(a3.shape, tm, tk, _I, _K)
        b_spec = _seg_spec(b3.shape, tk, tn, _K, _J)
        dims = (((1,), (0,)), ((), ()))
    elif mode == "nt":
        M, K, N = a3.shape[1], a3.shape[0] * a3.shape[2], b3.shape[1]
        assert b3.shape[0] * b3.shape[2] == K
        a_spec = _seg_spec(a3.shape, tm, tk, _I, _K)
        b_spec = _seg_spec(b3.shape, tn, tk, _J, _K)
        dims = (((1,), (1,)), ((), ()))
    else:
        K, M, N = a3.shape[1], a3.shape[0] * a3.shape[2], b3.shape[0] * b3.shape[2]
        assert b3.shape[1] == K
        a_spec = _seg_spec(a3.shape, tk, tm, _K, _I)
        b_spec = _seg_spec(b3.shape, tk, tn, _K, _J)
        dims = (((0,), (0,)), ((), ()))
    out3 = (out_seg, M, N // out_seg)
    o_spec = _seg_spec(out3, tm, tn, _I, _J)
    grid = (M // tm, N // tn, K // tk)
    nk = grid[2]
    n_steps = grid[0] * grid[1] * grid[2]
    has_res = res is not None
    n_sin = len(side.ins) if side else 0
    n_sout = len(side.out_shapes) if side else 0
    n_ssem = len(side.sem_shapes) if side else 0

    def body(*refs):
        a_ref, b_ref = refs[0], refs[1]
        pos = 2
        r_ref = refs[pos] if has_res else None
        pos += 1 if has_res else 0
        s_in = refs[pos:pos + n_sin]
        pos += n_sin
        o_ref = refs[pos]
        pos += 1
        s_out = refs[pos:pos + n_sout]
        pos += n_sout
        acc_ref = refs[pos] if nk > 1 else None
        pos += 1 if nk > 1 else 0
        s_sem = refs[pos:pos + n_ssem]

        if side:
            step = (pl.program_id(0) * grid[1] + pl.program_id(1)) * nk + pl.program_id(2)

            @pl.when(step == 0)
            def _():
                side.start(s_in, s_out, s_sem)

        prod = lax.dot_general(a_ref[...], b_ref[...], dims, preferred_element_type=F32)

        def finish(val):
            if has_res:
                val = val + r_ref[...]
            o_ref[...] = val.astype(o_ref.dtype)

        if nk == 1:
            finish(prod)
        else:
            k = pl.program_id(2)

            @pl.when(k == 0)
            def _():
                acc_ref[...] = prod

            @pl.when(k > 0)
            def _():
                acc_ref[...] += prod

            @pl.when(k == nk - 1)
            def _():
                finish(acc_ref[...])

        if side:
            if side.mid is not None:
                @pl.when(step == (2 * n_steps) // 3)
                def _():
                    side.mid(s_in, s_out, s_sem)

            @pl.when(step == n_steps - 1)
            def _():
                side.finish(s_in, s_out, s_sem)

    in_specs = [a_spec, b_spec]
    args = [a3, b3]
    if has_res:
        in_specs.append(pl.BlockSpec((tm, tn), lambda i, j, k: (i, j)))
        args.append(res)
    out_specs, out_shape = o_spec, SDS(out3, out_dtype)
    scratch = [pltpu.VMEM((tm, tn), F32)] if nk > 1 else []
    sem = ("parallel", "parallel", "arbitrary")
    if side:
        assert n_steps >= 3
        in_specs += [ANY] * n_sin
        args += side.ins
        out_specs = [o_spec] + [ANY] * n_sout
        out_shape = [out_shape] + side.out_shapes
        scratch += side.sem_shapes
        sem = ("arbitrary", "arbitrary", "arbitrary")
    return pl.pallas_call(
        body, name=name, grid=grid, in_specs=in_specs, out_specs=out_specs, out_shape=out_shape,
        scratch_shapes=scratch, compiler_params=_cparams(dimension_semantics=sem),
    )(*args)


def _bd_fwd(name, a, ws, tm, scale=None):
    T = a.shape[0]
    H, w, _ = ws[0].shape
    nw = len(ws)
    has_scale = scale is not None

    def body(*refs):
        a_ref = refs[0]
        w_refs = refs[1:1 + nw]
        pos = 1 + nw
        s_ref = refs[pos] if has_scale else None
        o_refs = refs[pos + (1 if has_scale else 0):]
        av = a_ref[...]
        for l in range(nw):
            r = jnp.dot(av, w_refs[l][...], preferred_element_type=F32)
            o_refs[l][...] = r
            if has_scale and l == 0:
                o_refs[nw][...] = (r * s_ref[...]).astype(BF16)

    blk = pl.BlockSpec((tm, w), lambda h, i: (i, h))
    wspec = pl.BlockSpec((None, w, w), lambda h, i: (h, 0, 0))
    in_specs = [blk] + [wspec] * nw
    args = [a] + list(ws)
    out_shape = [SDS((T, H * w), F32)] * nw
    out_specs = [blk] * nw
    if has_scale:
        in_specs.append(pl.BlockSpec((1, w), lambda h, i: (0, h)))
        args.append(scale)
        out_shape = out_shape + [SDS((T, H * w), BF16)]
        out_specs = out_specs + [blk]
    return pl.pallas_call(
        body, name=name, grid=(H, T // tm), in_specs=in_specs, out_specs=out_specs, out_shape=out_shape,
        compiler_params=_cparams(dimension_semantics=("parallel", "parallel")),
    )(*args)


def _bd_bwd_x(name, dys, ws, tm, res=None):
    T = dys[0].shape[0]
    H, w, _ = ws[0].shape
    nw = len(ws)
    has_res = res is not None

    def body(*refs):
        d_refs = refs[:nw]
        w_refs = refs[nw:2 * nw]
        r_ref = refs[2 * nw] if has_res else None
        o_ref = refs[-1]
        acc = None
        for l in range(nw):
            p = lax.dot_general(d_refs[l][...], w_refs[l][...], (((1,), (1,)), ((), ())),
                                preferred_element_type=F32)
            acc = p if acc is None else acc + p
        if has_res:
            acc = acc + r_ref[...]
        o_ref[...] = acc

    blk = pl.BlockSpec((tm, w), lambda h, i: (i, h))
    wspec = pl.BlockSpec((None, w, w), lambda h, i: (h, 0, 0))
    in_specs = [blk] * nw + [wspec] * nw + ([blk] if has_res else [])
    args = list(dys) + list(ws) + ([res] if has_res else [])
    return pl.pallas_call(
        body, name=name, grid=(H, T // tm), in_specs=in_specs, out_specs=blk, out_shape=SDS((T, H * w), F32),
        compiler_params=_cparams(dimension_semantics=("parallel", "parallel")),
    )(*args)


def _bd_bwd_w(name, a, dys, w, tk):
    T, HW = a.shape
    H = HW // w
    nw = len(dys)

    def body(*refs):
        a_ref = refs[0]
        d_refs = refs[1:1 + nw]
        o_refs = refs[1 + nw:]
        k = pl.program_id(1)
        av = a_ref[...]
        for l in range(nw):
            p = lax.dot_general(av, d_refs[l][...], (((0,), (0,)), ((), ())), preferred_element_type=F32)

            @pl.when(k == 0)
            def _(p=p, l=l):
                o_refs[l][...] = p

            @pl.when(k > 0)
            def _(p=p, l=l):
                o_refs[l][...] += p

    blk = pl.BlockSpec((tk, w), lambda h, k: (k, h))
    ospec = pl.BlockSpec((None, w, w), lambda h, k: (h, 0, 0))
    return pl.pallas_call(
        body, name=name, grid=(H, T // tk), in_specs=[blk] * (1 + nw), out_specs=[ospec] * nw,
        out_shape=[SDS((H, w, w), F32)] * nw,
        compiler_params=_cparams(dimension_semantics=("parallel", "arbitrary")),
    )(a, *dys)


def _rms_fwd(name, x, g, tt):
    T, D = x.shape

    def body(x_ref, g_ref, h_ref):
        xv = x_ref[...]
        rstd = lax.rsqrt(jnp.mean(xv * xv, axis=-1, keepdims=True) + EPS)
        h_ref[...] = (xv * rstd * g_ref[...]).astype(BF16)

    row = pl.BlockSpec((tt, D), lambda i: (i, 0))
    vec = pl.BlockSpec((1, D), lambda i: (0, 0))
    return pl.pallas_call(
        body, name=name, grid=(T // tt,), in_specs=[row, vec], out_specs=row, out_shape=SDS((T, D), BF16),
        compiler_params=_cparams(dimension_semantics=("parallel",)),
    )(x, g)


def _rms_bwd(name, dh, x, g, dres, tt, want_bf16):
    T, D = x.shape

    def body(dh_ref, x_ref, g_ref, dres_ref, *outs):
        i = pl.program_id(0)
        dx_ref = outs[0]
        dg_ref = outs[-1]
        xv = x_ref[...]
        rstd = lax.rsqrt(jnp.mean(xv * xv, axis=-1, keepdims=True) + EPS)
        xn = xv * rstd
        dhv = dh_ref[...]
        dyg = dhv * g_ref[...]
        dx = dres_ref[...] + rstd * (dyg - xn * jnp.mean(dyg * xn, axis=-1, keepdims=True))
        dx_ref[...] = dx
        if want_bf16:
            outs[1][...] = dx.astype(BF16)

        @pl.when(i == 0)
        def _():
            dg_ref[...] = jnp.zeros_like(dg_ref)

        dg_ref[...] += _colsum(dhv * xn)

    row = pl.BlockSpec((tt, D), lambda i: (i, 0))
    vec = pl.BlockSpec((1, D), lambda i: (0, 0))
    out_shape = [SDS((T, D), F32)] + ([SDS((T, D), BF16)] if want_bf16 else []) + [SDS((1, D), F32)]
    out_specs = [row] + ([row] if want_bf16 else []) + [vec]
    return pl.pallas_call(
        body, name=name, grid=(T // tt,), in_specs=[row, row, vec, row], out_specs=out_specs, out_shape=out_shape,
        compiler_params=_cparams(dimension_semantics=("arbitrary",)),
    )(dh, x, g, dres)


def _final_loss_bwd(name, x2, tgt, g, tt):
    T, D = x2.shape

    def body(x_ref, t_ref, g_ref, dx_ref, dxb_ref, dg_ref, lv_ref):
        i = pl.program_id(0)
        xv = x_ref[...]
        gv = g_ref[...]
        rstd = lax.rsqrt(jnp.mean(xv * xv, axis=-1, keepdims=True) + EPS)
        xn = xv * rstd
        e = xn * gv - t_ref[...]
        dy = e * (1.0 / D)
        dyg = dy * gv
        dx = rstd * (dyg - xn * jnp.mean(dyg * xn, axis=-1, keepdims=True))
        dx_ref[...] = dx
        dxb_ref[...] = dx.astype(BF16)

        @pl.when(i == 0)
        def _():
            dg_ref[...] = jnp.zeros_like(dg_ref)
            lv_ref[...] = jnp.zeros_like(lv_ref)

        dg_ref[...] += _colsum(dy * xn)
        lv_ref[...] += _colsum(e * e)

    row = pl.BlockSpec((tt, D), lambda i: (i, 0))
    vec = pl.BlockSpec((1, D), lambda i: (0, 0))
    return pl.pallas_call(
        body, name=name, grid=(T // tt,), in_specs=[row, row, vec], out_specs=[row, row, vec, vec],
        out_shape=[SDS((T, D), F32), SDS((T, D), BF16), SDS((1, D), F32), SDS((1, D), F32)],
        compiler_params=_cparams(dimension_semantics=("arbitrary",)),
    )(x2, tgt, g)


def _blk(tt, tc, off=0):
    return pl.BlockSpec((tt, tc), lambda j, i: (i, j + off))


def _vec(r, tc, off=0):
    return pl.BlockSpec((r, tc), lambda j, i: (0, j + off))


def _prev_halo(tt, tc, off=0):
    r = tt // HALO
    return pl.BlockSpec((HALO, tc), lambda j, i: (jnp.maximum(i * r - 1, 0), j + off))


def _next_halo(tt, tc, T, off=0):
    r = tt // HALO
    last = T // HALO - 1
    return pl.BlockSpec((HALO, tc), lambda j, i: (jnp.minimum((i + 1) * r, last), j + off))


def _seq_params():
    return _cparams(dimension_semantics=("parallel", "arbitrary"))


def _pool_fwd(name, proj, Cp, gw, tt):
    T = proj.shape[0]
    G = Cp // gw
    assert G == 4

    def body(u_ref, halo_ref, d_ref):
        j, i = pl.program_id(0), pl.program_id(1)
        u = u_ref[...]
        halo = jnp.where(i > 0, halo_ref[...], 0.0)
        ext = jnp.concatenate([halo, u], axis=0)
        e2 = ext + pltpu.roll(ext, 1, 0)
        e4 = e2 + pltpu.roll(e2, 2, 0)
        e8 = e4 + pltpu.roll(e4, 4, 0)
        e16 = e8 + pltpu.roll(e8, 8, 0)
        s = jnp.where(j == 0, e2, jnp.where(j == 1, e4, jnp.where(j == 2, e8, e16)))[HALO:]
        t1 = (i * tt + 1 + lax.broadcasted_iota(jnp.int32, (tt, 1), 0)).astype(F32)
        win = lax.shift_left(jnp.int32(2), j).astype(F32)
        d_ref[...] = (s / jnp.minimum(t1, win) - u).astype(BF16)

    return pl.pallas_call(
        body, name=name, grid=(G, T // tt), in_specs=[_blk(tt, gw), _prev_halo(tt, gw)], out_specs=_blk(tt, gw),
        out_shape=SDS((T, Cp), BF16), compiler_params=_seq_params(),
    )(proj, proj)


def _pool_bwd_a(name, dy, z, scale, tt):
    T, Cp = dy.shape

    def body(dy_ref, z_ref, s_ref, dz_ref, ds_ref):
        i = pl.program_id(0)
        dyv = dy_ref[...]
        dz_ref[...] = (dyv * s_ref[...]).astype(BF16)

        @pl.when(i == 0)
        def _():
            ds_ref[...] = jnp.zeros_like(ds_ref)

        ds_ref[...] += _colsum(dyv * z_ref[...])

    row = pl.BlockSpec((tt, Cp), lambda i: (i, 0))
    vec = pl.BlockSpec((1, Cp), lambda i: (0, 0))
    return pl.pallas_call(
        body, name=name, grid=(T // tt,), in_specs=[row, row, vec], out_specs=[row, vec],
        out_shape=[SDS((T, Cp), BF16), SDS((1, Cp), F32)],
        compiler_params=_cparams(dimension_semantics=("arbitrary",)),
    )(dy, z, scale)


def _pool_bwd_b(name, dd, gw, tt):
    T, Cp = dd.shape
    G = Cp // gw
    nT = T // tt

    def body(d_ref, nxt_ref, du_ref):
        j, i = pl.program_id(0), pl.program_id(1)
        win = lax.shift_left(jnp.int32(2), j).astype(F32)
        dv = d_ref[...]
        t1 = (i * tt + 1 + lax.broadcasted_iota(jnp.int32, (tt + HALO, 1), 0)).astype(F32)
        nxt = jnp.where(i < nT - 1, nxt_ref[...], 0.0)
        ext = jnp.concatenate([dv, nxt], axis=0) / jnp.minimum(t1, win)
        n = tt + HALO
        f2 = ext + pltpu.roll(ext, n - 1, 0)
        f4 = f2 + pltpu.roll(f2, n - 2, 0)
        f8 = f4 + pltpu.roll(f4, n - 4, 0)
        f16 = f8 + pltpu.roll(f8, n - 8, 0)
        s = jnp.where(j == 0, f2, jnp.where(j == 1, f4, jnp.where(j == 2, f8, f16)))[:tt]
        du_ref[...] = (s - dv).astype(BF16)

    return pl.pallas_call(
        body, name=name, grid=(G, nT), in_specs=[_blk(tt, gw), _next_halo(tt, gw, T)], out_specs=_blk(tt, gw),
        out_shape=SDS((T, Cp), BF16), compiler_params=_seq_params(),
    )(dd, dd)


def _conv_fwd(name, src, off_cols, w, b, tt, tc):
    T = src.shape[0]
    K, C = w.shape
    off = off_cols // tc

    def body(u_ref, halo_ref, w_ref, b_ref, v_ref, vb_ref):
        i = pl.program_id(1)
        u = u_ref[...]
        halo = jnp.where(i > 0, halo_ref[...], 0.0)
        ext = jnp.concatenate([halo, u], axis=0)
        wv = w_ref[...]
        acc = b_ref[...] + u * wv[K - 1:K]
        for s in range(1, K):
            acc = acc + _lookback(ext, s) * wv[K - 1 - s:K - s]
        v_ref[...] = acc
        vb_ref[...] = acc.astype(BF16)

    return pl.pallas_call(
        body, name=name, grid=(C // tc, T // tt),
        in_specs=[_blk(tt, tc, off), _prev_halo(tt, tc, off), _vec(K, tc), _vec(1, tc)],
        out_specs=[_blk(tt, tc), _blk(tt, tc)], out_shape=[SDS((T, C), F32), SDS((T, C), BF16)],
        compiler_params=_seq_params(),
    )(src, src, w, b)


def _conv_bwd(name, dv, src, off_cols, w, tt, tc):
    T, C = dv.shape
    K = w.shape[0]
    off = off_cols // tc
    nT = T // tt

    def body(dv_ref, nxt_ref, u_ref, halo_ref, w_ref, du_ref, dw_ref, db_ref):
        i = pl.program_id(1)
        d = dv_ref[...]
        nxt = jnp.where(i < nT - 1, nxt_ref[...], 0.0)
        dext = jnp.concatenate([d, nxt], axis=0)
        u = u_ref[...]
        halo = jnp.where(i > 0, halo_ref[...], 0.0)
        uext = jnp.concatenate([halo, u], axis=0)
        wv = w_ref[...]
        du = d * wv[K - 1:K]
        rows = [_colsum(d * u)]
        for s in range(1, K):
            du = du + _lookahead(dext, s, tt) * wv[K - 1 - s:K - s]
            rows.append(_colsum(d * _lookback(uext, s)))
        du_ref[...] = du.astype(BF16)
        dw = jnp.concatenate(rows[::-1] + [jnp.zeros((8 - K, tc), F32)], axis=0)

        @pl.when(i == 0)
        def _():
            dw_ref[...] = jnp.zeros_like(dw_ref)
            db_ref[...] = jnp.zeros_like(db_ref)

        dw_ref[...] += dw
        db_ref[...] += _colsum(d)

    return pl.pallas_call(
        body, name=name, grid=(C // tc, nT),
        in_specs=[_blk(tt, tc), _next_halo(tt, tc, T), _blk(tt, tc, off), _prev_halo(tt, tc, off), _vec(K, tc)],
        out_specs=[_blk(tt, tc), _vec(8, tc), _vec(1, tc)],
        out_shape=[SDS((T, C), BF16), SDS((8, C), F32), SDS((1, C), F32)],
        compiler_params=_seq_params(),
    )(dv, dv, src, src, w)


def _lru_gates(ra, ia, v, ba, bi, lam):
    r = _sigmoid(ra + ba)
    ig = _sigmoid(ia + bi)
    nl = -lam
    ex = jnp.exp(-jnp.abs(nl))
    one_p = 1.0 + ex
    l1p = jnp.where(one_p == 1.0, ex, jnp.log(one_p) * ex / (one_p - 1.0))
    sp = jnp.maximum(nl, 0.0) + l1p
    a = jnp.exp(-LRU_C * r * sp)
    mult = jnp.sqrt(1.0 - a * a)
    return r, ig, sp, a, mult


def _lru_fwd(name, ra_pre, ia_pre, v, proj, off_cols, ba, bi, lam, tt, tc):
    T, R = v.shape
    off = off_cols // tc

    def body(ra_ref, ia_ref, v_ref, ug_ref, ba_ref, bi_ref, lam_ref, h_ref, y_ref, carry_ref):
        i = pl.program_id(1)

        @pl.when(i == 0)
        def _():
            carry_ref[...] = jnp.zeros_like(carry_ref)

        vv = v_ref[...]
        _, ig, _, a, mult = _lru_gates(ra_ref[...], ia_ref[...], vv, ba_ref[...], bi_ref[...], lam_ref[...])
        A = a
        B = mult * (ig * vv)
        row = lax.broadcasted_iota(jnp.int32, (tt, 1), 0)
        s = 1
        while s < tt:
            m = row >= s
            B = jnp.where(m, A * pltpu.roll(B, s, 0) + B, B)
            A = jnp.where(m, A * pltpu.roll(A, s, 0), A)
            s *= 2
        h = A * carry_ref[7:8, :] + B
        h_ref[...] = h
        carry_ref[...] = h[tt - 8:, :]
        y_ref[...] = (h * _gelu(ug_ref[...])).astype(BF16)

    return pl.pallas_call(
        body, name=name, grid=(R // tc, T // tt),
        in_specs=[_blk(tt, tc), _blk(tt, tc), _blk(tt, tc), _blk(tt, tc, off), _vec(1, tc), _vec(1, tc), _vec(1, tc)],
        out_specs=[_blk(tt, tc), _blk(tt, tc)], out_shape=[SDS((T, R), F32), SDS((T, R), BF16)],
        scratch_shapes=[pltpu.VMEM((8, tc), F32)], compiler_params=_seq_params(),
    )(ra_pre, ia_pre, v, proj, ba, bi, lam)


def _lru_bwd(name, dy, hs, proj, off_cols, ra_pre, ia_pre, v, ba, bi, lam, tt, tc):
    T, R = v.shape
    off = off_cols // tc
    nT = T // tt
    r16 = tt // HALO

    def rblk(o=0):
        return pl.BlockSpec((tt, tc), lambda j, i: (nT - 1 - i, j + o))

    hprev = pl.BlockSpec((HALO, tc), lambda j, i: (jnp.maximum((nT - 1 - i) * r16 - 1, 0), j))

    def body(dy_ref, h_ref, hp_ref, ug_ref, ra_ref, ia_ref, v_ref, ba_ref, bi_ref, lam_ref,
             dra_ref, dia_ref, dv_ref, dug_ref, dba_ref, dbi_ref, dlam_ref, dh_carry, a_carry):
        i = pl.program_id(1)

        @pl.when(i == 0)
        def _():
            dh_carry[...] = jnp.zeros_like(dh_carry)
            a_carry[...] = jnp.zeros_like(a_carry)
            dba_ref[...] = jnp.zeros_like(dba_ref)
            dbi_ref[...] = jnp.zeros_like(dbi_ref)
            dlam_ref[...] = jnp.zeros_like(dlam_ref)

        vv = v_ref[...]
        lamv = lam_ref[...]
        r, ig, sp, a, mult = _lru_gates(ra_ref[...], ia_ref[...], vv, ba_ref[...], bi_ref[...], lamv)
        h = h_ref[...]
        gel, dgel = _gelu_and_grad(ug_ref[...])
        dyv = dy_ref[...]
        dug_ref[...] = (dyv * h * dgel).astype(BF16)
        dhs = dyv * gel

        a_ext = jnp.concatenate([a, jnp.broadcast_to(a_carry[0:1, :], (HALO, tc))], axis=0)
        A = _lookahead(a_ext, 1, tt)
        B = dhs
        row = lax.broadcasted_iota(jnp.int32, (tt, 1), 0)
        s = 1
        while s < tt:
            m = row < tt - s
            B = jnp.where(m, B + A * pltpu.roll(B, tt - s, 0), B)
            A = jnp.where(m, A * pltpu.roll(A, tt - s, 0), A)
            s *= 2
        dH = B + A * dh_carry[0:1, :]
        dh_carry[...] = dH[:8, :]
        a_carry[...] = a[:8, :]

        hp = jnp.where(i < nT - 1, hp_ref[...], 0.0)
        h_prev = _lookback(jnp.concatenate([hp, h], axis=0), 1)
        da = dH * h_prev
        dmult = dH * (ig * vv)
        div = dH * mult
        dv_ref[...] = div * ig
        dig = div * vv
        dlog_a = da * a - dmult * (a * a) / mult
        dr = dlog_a * (-LRU_C * sp)
        dra = dr * r * (1.0 - r)
        dia = dig * ig * (1.0 - ig)
        dra_ref[...] = dra.astype(BF16)
        dia_ref[...] = dia.astype(BF16)
        dba_ref[...] += _colsum(dra)
        dbi_ref[...] += _colsum(dia)
        dlam_ref[...] += _colsum(dlog_a * (-LRU_C * r))

        @pl.when(i == nT - 1)
        def _():
            dlam_ref[...] = dlam_ref[...] * (-_sigmoid(-lamv))

    return pl.pallas_call(
        body, name=name, grid=(R // tc, nT),
        in_specs=[rblk(), rblk(), hprev, rblk(off), rblk(), rblk(), rblk(), _vec(1, tc), _vec(1, tc), _vec(1, tc)],
        out_specs=[rblk(), rblk(), rblk(), rblk(), _vec(1, tc), _vec(1, tc), _vec(1, tc)],
        out_shape=[SDS((T, R), BF16), SDS((T, R), BF16), SDS((T, R), F32), SDS((T, R), BF16),
                   SDS((1, R), F32), SDS((1, R), F32), SDS((1, R), F32)],
        scratch_shapes=[pltpu.VMEM((8, tc), F32), pltpu.VMEM((8, tc), F32)], compiler_params=_seq_params(),
    )(dy, hs, hs, proj, ra_pre, ia_pre, v, ba, bi, lam)


def _merge_fwd(name, proj, off_cols, b_gate, P, Q, tt, tc):
    T, D = P.shape
    o0 = off_cols // tc
    o1 = o0 + D // tc

    def body(g0_ref, g1_ref, b0_ref, b1_ref, p_ref, q_ref, m_ref):
        g0 = _sigmoid(g0_ref[...] + b0_ref[...])
        g1 = _sigmoid(g1_ref[...] + b1_ref[...])
        m_ref[...] = (g0 * p_ref[...] + g1 * q_ref[...]).astype(BF16)

    return pl.pallas_call(
        body, name=name, grid=(D // tc, T // tt),
        in_specs=[_blk(tt, tc, o0), _blk(tt, tc, o1), _vec(1, tc), _vec(1, tc, D // tc), _blk(tt, tc), _blk(tt, tc)],
        out_specs=_blk(tt, tc), out_shape=SDS((T, D), BF16), compiler_params=_seq_params(),
    )(proj, proj, b_gate, b_gate, P, Q)


def _merge_bwd(name, dm, proj, off_cols, b_gate, P, Q, tt, tc):
    T, D = P.shape
    o0 = off_cols // tc
    o1 = o0 + D // tc

    def body(dm_ref, g0_ref, g1_ref, b0_ref, b1_ref, p_ref, q_ref, dp_ref, dq_ref, dl0_ref, dl1_ref, db0_ref, db1_ref):
        i = pl.program_id(1)
        g0 = _sigmoid(g0_ref[...] + b0_ref[...])
        g1 = _sigmoid(g1_ref[...] + b1_ref[...])
        d = dm_ref[...].astype(F32)
        dp_ref[...] = (d * g0).astype(BF16)
        dq_ref[...] = (d * g1).astype(BF16)
        dl0 = d * p_ref[...] * g0 * (1.0 - g0)
        dl1 = d * q_ref[...] * g1 * (1.0 - g1)
        dl0_ref[...] = dl0.astype(BF16)
        dl1_ref[...] = dl1.astype(BF16)

        @pl.when(i == 0)
        def _():
            db0_ref[...] = jnp.zeros_like(db0_ref)
            db1_ref[...] = jnp.zeros_like(db1_ref)

        db0_ref[...] += _colsum(dl0)
        db1_ref[...] += _colsum(dl1)

    b = _blk(tt, tc)
    return pl.pallas_call(
        body, name=name, grid=(D // tc, T // tt),
        in_specs=[b, _blk(tt, tc, o0), _blk(tt, tc, o1), _vec(1, tc), _vec(1, tc, D // tc), b, b],
        out_specs=[b, b, b, b, _vec(1, tc), _vec(1, tc)],
        out_shape=[SDS((T, D), BF16)] * 4 + [SDS((1, D), F32)] * 2, compiler_params=_seq_params(),
    )(dm, proj, proj, b_gate, b_gate, P, Q)


def _ffn_act_fwd(name, up, w, b, tt, tc):
    T = up.shape[0]
    K, F = w.shape
    nF = F // tc

    def body(gp_ref, halo_ref, val_ref, w_ref, b_ref, z_ref):
        i = pl.program_id(1)
        gp = gp_ref[...].astype(F32)
        halo = jnp.where(i > 0, halo_ref[...].astype(F32), 0.0)
        ext = jnp.concatenate([halo, gp], axis=0)
        wv = w_ref[...]
        c = b_ref[...] + gp * wv[K - 1:K]
        for s in range(1, K):
            c = c + _lookback(ext, s) * wv[K - 1 - s:K - s]
        z_ref[...] = (_gelu(c) * val_ref[...].astype(F32)).astype(BF16)

    return pl.pallas_call(
        body, name=name, grid=(nF, T // tt),
        in_specs=[_blk(tt, tc), _prev_halo(tt, tc), _blk(tt, tc, nF), _vec(K, tc), _vec(1, tc)],
        out_specs=_blk(tt, tc), out_shape=SDS((T, F), BF16), compiler_params=_seq_params(),
    )(up, up, up, w, b)


def _ffn_act_bwd(name, up, dz, w, b, tt, tc):
    T = up.shape[0]
    K, F = w.shape
    nF = F // tc
    nT = T // tt

    def body(gp_ref, gph_ref, gpn_ref, val_ref, valn_ref, dz_ref, dzn_ref, w_ref, b_ref, dup_ref, dw_ref, db_ref):
        i = pl.program_id(1)
        wv = w_ref[...]
        gp = gp_ref[...].astype(F32)
        halo = jnp.where(i > 0, gph_ref[...].astype(F32), 0.0)
        full = jnp.concatenate([halo, gp, gpn_ref[...].astype(F32)], axis=0)
        c = b_ref[...] + full * wv[K - 1:K]
        for s in range(1, K):
            c = c + pltpu.roll(full, s, 0) * wv[K - 1 - s:K - s]
        c = c[HALO:]
        gate, dgate = _gelu_and_grad(c)
        dz_e = jnp.concatenate([dz_ref[...], dzn_ref[...]], axis=0).astype(F32)
        val_e = jnp.concatenate([val_ref[...], valn_ref[...]], axis=0).astype(F32)
        dc_e = dz_e * val_e * dgate
        row = lax.broadcasted_iota(jnp.int32, (tt + HALO, 1), 0)
        dc_e = jnp.where((row < tt) | (i < nT - 1), dc_e, 0.0)
        dc = dc_e[:tt]
        dgp = dc * wv[K - 1:K]
        uext = full[:tt + HALO]
        rows = [_colsum(dc * gp)]
        for s in range(1, K):
            dgp = dgp + _lookahead(dc_e, s, tt) * wv[K - 1 - s:K - s]
            rows.append(_colsum(dc * _lookback(uext, s)))
        dup_ref[0] = dgp.astype(BF16)
        dup_ref[1] = (dz_e[:tt] * gate[:tt]).astype(BF16)
        dw = jnp.concatenate(rows[::-1] + [jnp.zeros((8 - K, tc), F32)], axis=0)

        @pl.when(i == 0)
        def _():
            dw_ref[...] = jnp.zeros_like(dw_ref)
            db_ref[...] = jnp.zeros_like(db_ref)

        dw_ref[...] += dw
        db_ref[...] += _colsum(dc)

    return pl.pallas_call(
        body, name=name, grid=(nF, nT),
        in_specs=[_blk(tt, tc), _prev_halo(tt, tc), _next_halo(tt, tc, T), _blk(tt, tc, nF), _next_halo(tt, tc, T, nF),
                  _blk(tt, tc), _next_halo(tt, tc, T), _vec(K, tc), _vec(1, tc)],
        out_specs=[pl.BlockSpec((2, tt, tc), lambda j, i: (0, i, j)), _vec(8, tc), _vec(1, tc)],
        out_shape=[SDS((2, T, F), BF16), SDS((8, F), F32), SDS((1, F), F32)],
        compiler_params=_seq_params(),
    )(up, up, up, up, up, dz, dz, w, b)


def _adamw(name, w, g, m, v):
    R, C = w.shape
    tr = R
    if R * C * 4 > ADAMW_BLOCK_BYTES:
        tr = _pick(R, [t for t in (256, 128, 64, 32, 16, 8) if t * C * 4 <= ADAMW_BLOCK_BYTES])
    c1 = 1.0 - ADAM_B1 ** ADAM_STEP
    c2 = 1.0 - ADAM_B2 ** ADAM_STEP

    def body(w_ref, g_ref, m_ref, v_ref, d_ref, nm_ref, nv_ref):
        gv = g_ref[...]
        nm = ADAM_B1 * m_ref[...] + (1.0 - ADAM_B1) * gv
        nv = ADAM_B2 * v_ref[...] + (1.0 - ADAM_B2) * (gv * gv)
        nm_ref[...] = nm
        nv_ref[...] = nv
        d_ref[...] = -ADAM_LR * ((nm / c1) / (jnp.sqrt(nv / c2) + ADAM_EPS) + ADAM_WD * w_ref[...])

    spec = pl.BlockSpec((tr, C), lambda i: (i, 0))
    return pl.pallas_call(
        body, name=name, grid=(R // tr,), in_specs=[spec] * 4, out_specs=[spec] * 3,
        out_shape=[SDS((R, C), F32)] * 3, compiler_params=_cparams(dimension_semantics=("parallel",)),
    )(w, g, m, v)


def _add_halves(name, part, rcv, pos):
    S, R, C = part.shape
    h = R // 2
    tr = _pick(h, (512, 256, 128, 64, 32, 16))
    nb = h // tr

    def body(pos_ref, p_ref, r_ref, o_ref):
        o_ref[...] = (p_ref[...].astype(F32) + r_ref[...].astype(F32)).astype(BF16)

    return pl.pallas_call(
        body, name=name,
        grid_spec=pltpu.PrefetchScalarGridSpec(
            num_scalar_prefetch=1, grid=(S, nb),
            in_specs=[pl.BlockSpec((None, tr, C), lambda s, r, pos: (s, pos[0] * nb + r, 0)),
                      pl.BlockSpec((None, tr, C), lambda s, r, pos: (s, r, 0))],
            out_specs=pl.BlockSpec((None, tr, C), lambda s, r, pos: (s, r, 0))),
        out_shape=SDS((S, h, C), BF16), compiler_params=_cparams(dimension_semantics=("parallel", "parallel")),
    )(pos, part, rcv)


def _add_chips(name, chipsum, rcv, pos):
    S, h, C = chipsum.shape
    tr = _pick(h, (512, 256, 128, 64, 32, 16))

    def body(pos_ref, own_ref, r_ref, o_ref):
        o_ref[...] = ((own_ref[...].astype(F32) + r_ref[0].astype(F32)) + r_ref[1].astype(F32)) + r_ref[2].astype(F32)

    return pl.pallas_call(
        body, name=name,
        grid_spec=pltpu.PrefetchScalarGridSpec(
            num_scalar_prefetch=1, grid=(h // tr,),
            in_specs=[pl.BlockSpec((None, tr, C), lambda r, pos: (pos[1], r, 0)),
                      pl.BlockSpec((3, tr, C), lambda r, pos: (0, r, 0))],
            out_specs=pl.BlockSpec((None, tr, C), lambda r, pos: (pos[0], r, 0))),
        out_shape=SDS((2, h, C), F32), compiler_params=_cparams(dimension_semantics=("parallel",)),
    )(pos, chipsum, rcv)


def _place():
    x, y, c = lax.axis_index("x"), lax.axis_index("y"), lax.axis_index("c")
    chips = [(1 - x, y), (x, 1 - y), (1 - x, 1 - y)]
    slots = [2 * cx + cy for cx, cy in chips]
    return x, y, c, chips, slots


def _gather_weights(shards, split):
    side = _gather_side(shards, split)
    n = len(shards)

    def body(*refs):
        ins, outs, sems = refs[:n], refs[n:2 * n], refs[2 * n:]
        side.start(ins, outs, sems)
        side.mid(ins, outs, sems)
        side.finish(ins, outs, sems)

    return pl.pallas_call(
        body, name="gather_weights", in_specs=[ANY] * n, out_specs=[ANY] * n, out_shape=side.out_shapes,
        scratch_shapes=side.sem_shapes, compiler_params=_cparams(has_side_effects=True),
    )(*shards)


def _gather_side(shards, split):
    n = len(shards)

    def copies(ins, outs, sems):
        send_sems, recv_sems, local_sems = sems
        x, y, c, chips, slots = _place()
        me = 2 * x + y
        sibling = (x, y, 1 - c)

        def rows(a, half):
            hrows = ins[a].shape[0] // 2
            return pl.ds(half * hrows, hrows)

        def ici(a, j):
            if split[a]:
                src, dst = ins[a].at[rows(a, c)], outs[a].at[me, rows(a, c)]
            else:
                src, dst = ins[a], outs[a].at[me]
            return pltpu.make_async_remote_copy(src_ref=src, dst_ref=dst, send_sem=send_sems.at[a, j],
                                                recv_sem=recv_sems.at[a, j], device_id=(*chips[j], c), device_id_type=MESH)

        def landed(a, j):
            dst = outs[a].at[slots[j], rows(a, c)] if split[a] else outs[a].at[slots[j]]
            return pltpu.make_async_remote_copy(src_ref=dst, dst_ref=dst, send_sem=send_sems.at[a, j],
                                                recv_sem=recv_sems.at[a, j], device_id=(*chips[j], c), device_id_type=MESH)

        def d2d(a, j, half):
            blk = outs[a].at[slots[j], rows(a, half)]
            return pltpu.make_async_remote_copy(src_ref=blk, dst_ref=blk, send_sem=send_sems.at[a, 3 + j],
                                                recv_sem=recv_sems.at[a, 3 + j], device_id=sibling, device_id_type=MESH)

        local = [pltpu.make_async_copy(ins[a], outs[a].at[me], local_sems.at[a]) for a in range(n)]
        return c, ici, landed, d2d, local

    def start(ins, outs, sems):
        c, ici, landed, d2d, local = copies(ins, outs, sems)
        for cp in local:
            cp.start()
        for a in range(n):
            for j in range(3):
                ici(a, j).start()

    def mid(ins, outs, sems):
        c, ici, landed, d2d, local = copies(ins, outs, sems)
        for a in range(n):
            for j in range(3):
                landed(a, j).wait_recv()
                if split[a]:
                    d2d(a, j, c).start()

    def finish(ins, outs, sems):
        c, ici, landed, d2d, local = copies(ins, outs, sems)
        for a in range(n):
            if split[a]:
                for j in range(3):
                    d2d(a, j, 1 - c).wait_recv()
        for a in range(n):
            for j in range(3):
                ici(a, j).wait_send()
                if split[a]:
                    d2d(a, j, c).wait_send()
        for cp in local:
            cp.wait()

    return _Side(shards, [SDS((N_SLOTS,) + s.shape, s.dtype) for s in shards],
                 [pltpu.SemaphoreType.DMA((n, 6)), pltpu.SemaphoreType.DMA((n, 6)), pltpu.SemaphoreType.DMA((n,))],
                 start, mid, finish)


def _swap_partial_halves(name, parts):
    n = len(parts)

    def body(*refs):
        ins, outs = refs[:n], refs[n:2 * n]
        send_sems, recv_sems = refs[2 * n:]
        x, y, c, _, _ = _place()
        sibling = (x, y, 1 - c)
        cps = []
        for a in range(n):
            h = ins[a].shape[1] // 2
            cps.append(pltpu.make_async_remote_copy(
                src_ref=ins[a].at[:, pl.ds((1 - c) * h, h)], dst_ref=outs[a], send_sem=send_sems.at[a],
                recv_sem=recv_sems.at[a], device_id=sibling, device_id_type=MESH))
        for cp in cps:
            cp.start()
        for cp in cps:
            cp.wait()

    return pl.pallas_call(
        body, name=name, in_specs=[ANY] * n, out_specs=[ANY] * n,
        out_shape=[SDS((p.shape[0], p.shape[1] // 2, p.shape[2]), p.dtype) for p in parts],
        scratch_shapes=[pltpu.SemaphoreType.DMA((n,)), pltpu.SemaphoreType.DMA((n,))],
        compiler_params=_cparams(has_side_effects=True),
    )(*parts)


def _exchange_chip_sums(sums):
    side = _exchange_side(sums)
    n = len(sums)

    def body(*refs):
        ins, outs, sems = refs[:n], refs[n:2 * n], refs[2 * n:]
        side.start(ins, outs, sems)
        side.finish(ins, outs, sems)

    return pl.pallas_call(
        body, name="exchange_chip_sums", in_specs=[ANY] * n, out_specs=[ANY] * n, out_shape=side.out_shapes,
        scratch_shapes=side.sem_shapes, compiler_params=_cparams(has_side_effects=True),
    )(*sums)


def _exchange_side(sums):
    n = len(sums)

    def copies(ins, outs, sems):
        send_sems, recv_sems = sems
        x, y, c, chips, slots = _place()
        return [pltpu.make_async_remote_copy(
            src_ref=ins[a].at[slots[j]], dst_ref=outs[a].at[j], send_sem=send_sems.at[a, j],
            recv_sem=recv_sems.at[a, j], device_id=(*chips[j], c), device_id_type=MESH)
            for a in range(n) for j in range(3)]

    def start(ins, outs, sems):
        for cp in copies(ins, outs, sems):
            cp.start()

    def finish(ins, outs, sems):
        for cp in copies(ins, outs, sems):
            cp.wait()

    return _Side(sums, [SDS((3,) + s.shape[1:], s.dtype) for s in sums],
                 [pltpu.SemaphoreType.DMA((n, 3)), pltpu.SemaphoreType.DMA((n, 3))], start, None, finish)


def _swap_reduced_halves(totals):
    n = len(totals)

    def body(*refs):
        ins, outs = refs[:n], refs[n:2 * n]
        send_sems, recv_sems = refs[2 * n:]
        x, y, c, _, _ = _place()
        sibling = (x, y, 1 - c)
        sends = [pltpu.make_async_remote_copy(src_ref=outs[a].at[c], dst_ref=outs[a].at[c], send_sem=send_sems.at[a],
                                              recv_sem=recv_sems.at[a], device_id=sibling, device_id_type=MESH)
                 for a in range(n)]
        for cp in sends:
            cp.start()
        for a in range(n):
            got = outs[a].at[1 - c]
            pltpu.make_async_remote_copy(src_ref=got, dst_ref=got, send_sem=send_sems.at[a], recv_sem=recv_sems.at[a],
                                         device_id=sibling, device_id_type=MESH).wait_recv()
        for cp in sends:
            cp.wait_send()

    return pl.pallas_call(
        body, name="swap_reduced_halves", in_specs=[ANY] * n, out_specs=[ANY] * n,
        out_shape=[SDS(t.shape, t.dtype) for t in totals], input_output_aliases={a: a for a in range(n)},
        scratch_shapes=[pltpu.SemaphoreType.DMA((n,)), pltpu.SemaphoreType.DMA((n,))],
        compiler_params=_cparams(has_side_effects=True),
    )(*totals)


def _allreduce_small(pack):
    rows = pack.shape[0]

    def body(in_ref, out_ref, buf, send_sems, recv_sems):
        x, y, c = lax.axis_index("x"), lax.axis_index("y"), lax.axis_index("c")
        me = 4 * x + 2 * y + c
        buf[me] = in_ref[...]
        cps = []
        for k in range(1, 8):
            kx, ky, kc = (k >> 2) & 1, (k >> 1) & 1, k & 1
            peer = (x ^ kx, y ^ ky, c ^ kc)
            cps.append(pltpu.make_async_remote_copy(src_ref=in_ref, dst_ref=buf.at[me], send_sem=send_sems.at[k - 1],
                                                    recv_sem=recv_sems.at[k - 1], device_id=peer, device_id_type=MESH))
        for cp in cps:
            cp.start()
        for k in range(1, 8):
            got = buf.at[me ^ k]
            pltpu.make_async_remote_copy(src_ref=got, dst_ref=got, send_sem=send_sems.at[k - 1],
                                         recv_sem=recv_sems.at[k - 1], device_id=(x, y, c), device_id_type=MESH).wait_recv()
        for cp in cps:
            cp.wait_send()
        acc = buf[0]
        for d in range(1, 8):
            acc = acc + buf[d]
        out_ref[...] = acc

    vm = pl.BlockSpec(memory_space=pltpu.VMEM)
    return pl.pallas_call(
        body, name="allreduce_small", in_specs=[vm], out_specs=vm, out_shape=SDS(pack.shape, F32),
        scratch_shapes=[pltpu.VMEM((8, rows, LANES), F32), pltpu.SemaphoreType.DMA((7,)), pltpu.SemaphoreType.DMA((7,))],
        compiler_params=_cparams(has_side_effects=True),
    )(pack)


def _rows2d(a):
    return a.reshape((-1, a.shape[-1]))


def _from_slots_blockdiag(g, H, w):
    q = w // N_SLOTS
    return g.reshape(N_SLOTS, H, q, w).transpose(1, 0, 2, 3).reshape(H, w, w)


def _to_slots_blockdiag(d, H, w):
    q = w // N_SLOTS
    return d.reshape(H, N_SLOTS, q, w).transpose(1, 0, 2, 3).reshape(N_SLOTS, H * q, w)


def kernel(x, g_mix, w_in, b_gate, w_pool, pool_scale, lru_conv_w, lru_conv_b, w_a, b_a, w_i, b_i, lru_lambda, w_pool_proj, w_lru_proj, w_out, g_mlp, w_up, ffn_conv_w, ffn_conv_b, w_down, g_final, loss_target, m_g_mix, m_w_in, m_b_gate, m_w_pool, m_pool_scale, m_lru_conv_w, m_lru_conv_b, m_w_a, m_b_a, m_w_i, m_b_i, m_lru_lambda, m_w_pool_proj, m_w_lru_proj, m_w_out, m_g_mlp, m_w_up, m_ffn_conv_w, m_ffn_conv_b, m_w_down, m_g_final, v_g_mix, v_w_in, v_b_gate, v_w_pool, v_pool_scale, v_lru_conv_w, v_lru_conv_b, v_w_a, v_b_a, v_w_i, v_b_i, v_lru_lambda, v_w_pool_proj, v_w_lru_proj, v_w_out, v_g_mlp, v_w_up, v_ffn_conv_w, v_ffn_conv_b, v_w_down, v_g_final):
    weights = dict(g_mix=g_mix, w_in=w_in, b_gate=b_gate, w_pool=w_pool, pool_scale=pool_scale, lru_conv_w=lru_conv_w,
                   lru_conv_b=lru_conv_b, w_a=w_a, b_a=b_a, w_i=w_i, b_i=b_i, lru_lambda=lru_lambda,
                   w_pool_proj=w_pool_proj, w_lru_proj=w_lru_proj, w_out=w_out, g_mlp=g_mlp, w_up=w_up,
                   ffn_conv_w=ffn_conv_w, ffn_conv_b=ffn_conv_b, w_down=w_down, g_final=g_final)
    mom_m = dict(g_mix=m_g_mix, w_in=m_w_in, b_gate=m_b_gate, w_pool=m_w_pool, pool_scale=m_pool_scale,
                 lru_conv_w=m_lru_conv_w, lru_conv_b=m_lru_conv_b, w_a=m_w_a, b_a=m_b_a, w_i=m_w_i, b_i=m_b_i,
                 lru_lambda=m_lru_lambda, w_pool_proj=m_w_pool_proj, w_lru_proj=m_w_lru_proj, w_out=m_w_out,
                 g_mlp=m_g_mlp, w_up=m_w_up, ffn_conv_w=m_ffn_conv_w, ffn_conv_b=m_ffn_conv_b, w_down=m_w_down,
                 g_final=m_g_final)
    mom_v = dict(g_mix=v_g_mix, w_in=v_w_in, b_gate=v_b_gate, w_pool=v_w_pool, pool_scale=v_pool_scale,
                 lru_conv_w=v_lru_conv_w, lru_conv_b=v_lru_conv_b, w_a=v_w_a, b_a=v_b_a, w_i=v_w_i, b_i=v_b_i,
                 lru_lambda=v_lru_lambda, w_pool_proj=v_w_pool_proj, w_lru_proj=v_w_lru_proj, w_out=v_w_out,
                 g_mlp=v_g_mlp, w_up=v_w_up, ffn_conv_w=v_ffn_conv_w, ffn_conv_b=v_ffn_conv_b, w_down=v_w_down,
                 g_final=v_g_final)
    order = list(weights)

    T, D = x.shape[1], x.shape[2]
    Cp = pool_scale.shape[1]
    G, gw = w_pool.shape[1], w_pool.shape[3]
    H, hw = w_a.shape[1], w_a.shape[3]
    R = b_a.shape[1]
    F = ffn_conv_b.shape[1]
    KC, KF = lru_conv_w.shape[1], ffn_conv_w.shape[1]
    p0, p1, p2 = Cp, Cp + R, Cp + 2 * R
    xs = x.reshape(T, D)
    tgt = loss_target.reshape(T, D)

    my_x, my_y, my_c = lax.axis_index("x"), lax.axis_index("y"), lax.axis_index("c")
    slot = 2 * my_x + my_y
    pos = jnp.stack([my_c, slot]).astype(jnp.int32)

    big = ["w_in", "w_pool", "w_a", "w_i", "w_pool_proj", "w_lru_proj", "w_out", "w_up", "w_down"]
    first, later = big[:4], big[4:]
    shard16 = {n: _rows2d(weights[n][0]).astype(BF16) for n in big}
    gathered = _gather_weights([shard16[n] for n in first] + [lru_conv_w[0], ffn_conv_w[0]], [True] * 4 + [False, False])
    gw_ = dict(zip(first + ["lru_conv_w", "ffn_conv_w"], gathered))
    W_in = gw_["w_in"]
    W_pool = _from_slots_blockdiag(gw_["w_pool"], G, gw)
    W_a = _from_slots_blockdiag(gw_["w_a"], H, hw)
    W_i = _from_slots_blockdiag(gw_["w_i"], H, hw)
    cw_lru = gw_["lru_conv_w"].transpose(1, 0, 2).reshape(KC, R)
    cw_ffn = gw_["ffn_conv_w"].transpose(1, 0, 2).reshape(KF, F)
    n_in, n_up, n_pp = W_in.shape[2], w_up.shape[2], w_pool_proj.shape[2]

    tt_row = _pick(T, (256, 128))
    tt_seq = _pick(T, (512, 256, 128))
    tt_pool = _pick(T, (1024, 512, 256, 128))
    tc_seq = _pick(R, (512, 256, 128))
    tm1 = _pick(T, (1024, 512, 256))
    tm2 = _pick(T, (2048, 1024, 512, 256))
    tkT = _pick(T, (1024, 512, 256))
    tkT2 = _pick(T, (4096, 2048, 1024, 512, 256))
    tn_in = _pick(n_in, (768, 1152, 384, 128))
    tn_up = _pick(n_up, (1536, 1024, 768, 512, 128))
    tk_up = _pick(n_up, (1024, 768, 512, 128))
    tn_pp = _pick(n_pp, (512, 256, 128))
    tD1 = _pick(D, (1024, 512))
    tD2 = _pick(D, (2048, 1024, 512))
    tF = _pick(F, (1536, 1024, 512))
    tkF = _pick(F, (2048, 1536, 1024, 512))

    h = _rms_fwd("rms1_fwd", xs, g_mix, tt_row)
    proj3, *late = _matmul("proj_fwd", "nn", h, W_in, out_dtype=F32, tm=tm2, tn=tn_in, tk=D,
                           side=_gather_side([shard16[n] for n in later], [True] * len(later)))
    proj = proj3[0]
    gw_.update(zip(later, late))
    W_pp = gw_["w_pool_proj"]
    W_lp = gw_["w_lru_proj"].reshape(R, D)
    W_out = gw_["w_out"].reshape(D, D)
    W_up = gw_["w_up"]
    W_down = gw_["w_down"].reshape(F, D)
    d_pool = _pool_fwd("pool_fwd", proj, Cp, gw, tt_pool)
    z_pool, y_pool = _bd_fwd("pool_mix_fwd", d_pool, [W_pool], tm1, scale=pool_scale)
    v_f, v_b = _conv_fwd("lru_conv_fwd", proj, p0, cw_lru, lru_conv_b, tt_seq, tc_seq)
    ra_pre, ia_pre = _bd_fwd("lru_gate_fwd", v_b, [W_a, W_i], tm1)
    hs, y_lru = _lru_fwd("lru_scan_fwd", ra_pre, ia_pre, v_f, proj, p1, b_a, b_i, lru_lambda, tt_seq, tc_seq)
    P = _matmul("pool_proj_fwd", "nn", y_pool, W_pp, out_dtype=F32, tm=tm2, tn=tn_pp, tk=Cp)[0]
    Q = _matmul("lru_proj_fwd", "nn", y_lru, W_lp, out_dtype=F32, tm=tm1, tn=tD1, tk=R)[0]
    merged = _merge_fwd("merge_fwd", proj, p2, b_gate, P, Q, tt_seq, tc_seq)
    x1 = _matmul("out_fwd", "nn", merged, W_out, out_dtype=F32, tm=tm1, tn=tD1, tk=D, res=xs)[0]
    h2 = _rms_fwd("rms2_fwd", x1, g_mlp, tt_row)
    up = _matmul("up_fwd", "nn", h2, W_up, out_dtype=BF16, tm=tm1, tn=tn_up, tk=D)[0]
    z = _ffn_act_fwd("ffn_act_fwd", up, cw_ffn, ffn_conv_b, tt_seq, tc_seq)
    x2 = _matmul("down_fwd", "nn", z, W_down, out_dtype=F32, tm=tm1, tn=tD1, tk=tkF, res=x1)[0]

    dx2, dx2b, d_g_final, lossvec = _final_loss_bwd("final_loss_bwd", x2, tgt, g_final.reshape(1, D), tt_row)
    dz = _matmul("down_bwd_x", "nt", dx2b, W_down, out_dtype=BF16, tm=tm1, tn=tF, tk=D)[0]
    dW_down = _matmul("down_bwd_w", "tn", z, dx2b, out_dtype=BF16, tm=_pick(F, (1024, 512)), tn=tD1, tk=tkT2)
    dup, d_cw_ffn, d_ffn_b = _ffn_act_bwd("ffn_act_bwd", up, dz, cw_ffn, ffn_conv_b, tt_seq, tc_seq)
    dh2 = _matmul("up_bwd_x", "nt", dup, W_up, out_dtype=F32, tm=tm1, tn=tD1, tk=n_up)[0]
    dW_up = _matmul("up_bwd_w", "tn", h2, dup, out_seg=N_SLOTS, out_dtype=BF16, tm=tD1, tn=tk_up, tk=tkT2)
    dx1, dx1b, d_g_mlp = _rms_bwd("rms2_bwd", dh2, x1, g_mlp, dx2, tt_row, True)
    dmerged = _matmul("out_bwd_x", "nt", dx1b, W_out, out_dtype=BF16, tm=tm1, tn=tD2, tk=D)[0]
    dW_out = _matmul("out_bwd_w", "tn", merged, dx1b, out_dtype=BF16, tm=tD1, tn=tD1, tk=tkT2)
    dP, dQ, dgl0, dgl1, d_bg0, d_bg1 = _merge_bwd("merge_bwd", dmerged, proj, p2, b_gate, P, Q, tt_seq, tc_seq)
    dy_pool = _matmul("pool_proj_bwd_x", "nt", dP, W_pp, out_dtype=F32, tm=tm2, tn=Cp, tk=tn_pp)[0]
    dW_pp = _matmul("pool_proj_bwd_w", "tn", y_pool, dP, out_seg=N_SLOTS, out_dtype=BF16, tm=Cp, tn=tn_pp,
                    tk=_pick(T, (2048, 1024, 512, 256)))
    dy_lru = _matmul("lru_proj_bwd_x", "nt", dQ, W_lp, out_dtype=F32, tm=tm1, tn=tD1, tk=D)[0]
    dW_lp = _matmul("lru_proj_bwd_w", "tn", y_lru, dQ, out_dtype=BF16, tm=_pick(R, (1024, 512)), tn=tD1, tk=tkT2)
    dra, dia, dv1, du_gelu, d_b_a, d_b_i, d_lam = _lru_bwd(
        "lru_scan_bwd", dy_lru, hs, proj, p1, ra_pre, ia_pre, v_f, b_a, b_i, lru_lambda, tt_seq, tc_seq)
    dv = _bd_bwd_x("lru_gate_bwd_x", [dra, dia], [W_a, W_i], tm1, res=dv1)
    dW_a, dW_i = _bd_bwd_w("lru_gate_bwd_w", v_b, [dra, dia], hw, tkT)
    du_lru, d_cw_lru, d_lru_b = _conv_bwd("lru_conv_bwd", dv, proj, p0, cw_lru, tt_seq, tc_seq)
    dzp, d_pool_scale = _pool_bwd_a("pool_scale_bwd", dy_pool, z_pool, pool_scale, tt_row)
    dd = _bd_bwd_x("pool_mix_bwd_x", [dzp], [W_pool], tm1)
    (dW_pool,) = _bd_bwd_w("pool_mix_bwd_w", d_pool, [dzp], gw, tkT)
    du_pool = _pool_bwd_b("pool_bwd", dd, gw, tt_pool)
    dproj = jnp.concatenate([du_pool, du_lru, du_gelu, dgl0, dgl1], axis=1)

    q_rows = lambda a, rows: a.reshape(N_SLOTS, rows // N_SLOTS, a.shape[-1])
    parts = [_to_slots_blockdiag(dW_pool.astype(BF16), G, gw),
             _to_slots_blockdiag(dW_a.astype(BF16), H, hw),
             _to_slots_blockdiag(dW_i.astype(BF16), H, hw),
             dW_pp, q_rows(dW_lp, R), q_rows(dW_out, D), dW_up, q_rows(dW_down, F)]
    from_sibling = _swap_partial_halves("swap_partial_halves", parts)
    chip_sums = [_add_halves(f"add_halves_{n}", p, r, pos) for n, p, r in zip(big[1:], parts, from_sibling)]
    dW_in, *from_chips = _matmul("proj_bwd_w", "tn", h, dproj, out_seg=N_SLOTS, out_dtype=BF16, tm=tD1, tn=tn_in,
                                 tk=tkT2, side=_exchange_side(chip_sums))
    (in_from_sibling,) = _swap_partial_halves("swap_partial_halves_w_in", [dW_in])
    in_chip_sum = _add_halves("add_halves_w_in", dW_in, in_from_sibling, pos)
    dh3, in_from_chips = _matmul("proj_bwd_x", "nt", dproj, W_in, out_dtype=F32, tm=tm1, tn=tD1, tk=n_in,
                                 side=_exchange_side([in_chip_sum]))
    grad_x, d_g_mix = _rms_bwd("rms1_bwd", dh3[0], xs, g_mix, dx1, tt_row, False)
    totals = [_add_chips(f"add_chips_{n}", s, r, pos)
              for n, s, r in zip(big, [in_chip_sum] + chip_sums, [in_from_chips] + from_chips)]
    full = _swap_reduced_halves(totals)
    grads = {n: f.reshape(weights[n].shape) for n, f in zip(big, full)}

    small = ["g_mix", "b_gate", "pool_scale", "lru_conv_b", "b_a", "b_i", "lru_lambda", "g_mlp", "ffn_conv_b", "g_final"]
    small_g = [d_g_mix, jnp.concatenate([d_bg0, d_bg1], axis=1), d_pool_scale, d_lru_b, d_b_a, d_b_i, d_lam, d_g_mlp,
               d_ffn_b, d_g_final]
    pieces = [g.reshape(-1) for g in small_g] + [d_cw_lru[:KC].reshape(-1), d_cw_ffn[:KF].reshape(-1), lossvec.reshape(-1)]
    sizes = [p.shape[0] for p in pieces]
    total = sum(sizes)
    rows = -(-total // (8 * LANES)) * 8
    pack = jnp.concatenate(pieces + [jnp.zeros((rows * LANES - total,), F32)]).reshape(rows, LANES)
    summed = _allreduce_small(pack).reshape(-1)
    offs = [0]
    for s in sizes:
        offs.append(offs[-1] + s)
    for k, n in enumerate(small):
        grads[n] = summed[offs[k]:offs[k + 1]].reshape(weights[n].shape)
    ns = len(small)
    g_cw_lru = summed[offs[ns]:offs[ns + 1]].reshape(KC, R)
    g_cw_ffn = summed[offs[ns + 1]:offs[ns + 2]].reshape(KF, F)
    grads["lru_conv_w"] = lax.dynamic_slice_in_dim(g_cw_lru, slot * (R // N_SLOTS), R // N_SLOTS, axis=1)[None]
    grads["ffn_conv_w"] = lax.dynamic_slice_in_dim(g_cw_ffn, slot * (F // N_SLOTS), F // N_SLOTS, axis=1)[None]
    loss = jnp.sum(summed[offs[ns + 2]:offs[ns + 3]]) * (0.5 / D)

    delta, new_m, new_v = {}, {}, {}
    for n in big:
        d_, m_, v_ = _adamw(f"adamw_{n}", _rows2d(weights[n][0]), _rows2d(grads[n][0]), _rows2d(mom_m[n][0]),
                            _rows2d(mom_v[n][0]))
        delta[n], new_m[n], new_v[n] = (t.reshape(weights[n].shape) for t in (d_, m_, v_))
    rest = small + ["lru_conv_w", "ffn_conv_w"]
    rsizes = [weights[n].size for n in rest]
    rtotal = sum(rsizes)
    rrows = -(-rtotal // (8 * LANES)) * 8

    def packed(tree):
        flat = [tree[n].reshape(-1) for n in rest] + [jnp.zeros((rrows * LANES - rtotal,), F32)]
        return jnp.concatenate(flat).reshape(rrows, LANES)

    d_, m_, v_ = _adamw("adamw_small", packed(weights), packed(grads), packed(mom_m), packed(mom_v))
    o = 0
    for n, s in zip(rest, rsizes):
        for tree, flat in ((delta, d_), (new_m, m_), (new_v, v_)):
            tree[n] = flat.reshape(-1)[o:o + s].reshape(weights[n].shape)
        o += s

    return (loss, grad_x.reshape(x.shape), *[grads[n] for n in order], *[delta[n] for n in order],
            *[new_m[n] for n in order], *[new_v[n] for n in order])
```

```python
import functools

import jax
import jax.numpy as jnp
from jax import lax
from jax.experimental import pallas as pl
from jax.experimental.pallas import tpu as pltpu

F32 = jnp.float32
BF16 = jnp.bfloat16
MESH = pl.DeviceIdType.MESH

VMEM_LIMIT_V7X = 56 * 1024 * 1024
LANES = 128
HALO = 16
N_SLOTS = 4
ADAMW_BLOCK_BYTES = 3 * 512 * 1024

EPS = 1e-6
LRU_C = 8.0
ADAM_LR = 0.001
ADAM_B1 = 0.9
ADAM_B2 = 0.999
ADAM_EPS = 1e-08
ADAM_WD = 0.01
ADAM_STEP = 10
GELU_C = 0.7978845608028654
GELU_A = 0.044715

SDS = jax.ShapeDtypeStruct
ANY = pl.BlockSpec(memory_space=pl.ANY)


def _pick(n, cands):
    for c in cands:
        if c <= n and n % c == 0:
            return c
    raise ValueError(f"no tile for {n} in {cands}")


def _cparams(**kw):
    return pltpu.CompilerParams(vmem_limit_bytes=VMEM_LIMIT_V7X, **kw)


def _sigmoid(x):
    return 1.0 / (1.0 + jnp.exp(-x))


def _gelu(x):
    return 0.5 * x * (1.0 + jnp.tanh(GELU_C * (x + GELU_A * x * x * x)))


def _gelu_and_grad(x):
    x2 = x * x
    th = jnp.tanh(GELU_C * x * (1.0 + GELU_A * x2))
    g = 0.5 * x * (1.0 + th)
    dg = 0.5 * (1.0 + th) + 0.5 * x * (1.0 - th * th) * (GELU_C * (1.0 + 3.0 * GELU_A * x2))
    return g, dg


def _lookback(ext, k):
    return pltpu.roll(ext, k, 0)[HALO:]


def _lookahead(ext, k, tt):
    return pltpu.roll(ext, ext.shape[0] - k, 0)[:tt]


def _colsum(v):
    return jnp.sum(v, axis=0, keepdims=True)


def _seg3(a):
    return a if a.ndim == 3 else a.reshape((1,) + a.shape)


def _seg_spec(shape3, br, bc, rowf, colf):
    assert shape3[1] % br == 0 and shape3[2] % bc == 0, (shape3, br, bc)
    nb = shape3[2] // bc

    def imap(i, j, k):
        cb = colf(i, j, k)
        return (cb // nb, rowf(i, j, k), cb % nb)

    return pl.BlockSpec((None, br, bc), imap)


_I = lambda i, j, k: i
_J = lambda i, j, k: j
_K = lambda i, j, k: k


class _Side:
    def __init__(self, ins, out_shapes, sem_shapes, start, mid, finish):
        self.ins, self.out_shapes, self.sem_shapes = list(ins), list(out_shapes), list(sem_shapes)
        self.start, self.mid, self.finish = start, mid, finish


def _ride(side, body, n_in, n_out, n_scr, grid):
    n_sin, n_sout = len(side.ins), len(side.out_shapes)
    n_steps = 1
    for g in grid:
        n_steps *= g

    def wrapped(*refs):
        p = 0
        ins, s_in = refs[p:p + n_in], refs[p + n_in:p + n_in + n_sin]
        p += n_in + n_sin
        outs, s_out = refs[p:p + n_out], refs[p + n_out:p + n_out + n_sout]
        p += n_out + n_sout
        scr, s_sem = refs[p:p + n_scr], refs[p + n_scr:]
        step = 0
        for d, g in enumerate(grid):
            step = step * g + pl.program_id(d)

        @pl.when(step == 0)
        def _():
            side.start(s_in, s_out, s_sem)

        body(*ins, *outs, *scr)

        if side.mid is not None:
            @pl.when(step == (2 * n_steps) // 3)
            def _():
                side.mid(s_in, s_out, s_sem)

        @pl.when(step == n_steps - 1)
        def _():
            side.finish(s_in, s_out, s_sem)

    return wrapped


def _ride_call(side, body, *, name, grid, in_specs, args, out_specs, out_shape, scratch, semantics):
    out_specs, out_shape = list(out_specs), list(out_shape)
    if side is not None:
        body = _ride(side, body, len(args), len(out_shape), len(scratch), grid)
        in_specs = list(in_specs) + [ANY] * len(side.ins)
        args = list(args) + side.ins
        out_specs = out_specs + [ANY] * len(side.out_shapes)
        out_shape = out_shape + side.out_shapes
        scratch = list(scratch) + side.sem_shapes
        semantics = ("arbitrary",) * len(grid)
    return pl.pallas_call(
        body, name=name, grid=grid, in_specs=in_specs, out_specs=out_specs, out_shape=out_shape,
        scratch_shapes=scratch, compiler_params=_cparams(dimension_semantics=semantics),
    )(*args)


def _matmul(name, mode, a, b, *, out_seg=1, out_dtype, tm, tn, tk, res=None, side=None):
    a3, b3 = _seg3(a), _seg3(b)
    if mode == "nn":
        M, K, N = a3.shape[1], a3.shape[0] * a3.shape[2], b3.shape[0] * b3.shape[2]
        assert b3.shape[1] == K
        a_spec = _seg_spec(a3.shape, tm, tk, _I, _K)
        b_spec = _seg_spec(b3.shape, tk, tn, _K, _J)
        dims = (((1,), (0,)), ((), ()))
    elif mode == "nt":
        M, K, N = a3.shape[1], a3.shape[0] * a3.shape[2], b3.shape[1]
        assert b3.shape[0] * b3.shape[2] == K
        a_spec = _seg_spec(a3.shape, tm, tk, _I, _K)
        b_spec = _seg_spec(b3.shape, tn, tk, _J, _K)
        dims = (((1,), (1,)), ((), ()))
    else:
        K, M, N = a3.shape[1], a3.shape[0] * a3.shape[2], b3.shape[0] * b3.shape[2]
        assert b3.shape[1] == K
        a_spec = _seg_spec(a3.shape, tk, tm, _K, _I)
        b_spec = _seg_spec(b3.shape, tk, tn, _K, _J)
        dims = (((0,), (0,)), ((), ()))
    out3 = (out_seg, M, N // out_seg)
    o_spec = _seg_spec(out3, tm, tn, _I, _J)
    grid = (M // tm, N // tn, K // tk)
    nk = grid[2]
    has_res = res is not None

    def body(*refs):
        a_ref, b_ref = refs[0], refs[1]
        r_ref = refs[2] if has_res else None
        o_ref = refs[3] if has_res else refs[2]
        acc_ref = refs[-1] if nk > 1 else None
        prod = lax.dot_general(a_ref[...], b_ref[...], dims, preferred_element_type=F32)

        def finish(val):
            if has_res:
                val = val + r_ref[...]
            o_ref[...] = val.astype(o_ref.dtype)

        if nk == 1:
            finish(prod)
        else:
            k = pl.program_id(2)

            @pl.when(k == 0)
            def _():
                acc_ref[...] = prod

            @pl.when(k > 0)
            def _():
                acc_ref[...] += prod

            @pl.when(k == nk - 1)
            def _():
                finish(acc_ref[...])

    in_specs = [a_spec, b_spec]
    args = [a3, b3]
    if has_res:
        in_specs.append(pl.BlockSpec((tm, tn), lambda i, j, k: (i, j)))
        args.append(res)
    outs = _ride_call(side, body, name=name, grid=grid, in_specs=in_specs, args=args, out_specs=[o_spec],
                      out_shape=[SDS(out3, out_dtype)], scratch=[pltpu.VMEM((tm, tn), F32)] if nk > 1 else [],
                      semantics=("parallel", "parallel", "arbitrary"))
    return outs if side is not None else outs[0]


def _bd_fwd(name, a, ws, tm, scale=None):
    T = a.shape[0]
    H, w, _ = ws[0].shape
    nw = len(ws)
    has_scale = scale is not None

    def body(*refs):
        a_ref = refs[0]
        w_refs = refs[1:1 + nw]
        pos = 1 + nw
        s_ref = refs[pos] if has_scale else None
        o_refs = refs[pos + (1 if has_scale else 0):]
        av = a_ref[...]
        for l in range(nw):
            r = jnp.dot(av, w_refs[l][...], preferred_element_type=F32)
            o_refs[l][...] = r
            if has_scale and l == 0:
                o_refs[nw][...] = (r * s_ref[...]).astype(BF16)

    blk = pl.BlockSpec((tm, w), lambda h, i: (i, h))
    wspec = pl.BlockSpec((None, w, w), lambda h, i: (h, 0, 0))
    in_specs = [blk] + [wspec] * nw
    args = [a] + list(ws)
    out_shape = [SDS((T, H * w), F32)] * nw
    out_specs = [blk] * nw
    if has_scale:
        in_specs.append(pl.BlockSpec((1, w), lambda h, i: (0, h)))
        args.append(scale)
        out_shape = out_shape + [SDS((T, H * w), BF16)]
        out_specs = out_specs + [blk]
    return pl.pallas_call(
        body, name=name, grid=(H, T // tm), in_specs=in_specs, out_specs=out_specs, out_shape=out_shape,
        compiler_params=_cparams(dimension_semantics=("parallel", "parallel")),
    )(*args)


def _bd_bwd_x(name, dys, ws, tm, res=None):
    T = dys[0].shape[0]
    H, w, _ = ws[0].shape
    nw = len(ws)
    has_res = res is not None

    def body(*refs):
        d_refs = refs[:nw]
        w_refs = refs[nw:2 * nw]
        r_ref = refs[2 * nw] if has_res else None
        o_ref = refs[-1]
        acc = None
        for l in range(nw):
            p = lax.dot_general(d_refs[l][...], w_refs[l][...], (((1,), (1,)), ((), ())),
                                preferred_element_type=F32)
            acc = p if acc is None else acc + p
        if has_res:
            acc = acc + r_ref[...]
        o_ref[...] = acc

    blk = pl.BlockSpec((tm, w), lambda h, i: (i, h))
    wspec = pl.BlockSpec((None, w, w), lambda h, i: (h, 0, 0))
    in_specs = [blk] * nw + [wspec] * nw + ([blk] if has_res else [])
    args = list(dys) + list(ws) + ([res] if has_res else [])
    return pl.pallas_call(
        body, name=name, grid=(H, T // tm), in_specs=in_specs, out_specs=blk, out_shape=SDS((T, H * w), F32),
        compiler_params=_cparams(dimension_semantics=("parallel", "parallel")),
    )(*args)


def _bd_bwd_w(name, a, dys, w, tk):
    T, HW = a.shape
    H = HW // w
    nw = len(dys)

    def body(*refs):
        a_ref = refs[0]
        d_refs = refs[1:1 + nw]
        o_refs = refs[1 + nw:]
        k = pl.program_id(1)
        av = a_ref[...]
        for l in range(nw):
            p = lax.dot_general(av, d_refs[l][...], (((0,), (0,)), ((), ())), preferred_element_type=F32)

            @pl.when(k == 0)
            def _(p=p, l=l):
                o_refs[l][...] = p

            @pl.when(k > 0)
            def _(p=p, l=l):
                o_refs[l][...] += p

    blk = pl.BlockSpec((tk, w), lambda h, k: (k, h))
    ospec = pl.BlockSpec((None, w, w), lambda h, k: (h, 0, 0))
    return pl.pallas_call(
        body, name=name, grid=(H, T // tk), in_specs=[blk] * (1 + nw), out_specs=[ospec] * nw,
        out_shape=[SDS((H, w, w), F32)] * nw,
        compiler_params=_cparams(dimension_semantics=("parallel", "arbitrary")),
    )(a, *dys)


def _rms_fwd(name, x, g, tt):
    T, D = x.shape

    def body(x_ref, g_ref, h_ref):
        xv = x_ref[...]
        rstd = lax.rsqrt(jnp.mean(xv * xv, axis=-1, keepdims=True) + EPS)
        h_ref[...] = (xv * rstd * g_ref[...]).astype(BF16)

    row = pl.BlockSpec((tt, D), lambda i: (i, 0))
    vec = pl.BlockSpec((1, D), lambda i: (0, 0))
    return pl.pallas_call(
        body, name=name, grid=(T // tt,), in_specs=[row, vec], out_specs=row, out_shape=SDS((T, D), BF16),
        compiler_params=_cparams(dimension_semantics=("parallel",)),
    )(x, g)


def _rms_bwd(name, dh, x, g, dres, tt, want_bf16):
    T, D = x.shape

    def body(dh_ref, x_ref, g_ref, dres_ref, *outs):
        i = pl.program_id(0)
        dx_ref = outs[0]
        dg_ref = outs[-1]
        xv = x_ref[...]
        rstd = lax.rsqrt(jnp.mean(xv * xv, axis=-1, keepdims=True) + EPS)
        xn = xv * rstd
        dhv = dh_ref[...]
        dyg = dhv * g_ref[...]
        dx = dres_ref[...] + rstd * (dyg - xn * jnp.mean(dyg * xn, axis=-1, keepdims=True))
        dx_ref[...] = dx
        if want_bf16:
            outs[1][...] = dx.astype(BF16)

        @pl.when(i == 0)
        def _():
            dg_ref[...] = jnp.zeros_like(dg_ref)

        dg_ref[...] += _colsum(dhv * xn)

    row = pl.BlockSpec((tt, D), lambda i: (i, 0))
    vec = pl.BlockSpec((1, D), lambda i: (0, 0))
    out_shape = [SDS((T, D), F32)] + ([SDS((T, D), BF16)] if want_bf16 else []) + [SDS((1, D), F32)]
    out_specs = [row] + ([row] if want_bf16 else []) + [vec]
    return pl.pallas_call(
        body, name=name, grid=(T // tt,), in_specs=[row, row, vec, row], out_specs=out_specs, out_shape=out_shape,
        compiler_params=_cparams(dimension_semantics=("arbitrary",)),
    )(dh, x, g, dres)


def _final_loss_bwd(name, x2, tgt, g, tt):
    T, D = x2.shape

    def body(x_ref, t_ref, g_ref, dx_ref, dxb_ref, dg_ref, lv_ref):
        i = pl.program_id(0)
        xv = x_ref[...]
        gv = g_ref[...]
        rstd = lax.rsqrt(jnp.mean(xv * xv, axis=-1, keepdims=True) + EPS)
        xn = xv * rstd
        e = xn * gv - t_ref[...]
        dy = e * (1.0 / D)
        dyg = dy * gv
        dx = rstd * (dyg - xn * jnp.mean(dyg * xn, axis=-1, keepdims=True))
        dx_ref[...] = dx
        dxb_ref[...] = dx.astype(BF16)

        @pl.when(i == 0)
        def _():
            dg_ref[...] = jnp.zeros_like(dg_ref)
            lv_ref[...] = jnp.zeros_like(lv_ref)

        dg_ref[...] += _colsum(dy * xn)
        lv_ref[...] += _colsum(e * e)

    row = pl.BlockSpec((tt, D), lambda i: (i, 0))
    vec = pl.BlockSpec((1, D), lambda i: (0, 0))
    return pl.pallas_call(
        body, name=name, grid=(T // tt,), in_specs=[row, row, vec], out_specs=[row, row, vec, vec],
        out_shape=[SDS((T, D), F32), SDS((T, D), BF16), SDS((1, D), F32), SDS((1, D), F32)],
        compiler_params=_cparams(dimension_semantics=("arbitrary",)),
    )(x2, tgt, g)


def _blk(tt, tc, off=0):
    return pl.BlockSpec((tt, tc), lambda j, i: (i, j + off))


def _vec(r, tc, off=0):
    return pl.BlockSpec((r, tc), lambda j, i: (0, j + off))


def _prev_halo(tt, tc, off=0):
    r = tt // HALO
    return pl.BlockSpec((HALO, tc), lambda j, i: (jnp.maximum(i * r - 1, 0), j + off))


def _next_halo(tt, tc, T, off=0):
    r = tt // HALO
    last = T // HALO - 1
    return pl.BlockSpec((HALO, tc), lambda j, i: (jnp.minimum((i + 1) * r, last), j + off))


def _seq_params():
    return _cparams(dimension_semantics=("parallel", "arbitrary"))


def _pool_fwd(name, proj, Cp, gw, tt):
    T = proj.shape[0]
    G = Cp // gw
    assert G == 4

    def body(u_ref, halo_ref, d_ref):
        j, i = pl.program_id(0), pl.program_id(1)
        u = u_ref[...]
        halo = jnp.where(i > 0, halo_ref[...], 0.0)
        ext = jnp.concatenate([halo, u], axis=0)
        e2 = ext + pltpu.roll(ext, 1, 0)
        e4 = e2 + pltpu.roll(e2, 2, 0)
        e8 = e4 + pltpu.roll(e4, 4, 0)
        e16 = e8 + pltpu.roll(e8, 8, 0)
        s = jnp.where(j == 0, e2, jnp.where(j == 1, e4, jnp.where(j == 2, e8, e16)))[HALO:]
        t1 = (i * tt + 1 + lax.broadcasted_iota(jnp.int32, (tt, 1), 0)).astype(F32)
        win = lax.shift_left(jnp.int32(2), j).astype(F32)
        d_ref[...] = (s / jnp.minimum(t1, win) - u).astype(BF16)

    return pl.pallas_call(
        body, name=name, grid=(G, T // tt), in_specs=[_blk(tt, gw), _prev_halo(tt, gw)], out_specs=_blk(tt, gw),
        out_shape=SDS((T, Cp), BF16), compiler_params=_seq_params(),
    )(proj, proj)


def _pool_bwd_a(name, dy, z, scale, tt):
    T, Cp = dy.shape

    def body(dy_ref, z_ref, s_ref, dz_ref, ds_ref):
        i = pl.program_id(0)
        dyv = dy_ref[...]
        dz_ref[...] = (dyv * s_ref[...]).astype(BF16)

        @pl.when(i == 0)
        def _():
            ds_ref[...] = jnp.zeros_like(ds_ref)

        ds_ref[...] += _colsum(dyv * z_ref[...])

    row = pl.BlockSpec((tt, Cp), lambda i: (i, 0))
    vec = pl.BlockSpec((1, Cp), lambda i: (0, 0))
    return pl.pallas_call(
        body, name=name, grid=(T // tt,), in_specs=[row, row, vec], out_specs=[row, vec],
        out_shape=[SDS((T, Cp), BF16), SDS((1, Cp), F32)],
        compiler_params=_cparams(dimension_semantics=("arbitrary",)),
    )(dy, z, scale)


def _pool_bwd_b(name, dd, gw, tt):
    T, Cp = dd.shape
    G = Cp // gw
    nT = T // tt

    def body(d_ref, nxt_ref, du_ref):
        j, i = pl.program_id(0), pl.program_id(1)
        win = lax.shift_left(jnp.int32(2), j).astype(F32)
        dv = d_ref[...]
        t1 = (i * tt + 1 + lax.broadcasted_iota(jnp.int32, (tt + HALO, 1), 0)).astype(F32)
        nxt = jnp.where(i < nT - 1, nxt_ref[...], 0.0)
        ext = jnp.concatenate([dv, nxt], axis=0) / jnp.minimum(t1, win)
        n = tt + HALO
        f2 = ext + pltpu.roll(ext, n - 1, 0)
        f4 = f2 + pltpu.roll(f2, n - 2, 0)
        f8 = f4 + pltpu.roll(f4, n - 4, 0)
        f16 = f8 + pltpu.roll(f8, n - 8, 0)
        s = jnp.where(j == 0, f2, jnp.where(j == 1, f4, jnp.where(j == 2, f8, f16)))[:tt]
        du_ref[...] = (s - dv).astype(BF16)

    return pl.pallas_call(
        body, name=name, grid=(G, nT), in_specs=[_blk(tt, gw), _next_halo(tt, gw, T)], out_specs=_blk(tt, gw),
        out_shape=SDS((T, Cp), BF16), compiler_params=_seq_params(),
    )(dd, dd)


def _conv_fwd(name, src, off_cols, w, b, tt, tc):
    T = src.shape[0]
    K, C = w.shape
    off = off_cols // tc

    def body(u_ref, halo_ref, w_ref, b_ref, v_ref, vb_ref):
        i = pl.program_id(1)
        u = u_ref[...]
        halo = jnp.where(i > 0, halo_ref[...], 0.0)
        ext = jnp.concatenate([halo, u], axis=0)
        wv = w_ref[...]
        acc = b_ref[...] + u * wv[K - 1:K]
        for s in range(1, K):
            acc = acc + _lookback(ext, s) * wv[K - 1 - s:K - s]
        v_ref[...] = acc
        vb_ref[...] = acc.astype(BF16)

    return pl.pallas_call(
        body, name=name, grid=(C // tc, T // tt),
        in_specs=[_blk(tt, tc, off), _prev_halo(tt, tc, off), _vec(K, tc), _vec(1, tc)],
        out_specs=[_blk(tt, tc), _blk(tt, tc)], out_shape=[SDS((T, C), F32), SDS((T, C), BF16)],
        compiler_params=_seq_params(),
    )(src, src, w, b)


def _conv_bwd(name, dv, src, off_cols, w, tt, tc):
    T, C = dv.shape
    K = w.shape[0]
    off = off_cols // tc
    nT = T // tt

    def body(dv_ref, nxt_ref, u_ref, halo_ref, w_ref, du_ref, dw_ref, db_ref):
        i = pl.program_id(1)
        d = dv_ref[...]
        nxt = jnp.where(i < nT - 1, nxt_ref[...], 0.0)
        dext = jnp.concatenate([d, nxt], axis=0)
        u = u_ref[...]
        halo = jnp.where(i > 0, halo_ref[...], 0.0)
        uext = jnp.concatenate([halo, u], axis=0)
        wv = w_ref[...]
        du = d * wv[K - 1:K]
        rows = [_colsum(d * u)]
        for s in range(1, K):
            du = du + _lookahead(dext, s, tt) * wv[K - 1 - s:K - s]
            rows.append(_colsum(d * _lookback(uext, s)))
        du_ref[...] = du.astype(BF16)
        dw = jnp.concatenate(rows[::-1] + [jnp.zeros((8 - K, tc), F32)], axis=0)

        @pl.when(i == 0)
        def _():
            dw_ref[...] = jnp.zeros_like(dw_ref)
            db_ref[...] = jnp.zeros_like(db_ref)

        dw_ref[...] += dw
        db_ref[...] += _colsum(d)

    return pl.pallas_call(
        body, name=name, grid=(C // tc, nT),
        in_specs=[_blk(tt, tc), _next_halo(tt, tc, T), _blk(tt, tc, off), _prev_halo(tt, tc, off), _vec(K, tc)],
        out_specs=[_blk(tt, tc), _vec(8, tc), _vec(1, tc)],
        out_shape=[SDS((T, C), BF16), SDS((8, C), F32), SDS((1, C), F32)],
        compiler_params=_seq_params(),
    )(dv, dv, src, src, w)


def _lru_gates(ra, ia, v, ba, bi, lam):
    r = _sigmoid(ra + ba)
    ig = _sigmoid(ia + bi)
    nl = -lam
    ex = jnp.exp(-jnp.abs(nl))
    one_p = 1.0 + ex
    l1p = jnp.where(one_p == 1.0, ex, jnp.log(one_p) * ex / (one_p - 1.0))
    sp = jnp.maximum(nl, 0.0) + l1p
    a = jnp.exp(-LRU_C * r * sp)
    mult = jnp.sqrt(1.0 - a * a)
    return r, ig, sp, a, mult


def _lru_fwd(name, ra_pre, ia_pre, v, proj, off_cols, ba, bi, lam, tt, tc, side=None):
    T, R = v.shape
    off = off_cols // tc

    def body(ra_ref, ia_ref, v_ref, ug_ref, ba_ref, bi_ref, lam_ref, h_ref, y_ref, carry_ref):
        i = pl.program_id(1)

        @pl.when(i == 0)
        def _():
            carry_ref[...] = jnp.zeros_like(carry_ref)

        vv = v_ref[...]
        _, ig, _, a, mult = _lru_gates(ra_ref[...], ia_ref[...], vv, ba_ref[...], bi_ref[...], lam_ref[...])
        A = a
        B = mult * (ig * vv)
        row = lax.broadcasted_iota(jnp.int32, (tt, 1), 0)
        s = 1
        while s < tt:
            m = row >= s
            B = jnp.where(m, A * pltpu.roll(B, s, 0) + B, B)
            A = jnp.where(m, A * pltpu.roll(A, s, 0), A)
            s *= 2
        h = A * carry_ref[7:8, :] + B
        h_ref[...] = h
        carry_ref[...] = h[tt - 8:, :]
        y_ref[...] = (h * _gelu(ug_ref[...])).astype(BF16)

    return _ride_call(
        side, body, name=name, grid=(R // tc, T // tt),
        in_specs=[_blk(tt, tc), _blk(tt, tc), _blk(tt, tc), _blk(tt, tc, off), _vec(1, tc), _vec(1, tc), _vec(1, tc)],
        args=[ra_pre, ia_pre, v, proj, ba, bi, lam],
        out_specs=[_blk(tt, tc), _blk(tt, tc)], out_shape=[SDS((T, R), F32), SDS((T, R), BF16)],
        scratch=[pltpu.VMEM((8, tc), F32)], semantics=("parallel", "arbitrary"))


def _lru_bwd(name, dy, hs, proj, off_cols, ra_pre, ia_pre, v, ba, bi, lam, tt, tc):
    T, R = v.shape
    off = off_cols // tc
    nT = T // tt
    r16 = tt // HALO

    def rblk(o=0):
        return pl.BlockSpec((tt, tc), lambda j, i: (nT - 1 - i, j + o))

    hprev = pl.BlockSpec((HALO, tc), lambda j, i: (jnp.maximum((nT - 1 - i) * r16 - 1, 0), j))

    def body(dy_ref, h_ref, hp_ref, ug_ref, ra_ref, ia_ref, v_ref, ba_ref, bi_ref, lam_ref,
             dra_ref, dia_ref, dv_ref, dug_ref, dba_ref, dbi_ref, dlam_ref, dh_carry, a_carry):
        i = pl.program_id(1)

        @pl.when(i == 0)
        def _():
            dh_carry[...] = jnp.zeros_like(dh_carry)
            a_carry[...] = jnp.zeros_like(a_carry)
            dba_ref[...] = jnp.zeros_like(dba_ref)
            dbi_ref[...] = jnp.zeros_like(dbi_ref)
            dlam_ref[...] = jnp.zeros_like(dlam_ref)

        vv = v_ref[...]
        lamv = lam_ref[...]
        r, ig, sp, a, mult = _lru_gates(ra_ref[...], ia_ref[...], vv, ba_ref[...], bi_ref[...], lamv)
        h = h_ref[...]
        gel, dgel = _gelu_and_grad(ug_ref[...])
        dyv = dy_ref[...]
        dug_ref[...] = (dyv * h * dgel).astype(BF16)
        dhs = dyv * gel

        a_ext = jnp.concatenate([a, jnp.broadcast_to(a_carry[0:1, :], (HALO, tc))], axis=0)
        A = _lookahead(a_ext, 1, tt)
        B = dhs
        row = lax.broadcasted_iota(jnp.int32, (tt, 1), 0)
        s = 1
        while s < tt:
            m = row < tt - s
            B = jnp.where(m, B + A * pltpu.roll(B, tt - s, 0), B)
            A = jnp.where(m, A * pltpu.roll(A, tt - s, 0), A)
            s *= 2
        dH = B + A * dh_carry[0:1, :]
        dh_carry[...] = dH[:8, :]
        a_carry[...] = a[:8, :]

        hp = jnp.where(i < nT - 1, hp_ref[...], 0.0)
        h_prev = _lookback(jnp.concatenate([hp, h], axis=0), 1)
        da = dH * h_prev
        dmult = dH * (ig * vv)
        div = dH * mult
        dv_ref[...] = div * ig
        dig = div * vv
        dlog_a = da * a - dmult * (a * a) / mult
        dr = dlog_a * (-LRU_C * sp)
        dra = dr * r * (1.0 - r)
        dia = dig * ig * (1.0 - ig)
        dra_ref[...] = dra.astype(BF16)
        dia_ref[...] = dia.astype(BF16)
        dba_ref[...] += _colsum(dra)
        dbi_ref[...] += _colsum(dia)
        dlam_ref[...] += _colsum(dlog_a * (-LRU_C * r))

        @pl.when(i == nT - 1)
        def _():
            dlam_ref[...] = dlam_ref[...] * (-_sigmoid(-lamv))

    return pl.pallas_call(
        body, name=name, grid=(R // tc, nT),
        in_specs=[rblk(), rblk(), hprev, rblk(off), rblk(), rblk(), rblk(), _vec(1, tc), _vec(1, tc), _vec(1, tc)],
        out_specs=[rblk(), rblk(), rblk(), rblk(), _vec(1, tc), _vec(1, tc), _vec(1, tc)],
        out_shape=[SDS((T, R), BF16), SDS((T, R), BF16), SDS((T, R), F32), SDS((T, R), BF16),
                   SDS((1, R), F32), SDS((1, R), F32), SDS((1, R), F32)],
        scratch_shapes=[pltpu.VMEM((8, tc), F32), pltpu.VMEM((8, tc), F32)], compiler_params=_seq_params(),
    )(dy, hs, hs, proj, ra_pre, ia_pre, v, ba, bi, lam)


def _merge_fwd(name, proj, off_cols, b_gate, P, Q, tt, tc):
    T, D = P.shape
    o0 = off_cols // tc
    o1 = o0 + D // tc

    def body(g0_ref, g1_ref, b0_ref, b1_ref, p_ref, q_ref, m_ref):
        g0 = _sigmoid(g0_ref[...] + b0_ref[...])
        g1 = _sigmoid(g1_ref[...] + b1_ref[...])
        m_ref[...] = (g0 * p_ref[...] + g1 * q_ref[...]).astype(BF16)

    return pl.pallas_call(
        body, name=name, grid=(D // tc, T // tt),
        in_specs=[_blk(tt, tc, o0), _blk(tt, tc, o1), _vec(1, tc), _vec(1, tc, D // tc), _blk(tt, tc), _blk(tt, tc)],
        out_specs=_blk(tt, tc), out_shape=SDS((T, D), BF16), compiler_params=_seq_params(),
    )(proj, proj, b_gate, b_gate, P, Q)


def _merge_bwd(name, dm, proj, off_cols, b_gate, P, Q, tt, tc):
    T, D = P.shape
    o0 = off_cols // tc
    o1 = o0 + D // tc

    def body(dm_ref, g0_ref, g1_ref, b0_ref, b1_ref, p_ref, q_ref, dp_ref, dq_ref, dl0_ref, dl1_ref, db0_ref, db1_ref):
        i = pl.program_id(1)
        g0 = _sigmoid(g0_ref[...] + b0_ref[...])
        g1 = _sigmoid(g1_ref[...] + b1_ref[...])
        d = dm_ref[...].astype(F32)
        dp_ref[...] = (d * g0).astype(BF16)
        dq_ref[...] = (d * g1).astype(BF16)
        dl0 = d * p_ref[...] * g0 * (1.0 - g0)
        dl1 = d * q_ref[...] * g1 * (1.0 - g1)
        dl0_ref[...] = dl0.astype(BF16)
        dl1_ref[...] = dl1.astype(BF16)

        @pl.when(i == 0)
        def _():
            db0_ref[...] = jnp.zeros_like(db0_ref)
            db1_ref[...] = jnp.zeros_like(db1_ref)

        db0_ref[...] += _colsum(dl0)
        db1_ref[...] += _colsum(dl1)

    b = _blk(tt, tc)
    return pl.pallas_call(
        body, name=name, grid=(D // tc, T // tt),
        in_specs=[b, _blk(tt, tc, o0), _blk(tt, tc, o1), _vec(1, tc), _vec(1, tc, D // tc), b, b],
        out_specs=[b, b, b, b, _vec(1, tc), _vec(1, tc)],
        out_shape=[SDS((T, D), BF16)] * 4 + [SDS((1, D), F32)] * 2, compiler_params=_seq_params(),
    )(dm, proj, proj, b_gate, b_gate, P, Q)


def _ffn_act_fwd(name, up, w, b, tt, tc):
    T = up.shape[0]
    K, F = w.shape
    nF = F // tc

    def body(gp_ref, halo_ref, val_ref, w_ref, b_ref, z_ref):
        i = pl.program_id(1)
        gp = gp_ref[...].astype(F32)
        halo = jnp.where(i > 0, halo_ref[...].astype(F32), 0.0)
        ext = jnp.concatenate([halo, gp], axis=0)
        wv = w_ref[...]
        c = b_ref[...] + gp * wv[K - 1:K]
        for s in range(1, K):
            c = c + _lookback(ext, s) * wv[K - 1 - s:K - s]
        z_ref[...] = (_gelu(c) * val_ref[...].astype(F32)).astype(BF16)

    return pl.pallas_call(
        body, name=name, grid=(nF, T // tt),
        in_specs=[_blk(tt, tc), _prev_halo(tt, tc), _blk(tt, tc, nF), _vec(K, tc), _vec(1, tc)],
        out_specs=_blk(tt, tc), out_shape=SDS((T, F), BF16), compiler_params=_seq_params(),
    )(up, up, up, w, b)


def _ffn_act_bwd(name, up, dz, w, b, tt, tc):
    T = up.shape[0]
    K, F = w.shape
    nF = F // tc
    nT = T // tt

    def body(gp_ref, gph_ref, gpn_ref, val_ref, valn_ref, dz_ref, dzn_ref, w_ref, b_ref, dup_ref, dw_ref, db_ref):
        i = pl.program_id(1)
        wv = w_ref[...]
        gp = gp_ref[...].astype(F32)
        halo = jnp.where(i > 0, gph_ref[...].astype(F32), 0.0)
        full = jnp.concatenate([halo, gp, gpn_ref[...].astype(F32)], axis=0)
        c = b_ref[...] + full * wv[K - 1:K]
        for s in range(1, K):
            c = c + pltpu.roll(full, s, 0) * wv[K - 1 - s:K - s]
        c = c[HALO:]
        gate, dgate = _gelu_and_grad(c)
        dz_e = jnp.concatenate([dz_ref[...], dzn_ref[...]], axis=0).astype(F32)
        val_e = jnp.concatenate([val_ref[...], valn_ref[...]], axis=0).astype(F32)
        dc_e = dz_e * val_e * dgate
        row = lax.broadcasted_iota(jnp.int32, (tt + HALO, 1), 0)
        dc_e = jnp.where((row < tt) | (i < nT - 1), dc_e, 0.0)
        dc = dc_e[:tt]
        dgp = dc * wv[K - 1:K]
        uext = full[:tt + HALO]
        rows = [_colsum(dc * gp)]
        for s in range(1, K):
            dgp = dgp + _lookahead(dc_e, s, tt) * wv[K - 1 - s:K - s]
            rows.append(_colsum(dc * _lookback(uext, s)))
        dup_ref[0] = dgp.astype(BF16)
        dup_ref[1] = (dz_e[:tt] * gate[:tt]).astype(BF16)
        dw = jnp.concatenate(rows[::-1] + [jnp.zeros((8 - K, tc), F32)], axis=0)

        @pl.when(i == 0)
        def _():
            dw_ref[...] = jnp.zeros_like(dw_ref)
            db_ref[...] = jnp.zeros_like(db_ref)

        dw_ref[...] += dw
        db_ref[...] += _colsum(dc)

    return pl.pallas_call(
        body, name=name, grid=(nF, nT),
        in_specs=[_blk(tt, tc), _prev_halo(tt, tc), _next_halo(tt, tc, T), _blk(tt, tc, nF), _next_halo(tt, tc, T, nF),
                  _blk(tt, tc), _next_halo(tt, tc, T), _vec(K, tc), _vec(1, tc)],
        out_specs=[pl.BlockSpec((2, tt, tc), lambda j, i: (0, i, j)), _vec(8, tc), _vec(1, tc)],
        out_shape=[SDS((2, T, F), BF16), SDS((8, F), F32), SDS((1, F), F32)],
        compiler_params=_seq_params(),
    )(up, up, up, up, up, dz, dz, w, b)


def _adamw(name, w, g, m, v):
    R, C = w.shape
    tr = R
    if R * C * 4 > ADAMW_BLOCK_BYTES:
        tr = _pick(R, [t for t in (256, 128, 64, 32, 16, 8) if t * C * 4 <= ADAMW_BLOCK_BYTES])
    c1 = 1.0 - ADAM_B1 ** ADAM_STEP
    c2 = 1.0 - ADAM_B2 ** ADAM_STEP

    def body(w_ref, g_ref, m_ref, v_ref, d_ref, nm_ref, nv_ref):
        gv = g_ref[...]
        nm = ADAM_B1 * m_ref[...] + (1.0 - ADAM_B1) * gv
        nv = ADAM_B2 * v_ref[...] + (1.0 - ADAM_B2) * (gv * gv)
        nm_ref[...] = nm
        nv_ref[...] = nv
        d_ref[...] = -ADAM_LR * ((nm / c1) / (jnp.sqrt(nv / c2) + ADAM_EPS) + ADAM_WD * w_ref[...])

    spec = pl.BlockSpec((tr, C), lambda i: (i, 0))
    return pl.pallas_call(
        body, name=name, grid=(R // tr,), in_specs=[spec] * 4, out_specs=[spec] * 3,
        out_shape=[SDS((R, C), F32)] * 3, compiler_params=_cparams(dimension_semantics=("parallel",)),
    )(w, g, m, v)


def _add_halves(name, part, rcv, pos):
    S, R, C = part.shape
    h = R // 2
    tr = _pick(h, (512, 256, 128, 64, 32, 16))
    nb = h // tr

    def body(pos_ref, p_ref, r_ref, o_ref):
        o_ref[...] = (p_ref[...].astype(F32) + r_ref[...].astype(F32)).astype(BF16)

    return pl.pallas_call(
        body, name=name,
        grid_spec=pltpu.PrefetchScalarGridSpec(
            num_scalar_prefetch=1, grid=(S, nb),
            in_specs=[pl.BlockSpec((None, tr, C), lambda s, r, pos: (s, pos[0] * nb + r, 0)),
                      pl.BlockSpec((None, tr, C), lambda s, r, pos: (s, r, 0))],
            out_specs=pl.BlockSpec((None, tr, C), lambda s, r, pos: (s, r, 0))),
        out_shape=SDS((S, h, C), BF16), compiler_params=_cparams(dimension_semantics=("parallel", "parallel")),
    )(pos, part, rcv)


def _add_chips(name, chipsum, rcv, pos):
    S, h, C = chipsum.shape
    tr = _pick(h, (512, 256, 128, 64, 32, 16))

    def body(pos_ref, own_ref, r_ref, o_ref):
        o_ref[...] = ((own_ref[...].astype(F32) + r_ref[0].astype(F32)) + r_ref[1].astype(F32)) + r_ref[2].astype(F32)

    return pl.pallas_call(
        body, name=name,
        grid_spec=pltpu.PrefetchScalarGridSpec(
            num_scalar_prefetch=1, grid=(h // tr,),
            in_specs=[pl.BlockSpec((None, tr, C), lambda r, pos: (pos[1], r, 0)),
                      pl.BlockSpec((3, tr, C), lambda r, pos: (0, r, 0))],
            out_specs=pl.BlockSpec((None, tr, C), lambda r, pos: (pos[0], r, 0))),
        out_shape=SDS((2, h, C), F32), compiler_params=_cparams(dimension_semantics=("parallel",)),
    )(pos, chipsum, rcv)


def _place():
    x, y, c = lax.axis_index("x"), lax.axis_index("y"), lax.axis_index("c")
    chips = [(1 - x, y), (x, 1 - y), (1 - x, 1 - y)]
    slots = [2 * cx + cy for cx, cy in chips]
    return x, y, c, chips, slots


def _gather_weights(shards, split):
    side = _gather_side(shards, split)
    n = len(shards)

    def body(*refs):
        ins, outs, sems = refs[:n], refs[n:2 * n], refs[2 * n:]
        side.start(ins, outs, sems)
        side.mid(ins, outs, sems)
        side.finish(ins, outs, sems)

    return pl.pallas_call(
        body, name="gather_weights", in_specs=[ANY] * n, out_specs=[ANY] * n, out_shape=side.out_shapes,
        scratch_shapes=side.sem_shapes, compiler_params=_cparams(has_side_effects=True),
    )(*shards)


def _gather_side(shards, split):
    n = len(shards)

    def copies(ins, outs, sems):
        send_sems, recv_sems, local_sems = sems
        x, y, c, chips, slots = _place()
        me = 2 * x + y
        sibling = (x, y, 1 - c)

        def rows(a, half):
            hrows = ins[a].shape[0] // 2
            return pl.ds(half * hrows, hrows)

        def ici(a, j):
            if split[a]:
                src, dst = ins[a].at[rows(a, c)], outs[a].at[me, rows(a, c)]
            else:
                src, dst = ins[a], outs[a].at[me]
            return pltpu.make_async_remote_copy(src_ref=src, dst_ref=dst, send_sem=send_sems.at[a, j],
                                                recv_sem=recv_sems.at[a, j], device_id=(*chips[j], c), device_id_type=MESH)

        def landed(a, j):
            dst = outs[a].at[slots[j], rows(a, c)] if split[a] else outs[a].at[slots[j]]
            return pltpu.make_async_remote_copy(src_ref=dst, dst_ref=dst, send_sem=send_sems.at[a, j],
                                                recv_sem=recv_sems.at[a, j], device_id=(*chips[j], c), device_id_type=MESH)

        def d2d(a, j, half):
            blk = outs[a].at[slots[j], rows(a, half)]
            return pltpu.make_async_remote_copy(src_ref=blk, dst_ref=blk, send_sem=send_sems.at[a, 3 + j],
                                                recv_sem=recv_sems.at[a, 3 + j], device_id=sibling, device_id_type=MESH)

        local = [pltpu.make_async_copy(ins[a], outs[a].at[me], local_sems.at[a]) for a in range(n)]
        return c, ici, landed, d2d, local

    def start(ins, outs, sems):
        c, ici, landed, d2d, local = copies(ins, outs, sems)
        for cp in local:
            cp.start()
        for a in range(n):
            for j in range(3):
                ici(a, j).start()

    def mid(ins, outs, sems):
        c, ici, landed, d2d, local = copies(ins, outs, sems)
        for a in range(n):
            for j in range(3):
                landed(a, j).wait_recv()
                if split[a]:
                    d2d(a, j, c).start()

    def finish(ins, outs, sems):
        c, ici, landed, d2d, local = copies(ins, outs, sems)
        for a in range(n):
            if split[a]:
                for j in range(3):
                    d2d(a, j, 1 - c).wait_recv()
        for a in range(n):
            for j in range(3):
                ici(a, j).wait_send()
                if split[a]:
                    d2d(a, j, c).wait_send()
        for cp in local:
            cp.wait()

    return _Side(shards, [SDS((N_SLOTS,) + s.shape, s.dtype) for s in shards],
                 [pltpu.SemaphoreType.DMA((n, 6)), pltpu.SemaphoreType.DMA((n, 6)), pltpu.SemaphoreType.DMA((n,))],
                 start, mid, finish)


def _swap_partial_halves(name, parts):
    side = _swap_side(parts)
    n = len(parts)

    def body(*refs):
        ins, outs, sems = refs[:n], refs[n:2 * n], refs[2 * n:]
        side.start(ins, outs, sems)
        side.finish(ins, outs, sems)

    return pl.pallas_call(
        body, name=name, in_specs=[ANY] * n, out_specs=[ANY] * n, out_shape=side.out_shapes,
        scratch_shapes=side.sem_shapes, compiler_params=_cparams(has_side_effects=True),
    )(*parts)


def _swap_side(parts):
    n = len(parts)

    def copies(ins, outs, sems):
        send_sems, recv_sems = sems
        x, y, c, _, _ = _place()
        cps = []
        for a in range(n):
            h = ins[a].shape[1] // 2
            cps.append(pltpu.make_async_remote_copy(
                src_ref=ins[a].at[:, pl.ds((1 - c) * h, h)], dst_ref=outs[a], send_sem=send_sems.at[a],
                recv_sem=recv_sems.at[a], device_id=(x, y, 1 - c), device_id_type=MESH))
        return cps

    def start(ins, outs, sems):
        for cp in copies(ins, outs, sems):
            cp.start()

    def finish(ins, outs, sems):
        for cp in copies(ins, outs, sems):
            cp.wait()

    return _Side(parts, [SDS((p.shape[0], p.shape[1] // 2, p.shape[2]), p.dtype) for p in parts],
                 [pltpu.SemaphoreType.DMA((n,)), pltpu.SemaphoreType.DMA((n,))], start, None, finish)


def _exchange_side(sums):
    n = len(sums)

    def copies(ins, outs, sems):
        send_sems, recv_sems = sems
        x, y, c, chips, slots = _place()
        return [pltpu.make_async_remote_copy(
            src_ref=ins[a].at[slots[j]], dst_ref=outs[a].at[j], send_sem=send_sems.at[a, j],
            recv_sem=recv_sems.at[a, j], device_id=(*chips[j], c), device_id_type=MESH)
            for a in range(n) for j in range(3)]

    def start(ins, outs, sems):
        for cp in copies(ins, outs, sems):
            cp.start()

    def finish(ins, outs, sems):
        for cp in copies(ins, outs, sems):
            cp.wait()

    return _Side(sums, [SDS((3,) + s.shape[1:], s.dtype) for s in sums],
                 [pltpu.SemaphoreType.DMA((n, 3)), pltpu.SemaphoreType.DMA((n, 3))], start, None, finish)


def _swap_reduced_halves(totals):
    n = len(totals)

    def body(*refs):
        ins, outs = refs[:n], refs[n:2 * n]
        send_sems, recv_sems = refs[2 * n:]
        x, y, c, _, _ = _place()
        sibling = (x, y, 1 - c)
        sends = [pltpu.make_async_remote_copy(src_ref=outs[a].at[c], dst_ref=outs[a].at[c], send_sem=send_sems.at[a],
                                              recv_sem=recv_sems.at[a], device_id=sibling, device_id_type=MESH)
                 for a in range(n)]
        for cp in sends:
            cp.start()
        for a in range(n):
            got = outs[a].at[1 - c]
            pltpu.make_async_remote_copy(src_ref=got, dst_ref=got, send_sem=send_sems.at[a], recv_sem=recv_sems.at[a],
                                         device_id=sibling, device_id_type=MESH).wait_recv()
        for cp in sends:
            cp.wait_send()

    return pl.pallas_call(
        body, name="swap_reduced_halves", in_specs=[ANY] * n, out_specs=[ANY] * n,
        out_shape=[SDS(t.shape, t.dtype) for t in totals], input_output_aliases={a: a for a in range(n)},
        scratch_shapes=[pltpu.SemaphoreType.DMA((n,)), pltpu.SemaphoreType.DMA((n,))],
        compiler_params=_cparams(has_side_effects=True),
    )(*totals)


def _allreduce_small(pack):
    rows = pack.shape[0]

    def body(in_ref, out_ref, buf, send_sems, recv_sems):
        x, y, c = lax.axis_index("x"), lax.axis_index("y"), lax.axis_index("c")
        me = 4 * x + 2 * y + c
        buf[me] = in_ref[...]
        cps = []
        for k in range(1, 8):
            kx, ky, kc = (k >> 2) & 1, (k >> 1) & 1, k & 1
            peer = (x ^ kx, y ^ ky, c ^ kc)
            cps.append(pltpu.make_async_remote_copy(src_ref=in_ref, dst_ref=buf.at[me], send_sem=send_sems.at[k - 1],
                                                    recv_sem=recv_sems.at[k - 1], device_id=peer, device_id_type=MESH))
        for cp in cps:
            cp.start()
        for k in range(1, 8):
            got = buf.at[me ^ k]
            pltpu.make_async_remote_copy(src_ref=got, dst_ref=got, send_sem=send_sems.at[k - 1],
                                         recv_sem=recv_sems.at[k - 1], device_id=(x, y, c), device_id_type=MESH).wait_recv()
        for cp in cps:
            cp.wait_send()
        acc = buf[0]
        for d in range(1, 8):
            acc = acc + buf[d]
        out_ref[...] = acc

    vm = pl.BlockSpec(memory_space=pltpu.VMEM)
    return pl.pallas_call(
        body, name="allreduce_small", in_specs=[vm], out_specs=vm, out_shape=SDS(pack.shape, F32),
        scratch_shapes=[pltpu.VMEM((8, rows, LANES), F32), pltpu.SemaphoreType.DMA((7,)), pltpu.SemaphoreType.DMA((7,))],
        compiler_params=_cparams(has_side_effects=True),
    )(pack)


def _rows2d(a):
    return a.reshape((-1, a.shape[-1]))


def _from_slots_blockdiag(g, H, w):
    q = w // N_SLOTS
    return g.reshape(N_SLOTS, H, q, w).transpose(1, 0, 2, 3).reshape(H, w, w)


def _to_slots_blockdiag(d, H, w):
    q = w // N_SLOTS
    return d.reshape(H, N_SLOTS, q, w).transpose(1, 0, 2, 3).reshape(N_SLOTS, H * q, w)


def kernel(x, g_mix, w_in, b_gate, w_pool, pool_scale, lru_conv_w, lru_conv_b, w_a, b_a, w_i, b_i, lru_lambda, w_pool_proj, w_lru_proj, w_out, g_mlp, w_up, ffn_conv_w, ffn_conv_b, w_down, g_final, loss_target, m_g_mix, m_w_in, m_b_gate, m_w_pool, m_pool_scale, m_lru_conv_w, m_lru_conv_b, m_w_a, m_b_a, m_w_i, m_b_i, m_lru_lambda, m_w_pool_proj, m_w_lru_proj, m_w_out, m_g_mlp, m_w_up, m_ffn_conv_w, m_ffn_conv_b, m_w_down, m_g_final, v_g_mix, v_w_in, v_b_gate, v_w_pool, v_pool_scale, v_lru_conv_w, v_lru_conv_b, v_w_a, v_b_a, v_w_i, v_b_i, v_lru_lambda, v_w_pool_proj, v_w_lru_proj, v_w_out, v_g_mlp, v_w_up, v_ffn_conv_w, v_ffn_conv_b, v_w_down, v_g_final):
    weights = dict(g_mix=g_mix, w_in=w_in, b_gate=b_gate, w_pool=w_pool, pool_scale=pool_scale, lru_conv_w=lru_conv_w,
                   lru_conv_b=lru_conv_b, w_a=w_a, b_a=b_a, w_i=w_i, b_i=b_i, lru_lambda=lru_lambda,
                   w_pool_proj=w_pool_proj, w_lru_proj=w_lru_proj, w_out=w_out, g_mlp=g_mlp, w_up=w_up,
                   ffn_conv_w=ffn_conv_w, ffn_conv_b=ffn_conv_b, w_down=w_down, g_final=g_final)
    mom_m = dict(g_mix=m_g_mix, w_in=m_w_in, b_gate=m_b_gate, w_pool=m_w_pool, pool_scale=m_pool_scale,
                 lru_conv_w=m_lru_conv_w, lru_conv_b=m_lru_conv_b, w_a=m_w_a, b_a=m_b_a, w_i=m_w_i, b_i=m_b_i,
                 lru_lambda=m_lru_lambda, w_pool_proj=m_w_pool_proj, w_lru_proj=m_w_lru_proj, w_out=m_w_out,
                 g_mlp=m_g_mlp, w_up=m_w_up, ffn_conv_w=m_ffn_conv_w, ffn_conv_b=m_ffn_conv_b, w_down=m_w_down,
                 g_final=m_g_final)
    mom_v = dict(g_mix=v_g_mix, w_in=v_w_in, b_gate=v_b_gate, w_pool=v_w_pool, pool_scale=v_pool_scale,
                 lru_conv_w=v_lru_conv_w, lru_conv_b=v_lru_conv_b, w_a=v_w_a, b_a=v_b_a, w_i=v_w_i, b_i=v_b_i,
                 lru_lambda=v_lru_lambda, w_pool_proj=v_w_pool_proj, w_lru_proj=v_w_lru_proj, w_out=v_w_out,
                 g_mlp=v_g_mlp, w_up=v_w_up, ffn_conv_w=v_ffn_conv_w, ffn_conv_b=v_ffn_conv_b, w_down=v_w_down,
                 g_final=v_g_final)
    order = list(weights)

    T, D = x.shape[1], x.shape[2]
    Cp = pool_scale.shape[1]
    G, gw = w_pool.shape[1], w_pool.shape[3]
    H, hw = w_a.shape[1], w_a.shape[3]
    R = b_a.shape[1]
    F = ffn_conv_b.shape[1]
    KC, KF = lru_conv_w.shape[1], ffn_conv_w.shape[1]
    p0, p1, p2 = Cp, Cp + R, Cp + 2 * R
    xs = x.reshape(T, D)
    tgt = loss_target.reshape(T, D)

    my_x, my_y, my_c = lax.axis_index("x"), lax.axis_index("y"), lax.axis_index("c")
    slot = 2 * my_x + my_y
    pos = jnp.stack([my_c, slot]).astype(jnp.int32)

    big = ["w_in", "w_pool", "w_a", "w_i", "w_pool_proj", "w_lru_proj", "w_out", "w_up", "w_down"]
    first = big[:4]
    shard16 = {n: _rows2d(weights[n][0]).astype(BF16) for n in big}
    gathered = _gather_weights([shard16[n] for n in first] + [lru_conv_w[0], ffn_conv_w[0]], [True] * 4 + [False, False])
    gw_ = dict(zip(first + ["lru_conv_w", "ffn_conv_w"], gathered))
    W_in = gw_["w_in"]
    W_pool = _from_slots_blockdiag(gw_["w_pool"], G, gw)
    W_a = _from_slots_blockdiag(gw_["w_a"], H, hw)
    W_i = _from_slots_blockdiag(gw_["w_i"], H, hw)
    cw_lru = gw_["lru_conv_w"].transpose(1, 0, 2).reshape(KC, R)
    cw_ffn = gw_["ffn_conv_w"].transpose(1, 0, 2).reshape(KF, F)
    n_in, n_up, n_pp = W_in.shape[2], w_up.shape[2], w_pool_proj.shape[2]

    tt_row = _pick(T, (256, 128))
    tt_seq = _pick(T, (512, 256, 128))
    tt_pool = _pick(T, (1024, 512, 256, 128))
    tc_seq = _pick(R, (512, 256, 128))
    tm1 = _pick(T, (1024, 512, 256))
    tm2 = _pick(T, (2048, 1024, 512, 256))
    tkT = _pick(T, (1024, 512, 256))
    tkT2 = _pick(T, (4096, 2048, 1024, 512, 256))
    tn_in = _pick(n_in, (768, 1152, 384, 128))
    tn_up = _pick(n_up, (1536, 1024, 768, 512, 128))
    tk_up = _pick(n_up, (1024, 768, 512, 128))
    tn_pp = _pick(n_pp, (512, 256, 128))
    tD1 = _pick(D, (1024, 512))
    tD2 = _pick(D, (2048, 1024, 512))
    tF = _pick(F, (1536, 1024, 512))
    tkF = _pick(F, (2048, 1536, 1024, 512))

    h = _rms_fwd("rms1_fwd", xs, g_mix, tt_row)
    on_proj = ["w_pool_proj", "w_lru_proj", "w_out", "w_down"]
    proj3, *late = _matmul("proj_fwd", "nn", h, W_in, out_dtype=F32, tm=tm2, tn=tn_in, tk=D,
                           side=_gather_side([shard16[n] for n in on_proj], [True] * len(on_proj)))
    proj = proj3[0]
    gw_.update(zip(on_proj, late))
    W_pp = gw_["w_pool_proj"]
    W_lp = gw_["w_lru_proj"].reshape(R, D)
    W_out = gw_["w_out"].reshape(D, D)
    W_down = gw_["w_down"].reshape(F, D)
    up16, qD = shard16["w_up"], D // 4
    d_pool = _pool_fwd("pool_fwd", proj, Cp, gw, tt_pool)
    z_pool, y_pool = _bd_fwd("pool_mix_fwd", d_pool, [W_pool], tm1, scale=pool_scale)
    v_f, v_b = _conv_fwd("lru_conv_fwd", proj, p0, cw_lru, lru_conv_b, tt_seq, tc_seq)
    ra_pre, ia_pre = _bd_fwd("lru_gate_fwd", v_b, [W_a, W_i], tm1)
    hs, y_lru, up_a = _lru_fwd("lru_scan_fwd", ra_pre, ia_pre, v_f, proj, p1, b_a, b_i, lru_lambda, tt_seq, tc_seq,
                               side=_gather_side([up16[:2 * qD]], [True]))
    P = _matmul("pool_proj_fwd", "nn", y_pool, W_pp, out_dtype=F32, tm=tm2, tn=tn_pp, tk=Cp)[0]
    Q3, up_b = _matmul("lru_proj_fwd", "nn", y_lru, W_lp, out_dtype=F32, tm=tm1, tn=tD1, tk=R,
                       side=_gather_side([up16[2 * qD:3 * qD]], [True]))
    Q = Q3[0]
    merged = _merge_fwd("merge_fwd", proj, p2, b_gate, P, Q, tt_seq, tc_seq)
    x13, up_c = _matmul("out_fwd", "nn", merged, W_out, out_dtype=F32, tm=tm1, tn=tD1, tk=D, res=xs,
                        side=_gather_side([up16[3 * qD:]], [True]))
    x1 = x13[0]
    W_up = jnp.concatenate([up_a, up_b, up_c], axis=1)
    h2 = _rms_fwd("rms2_fwd", x1, g_mlp, tt_row)
    up = _matmul("up_fwd", "nn", h2, W_up, out_dtype=BF16, tm=tm1, tn=tn_up, tk=D)[0]
    z = _ffn_act_fwd("ffn_act_fwd", up, cw_ffn, ffn_conv_b, tt_seq, tc_seq)
    x2 = _matmul("down_fwd", "nn", z, W_down, out_dtype=F32, tm=tm1, tn=tD1, tk=tkF, res=x1)[0]

    dx2, dx2b, d_g_final, lossvec = _final_loss_bwd("final_loss_bwd", x2, tgt, g_final.reshape(1, D), tt_row)
    dz = _matmul("down_bwd_x", "nt", dx2b, W_down, out_dtype=BF16, tm=tm1, tn=tF, tk=D)[0]
    dW_down = _matmul("down_bwd_w", "tn", z, dx2b, out_dtype=BF16, tm=_pick(F, (1024, 512)), tn=tD1, tk=tkT2)
    dup, d_cw_ffn, d_ffn_b = _ffn_act_bwd("ffn_act_bwd", up, dz, cw_ffn, ffn_conv_b, tt_seq, tc_seq)
    dh2 = _matmul("up_bwd_x", "nt", dup, W_up, out_dtype=F32, tm=tm1, tn=tD1, tk=n_up)[0]
    dW_up = _matmul("up_bwd_w", "tn", h2, dup, out_seg=N_SLOTS, out_dtype=BF16, tm=tD1, tn=tk_up, tk=tkT2)
    dx1, dx1b, d_g_mlp = _rms_bwd("rms2_bwd", dh2, x1, g_mlp, dx2, tt_row, True)
    dmerged = _matmul("out_bwd_x", "nt", dx1b, W_out, out_dtype=BF16, tm=tm1, tn=tD2, tk=D)[0]
    q_rows = lambda a, rows: a.reshape(N_SLOTS, rows // N_SLOTS, a.shape[-1])
    ffn_parts = [dW_up, q_rows(dW_down, F)]
    dW_out, *ffn_sib = _matmul("out_bwd_w", "tn", merged, dx1b, out_dtype=BF16, tm=tD1, tn=tD1, tk=tkT2,
                               side=_swap_side(ffn_parts))
    ffn_sums = [_add_halves(f"add_halves_{n}", p, r, pos) for n, p, r in zip(big[7:], ffn_parts, ffn_sib)]
    dP, dQ, dgl0, dgl1, d_bg0, d_bg1 = _merge_bwd("merge_bwd", dmerged, proj, p2, b_gate, P, Q, tt_seq, tc_seq)
    dy_pool = _matmul("pool_proj_bwd_x", "nt", dP, W_pp, out_dtype=F32, tm=tm2, tn=Cp, tk=tn_pp)[0]
    dW_pp = _matmul("pool_proj_bwd_w", "tn", y_pool, dP, out_seg=N_SLOTS, out_dtype=BF16, tm=Cp, tn=tn_pp,
                    tk=_pick(T, (2048, 1024, 512, 256)))
    dy_lru = _matmul("lru_proj_bwd_x", "nt", dQ, W_lp, out_dtype=F32, tm=tm1, tn=tD1, tk=D)[0]
    dW_lp = _matmul("lru_proj_bwd_w", "tn", y_lru, dQ, out_dtype=BF16, tm=_pick(R, (1024, 512)), tn=tD1, tk=tkT2)
    dra, dia, dv1, du_gelu, d_b_a, d_b_i, d_lam = _lru_bwd(
        "lru_scan_bwd", dy_lru, hs, proj, p1, ra_pre, ia_pre, v_f, b_a, b_i, lru_lambda, tt_seq, tc_seq)
    dv = _bd_bwd_x("lru_gate_bwd_x", [dra, dia], [W_a, W_i], tm1, res=dv1)
    dW_a, dW_i = _bd_bwd_w("lru_gate_bwd_w", v_b, [dra, dia], hw, tkT)
    du_lru, d_cw_lru, d_lru_b = _conv_bwd("lru_conv_bwd", dv, proj, p0, cw_lru, tt_seq, tc_seq)
    dzp, d_pool_scale = _pool_bwd_a("pool_scale_bwd", dy_pool, z_pool, pool_scale, tt_row)
    dd = _bd_bwd_x("pool_mix_bwd_x", [dzp], [W_pool], tm1)
    (dW_pool,) = _bd_bwd_w("pool_mix_bwd_w", d_pool, [dzp], gw, tkT)
    du_pool = _pool_bwd_b("pool_bwd", dd, gw, tt_pool)
    dproj = jnp.concatenate([du_pool, du_lru, du_gelu, dgl0, dgl1], axis=1)

    mix_parts = [_to_slots_blockdiag(dW_pool.astype(BF16), G, gw),
                 _to_slots_blockdiag(dW_a.astype(BF16), H, hw),
                 _to_slots_blockdiag(dW_i.astype(BF16), H, hw),
                 dW_pp, q_rows(dW_lp, R), q_rows(dW_out, D)]
    mix_sib = _swap_partial_halves("swap_partial_halves", mix_parts)
    mix_sums = [_add_halves(f"add_halves_{n}", p, r, pos) for n, p, r in zip(big[1:7], mix_parts, mix_sib)]
    dW_in, *ffn_chips = _matmul("proj_bwd_w", "tn", h, dproj, out_seg=N_SLOTS, out_dtype=BF16, tm=tD1, tn=tn_in,
                                tk=tkT2, side=_exchange_side(ffn_sums))
    (in_sib,) = _swap_partial_halves("swap_partial_halves_w_in", [dW_in])
    in_sum = _add_halves("add_halves_w_in", dW_in, in_sib, pos)
    dh3, in_chips, *mix_chips = _matmul("proj_bwd_x", "nt", dproj, W_in, out_dtype=F32, tm=tm1, tn=tD1, tk=n_in,
                                        side=_exchange_side([in_sum] + mix_sums))
    grad_x, d_g_mix = _rms_bwd("rms1_bwd", dh3[0], xs, g_mix, dx1, tt_row, False)
    totals = [_add_chips(f"add_chips_{n}", s, r, pos)
              for n, s, r in zip(big, [in_sum] + mix_sums + ffn_sums, [in_chips] + mix_chips + ffn_chips)]
    full = _swap_reduced_halves(totals)
    grads = {n: f.reshape(weights[n].shape) for n, f in zip(big, full)}

    small = ["g_mix", "b_gate", "pool_scale", "lru_conv_b", "b_a", "b_i", "lru_lambda", "g_mlp", "ffn_conv_b", "g_final"]
    small_g = [d_g_mix, jnp.concatenate([d_bg0, d_bg1], axis=1), d_pool_scale, d_lru_b, d_b_a, d_b_i, d_lam, d_g_mlp,
               d_ffn_b, d_g_final]
    pieces = [g.reshape(-1) for g in small_g] + [d_cw_lru[:KC].reshape(-1), d_cw_ffn[:KF].reshape(-1), lossvec.reshape(-1)]
    sizes = [p.shape[0] for p in pieces]
    total = sum(sizes)
    rows = -(-total // (8 * LANES)) * 8
    pack = jnp.concatenate(pieces + [jnp.zeros((rows * LANES - total,), F32)]).reshape(rows, LANES)
    summed = _allreduce_small(pack).reshape(-1)
    offs = [0]
    for s in sizes:
        offs.append(offs[-1] + s)
    for k, n in enumerate(small):
        grads[n] = summed[offs[k]:offs[k + 1]].reshape(weights[n].shape)
    ns = len(small)
    g_cw_lru = summed[offs[ns]:offs[ns + 1]].reshape(KC, R)
    g_cw_ffn = summed[offs[ns + 1]:offs[ns + 2]].reshape(KF, F)
    grads["lru_conv_w"] = lax.dynamic_slice_in_dim(g_cw_lru, slot * (R // N_SLOTS), R // N_SLOTS, axis=1)[None]
    grads["ffn_conv_w"] = lax.dynamic_slice_in_dim(g_cw_ffn, slot * (F // N_SLOTS), F // N_SLOTS, axis=1)[None]
    loss = jnp.sum(summed[offs[ns + 2]:offs[ns + 3]]) * (0.5 / D)

    delta, new_m, new_v = {}, {}, {}
    for n in big:
        d_, m_, v_ = _adamw(f"adamw_{n}", _rows2d(weights[n][0]), _rows2d(grads[n][0]), _rows2d(mom_m[n][0]),
                            _rows2d(mom_v[n][0]))
        delta[n], new_m[n], new_v[n] = (t.reshape(weights[n].shape) for t in (d_, m_, v_))
    rest = small + ["lru_conv_w", "ffn_conv_w"]
    rsizes = [weights[n].size for n in rest]
    rtotal = sum(rsizes)
    rrows = -(-rtotal // (8 * LANES)) * 8

    def packed(tree):
        flat = [tree[n].reshape(-1) for n in rest] + [jnp.zeros((rrows * LANES - rtotal,), F32)]
        return jnp.concatenate(flat).reshape(rrows, LANES)

    d_, m_, v_ = _adamw("adamw_small", packed(weights), packed(grads), packed(mom_m), packed(mom_v))
    o = 0
    for n, s in zip(rest, rsizes):
        for tree, flat in ((delta, d_), (new_m, m_), (new_v, v_)):
            tree[n] = flat.reshape(-1)[o:o + s].reshape(weights[n].shape)
        o += s

    return (loss, grad_x.reshape(x.shape), *[grads[n] for n in order], *[delta[n] for n in order],
            *[new_m[n] for n in order], *[new_v[n] for n in order])
```

```python
import functools

import jax
import jax.numpy as jnp
from jax import lax
from jax.experimental import pallas as pl
from jax.experimental.pallas import tpu as pltpu

F32 = jnp.float32
BF16 = jnp.bfloat16
MESH = pl.DeviceIdType.MESH

VMEM_LIMIT_V7X = 56 * 1024 * 1024
LANES = 128
HALO = 16
N_SLOTS = 4
ADAMW_BLOCK_BYTES = 3 * 512 * 1024

EPS = 1e-6
LRU_C = 8.0
ADAM_LR = 0.001
ADAM_B1 = 0.9
ADAM_B2 = 0.999
ADAM_EPS = 1e-08
ADAM_WD = 0.01
ADAM_STEP = 10
GELU_C = 0.7978845608028654
GELU_A = 0.044715

SDS = jax.ShapeDtypeStruct
ANY = pl.BlockSpec(memory_space=pl.ANY)


def _pick(n, cands):
    for c in cands:
        if c <= n and n % c == 0:
            return c
    raise ValueError(f"no tile for {n} in {cands}")


def _cparams(**kw):
    return pltpu.CompilerParams(vmem_limit_bytes=VMEM_LIMIT_V7X, **kw)


def _sigmoid(x):
    return 0.5 * jnp.tanh(0.5 * x) + 0.5


def _gelu(x):
    th = jnp.tanh(x * (GELU_C + (GELU_C * GELU_A) * (x * x)))
    hx = 0.5 * x
    return hx + hx * th


def _gelu_and_grad(x):
    x2 = x * x
    th = jnp.tanh(x * (GELU_C + (GELU_C * GELU_A) * x2))
    hx = 0.5 * x
    g = hx + hx * th
    dg = (0.5 + 0.5 * th) + (hx * (1.0 - th * th)) * (GELU_C + (3.0 * GELU_C * GELU_A) * x2)
    return g, dg


def _lookback(ext, k):
    return pltpu.roll(ext, k, 0)[HALO:]


def _lookahead(ext, k, tt):
    return pltpu.roll(ext, ext.shape[0] - k, 0)[:tt]


def _colsum(v):
    return jnp.sum(v, axis=0, keepdims=True)


def _seg3(a):
    return a if a.ndim == 3 else a.reshape((1,) + a.shape)


def _seg_spec(shape3, br, bc, rowf, colf):
    assert shape3[1] % br == 0 and shape3[2] % bc == 0, (shape3, br, bc)
    nb = shape3[2] // bc

    def imap(i, j, k):
        cb = colf(i, j, k)
        return (cb // nb, rowf(i, j, k), cb % nb)

    return pl.BlockSpec((None, br, bc), imap)


_I = lambda i, j, k: i
_J = lambda i, j, k: j
_K = lambda i, j, k: k


class _Side:
    def __init__(self, ins, out_shapes, sem_shapes, start, mid, finish):
        self.ins, self.out_shapes, self.sem_shapes = list(ins), list(out_shapes), list(sem_shapes)
        self.start, self.mid, self.finish = start, mid, finish


def _ride(side, body, n_in, n_out, n_scr, grid):
    n_sin, n_sout = len(side.ins), len(side.out_shapes)
    n_steps = 1
    for g in grid:
        n_steps *= g

    def wrapped(*refs):
        p = 0
        ins, s_in = refs[p:p + n_in], refs[p + n_in:p + n_in + n_sin]
        p += n_in + n_sin
        outs, s_out = refs[p:p + n_out], refs[p + n_out:p + n_out + n_sout]
        p += n_out + n_sout
        scr, s_sem = refs[p:p + n_scr], refs[p + n_scr:]
        step = 0
        for d, g in enumerate(grid):
            step = step * g + pl.program_id(d)

        @pl.when(step == 0)
        def _():
            side.start(s_in, s_out, s_sem)

        body(*ins, *outs, *scr)

        if side.mid is not None:
            @pl.when(step == (2 * n_steps) // 3)
            def _():
                side.mid(s_in, s_out, s_sem)

        @pl.when(step == n_steps - 1)
        def _():
            side.finish(s_in, s_out, s_sem)

    return wrapped


def _ride_call(side, body, *, name, grid, in_specs, args, out_specs, out_shape, scratch, semantics):
    out_specs, out_shape = list(out_specs), list(out_shape)
    if side is not None:
        body = _ride(side, body, len(args), len(out_shape), len(scratch), grid)
        in_specs = list(in_specs) + [ANY] * len(side.ins)
        args = list(args) + side.ins
        out_specs = out_specs + [ANY] * len(side.out_shapes)
        out_shape = out_shape + side.out_shapes
        scratch = list(scratch) + side.sem_shapes
        semantics = ("arbitrary",) * len(grid)
    return pl.pallas_call(
        body, name=name, grid=grid, in_specs=in_specs, out_specs=out_specs, out_shape=out_shape,
        scratch_shapes=scratch, compiler_params=_cparams(dimension_semantics=semantics),
    )(*args)


def _matmul(name, mode, a, b, *, out_seg=1, out_dtype, tm, tn, tk, res=None, side=None):
    a3, b3 = _seg3(a), _seg3(b)
    if mode == "nn":
        M, K, N = a3.shape[1], a3.shape[0] * a3.shape[2], b3.shape[0] * b3.shape[2]
        assert b3.shape[1] == K
        a_spec = _seg_spec(a3.shape, tm, tk, _I, _K)
        b_spec = _seg_spec(b3.shape, tk, tn, _K, _J)
        dims = (((1,), (0,)), ((), ()))
    elif mode == "nt":
        M, K, N = a3.shape[1], a3.shape[0] * a3.shape[2], b3.shape[1]
        assert b3.shape[0] * b3.shape[2] == K
        a_spec = _seg_spec(a3.shape, tm, tk, _I, _K)
        b_spec = _seg_spec(b3.shape, tn, tk, _J, _K)
        dims = (((1,), (1,)), ((), ()))
    else:
        K, M, N = a3.shape[1], a3.shape[0] * a3.shape[2], b3.shape[0] * b3.shape[2]
        assert b3.shape[1] == K
        a_spec = _seg_spec(a3.shape, tk, tm, _K, _I)
        b_spec = _seg_spec(b3.shape, tk, tn, _K, _J)
        dims = (((0,), (0,)), ((), ()))
    out3 = (out_seg, M, N // out_seg)
    o_spec = _seg_spec(out3, tm, tn, _I, _J)
    grid = (M // tm, N // tn, K // tk)
    nk = grid[2]
    has_res = res is not None

    def body(*refs):
        a_ref, b_ref = refs[0], refs[1]
        r_ref = refs[2] if has_res else None
        o_ref = refs[3] if has_res else refs[2]
        acc_ref = refs[-1] if nk > 1 else None
        prod = lax.dot_general(a_ref[...], b_ref[...], dims, preferred_element_type=F32)

        def finish(val):
            if has_res:
                val = val + r_ref[...]
            o_ref[...] = val.astype(o_ref.dtype)

        if nk == 1:
            finish(prod)
        else:
            k = pl.program_id(2)

            @pl.when(k == 0)
            def _():
                acc_ref[...] = prod

            @pl.when(k > 0)
            def _():
                acc_ref[...] += prod

            @pl.when(k == nk - 1)
            def _():
                finish(acc_ref[...])

    in_specs = [a_spec, b_spec]
    args = [a3, b3]
    if has_res:
        in_specs.append(pl.BlockSpec((tm, tn), lambda i, j, k: (i, j)))
        args.append(res)
    outs = _ride_call(side, body, name=name, grid=grid, in_specs=in_specs, args=args, out_specs=[o_spec],
                      out_shape=[SDS(out3, out_dtype)], scratch=[pltpu.VMEM((tm, tn), F32)] if nk > 1 else [],
                      semantics=("parallel", "parallel", "arbitrary"))
    return outs if side is not None else outs[0]


def _bd_fwd(name, a, ws, tm, scale=None):
    T = a.shape[0]
    H, w, _ = ws[0].shape
    nw = len(ws)
    has_scale = scale is not None

    def body(*refs):
        a_ref = refs[0]
        w_refs = refs[1:1 + nw]
        pos = 1 + nw
        s_ref = refs[pos] if has_scale else None
        o_refs = refs[pos + (1 if has_scale else 0):]
        av = a_ref[...]
        for l in range(nw):
            r = jnp.dot(av, w_refs[l][...], preferred_element_type=F32)
            o_refs[l][...] = r
            if has_scale and l == 0:
                o_refs[nw][...] = (r * s_ref[...]).astype(BF16)

    blk = pl.BlockSpec((tm, w), lambda h, i: (i, h))
    wspec = pl.BlockSpec((None, w, w), lambda h, i: (h, 0, 0))
    in_specs = [blk] + [wspec] * nw
    args = [a] + list(ws)
    out_shape = [SDS((T, H * w), F32)] * nw
    out_specs = [blk] * nw
    if has_scale:
        in_specs.append(pl.BlockSpec((1, w), lambda h, i: (0, h)))
        args.append(scale)
        out_shape = out_shape + [SDS((T, H * w), BF16)]
        out_specs = out_specs + [blk]
    return pl.pallas_call(
        body, name=name, grid=(H, T // tm), in_specs=in_specs, out_specs=out_specs, out_shape=out_shape,
        compiler_params=_cparams(dimension_semantics=("parallel", "parallel")),
    )(*args)


def _bd_bwd_x(name, dys, ws, tm, res=None):
    T = dys[0].shape[0]
    H, w, _ = ws[0].shape
    nw = len(ws)
    has_res = res is not None

    def body(*refs):
        d_refs = refs[:nw]
        w_refs = refs[nw:2 * nw]
        r_ref = refs[2 * nw] if has_res else None
        o_ref = refs[-1]
        acc = None
        for l in range(nw):
            p = lax.dot_general(d_refs[l][...], w_refs[l][...], (((1,), (1,)), ((), ())),
                                preferred_element_type=F32)
            acc = p if acc is None else acc + p
        if has_res:
            acc = acc + r_ref[...]
        o_ref[...] = acc

    blk = pl.BlockSpec((tm, w), lambda h, i: (i, h))
    wspec = pl.BlockSpec((None, w, w), lambda h, i: (h, 0, 0))
    in_specs = [blk] * nw + [wspec] * nw + ([blk] if has_res else [])
    args = list(dys) + list(ws) + ([res] if has_res else [])
    return pl.pallas_call(
        body, name=name, grid=(H, T // tm), in_specs=in_specs, out_specs=blk, out_shape=SDS((T, H * w), F32),
        compiler_params=_cparams(dimension_semantics=("parallel", "parallel")),
    )(*args)


def _bd_bwd_w(name, a, dys, w, tk):
    T, HW = a.shape
    H = HW // w
    nw = len(dys)

    def body(*refs):
        a_ref = refs[0]
        d_refs = refs[1:1 + nw]
        o_refs = refs[1 + nw:]
        k = pl.program_id(1)
        av = a_ref[...]
        for l in range(nw):
            p = lax.dot_general(av, d_refs[l][...], (((0,), (0,)), ((), ())), preferred_element_type=F32)

            @pl.when(k == 0)
            def _(p=p, l=l):
                o_refs[l][...] = p

            @pl.when(k > 0)
            def _(p=p, l=l):
                o_refs[l][...] += p

    blk = pl.BlockSpec((tk, w), lambda h, k: (k, h))
    ospec = pl.BlockSpec((None, w, w), lambda h, k: (h, 0, 0))
    return pl.pallas_call(
        body, name=name, grid=(H, T // tk), in_specs=[blk] * (1 + nw), out_specs=[ospec] * nw,
        out_shape=[SDS((H, w, w), F32)] * nw,
        compiler_params=_cparams(dimension_semantics=("parallel", "arbitrary")),
    )(a, *dys)


def _rms_fwd(name, x, g, tt, side=None):
    T, D = x.shape

    def body(x_ref, g_ref, h_ref):
        xv = x_ref[...]
        rstd = lax.rsqrt(jnp.mean(xv * xv, axis=-1, keepdims=True) + EPS)
        h_ref[...] = (xv * rstd * g_ref[...]).astype(BF16)

    row = pl.BlockSpec((tt, D), lambda i: (i, 0))
    vec = pl.BlockSpec((1, D), lambda i: (0, 0))
    outs = _ride_call(side, body, name=name, grid=(T // tt,), in_specs=[row, vec], args=[x, g], out_specs=[row],
                      out_shape=[SDS((T, D), BF16)], scratch=[], semantics=("parallel",))
    return outs if side is not None else outs[0]


def _rms_bwd(name, dh, x, g, dres, tt, want_bf16):
    T, D = x.shape

    def body(dh_ref, x_ref, g_ref, dres_ref, *outs):
        i = pl.program_id(0)
        dx_ref = outs[0]
        dg_ref = outs[-1]
        xv = x_ref[...]
        rstd = lax.rsqrt(jnp.mean(xv * xv, axis=-1, keepdims=True) + EPS)
        xn = xv * rstd
        dhv = dh_ref[...]
        dyg = dhv * g_ref[...]
        dx = dres_ref[...] + rstd * (dyg - xn * jnp.mean(dyg * xn, axis=-1, keepdims=True))
        dx_ref[...] = dx
        if want_bf16:
            outs[1][...] = dx.astype(BF16)

        @pl.when(i == 0)
        def _():
            dg_ref[...] = jnp.zeros_like(dg_ref)

        dg_ref[...] += _colsum(dhv * xn)

    row = pl.BlockSpec((tt, D), lambda i: (i, 0))
    vec = pl.BlockSpec((1, D), lambda i: (0, 0))
    out_shape = [SDS((T, D), F32)] + ([SDS((T, D), BF16)] if want_bf16 else []) + [SDS((1, D), F32)]
    out_specs = [row] + ([row] if want_bf16 else []) + [vec]
    return pl.pallas_call(
        body, name=name, grid=(T // tt,), in_specs=[row, row, vec, row], out_specs=out_specs, out_shape=out_shape,
        compiler_params=_cparams(dimension_semantics=("arbitrary",)),
    )(dh, x, g, dres)


def _final_loss_bwd(name, x2, tgt, g, tt):
    T, D = x2.shape

    def body(x_ref, t_ref, g_ref, dx_ref, dxb_ref, dg_ref, lv_ref):
        i = pl.program_id(0)
        xv = x_ref[...]
        gv = g_ref[...]
        rstd = lax.rsqrt(jnp.mean(xv * xv, axis=-1, keepdims=True) + EPS)
        xn = xv * rstd
        e = xn * gv - t_ref[...]
        dy = e * (1.0 / D)
        dyg = dy * gv
        dx = rstd * (dyg - xn * jnp.mean(dyg * xn, axis=-1, keepdims=True))
        dx_ref[...] = dx
        dxb_ref[...] = dx.astype(BF16)

        @pl.when(i == 0)
        def _():
            dg_ref[...] = jnp.zeros_like(dg_ref)
            lv_ref[...] = jnp.zeros_like(lv_ref)

        dg_ref[...] += _colsum(dy * xn)
        lv_ref[...] += _colsum(e * e)

    row = pl.BlockSpec((tt, D), lambda i: (i, 0))
    vec = pl.BlockSpec((1, D), lambda i: (0, 0))
    return pl.pallas_call(
        body, name=name, grid=(T // tt,), in_specs=[row, row, vec], out_specs=[row, row, vec, vec],
        out_shape=[SDS((T, D), F32), SDS((T, D), BF16), SDS((1, D), F32), SDS((1, D), F32)],
        compiler_params=_cparams(dimension_semantics=("arbitrary",)),
    )(x2, tgt, g)


def _blk(tt, tc, off=0):
    return pl.BlockSpec((tt, tc), lambda j, i: (i, j + off))


def _vec(r, tc, off=0):
    return pl.BlockSpec((r, tc), lambda j, i: (0, j + off))


def _prev_halo(tt, tc, off=0):
    r = tt // HALO
    return pl.BlockSpec((HALO, tc), lambda j, i: (jnp.maximum(i * r - 1, 0), j + off))


def _next_halo(tt, tc, T, off=0):
    r = tt // HALO
    last = T // HALO - 1
    return pl.BlockSpec((HALO, tc), lambda j, i: (jnp.minimum((i + 1) * r, last), j + off))


def _seq_params():
    return _cparams(dimension_semantics=("parallel", "arbitrary"))


def _pool_fwd(name, proj, Cp, gw, tt):
    T = proj.shape[0]
    G = Cp // gw
    assert G == 4

    def body(u_ref, halo_ref, d_ref):
        j, i = pl.program_id(0), pl.program_id(1)
        u = u_ref[...]
        halo = jnp.where(i > 0, halo_ref[...], 0.0)
        ext = jnp.concatenate([halo, u], axis=0)
        e2 = ext + pltpu.roll(ext, 1, 0)
        e4 = e2 + pltpu.roll(e2, 2, 0)
        e8 = e4 + pltpu.roll(e4, 4, 0)
        e16 = e8 + pltpu.roll(e8, 8, 0)
        s = jnp.where(j == 0, e2, jnp.where(j == 1, e4, jnp.where(j == 2, e8, e16)))[HALO:]
        t1 = (i * tt + 1 + lax.broadcasted_iota(jnp.int32, (tt, 1), 0)).astype(F32)
        win = lax.shift_left(jnp.int32(2), j).astype(F32)
        d_ref[...] = (s * (1.0 / jnp.minimum(t1, win)) - u).astype(BF16)

    return pl.pallas_call(
        body, name=name, grid=(G, T // tt), in_specs=[_blk(tt, gw), _prev_halo(tt, gw)], out_specs=_blk(tt, gw),
        out_shape=SDS((T, Cp), BF16), compiler_params=_seq_params(),
    )(proj, proj)


def _pool_bwd_a(name, dy, z, scale, tt):
    T, Cp = dy.shape

    def body(dy_ref, z_ref, s_ref, dz_ref, ds_ref):
        i = pl.program_id(0)
        dyv = dy_ref[...]
        dz_ref[...] = (dyv * s_ref[...]).astype(BF16)

        @pl.when(i == 0)
        def _():
            ds_ref[...] = jnp.zeros_like(ds_ref)

        ds_ref[...] += _colsum(dyv * z_ref[...])

    row = pl.BlockSpec((tt, Cp), lambda i: (i, 0))
    vec = pl.BlockSpec((1, Cp), lambda i: (0, 0))
    return pl.pallas_call(
        body, name=name, grid=(T // tt,), in_specs=[row, row, vec], out_specs=[row, vec],
        out_shape=[SDS((T, Cp), BF16), SDS((1, Cp), F32)],
        compiler_params=_cparams(dimension_semantics=("arbitrary",)),
    )(dy, z, scale)


def _pool_bwd_b(name, dd, gw, tt):
    T, Cp = dd.shape
    G = Cp // gw
    nT = T // tt

    def body(d_ref, nxt_ref, du_ref):
        j, i = pl.program_id(0), pl.program_id(1)
        win = lax.shift_left(jnp.int32(2), j).astype(F32)
        dv = d_ref[...]
        t1 = (i * tt + 1 + lax.broadcasted_iota(jnp.int32, (tt + HALO, 1), 0)).astype(F32)
        nxt = jnp.where(i < nT - 1, nxt_ref[...], 0.0)
        ext = jnp.concatenate([dv, nxt], axis=0) * (1.0 / jnp.minimum(t1, win))
        n = tt + HALO
        f2 = ext + pltpu.roll(ext, n - 1, 0)
        f4 = f2 + pltpu.roll(f2, n - 2, 0)
        f8 = f4 + pltpu.roll(f4, n - 4, 0)
        f16 = f8 + pltpu.roll(f8, n - 8, 0)
        s = jnp.where(j == 0, f2, jnp.where(j == 1, f4, jnp.where(j == 2, f8, f16)))[:tt]
        du_ref[...] = (s - dv).astype(BF16)

    return pl.pallas_call(
        body, name=name, grid=(G, nT), in_specs=[_blk(tt, gw), _next_halo(tt, gw, T)], out_specs=_blk(tt, gw),
        out_shape=SDS((T, Cp), BF16), compiler_params=_seq_params(),
    )(dd, dd)


def _conv_fwd(name, src, off_cols, w, b, tt, tc):
    T = src.shape[0]
    K, C = w.shape
    off = off_cols // tc

    def body(u_ref, halo_ref, w_ref, b_ref, v_ref, vb_ref):
        i = pl.program_id(1)
        u = u_ref[...]
        halo = jnp.where(i > 0, halo_ref[...], 0.0)
        ext = jnp.concatenate([halo, u], axis=0)
        wv = w_ref[...]
        acc = b_ref[...] + u * wv[K - 1:K]
        for s in range(1, K):
            acc = acc + _lookback(ext, s) * wv[K - 1 - s:K - s]
        v_ref[...] = acc
        vb_ref[...] = acc.astype(BF16)

    return pl.pallas_call(
        body, name=name, grid=(C // tc, T // tt),
        in_specs=[_blk(tt, tc, off), _prev_halo(tt, tc, off), _vec(K, tc), _vec(1, tc)],
        out_specs=[_blk(tt, tc), _blk(tt, tc)], out_shape=[SDS((T, C), F32), SDS((T, C), BF16)],
        compiler_params=_seq_params(),
    )(src, src, w, b)


def _conv_bwd(name, dv, src, off_cols, w, tt, tc):
    T, C = dv.shape
    K = w.shape[0]
    off = off_cols // tc
    nT = T // tt

    def body(dv_ref, nxt_ref, u_ref, halo_ref, w_ref, du_ref, dw_ref, db_ref):
        i = pl.program_id(1)
        d = dv_ref[...]
        nxt = jnp.where(i < nT - 1, nxt_ref[...], 0.0)
        dext = jnp.concatenate([d, nxt], axis=0)
        u = u_ref[...]
        halo = jnp.where(i > 0, halo_ref[...], 0.0)
        uext = jnp.concatenate([halo, u], axis=0)
        wv = w_ref[...]
        du = d * wv[K - 1:K]
        rows = [_colsum(d * u)]
        for s in range(1, K):
            du = du + _lookahead(dext, s, tt) * wv[K - 1 - s:K - s]
            rows.append(_colsum(d * _lookback(uext, s)))
        du_ref[...] = du.astype(BF16)
        dw = jnp.concatenate(rows[::-1] + [jnp.zeros((8 - K, tc), F32)], axis=0)

        @pl.when(i == 0)
        def _():
            dw_ref[...] = jnp.zeros_like(dw_ref)
            db_ref[...] = jnp.zeros_like(db_ref)

        dw_ref[...] += dw
        db_ref[...] += _colsum(d)

    return pl.pallas_call(
        body, name=name, grid=(C // tc, nT),
        in_specs=[_blk(tt, tc), _next_halo(tt, tc, T), _blk(tt, tc, off), _prev_halo(tt, tc, off), _vec(K, tc)],
        out_specs=[_blk(tt, tc), _vec(8, tc), _vec(1, tc)],
        out_shape=[SDS((T, C), BF16), SDS((8, C), F32), SDS((1, C), F32)],
        compiler_params=_seq_params(),
    )(dv, dv, src, src, w)


def _lru_gates(ra, ia, v, ba, bi, lam):
    r = _sigmoid(ra + ba)
    ig = _sigmoid(ia + bi)
    nl = -lam
    ex = jnp.exp(-jnp.abs(nl))
    one_p = 1.0 + ex
    l1p = jnp.where(one_p == 1.0, ex, jnp.log(one_p) * ex / (one_p - 1.0))
    sp = jnp.maximum(nl, 0.0) + l1p
    a = jnp.exp(-LRU_C * r * sp)
    om = 1.0 - a * a
    rs = lax.rsqrt(jnp.maximum(om, 1e-30))
    return r, ig, sp, a, om * rs, rs


def _lru_fwd(name, ra_pre, ia_pre, v, proj, off_cols, ba, bi, lam, tt, tc, side=None):
    T, R = v.shape
    off = off_cols // tc

    def body(ra_ref, ia_ref, v_ref, ug_ref, ba_ref, bi_ref, lam_ref, h_ref, y_ref, carry_ref):
        i = pl.program_id(1)

        @pl.when(i == 0)
        def _():
            carry_ref[...] = jnp.zeros_like(carry_ref)

        vv = v_ref[...]
        _, ig, _, a, mult, _ = _lru_gates(ra_ref[...], ia_ref[...], vv, ba_ref[...], bi_ref[...], lam_ref[...])
        A = a
        B = mult * (ig * vv)
        row8 = lax.broadcasted_iota(jnp.int32, (tt, 1), 0) % 8
        for s in (1, 2, 4):
            m = row8 >= s
            B = jnp.where(m, A * pltpu.roll(B, s, 0) + B, B)
            A = jnp.where(m, A * pltpu.roll(A, s, 0), A)
        carry = carry_ref[7:8, :]
        groups = []
        for g in range(tt // 8):
            hg = A[g * 8:(g + 1) * 8] * carry + B[g * 8:(g + 1) * 8]
            carry = hg[7:8]
            groups.append(hg)
        h = jnp.concatenate(groups, axis=0)
        h_ref[...] = h
        carry_ref[...] = groups[-1]
        y_ref[...] = (h * _gelu(ug_ref[...])).astype(BF16)

    return _ride_call(
        side, body, name=name, grid=(R // tc, T // tt),
        in_specs=[_blk(tt, tc), _blk(tt, tc), _blk(tt, tc), _blk(tt, tc, off), _vec(1, tc), _vec(1, tc), _vec(1, tc)],
        args=[ra_pre, ia_pre, v, proj, ba, bi, lam],
        out_specs=[_blk(tt, tc), _blk(tt, tc)], out_shape=[SDS((T, R), F32), SDS((T, R), BF16)],
        scratch=[pltpu.VMEM((8, tc), F32)], semantics=("parallel", "arbitrary"))


def _lru_bwd(name, dy, hs, proj, off_cols, ra_pre, ia_pre, v, ba, bi, lam, tt, tc):
    T, R = v.shape
    off = off_cols // tc
    nT = T // tt
    r16 = tt // HALO

    def rblk(o=0):
        return pl.BlockSpec((tt, tc), lambda j, i: (nT - 1 - i, j + o))

    hprev = pl.BlockSpec((HALO, tc), lambda j, i: (jnp.maximum((nT - 1 - i) * r16 - 1, 0), j))

    def body(dy_ref, h_ref, hp_ref, ug_ref, ra_ref, ia_ref, v_ref, ba_ref, bi_ref, lam_ref,
             dra_ref, dia_ref, dv_ref, dug_ref, dba_ref, dbi_ref, dlam_ref, dh_carry, a_carry):
        i = pl.program_id(1)

        @pl.when(i == 0)
        def _():
            dh_carry[...] = jnp.zeros_like(dh_carry)
            a_carry[...] = jnp.zeros_like(a_carry)
            dba_ref[...] = jnp.zeros_like(dba_ref)
            dbi_ref[...] = jnp.zeros_like(dbi_ref)
            dlam_ref[...] = jnp.zeros_like(dlam_ref)

        vv = v_ref[...]
        lamv = lam_ref[...]
        r, ig, sp, a, mult, inv_mult = _lru_gates(ra_ref[...], ia_ref[...], vv, ba_ref[...], bi_ref[...], lamv)
        h = h_ref[...]
        gel, dgel = _gelu_and_grad(ug_ref[...])
        dyv = dy_ref[...]
        dug_ref[...] = (dyv * h * dgel).astype(BF16)
        dhs = dyv * gel

        a_ext = jnp.concatenate([a, jnp.broadcast_to(a_carry[0:1, :], (HALO, tc))], axis=0)
        A = _lookahead(a_ext, 1, tt)
        B = dhs
        row8 = lax.broadcasted_iota(jnp.int32, (tt, 1), 0) % 8
        for s in (1, 2, 4):
            m = row8 < 8 - s
            B = jnp.where(m, B + A * pltpu.roll(B, tt - s, 0), B)
            A = jnp.where(m, A * pltpu.roll(A, tt - s, 0), A)
        carry = dh_carry[0:1, :]
        groups = []
        for g in reversed(range(tt // 8)):
            dg = B[g * 8:(g + 1) * 8] + A[g * 8:(g + 1) * 8] * carry
            carry = dg[0:1]
            groups.append(dg)
        dH = jnp.concatenate(groups[::-1], axis=0)
        dh_carry[...] = groups[-1]
        a_carry[...] = a[:8, :]

        hp = jnp.where(i < nT - 1, hp_ref[...], 0.0)
        h_prev = _lookback(jnp.concatenate([hp, h], axis=0), 1)
        da = dH * h_prev
        dmult = dH * (ig * vv)
        div = dH * mult
        dv_ref[...] = div * ig
        dig = div * vv
        dlog_a = da * a - dmult * (a * a) * inv_mult
        dr = dlog_a * (-LRU_C * sp)
        dra = dr * r * (1.0 - r)
        dia = dig * ig * (1.0 - ig)
        dra_ref[...] = dra.astype(BF16)
        dia_ref[...] = dia.astype(BF16)
        dba_ref[...] += _colsum(dra)
        dbi_ref[...] += _colsum(dia)
        dlam_ref[...] += _colsum(dlog_a * (-LRU_C * r))

        @pl.when(i == nT - 1)
        def _():
            dlam_ref[...] = dlam_ref[...] * (-_sigmoid(-lamv))

    return pl.pallas_call(
        body, name=name, grid=(R // tc, nT),
        in_specs=[rblk(), rblk(), hprev, rblk(off), rblk(), rblk(), rblk(), _vec(1, tc), _vec(1, tc), _vec(1, tc)],
        out_specs=[rblk(), rblk(), rblk(), rblk(), _vec(1, tc), _vec(1, tc), _vec(1, tc)],
        out_shape=[SDS((T, R), BF16), SDS((T, R), BF16), SDS((T, R), F32), SDS((T, R), BF16),
                   SDS((1, R), F32), SDS((1, R), F32), SDS((1, R), F32)],
        scratch_shapes=[pltpu.VMEM((8, tc), F32), pltpu.VMEM((8, tc), F32)], compiler_params=_seq_params(),
    )(dy, hs, hs, proj, ra_pre, ia_pre, v, ba, bi, lam)


def _merge_fwd(name, proj, off_cols, b_gate, P, Q, tt, tc):
    T, D = P.shape
    o0 = off_cols // tc
    o1 = o0 + D // tc

    def body(g0_ref, g1_ref, b0_ref, b1_ref, p_ref, q_ref, m_ref):
        g0 = _sigmoid(g0_ref[...] + b0_ref[...])
        g1 = _sigmoid(g1_ref[...] + b1_ref[...])
        m_ref[...] = (g0 * p_ref[...] + g1 * q_ref[...]).astype(BF16)

    return pl.pallas_call(
        body, name=name, grid=(D // tc, T // tt),
        in_specs=[_blk(tt, tc, o0), _blk(tt, tc, o1), _vec(1, tc), _vec(1, tc, D // tc), _blk(tt, tc), _blk(tt, tc)],
        out_specs=_blk(tt, tc), out_shape=SDS((T, D), BF16), compiler_params=_seq_params(),
    )(proj, proj, b_gate, b_gate, P, Q)


def _merge_bwd(name, dm, proj, off_cols, b_gate, P, Q, tt, tc):
    T, D = P.shape
    o0 = off_cols // tc
    o1 = o0 + D // tc

    def body(dm_ref, g0_ref, g1_ref, b0_ref, b1_ref, p_ref, q_ref, dp_ref, dq_ref, dl0_ref, dl1_ref, db0_ref, db1_ref):
        i = pl.program_id(1)
        g0 = _sigmoid(g0_ref[...] + b0_ref[...])
        g1 = _sigmoid(g1_ref[...] + b1_ref[...])
        d = dm_ref[...].astype(F32)
        dp_ref[...] = (d * g0).astype(BF16)
        dq_ref[...] = (d * g1).astype(BF16)
        dl0 = d * p_ref[...] * g0 * (1.0 - g0)
        dl1 = d * q_ref[...] * g1 * (1.0 - g1)
        dl0_ref[...] = dl0.astype(BF16)
        dl1_ref[...] = dl1.astype(BF16)

        @pl.when(i == 0)
        def _():
            db0_ref[...] = jnp.zeros_like(db0_ref)
            db1_ref[...] = jnp.zeros_like(db1_ref)

        db0_ref[...] += _colsum(dl0)
        db1_ref[...] += _colsum(dl1)

    b = _blk(tt, tc)
    return pl.pallas_call(
        body, name=name, grid=(D // tc, T // tt),
        in_specs=[b, _blk(tt, tc, o0), _blk(tt, tc, o1), _vec(1, tc), _vec(1, tc, D // tc), b, b],
        out_specs=[b, b, b, b, _vec(1, tc), _vec(1, tc)],
        out_shape=[SDS((T, D), BF16)] * 4 + [SDS((1, D), F32)] * 2, compiler_params=_seq_params(),
    )(dm, proj, proj, b_gate, b_gate, P, Q)


def _ffn_act_fwd(name, up, w, b, tt, tc):
    T = up.shape[0]
    K, F = w.shape
    nF = F // tc

    def body(gp_ref, halo_ref, val_ref, w_ref, b_ref, z_ref):
        i = pl.program_id(1)
        gp = gp_ref[...].astype(F32)
        halo = jnp.where(i > 0, halo_ref[...].astype(F32), 0.0)
        ext = jnp.concatenate([halo, gp], axis=0)
        wv = w_ref[...]
        c = b_ref[...] + gp * wv[K - 1:K]
        for s in range(1, K):
            c = c + _lookback(ext, s) * wv[K - 1 - s:K - s]
        z_ref[...] = (_gelu(c) * val_ref[...].astype(F32)).astype(BF16)

    return pl.pallas_call(
        body, name=name, grid=(nF, T // tt),
        in_specs=[_blk(tt, tc), _prev_halo(tt, tc), _blk(tt, tc, nF), _vec(K, tc), _vec(1, tc)],
        out_specs=_blk(tt, tc), out_shape=SDS((T, F), BF16), compiler_params=_seq_params(),
    )(up, up, up, w, b)


def _ffn_act_bwd(name, up, dz, w, b, tt, tc):
    T = up.shape[0]
    K, F = w.shape
    nF = F // tc
    nT = T // tt

    def body(gp_ref, gph_ref, gpn_ref, val_ref, valn_ref, dz_ref, dzn_ref, w_ref, b_ref, dup_ref, dw_ref, db_ref):
        i = pl.program_id(1)
        wv = w_ref[...]
        gp = gp_ref[...].astype(F32)
        halo = jnp.where(i > 0, gph_ref[...].astype(F32), 0.0)
        full = jnp.concatenate([halo, gp, gpn_ref[...].astype(F32)], axis=0)
        c = b_ref[...] + full * wv[K - 1:K]
        for s in range(1, K):
            c = c + pltpu.roll(full, s, 0) * wv[K - 1 - s:K - s]
        c = c[HALO:]
        gate, dgate = _gelu_and_grad(c)
        dz_e = jnp.concatenate([dz_ref[...], dzn_ref[...]], axis=0).astype(F32)
        val_e = jnp.concatenate([val_ref[...], valn_ref[...]], axis=0).astype(F32)
        dc_e = dz_e * val_e * dgate
        row = lax.broadcasted_iota(jnp.int32, (tt + HALO, 1), 0)
        dc_e = jnp.where((row < tt) | (i < nT - 1), dc_e, 0.0)
        dc = dc_e[:tt]
        dgp = dc * wv[K - 1:K]
        uext = full[:tt + HALO]
        rows = [_colsum(dc * gp)]
        for s in range(1, K):
            dgp = dgp + _lookahead(dc_e, s, tt) * wv[K - 1 - s:K - s]
            rows.append(_colsum(dc * _lookback(uext, s)))
        dup_ref[0] = dgp.astype(BF16)
        dup_ref[1] = (dz_e[:tt] * gate[:tt]).astype(BF16)
        dw = jnp.concatenate(rows[::-1] + [jnp.zeros((8 - K, tc), F32)], axis=0)

        @pl.when(i == 0)
        def _():
            dw_ref[...] = jnp.zeros_like(dw_ref)
            db_ref[...] = jnp.zeros_like(db_ref)

        dw_ref[...] += dw
        db_ref[...] += _colsum(dc)

    return pl.pallas_call(
        body, name=name, grid=(nF, nT),
        in_specs=[_blk(tt, tc), _prev_halo(tt, tc), _next_halo(tt, tc, T), _blk(tt, tc, nF), _next_halo(tt, tc, T, nF),
                  _blk(tt, tc), _next_halo(tt, tc, T), _vec(K, tc), _vec(1, tc)],
        out_specs=[pl.BlockSpec((2, tt, tc), lambda j, i: (0, i, j)), _vec(8, tc), _vec(1, tc)],
        out_shape=[SDS((2, T, F), BF16), SDS((8, F), F32), SDS((1, F), F32)],
        compiler_params=_seq_params(),
    )(up, up, up, up, up, dz, dz, w, b)


def _adamw(name, w, g, m, v):
    R, C = w.shape
    tr = R
    if R * C * 4 > ADAMW_BLOCK_BYTES:
        tr = _pick(R, [t for t in (256, 128, 64, 32, 16, 8) if t * C * 4 <= ADAMW_BLOCK_BYTES])
    c1 = 1.0 - ADAM_B1 ** ADAM_STEP
    c2 = 1.0 - ADAM_B2 ** ADAM_STEP

    def body(w_ref, g_ref, m_ref, v_ref, d_ref, nm_ref, nv_ref):
        gv = g_ref[...]
        nm = ADAM_B1 * m_ref[...] + (1.0 - ADAM_B1) * gv
        nv = ADAM_B2 * v_ref[...] + (1.0 - ADAM_B2) * (gv * gv)
        nm_ref[...] = nm
        nv_ref[...] = nv
        d_ref[...] = -ADAM_LR * ((nm / c1) / (jnp.sqrt(nv / c2) + ADAM_EPS) + ADAM_WD * w_ref[...])

    spec = pl.BlockSpec((tr, C), lambda i: (i, 0))
    return pl.pallas_call(
        body, name=name, grid=(R // tr,), in_specs=[spec] * 4, out_specs=[spec] * 3,
        out_shape=[SDS((R, C), F32)] * 3, compiler_params=_cparams(dimension_semantics=("parallel",)),
    )(w, g, m, v)


def _add_halves(name, part, rcv, pos):
    S, R, C = part.shape
    h = R // 2
    tr = _pick(h, (512, 256, 128, 64, 32, 16))
    nb = h // tr

    def body(pos_ref, p_ref, r_ref, o_ref):
        o_ref[...] = (p_ref[...].astype(F32) + r_ref[...].astype(F32)).astype(BF16)

    return pl.pallas_call(
        body, name=name,
        grid_spec=pltpu.PrefetchScalarGridSpec(
            num_scalar_prefetch=1, grid=(S, nb),
            in_specs=[pl.BlockSpec((None, tr, C), lambda s, r, pos: (s, pos[0] * nb + r, 0)),
                      pl.BlockSpec((None, tr, C), lambda s, r, pos: (s, r, 0))],
            out_specs=pl.BlockSpec((None, tr, C), lambda s, r, pos: (s, r, 0))),
        out_shape=SDS((S, h, C), BF16), compiler_params=_cparams(dimension_semantics=("parallel", "parallel")),
    )(pos, part, rcv)


def _add_chips(name, chipsum, rcv, pos):
    S, h, C = chipsum.shape
    tr = _pick(h, (512, 256, 128, 64, 32, 16))

    def body(pos_ref, own_ref, r_ref, o_ref):
        o_ref[...] = ((own_ref[...].astype(F32) + r_ref[0].astype(F32)) + r_ref[1].astype(F32)) + r_ref[2].astype(F32)

    return pl.pallas_call(
        body, name=name,
        grid_spec=pltpu.PrefetchScalarGridSpec(
            num_scalar_prefetch=1, grid=(h // tr,),
            in_specs=[pl.BlockSpec((None, tr, C), lambda r, pos: (pos[1], r, 0)),
                      pl.BlockSpec((3, tr, C), lambda r, pos: (0, r, 0))],
            out_specs=pl.BlockSpec((None, tr, C), lambda r, pos: (pos[0], r, 0))),
        out_shape=SDS((2, h, C), F32), compiler_params=_cparams(dimension_semantics=("parallel",)),
    )(pos, chipsum, rcv)


def _place():
    x, y, c = lax.axis_index("x"), lax.axis_index("y"), lax.axis_index("c")
    chips = [(1 - x, y), (x, 1 - y), (1 - x, 1 - y)]
    slots = [2 * cx + cy for cx, cy in chips]
    return x, y, c, chips, slots


def _gather_side(shards, split):
    n = len(shards)

    def copies(ins, outs, sems):
        send_sems, recv_sems, local_sems = sems
        x, y, c, chips, slots = _place()
        me = 2 * x + y
        sibling = (x, y, 1 - c)

        def rows(a, half):
            hrows = ins[a].shape[0] // 2
            return pl.ds(half * hrows, hrows)

        def ici(a, j):
            if split[a]:
                src, dst = ins[a].at[rows(a, c)], outs[a].at[me, rows(a, c)]
            else:
                src, dst = ins[a], outs[a].at[me]
            return pltpu.make_async_remote_copy(src_ref=src, dst_ref=dst, send_sem=send_sems.at[a, j],
                                                recv_sem=recv_sems.at[a, j], device_id=(*chips[j], c), device_id_type=MESH)

        def landed(a, j):
            dst = outs[a].at[slots[j], rows(a, c)] if split[a] else outs[a].at[slots[j]]
            return pltpu.make_async_remote_copy(src_ref=dst, dst_ref=dst, send_sem=send_sems.at[a, j],
                                                recv_sem=recv_sems.at[a, j], device_id=(*chips[j], c), device_id_type=MESH)

        def d2d(a, j, half):
            blk = outs[a].at[slots[j], rows(a, half)]
            return pltpu.make_async_remote_copy(src_ref=blk, dst_ref=blk, send_sem=send_sems.at[a, 3 + j],
                                                recv_sem=recv_sems.at[a, 3 + j], device_id=sibling, device_id_type=MESH)

        local = [pltpu.make_async_copy(ins[a], outs[a].at[me], local_sems.at[a]) for a in range(n)]
        return c, ici, landed, d2d, local

    def start(ins, outs, sems):
        c, ici, landed, d2d, local = copies(ins, outs, sems)
        for cp in local:
            cp.start()
        for a in range(n):
            for j in range(3):
                ici(a, j).start()

    def mid(ins, outs, sems):
        c, ici, landed, d2d, local = copies(ins, outs, sems)
        for a in range(n):
            for j in range(3):
                landed(a, j).wait_recv()
                if split[a]:
                    d2d(a, j, c).start()

    def finish(ins, outs, sems):
        c, ici, landed, d2d, local = copies(ins, outs, sems)
        for a in range(n):
            if split[a]:
                for j in range(3):
                    d2d(a, j, 1 - c).wait_recv()
        for a in range(n):
            for j in range(3):
                ici(a, j).wait_send()
                if split[a]:
                    d2d(a, j, c).wait_send()
        for cp in local:
            cp.wait()

    return _Side(shards, [SDS((N_SLOTS,) + s.shape, s.dtype) for s in shards],
                 [pltpu.SemaphoreType.DMA((n, 6)), pltpu.SemaphoreType.DMA((n, 6)), pltpu.SemaphoreType.DMA((n,))],
                 start, mid, finish)


def _swap_partial_halves(name, parts):
    side = _swap_side(parts)
    n = len(parts)

    def body(*refs):
        ins, outs, sems = refs[:n], refs[n:2 * n], refs[2 * n:]
        side.start(ins, outs, sems)
        side.finish(ins, outs, sems)

    return pl.pallas_call(
        body, name=name, in_specs=[ANY] * n, out_specs=[ANY] * n, out_shape=side.out_shapes,
        scratch_shapes=side.sem_shapes, compiler_params=_cparams(has_side_effects=True),
    )(*parts)


def _swap_side(parts):
    n = len(parts)

    def copies(ins, outs, sems):
        send_sems, recv_sems = sems
        x, y, c, _, _ = _place()
        cps = []
        for a in range(n):
            h = ins[a].shape[1] // 2
            cps.append(pltpu.make_async_remote_copy(
                src_ref=ins[a].at[:, pl.ds((1 - c) * h, h)], dst_ref=outs[a], send_sem=send_sems.at[a],
                recv_sem=recv_sems.at[a], device_id=(x, y, 1 - c), device_id_type=MESH))
        return cps

    def start(ins, outs, sems):
        for cp in copies(ins, outs, sems):
            cp.start()

    def finish(ins, outs, sems):
        for cp in copies(ins, outs, sems):
            cp.wait()

    return _Side(parts, [SDS((p.shape[0], p.shape[1] // 2, p.shape[2]), p.dtype) for p in parts],
                 [pltpu.SemaphoreType.DMA((n,)), pltpu.SemaphoreType.DMA((n,))], start, None, finish)


def _exchange_side(sums):
    n = len(sums)

    def copies(ins, outs, sems):
        send_sems, recv_sems = sems
        x, y, c, chips, slots = _place()
        return [pltpu.make_async_remote_copy(
            src_ref=ins[a].at[slots[j]], dst_ref=outs[a].at[j], send_sem=send_sems.at[a, j],
            recv_sem=recv_sems.at[a, j], device_id=(*chips[j], c), device_id_type=MESH)
            for a in range(n) for j in range(3)]

    def start(ins, outs, sems):
        for cp in copies(ins, outs, sems):
            cp.start()

    def finish(ins, outs, sems):
        for cp in copies(ins, outs, sems):
            cp.wait()

    return _Side(sums, [SDS((3,) + s.shape[1:], s.dtype) for s in sums],
                 [pltpu.SemaphoreType.DMA((n, 3)), pltpu.SemaphoreType.DMA((n, 3))], start, None, finish)


def _swap_reduced_halves(totals):
    n = len(totals)

    def body(*refs):
        ins, outs = refs[:n], refs[n:2 * n]
        send_sems, recv_sems = refs[2 * n:]
        x, y, c, _, _ = _place()
        sibling = (x, y, 1 - c)
        sends = [pltpu.make_async_remote_copy(src_ref=outs[a].at[c], dst_ref=outs[a].at[c], send_sem=send_sems.at[a],
                                              recv_sem=recv_sems.at[a], device_id=sibling, device_id_type=MESH)
                 for a in range(n)]
        for cp in sends:
            cp.start()
        for a in range(n):
            got = outs[a].at[1 - c]
            pltpu.make_async_remote_copy(src_ref=got, dst_ref=got, send_sem=send_sems.at[a], recv_sem=recv_sems.at[a],
                                         device_id=sibling, device_id_type=MESH).wait_recv()
        for cp in sends:
            cp.wait_send()

    return pl.pallas_call(
        body, name="swap_reduced_halves", in_specs=[ANY] * n, out_specs=[ANY] * n,
        out_shape=[SDS(t.shape, t.dtype) for t in totals], input_output_aliases={a: a for a in range(n)},
        scratch_shapes=[pltpu.SemaphoreType.DMA((n,)), pltpu.SemaphoreType.DMA((n,))],
        compiler_params=_cparams(has_side_effects=True),
    )(*totals)


def _allreduce_small(pack):
    rows = pack.shape[0]

    def body(in_ref, out_ref, buf, send_sems, recv_sems):
        x, y, c = lax.axis_index("x"), lax.axis_index("y"), lax.axis_index("c")
        me = 4 * x + 2 * y + c
        buf[me] = in_ref[...]
        cps = []
        for k in range(1, 8):
            kx, ky, kc = (k >> 2) & 1, (k >> 1) & 1, k & 1
            peer = (x ^ kx, y ^ ky, c ^ kc)
            cps.append(pltpu.make_async_remote_copy(src_ref=in_ref, dst_ref=buf.at[me], send_sem=send_sems.at[k - 1],
                                                    recv_sem=recv_sems.at[k - 1], device_id=peer, device_id_type=MESH))
        for cp in cps:
            cp.start()
        for k in range(1, 8):
            got = buf.at[me ^ k]
            pltpu.make_async_remote_copy(src_ref=got, dst_ref=got, send_sem=send_sems.at[k - 1],
                                         recv_sem=recv_sems.at[k - 1], device_id=(x, y, c), device_id_type=MESH).wait_recv()
        for cp in cps:
            cp.wait_send()
        acc = buf[0]
        for d in range(1, 8):
            acc = acc + buf[d]
        out_ref[...] = acc

    vm = pl.BlockSpec(memory_space=pltpu.VMEM)
    return pl.pallas_call(
        body, name="allreduce_small", in_specs=[vm], out_specs=vm, out_shape=SDS(pack.shape, F32),
        scratch_shapes=[pltpu.VMEM((8, rows, LANES), F32), pltpu.SemaphoreType.DMA((7,)), pltpu.SemaphoreType.DMA((7,))],
        compiler_params=_cparams(has_side_effects=True),
    )(pack)


def _rows2d(a):
    return a.reshape((-1, a.shape[-1]))


def _from_slots_blockdiag(g, H, w):
    q = w // N_SLOTS
    return g.reshape(N_SLOTS, H, q, w).transpose(1, 0, 2, 3).reshape(H, w, w)


def _to_slots_blockdiag(d, H, w):
    q = w // N_SLOTS
    return d.reshape(H, N_SLOTS, q, w).transpose(1, 0, 2, 3).reshape(N_SLOTS, H * q, w)


def kernel(x, g_mix, w_in, b_gate, w_pool, pool_scale, lru_conv_w, lru_conv_b, w_a, b_a, w_i, b_i, lru_lambda, w_pool_proj, w_lru_proj, w_out, g_mlp, w_up, ffn_conv_w, ffn_conv_b, w_down, g_final, loss_target, m_g_mix, m_w_in, m_b_gate, m_w_pool, m_pool_scale, m_lru_conv_w, m_lru_conv_b, m_w_a, m_b_a, m_w_i, m_b_i, m_lru_lambda, m_w_pool_proj, m_w_lru_proj, m_w_out, m_g_mlp, m_w_up, m_ffn_conv_w, m_ffn_conv_b, m_w_down, m_g_final, v_g_mix, v_w_in, v_b_gate, v_w_pool, v_pool_scale, v_lru_conv_w, v_lru_conv_b, v_w_a, v_b_a, v_w_i, v_b_i, v_lru_lambda, v_w_pool_proj, v_w_lru_proj, v_w_out, v_g_mlp, v_w_up, v_ffn_conv_w, v_ffn_conv_b, v_w_down, v_g_final):
    weights = dict(g_mix=g_mix, w_in=w_in, b_gate=b_gate, w_pool=w_pool, pool_scale=pool_scale, lru_conv_w=lru_conv_w,
                   lru_conv_b=lru_conv_b, w_a=w_a, b_a=b_a, w_i=w_i, b_i=b_i, lru_lambda=lru_lambda,
                   w_pool_proj=w_pool_proj, w_lru_proj=w_lru_proj, w_out=w_out, g_mlp=g_mlp, w_up=w_up,
                   ffn_conv_w=ffn_conv_w, ffn_conv_b=ffn_conv_b, w_down=w_down, g_final=g_final)
    mom_m = dict(g_mix=m_g_mix, w_in=m_w_in, b_gate=m_b_gate, w_pool=m_w_pool, pool_scale=m_pool_scale,
                 lru_conv_w=m_lru_conv_w, lru_conv_b=m_lru_conv_b, w_a=m_w_a, b_a=m_b_a, w_i=m_w_i, b_i=m_b_i,
                 lru_lambda=m_lru_lambda, w_pool_proj=m_w_pool_proj, w_lru_proj=m_w_lru_proj, w_out=m_w_out,
                 g_mlp=m_g_mlp, w_up=m_w_up, ffn_conv_w=m_ffn_conv_w, ffn_conv_b=m_ffn_conv_b, w_down=m_w_down,
                 g_final=m_g_final)
    mom_v = dict(g_mix=v_g_mix, w_in=v_w_in, b_gate=v_b_gate, w_pool=v_w_pool, pool_scale=v_pool_scale,
                 lru_conv_w=v_lru_conv_w, lru_conv_b=v_lru_conv_b, w_a=v_w_a, b_a=v_b_a, w_i=v_w_i, b_i=v_b_i,
                 lru_lambda=v_lru_lambda, w_pool_proj=v_w_pool_proj, w_lru_proj=v_w_lru_proj, w_out=v_w_out,
                 g_mlp=v_g_mlp, w_up=v_w_up, ffn_conv_w=v_ffn_conv_w, ffn_conv_b=v_ffn_conv_b, w_down=v_w_down,
                 g_final=v_g_final)
    order = list(weights)

    T, D = x.shape[1], x.shape[2]
    Cp = pool_scale.shape[1]
    G, gw = w_pool.shape[1], w_pool.shape[3]
    H, hw = w_a.shape[1], w_a.shape[3]
    R = b_a.shape[1]
    F = ffn_conv_b.shape[1]
    KC, KF = lru_conv_w.shape[1], ffn_conv_w.shape[1]
    p0, p1, p2 = Cp, Cp + R, Cp + 2 * R
    xs = x.reshape(T, D)
    tgt = loss_target.reshape(T, D)

    my_x, my_y, my_c = lax.axis_index("x"), lax.axis_index("y"), lax.axis_index("c")
    slot = 2 * my_x + my_y
    pos = jnp.stack([my_c, slot]).astype(jnp.int32)

    big = ["w_in", "w_pool", "w_a", "w_i", "w_pool_proj", "w_lru_proj", "w_out", "w_up", "w_down"]
    first = big[:4]
    shard16 = {n: _rows2d(weights[n][0]).astype(BF16) for n in big}
    first_side = _gather_side([shard16[n] for n in first] + [lru_conv_w[0], ffn_conv_w[0]], [True] * 4 + [False, False])
    n_in, n_up, n_pp = w_in.shape[2], w_up.shape[2], w_pool_proj.shape[2]

    tt_row = _pick(T, (256, 128))
    tt_seq = _pick(T, (512, 256, 128))
    tt_pool = _pick(T, (1024, 512, 256, 128))
    tc_seq = _pick(R, (512, 256, 128))
    tm1 = _pick(T, (1024, 512, 256))
    tm2 = _pick(T, (2048, 1024, 512, 256))
    tkT = _pick(T, (1024, 512, 256))
    tkT2 = _pick(T, (4096, 2048, 1024, 512, 256))
    tn_in = _pick(n_in, (768, 1152, 384, 128))
    tn_up = _pick(n_up, (1536, 1024, 768, 512, 128))
    tk_up = _pick(n_up, (1024, 768, 512, 128))
    tn_pp = _pick(n_pp, (512, 256, 128))
    tD1 = _pick(D, (1024, 512))
    tD2 = _pick(D, (2048, 1024, 512))
    tF = _pick(F, (1536, 1024, 512))
    tkF = _pick(F, (2048, 1536, 1024, 512))

    h, *gathered = _rms_fwd("rms1_fwd", xs, g_mix, tt_row, side=first_side)
    gw_ = dict(zip(first + ["lru_conv_w", "ffn_conv_w"], gathered))
    W_in = gw_["w_in"]
    W_pool = _from_slots_blockdiag(gw_["w_pool"], G, gw)
    W_a = _from_slots_blockdiag(gw_["w_a"], H, hw)
    W_i = _from_slots_blockdiag(gw_["w_i"], H, hw)
    cw_lru = gw_["lru_conv_w"].transpose(1, 0, 2).reshape(KC, R)
    cw_ffn = gw_["ffn_conv_w"].transpose(1, 0, 2).reshape(KF, F)
    on_proj = ["w_pool_proj", "w_lru_proj", "w_out", "w_down"]
    proj3, *late = _matmul("proj_fwd", "nn", h, W_in, out_dtype=F32, tm=tm2, tn=tn_in, tk=D,
                           side=_gather_side([shard16[n] for n in on_proj], [True] * len(on_proj)))
    proj = proj3[0]
    gw_.update(zip(on_proj, late))
    W_pp = gw_["w_pool_proj"]
    W_lp = gw_["w_lru_proj"].reshape(R, D)
    W_out = gw_["w_out"].reshape(D, D)
    W_down = gw_["w_down"].reshape(F, D)
    up16, qD = shard16["w_up"], D // 4
    d_pool = _pool_fwd("pool_fwd", proj, Cp, gw, tt_pool)
    z_pool, y_pool = _bd_fwd("pool_mix_fwd", d_pool, [W_pool], tm1, scale=pool_scale)
    v_f, v_b = _conv_fwd("lru_conv_fwd", proj, p0, cw_lru, lru_conv_b, tt_seq, tc_seq)
    ra_pre, ia_pre = _bd_fwd("lru_gate_fwd", v_b, [W_a, W_i], tm1)
    hs, y_lru, up_a = _lru_fwd("lru_scan_fwd", ra_pre, ia_pre, v_f, proj, p1, b_a, b_i, lru_lambda, tt_seq, tc_seq,
                               side=_gather_side([up16[:2 * qD]], [True]))
    P = _matmul("pool_proj_fwd", "nn", y_pool, W_pp, out_dtype=F32, tm=tm2, tn=tn_pp, tk=Cp)[0]
    Q3, up_b = _matmul("lru_proj_fwd", "nn", y_lru, W_lp, out_dtype=F32, tm=tm1, tn=tD1, tk=R,
                       side=_gather_side([up16[2 * qD:3 * qD]], [True]))
    Q = Q3[0]
    merged = _merge_fwd("merge_fwd", proj, p2, b_gate, P, Q, tt_seq, tc_seq)
    x13, up_c = _matmul("out_fwd", "nn", merged, W_out, out_dtype=F32, tm=tm1, tn=tD1, tk=D, res=xs,
                        side=_gather_side([up16[3 * qD:]], [True]))
    x1 = x13[0]
    W_up = jnp.concatenate([up_a, up_b, up_c], axis=1)
    h2 = _rms_fwd("rms2_fwd", x1, g_mlp, tt_row)
    up = _matmul("up_fwd", "nn", h2, W_up, out_dtype=BF16, tm=tm1, tn=tn_up, tk=D)[0]
    z = _ffn_act_fwd("ffn_act_fwd", up, cw_ffn, ffn_conv_b, tt_seq, tc_seq)
    x2 = _matmul("down_fwd", "nn", z, W_down, out_dtype=F32, tm=tm1, tn=tD1, tk=tkF, res=x1)[0]

    dx2, dx2b, d_g_final, lossvec = _final_loss_bwd("final_loss_bwd", x2, tgt, g_final.reshape(1, D), tt_row)
    dz = _matmul("down_bwd_x", "nt", dx2b, W_down, out_dtype=BF16, tm=tm1, tn=tF, tk=D)[0]
    dW_down = _matmul("down_bwd_w", "tn", z, dx2b, out_dtype=BF16, tm=_pick(F, (1024, 512)), tn=tD1, tk=tkT2)
    dup, d_cw_ffn, d_ffn_b = _ffn_act_bwd("ffn_act_bwd", up, dz, cw_ffn, ffn_conv_b, tt_seq, tc_seq)
    dh2 = _matmul("up_bwd_x", "nt", dup, W_up, out_dtype=F32, tm=tm1, tn=tD1, tk=n_up)[0]
    dW_up = _matmul("up_bwd_w", "tn", h2, dup, out_seg=N_SLOTS, out_dtype=BF16, tm=tD1, tn=tk_up, tk=tkT2)
    dx1, dx1b, d_g_mlp = _rms_bwd("rms2_bwd", dh2, x1, g_mlp, dx2, tt_row, True)
    dmerged = _matmul("out_bwd_x", "nt", dx1b, W_out, out_dtype=BF16, tm=tm1, tn=tD2, tk=D)[0]
    q_rows = lambda a, rows: a.reshape(N_SLOTS, rows // N_SLOTS, a.shape[-1])
    ffn_parts = [dW_up, q_rows(dW_down, F)]
    dW_out, *ffn_sib = _matmul("out_bwd_w", "tn", merged, dx1b, out_dtype=BF16, tm=tD1, tn=tD1, tk=tkT2,
                               side=_swap_side(ffn_parts))
    ffn_sums = [_add_halves(f"add_halves_{n}", p, r, pos) for n, p, r in zip(big[7:], ffn_parts, ffn_sib)]
    dP, dQ, dgl0, dgl1, d_bg0, d_bg1 = _merge_bwd("merge_bwd", dmerged, proj, p2, b_gate, P, Q, tt_seq, tc_seq)
    dy_pool = _matmul("pool_proj_bwd_x", "nt", dP, W_pp, out_dtype=F32, tm=tm2, tn=Cp, tk=tn_pp)[0]
    dW_pp = _matmul("pool_proj_bwd_w", "tn", y_pool, dP, out_seg=N_SLOTS, out_dtype=BF16, tm=Cp, tn=tn_pp,
                    tk=_pick(T, (2048, 1024, 512, 256)))
    dy_lru = _matmul("lru_proj_bwd_x", "nt", dQ, W_lp, out_dtype=F32, tm=tm1, tn=tD1, tk=D)[0]
    dW_lp = _matmul("lru_proj_bwd_w", "tn", y_lru, dQ, out_dtype=BF16, tm=_pick(R, (1024, 512)), tn=tD1, tk=tkT2)
    dra, dia, dv1, du_gelu, d_b_a, d_b_i, d_lam = _lru_bwd(
        "lru_scan_bwd", dy_lru, hs, proj, p1, ra_pre, ia_pre, v_f, b_a, b_i, lru_lambda, tt_seq, tc_seq)
    dv = _bd_bwd_x("lru_gate_bwd_x", [dra, dia], [W_a, W_i], tm1, res=dv1)
    dW_a, dW_i = _bd_bwd_w("lru_gate_bwd_w", v_b, [dra, dia], hw, tkT)
    du_lru, d_cw_lru, d_lru_b = _conv_bwd("lru_conv_bwd", dv, proj, p0, cw_lru, tt_seq, tc_seq)
    dzp, d_pool_scale = _pool_bwd_a("pool_scale_bwd", dy_pool, z_pool, pool_scale, tt_row)
    dd = _bd_bwd_x("pool_mix_bwd_x", [dzp], [W_pool], tm1)
    (dW_pool,) = _bd_bwd_w("pool_mix_bwd_w", d_pool, [dzp], gw, tkT)
    du_pool = _pool_bwd_b("pool_bwd", dd, gw, tt_pool)
    dproj = jnp.concatenate([du_pool, du_lru, du_gelu, dgl0, dgl1], axis=1)

    mix_parts = [_to_slots_blockdiag(dW_pool.astype(BF16), G, gw),
                 _to_slots_blockdiag(dW_a.astype(BF16), H, hw),
                 _to_slots_blockdiag(dW_i.astype(BF16), H, hw),
                 dW_pp, q_rows(dW_lp, R), q_rows(dW_out, D)]
    mix_sib = _swap_partial_halves("swap_partial_halves", mix_parts)
    mix_sums = [_add_halves(f"add_halves_{n}", p, r, pos) for n, p, r in zip(big[1:7], mix_parts, mix_sib)]
    dW_in, *ffn_chips = _matmul("proj_bwd_w", "tn", h, dproj, out_seg=N_SLOTS, out_dtype=BF16, tm=tD1, tn=tn_in,
                                tk=tkT2, side=_exchange_side(ffn_sums))
    (in_sib,) = _swap_partial_halves("swap_partial_halves_w_in", [dW_in])
    in_sum = _add_halves("add_halves_w_in", dW_in, in_sib, pos)
    dh3, in_chips, *mix_chips = _matmul("proj_bwd_x", "nt", dproj, W_in, out_dtype=F32, tm=tm1, tn=tD1, tk=n_in,
                                        side=_exchange_side([in_sum] + mix_sums))
    grad_x, d_g_mix = _rms_bwd("rms1_bwd", dh3[0], xs, g_mix, dx1, tt_row, False)
    totals = [_add_chips(f"add_chips_{n}", s, r, pos)
              for n, s, r in zip(big, [in_sum] + mix_sums + ffn_sums, [in_chips] + mix_chips + ffn_chips)]
    full = _swap_reduced_halves(totals)
    grads = {n: f.reshape(weights[n].shape) for n, f in zip(big, full)}

    small = ["g_mix", "b_gate", "pool_scale", "lru_conv_b", "b_a", "b_i", "lru_lambda", "g_mlp", "ffn_conv_b", "g_final"]
    small_g = [d_g_mix, jnp.concatenate([d_bg0, d_bg1], axis=1), d_pool_scale, d_lru_b, d_b_a, d_b_i, d_lam, d_g_mlp,
               d_ffn_b, d_g_final]
    pieces = [g.reshape(-1) for g in small_g] + [d_cw_lru[:KC].reshape(-1), d_cw_ffn[:KF].reshape(-1), lossvec.reshape(-1)]
    sizes = [p.shape[0] for p in pieces]
    total = sum(sizes)
    rows = -(-total // (8 * LANES)) * 8
    pack = jnp.concatenate(pieces + [jnp.zeros((rows * LANES - total,), F32)]).reshape(rows, LANES)
    summed = _allreduce_small(pack).reshape(-1)
    offs = [0]
    for s in sizes:
        offs.append(offs[-1] + s)
    for k, n in enumerate(small):
        grads[n] = summed[offs[k]:offs[k + 1]].reshape(weights[n].shape)
    ns = len(small)
    g_cw_lru = summed[offs[ns]:offs[ns + 1]].reshape(KC, R)
    g_cw_ffn = summed[offs[ns + 1]:offs[ns + 2]].reshape(KF, F)
    grads["lru_conv_w"] = lax.dynamic_slice_in_dim(g_cw_lru, slot * (R // N_SLOTS), R // N_SLOTS, axis=1)[None]
    grads["ffn_conv_w"] = lax.dynamic_slice_in_dim(g_cw_ffn, slot * (F // N_SLOTS), F // N_SLOTS, axis=1)[None]
    loss = jnp.sum(summed[offs[ns + 2]:offs[ns + 3]]) * (0.5 / D)

    delta, new_m, new_v = {}, {}, {}
    for n in big:
        d_, m_, v_ = _adamw(f"adamw_{n}", _rows2d(weights[n][0]), _rows2d(grads[n][0]), _rows2d(mom_m[n][0]),
                            _rows2d(mom_v[n][0]))
        delta[n], new_m[n], new_v[n] = (t.reshape(weights[n].shape) for t in (d_, m_, v_))
    rest = small + ["lru_conv_w", "ffn_conv_w"]
    rsizes = [weights[n].size for n in rest]
    rtotal = sum(rsizes)
    rrows = -(-rtotal // (8 * LANES)) * 8

    def packed(tree):
        flat = [tree[n].reshape(-1) for n in rest] + [jnp.zeros((rrows * LANES - rtotal,), F32)]
        return jnp.concatenate(flat).reshape(rrows, LANES)

    d_, m_, v_ = _adamw("adamw_small", packed(weights), packed(grads), packed(mom_m), packed(mom_v))
    o = 0
    for n, s in zip(rest, rsizes):
        for tree, flat in ((delta, d_), (new_m, m_), (new_v, v_)):
            tree[n] = flat.reshape(-1)[o:o + s].reshape(weights[n].shape)
        o += s

    return (loss, grad_x.reshape(x.shape), *[grads[n] for n in order], *[delta[n] for n in order],
            *[new_m[n] for n in order], *[new_v[n] for n in order])
```

```python
import functools

import jax
import jax.numpy as jnp
from jax import lax
from jax.experimental import pallas as pl
from jax.experimental.pallas import tpu as pltpu

F32 = jnp.float32
BF16 = jnp.bfloat16
MESH = pl.DeviceIdType.MESH

VMEM_LIMIT_V7X = 56 * 1024 * 1024
LANES = 128
HALO = 16
N_SLOTS = 4
ADAMW_BLOCK_BYTES = 3 * 512 * 1024

EPS = 1e-6
LRU_C = 8.0
ADAM_LR = 0.001
ADAM_B1 = 0.9
ADAM_B2 = 0.999
ADAM_EPS = 1e-08
ADAM_WD = 0.01
ADAM_STEP = 10
GELU_C = 0.7978845608028654
GELU_A = 0.044715

SDS = jax.ShapeDtypeStruct
ANY = pl.BlockSpec(memory_space=pl.ANY)


def _pick(n, cands):
    for c in cands:
        if c <= n and n % c == 0:
            return c
    raise ValueError(f"no tile for {n} in {cands}")


def _cparams(**kw):
    return pltpu.CompilerParams(vmem_limit_bytes=VMEM_LIMIT_V7X, **kw)


def _sigmoid(x):
    return 0.5 * jnp.tanh(0.5 * x) + 0.5


def _gelu(x):
    th = jnp.tanh(x * (GELU_C + (GELU_C * GELU_A) * (x * x)))
    hx = 0.5 * x
    return hx + hx * th


def _gelu_and_grad(x):
    x2 = x * x
    th = jnp.tanh(x * (GELU_C + (GELU_C * GELU_A) * x2))
    hx = 0.5 * x
    g = hx + hx * th
    dg = (0.5 + 0.5 * th) + (hx * (1.0 - th * th)) * (GELU_C + (3.0 * GELU_C * GELU_A) * x2)
    return g, dg


def _lookback(ext, k):
    return pltpu.roll(ext, k, 0)[HALO:]


def _lookahead(ext, k, tt):
    return pltpu.roll(ext, ext.shape[0] - k, 0)[:tt]


def _colsum(v):
    return jnp.sum(v, axis=0, keepdims=True)


def _seg3(a):
    return a if a.ndim == 3 else a.reshape((1,) + a.shape)


def _seg_spec(shape3, br, bc, rowf, colf):
    assert shape3[1] % br == 0 and shape3[2] % bc == 0, (shape3, br, bc)
    nb = shape3[2] // bc

    def imap(i, j, k):
        cb = colf(i, j, k)
        return (cb // nb, rowf(i, j, k), cb % nb)

    return pl.BlockSpec((None, br, bc), imap)


_I = lambda i, j, k: i
_J = lambda i, j, k: j
_K = lambda i, j, k: k


class _Side:
    def __init__(self, ins, out_shapes, sem_shapes, start, mid, finish, in_place=False):
        self.ins, self.out_shapes, self.sem_shapes = list(ins), list(out_shapes), list(sem_shapes)
        self.start, self.mid, self.finish = start, mid, finish
        self.alias = [(a, a) for a in range(len(self.ins))] if in_place else []


def _both(s1, s2):
    ni, no, ns = len(s1.ins), len(s1.out_shapes), len(s1.sem_shapes)

    def phase(f1, f2):
        if f1 is None and f2 is None:
            return None

        def run(ins, outs, sems):
            if f1 is not None:
                f1(ins[:ni], outs[:no], sems[:ns])
            if f2 is not None:
                f2(ins[ni:], outs[no:], sems[ns:])

        return run

    both = _Side(s1.ins + s2.ins, s1.out_shapes + s2.out_shapes, s1.sem_shapes + s2.sem_shapes,
                 phase(s1.start, s2.start), phase(s1.mid, s2.mid), phase(s1.finish, s2.finish))
    both.alias = s1.alias + [(ni + a, no + b) for a, b in s2.alias]
    return both


def _ride(side, body, n_in, n_out, n_scr, grid):
    n_sin, n_sout = len(side.ins), len(side.out_shapes)
    n_steps = 1
    for g in grid:
        n_steps *= g

    def wrapped(*refs):
        p = 0
        ins, s_in = refs[p:p + n_in], refs[p + n_in:p + n_in + n_sin]
        p += n_in + n_sin
        outs, s_out = refs[p:p + n_out], refs[p + n_out:p + n_out + n_sout]
        p += n_out + n_sout
        scr, s_sem = refs[p:p + n_scr], refs[p + n_scr:]
        step = 0
        for d, g in enumerate(grid):
            step = step * g + pl.program_id(d)

        @pl.when(step == 0)
        def _():
            side.start(s_in, s_out, s_sem)

        body(*ins, *outs, *scr)

        if side.mid is not None:
            @pl.when(step == (2 * n_steps) // 3)
            def _():
                side.mid(s_in, s_out, s_sem)

        @pl.when(step == n_steps - 1)
        def _():
            side.finish(s_in, s_out, s_sem)

    return wrapped


def _ride_call(side, body, *, name, grid, in_specs, args, out_specs, out_shape, scratch, semantics):
    out_specs, out_shape = list(out_specs), list(out_shape)
    aliases = {}
    if side is not None:
        aliases = {len(args) + a: len(out_shape) + b for a, b in side.alias}
        body = _ride(side, body, len(args), len(out_shape), len(scratch), grid)
        in_specs = list(in_specs) + [ANY] * len(side.ins)
        args = list(args) + side.ins
        out_specs = out_specs + [ANY] * len(side.out_shapes)
        out_shape = out_shape + side.out_shapes
        scratch = list(scratch) + side.sem_shapes
        semantics = ("arbitrary",) * len(grid)
    return pl.pallas_call(
        body, name=name, grid=grid, in_specs=in_specs, out_specs=out_specs, out_shape=out_shape,
        scratch_shapes=scratch, input_output_aliases=aliases,
        compiler_params=_cparams(dimension_semantics=semantics),
    )(*args)


def _matmul(name, mode, a, b, *, out_seg=1, out_dtype, tm, tn, tk, res=None, side=None):
    a3, b3 = _seg3(a), _seg3(b)
    if mode == "nn":
        M, K, N = a3.shape[1], a3.shape[0] * a3.shape[2], b3.shape[0] * b3.shape[2]
        assert b3.shape[1] == K
        a_spec = _seg_spec(a3.shape, tm, tk, _I, _K)
        b_spec = _seg_spec(b3.shape, tk, tn, _K, _J)
        dims = (((1,), (0,)), ((), ()))
    elif mode == "nt":
        M, K, N = a3.shape[1], a3.shape[0] * a3.shape[2], b3.shape[1]
        assert b3.shape[0] * b3.shape[2] == K
        a_spec = _seg_spec(a3.shape, tm, tk, _I, _K)
        b_spec = _seg_spec(b3.shape, tn, tk, _J, _K)
        dims = (((1,), (1,)), ((), ()))
    else:
        K, M, N = a3.shape[1], a3.shape[0] * a3.shape[2], b3.shape[0] * b3.shape[2]
        assert b3.shape[1] == K
        a_spec = _seg_spec(a3.shape, tk, tm, _K, _I)
        b_spec = _seg_spec(b3.shape, tk, tn, _K, _J)
        dims = (((0,), (0,)), ((), ()))
    out3 = (out_seg, M, N // out_seg)
    o_spec = _seg_spec(out3, tm, tn, _I, _J)
    grid = (M // tm, N // tn, K // tk)
    nk = grid[2]
    has_res = res is not None

    def body(*refs):
        a_ref, b_ref = refs[0], refs[1]
        r_ref = refs[2] if has_res else None
        o_ref = refs[3] if has_res else refs[2]
        acc_ref = refs[-1] if nk > 1 else None
        prod = lax.dot_general(a_ref[...], b_ref[...], dims, preferred_element_type=F32)

        def finish(val):
            if has_res:
                val = val + r_ref[...]
            o_ref[...] = val.astype(o_ref.dtype)

        if nk == 1:
            finish(prod)
        else:
            k = pl.program_id(2)

            @pl.when(k == 0)
            def _():
                acc_ref[...] = prod

            @pl.when(k > 0)
            def _():
                acc_ref[...] += prod

            @pl.when(k == nk - 1)
            def _():
                finish(acc_ref[...])

    in_specs = [a_spec, b_spec]
    args = [a3, b3]
    if has_res:
        in_specs.append(pl.BlockSpec((tm, tn), lambda i, j, k: (i, j)))
        args.append(res)
    outs = _ride_call(side, body, name=name, grid=grid, in_specs=in_specs, args=args, out_specs=[o_spec],
                      out_shape=[SDS(out3, out_dtype)], scratch=[pltpu.VMEM((tm, tn), F32)] if nk > 1 else [],
                      semantics=("parallel", "parallel", "arbitrary"))
    return outs if side is not None else outs[0]


def _bd_fwd(name, a, ws, tm, scale=None):
    T = a.shape[0]
    H, w, _ = ws[0].shape
    nw = len(ws)
    has_scale = scale is not None

    def body(*refs):
        a_ref = refs[0]
        w_refs = refs[1:1 + nw]
        pos = 1 + nw
        s_ref = refs[pos] if has_scale else None
        o_refs = refs[pos + (1 if has_scale else 0):]
        av = a_ref[...]
        for l in range(nw):
            r = jnp.dot(av, w_refs[l][...], preferred_element_type=F32)
            o_refs[l][...] = r
            if has_scale and l == 0:
                o_refs[nw][...] = (r * s_ref[...]).astype(BF16)

    blk = pl.BlockSpec((tm, w), lambda h, i: (i, h))
    wspec = pl.BlockSpec((None, w, w), lambda h, i: (h, 0, 0))
    in_specs = [blk] + [wspec] * nw
    args = [a] + list(ws)
    out_shape = [SDS((T, H * w), F32)] * nw
    out_specs = [blk] * nw
    if has_scale:
        in_specs.append(pl.BlockSpec((1, w), lambda h, i: (0, h)))
        args.append(scale)
        out_shape = out_shape + [SDS((T, H * w), BF16)]
        out_specs = out_specs + [blk]
    return pl.pallas_call(
        body, name=name, grid=(H, T // tm), in_specs=in_specs, out_specs=out_specs, out_shape=out_shape,
        compiler_params=_cparams(dimension_semantics=("parallel", "parallel")),
    )(*args)


def _bd_bwd_x(name, dys, ws, tm, res=None):
    T = dys[0].shape[0]
    H, w, _ = ws[0].shape
    nw = len(ws)
    has_res = res is not None

    def body(*refs):
        d_refs = refs[:nw]
        w_refs = refs[nw:2 * nw]
        r_ref = refs[2 * nw] if has_res else None
        o_ref = refs[-1]
        acc = None
        for l in range(nw):
            p = lax.dot_general(d_refs[l][...], w_refs[l][...], (((1,), (1,)), ((), ())),
                                preferred_element_type=F32)
            acc = p if acc is None else acc + p
        if has_res:
            acc = acc + r_ref[...]
        o_ref[...] = acc

    blk = pl.BlockSpec((tm, w), lambda h, i: (i, h))
    wspec = pl.BlockSpec((None, w, w), lambda h, i: (h, 0, 0))
    in_specs = [blk] * nw + [wspec] * nw + ([blk] if has_res else [])
    args = list(dys) + list(ws) + ([res] if has_res else [])
    return pl.pallas_call(
        body, name=name, grid=(H, T // tm), in_specs=in_specs, out_specs=blk, out_shape=SDS((T, H * w), F32),
        compiler_params=_cparams(dimension_semantics=("parallel", "parallel")),
    )(*args)


def _bd_bwd_w(name, a, dys, w, tk):
    T, HW = a.shape
    H = HW // w
    nw = len(dys)

    def body(*refs):
        a_ref = refs[0]
        d_refs = refs[1:1 + nw]
        o_refs = refs[1 + nw:]
        k = pl.program_id(1)
        av = a_ref[...]
        for l in range(nw):
            p = lax.dot_general(av, d_refs[l][...], (((0,), (0,)), ((), ())), preferred_element_type=F32)

            @pl.when(k == 0)
            def _(p=p, l=l):
                o_refs[l][...] = p

            @pl.when(k > 0)
            def _(p=p, l=l):
                o_refs[l][...] += p

    blk = pl.BlockSpec((tk, w), lambda h, k: (k, h))
    ospec = pl.BlockSpec((None, w, w), lambda h, k: (h, 0, 0))
    return pl.pallas_call(
        body, name=name, grid=(H, T // tk), in_specs=[blk] * (1 + nw), out_specs=[ospec] * nw,
        out_shape=[SDS((H, w, w), F32)] * nw,
        compiler_params=_cparams(dimension_semantics=("parallel", "arbitrary")),
    )(a, *dys)


def _rms_fwd(name, x, g, tt, side=None):
    T, D = x.shape

    def body(x_ref, g_ref, h_ref):
        xv = x_ref[...]
        rstd = lax.rsqrt(jnp.mean(xv * xv, axis=-1, keepdims=True) + EPS)
        h_ref[...] = (xv * rstd * g_ref[...]).astype(BF16)

    row = pl.BlockSpec((tt, D), lambda i: (i, 0))
    vec = pl.BlockSpec((1, D), lambda i: (0, 0))
    outs = _ride_call(side, body, name=name, grid=(T // tt,), in_specs=[row, vec], args=[x, g], out_specs=[row],
                      out_shape=[SDS((T, D), BF16)], scratch=[], semantics=("parallel",))
    return outs if side is not None else outs[0]


def _rms_bwd(name, dh, x, g, dres, tt, want_bf16):
    T, D = x.shape

    def body(dh_ref, x_ref, g_ref, dres_ref, *outs):
        i = pl.program_id(0)
        dx_ref = outs[0]
        dg_ref = outs[-1]
        xv = x_ref[...]
        rstd = lax.rsqrt(jnp.mean(xv * xv, axis=-1, keepdims=True) + EPS)
        xn = xv * rstd
        dhv = dh_ref[...]
        dyg = dhv * g_ref[...]
        dx = dres_ref[...] + rstd * (dyg - xn * jnp.mean(dyg * xn, axis=-1, keepdims=True))
        dx_ref[...] = dx
        if want_bf16:
            outs[1][...] = dx.astype(BF16)

        @pl.when(i == 0)
        def _():
            dg_ref[...] = jnp.zeros_like(dg_ref)

        dg_ref[...] += _colsum(dhv * xn)

    row = pl.BlockSpec((tt, D), lambda i: (i, 0))
    vec = pl.BlockSpec((1, D), lambda i: (0, 0))
    out_shape = [SDS((T, D), F32)] + ([SDS((T, D), BF16)] if want_bf16 else []) + [SDS((1, D), F32)]
    out_specs = [row] + ([row] if want_bf16 else []) + [vec]
    return pl.pallas_call(
        body, name=name, grid=(T // tt,), in_specs=[row, row, vec, row], out_specs=out_specs, out_shape=out_shape,
        compiler_params=_cparams(dimension_semantics=("arbitrary",)),
    )(dh, x, g, dres)


def _final_loss_bwd(name, x2, tgt, g, tt):
    T, D = x2.shape

    def body(x_ref, t_ref, g_ref, dx_ref, dxb_ref, dg_ref, lv_ref):
        i = pl.program_id(0)
        xv = x_ref[...]
        gv = g_ref[...]
        rstd = lax.rsqrt(jnp.mean(xv * xv, axis=-1, keepdims=True) + EPS)
        xn = xv * rstd
        e = xn * gv - t_ref[...]
        dy = e * (1.0 / D)
        dyg = dy * gv
        dx = rstd * (dyg - xn * jnp.mean(dyg * xn, axis=-1, keepdims=True))
        dx_ref[...] = dx
        dxb_ref[...] = dx.astype(BF16)

        @pl.when(i == 0)
        def _():
            dg_ref[...] = jnp.zeros_like(dg_ref)
            lv_ref[...] = jnp.zeros_like(lv_ref)

        dg_ref[...] += _colsum(dy * xn)
        lv_ref[...] += _colsum(e * e)

    row = pl.BlockSpec((tt, D), lambda i: (i, 0))
    vec = pl.BlockSpec((1, D), lambda i: (0, 0))
    return pl.pallas_call(
        body, name=name, grid=(T // tt,), in_specs=[row, row, vec], out_specs=[row, row, vec, vec],
        out_shape=[SDS((T, D), F32), SDS((T, D), BF16), SDS((1, D), F32), SDS((1, D), F32)],
        compiler_params=_cparams(dimension_semantics=("arbitrary",)),
    )(x2, tgt, g)


def _blk(tt, tc, off=0):
    return pl.BlockSpec((tt, tc), lambda j, i: (i, j + off))


def _vec(r, tc, off=0):
    return pl.BlockSpec((r, tc), lambda j, i: (0, j + off))


def _prev_halo(tt, tc, off=0):
    r = tt // HALO
    return pl.BlockSpec((HALO, tc), lambda j, i: (jnp.maximum(i * r - 1, 0), j + off))


def _next_halo(tt, tc, T, off=0):
    r = tt // HALO
    last = T // HALO - 1
    return pl.BlockSpec((HALO, tc), lambda j, i: (jnp.minimum((i + 1) * r, last), j + off))


def _seq_params():
    return _cparams(dimension_semantics=("parallel", "arbitrary"))


def _pool_fwd(name, proj, Cp, gw, tt):
    T = proj.shape[0]
    G = Cp // gw
    assert G == 4

    def body(u_ref, halo_ref, d_ref):
        j, i = pl.program_id(0), pl.program_id(1)
        u = u_ref[...]
        halo = jnp.where(i > 0, halo_ref[...], 0.0)
        ext = jnp.concatenate([halo, u], axis=0)
        e2 = ext + pltpu.roll(ext, 1, 0)
        e4 = e2 + pltpu.roll(e2, 2, 0)
        e8 = e4 + pltpu.roll(e4, 4, 0)
        e16 = e8 + pltpu.roll(e8, 8, 0)
        s = jnp.where(j == 0, e2, jnp.where(j == 1, e4, jnp.where(j == 2, e8, e16)))[HALO:]
        t1 = (i * tt + 1 + lax.broadcasted_iota(jnp.int32, (tt, 1), 0)).astype(F32)
        win = lax.shift_left(jnp.int32(2), j).astype(F32)
        d_ref[...] = (s * (1.0 / jnp.minimum(t1, win)) - u).astype(BF16)

    return pl.pallas_call(
        body, name=name, grid=(G, T // tt), in_specs=[_blk(tt, gw), _prev_halo(tt, gw)], out_specs=_blk(tt, gw),
        out_shape=SDS((T, Cp), BF16), compiler_params=_seq_params(),
    )(proj, proj)


def _pool_bwd_a(name, dy, z, scale, tt):
    T, Cp = dy.shape

    def body(dy_ref, z_ref, s_ref, dz_ref, ds_ref):
        i = pl.program_id(0)
        dyv = dy_ref[...]
        dz_ref[...] = (dyv * s_ref[...]).astype(BF16)

        @pl.when(i == 0)
        def _():
            ds_ref[...] = jnp.zeros_like(ds_ref)

        ds_ref[...] += _colsum(dyv * z_ref[...])

    row = pl.BlockSpec((tt, Cp), lambda i: (i, 0))
    vec = pl.BlockSpec((1, Cp), lambda i: (0, 0))
    return pl.pallas_call(
        body, name=name, grid=(T // tt,), in_specs=[row, row, vec], out_specs=[row, vec],
        out_shape=[SDS((T, Cp), BF16), SDS((1, Cp), F32)],
        compiler_params=_cparams(dimension_semantics=("arbitrary",)),
    )(dy, z, scale)


def _pool_bwd_b(name, dd, gw, tt):
    T, Cp = dd.shape
    G = Cp // gw
    nT = T // tt

    def body(d_ref, nxt_ref, du_ref):
        j, i = pl.program_id(0), pl.program_id(1)
        win = lax.shift_left(jnp.int32(2), j).astype(F32)
        dv = d_ref[...]
        t1 = (i * tt + 1 + lax.broadcasted_iota(jnp.int32, (tt + HALO, 1), 0)).astype(F32)
        nxt = jnp.where(i < nT - 1, nxt_ref[...], 0.0)
        ext = jnp.concatenate([dv, nxt], axis=0) * (1.0 / jnp.minimum(t1, win))
        n = tt + HALO
        f2 = ext + pltpu.roll(ext, n - 1, 0)
        f4 = f2 + pltpu.roll(f2, n - 2, 0)
        f8 = f4 + pltpu.roll(f4, n - 4, 0)
        f16 = f8 + pltpu.roll(f8, n - 8, 0)
        s = jnp.where(j == 0, f2, jnp.where(j == 1, f4, jnp.where(j == 2, f8, f16)))[:tt]
        du_ref[...] = (s - dv).astype(BF16)

    return pl.pallas_call(
        body, name=name, grid=(G, nT), in_specs=[_blk(tt, gw), _next_halo(tt, gw, T)], out_specs=_blk(tt, gw),
        out_shape=SDS((T, Cp), BF16), compiler_params=_seq_params(),
    )(dd, dd)


def _conv_fwd(name, src, off_cols, w, b, tt, tc):
    T = src.shape[0]
    K, C = w.shape
    off = off_cols // tc

    def body(u_ref, halo_ref, w_ref, b_ref, v_ref, vb_ref):
        i = pl.program_id(1)
        u = u_ref[...]
        halo = jnp.where(i > 0, halo_ref[...], 0.0)
        ext = jnp.concatenate([halo, u], axis=0)
        wv = w_ref[...]
        acc = b_ref[...] + u * wv[K - 1:K]
        for s in range(1, K):
            acc = acc + _lookback(ext, s) * wv[K - 1 - s:K - s]
        v_ref[...] = acc
        vb_ref[...] = acc.astype(BF16)

    return pl.pallas_call(
        body, name=name, grid=(C // tc, T // tt),
        in_specs=[_blk(tt, tc, off), _prev_halo(tt, tc, off), _vec(K, tc), _vec(1, tc)],
        out_specs=[_blk(tt, tc), _blk(tt, tc)], out_shape=[SDS((T, C), F32), SDS((T, C), BF16)],
        compiler_params=_seq_params(),
    )(src, src, w, b)


def _conv_bwd(name, dv, src, off_cols, w, tt, tc):
    T, C = dv.shape
    K = w.shape[0]
    off = off_cols // tc
    nT = T // tt

    def body(dv_ref, nxt_ref, u_ref, halo_ref, w_ref, du_ref, dw_ref, db_ref):
        i = pl.program_id(1)
        d = dv_ref[...]
        nxt = jnp.where(i < nT - 1, nxt_ref[...], 0.0)
        dext = jnp.concatenate([d, nxt], axis=0)
        u = u_ref[...]
        halo = jnp.where(i > 0, halo_ref[...], 0.0)
        uext = jnp.concatenate([halo, u], axis=0)
        wv = w_ref[...]
        du = d * wv[K - 1:K]
        rows = [_colsum(d * u)]
        for s in range(1, K):
            du = du + _lookahead(dext, s, tt) * wv[K - 1 - s:K - s]
            rows.append(_colsum(d * _lookback(uext, s)))
        du_ref[...] = du.astype(BF16)
        dw = jnp.concatenate(rows[::-1] + [jnp.zeros((8 - K, tc), F32)], axis=0)

        @pl.when(i == 0)
        def _():
            dw_ref[...] = jnp.zeros_like(dw_ref)
            db_ref[...] = jnp.zeros_like(db_ref)

        dw_ref[...] += dw
        db_ref[...] += _colsum(d)

    return pl.pallas_call(
        body, name=name, grid=(C // tc, nT),
        in_specs=[_blk(tt, tc), _next_halo(tt, tc, T), _blk(tt, tc, off), _prev_halo(tt, tc, off), _vec(K, tc)],
        out_specs=[_blk(tt, tc), _vec(8, tc), _vec(1, tc)],
        out_shape=[SDS((T, C), BF16), SDS((8, C), F32), SDS((1, C), F32)],
        compiler_params=_seq_params(),
    )(dv, dv, src, src, w)


def _lru_gates(ra, ia, v, ba, bi, lam):
    r = _sigmoid(ra + ba)
    ig = _sigmoid(ia + bi)
    nl = -lam
    ex = jnp.exp(-jnp.abs(nl))
    one_p = 1.0 + ex
    l1p = jnp.where(one_p == 1.0, ex, jnp.log(one_p) * ex / (one_p - 1.0))
    sp = jnp.maximum(nl, 0.0) + l1p
    a = jnp.exp(-LRU_C * r * sp)
    om = 1.0 - a * a
    rs = lax.rsqrt(jnp.maximum(om, 1e-30))
    return r, ig, sp, a, om * rs, rs


def _lru_fwd(name, ra_pre, ia_pre, v, proj, off_cols, ba, bi, lam, tt, tc, side=None):
    T, R = v.shape
    off = off_cols // tc

    def body(ra_ref, ia_ref, v_ref, ug_ref, ba_ref, bi_ref, lam_ref, h_ref, y_ref, carry_ref):
        i = pl.program_id(1)

        @pl.when(i == 0)
        def _():
            carry_ref[...] = jnp.zeros_like(carry_ref)

        vv = v_ref[...]
        _, ig, _, a, mult, _ = _lru_gates(ra_ref[...], ia_ref[...], vv, ba_ref[...], bi_ref[...], lam_ref[...])
        A = a
        B = mult * (ig * vv)
        row8 = lax.broadcasted_iota(jnp.int32, (tt, 1), 0) % 8
        for s in (1, 2, 4):
            m = row8 >= s
            B = jnp.where(m, A * pltpu.roll(B, s, 0) + B, B)
            A = jnp.where(m, A * pltpu.roll(A, s, 0), A)
        carry = carry_ref[7:8, :]
        groups = []
        for g in range(tt // 8):
            hg = A[g * 8:(g + 1) * 8] * carry + B[g * 8:(g + 1) * 8]
            carry = hg[7:8]
            groups.append(hg)
        h = jnp.concatenate(groups, axis=0)
        h_ref[...] = h
        carry_ref[...] = groups[-1]
        y_ref[...] = (h * _gelu(ug_ref[...])).astype(BF16)

    return _ride_call(
        side, body, name=name, grid=(R // tc, T // tt),
        in_specs=[_blk(tt, tc), _blk(tt, tc), _blk(tt, tc), _blk(tt, tc, off), _vec(1, tc), _vec(1, tc), _vec(1, tc)],
        args=[ra_pre, ia_pre, v, proj, ba, bi, lam],
        out_specs=[_blk(tt, tc), _blk(tt, tc)], out_shape=[SDS((T, R), F32), SDS((T, R), BF16)],
        scratch=[pltpu.VMEM((8, tc), F32)], semantics=("parallel", "arbitrary"))


def _lru_bwd(name, dy, hs, proj, off_cols, ra_pre, ia_pre, v, ba, bi, lam, tt, tc):
    T, R = v.shape
    off = off_cols // tc
    nT = T // tt
    r16 = tt // HALO

    def rblk(o=0):
        return pl.BlockSpec((tt, tc), lambda j, i: (nT - 1 - i, j + o))

    hprev = pl.BlockSpec((HALO, tc), lambda j, i: (jnp.maximum((nT - 1 - i) * r16 - 1, 0), j))

    def body(dy_ref, h_ref, hp_ref, ug_ref, ra_ref, ia_ref, v_ref, ba_ref, bi_ref, lam_ref,
             dra_ref, dia_ref, dv_ref, dug_ref, dba_ref, dbi_ref, dlam_ref, dh_carry, a_carry):
        i = pl.program_id(1)

        @pl.when(i == 0)
        def _():
            dh_carry[...] = jnp.zeros_like(dh_carry)
            a_carry[...] = jnp.zeros_like(a_carry)
            dba_ref[...] = jnp.zeros_like(dba_ref)
            dbi_ref[...] = jnp.zeros_like(dbi_ref)
            dlam_ref[...] = jnp.zeros_like(dlam_ref)

        vv = v_ref[...]
        lamv = lam_ref[...]
        r, ig, sp, a, mult, inv_mult = _lru_gates(ra_ref[...], ia_ref[...], vv, ba_ref[...], bi_ref[...], lamv)
        h = h_ref[...]
        gel, dgel = _gelu_and_grad(ug_ref[...])
        dyv = dy_ref[...]
        dug_ref[...] = (dyv * h * dgel).astype(BF16)
        dhs = dyv * gel

        a_ext = jnp.concatenate([a, jnp.broadcast_to(a_carry[0:1, :], (HALO, tc))], axis=0)
        A = _lookahead(a_ext, 1, tt)
        B = dhs
        row8 = lax.broadcasted_iota(jnp.int32, (tt, 1), 0) % 8
        for s in (1, 2, 4):
            m = row8 < 8 - s
            B = jnp.where(m, B + A * pltpu.roll(B, tt - s, 0), B)
            A = jnp.where(m, A * pltpu.roll(A, tt - s, 0), A)
        carry = dh_carry[0:1, :]
        groups = []
        for g in reversed(range(tt // 8)):
            dg = B[g * 8:(g + 1) * 8] + A[g * 8:(g + 1) * 8] * carry
            carry = dg[0:1]
            groups.append(dg)
        dH = jnp.concatenate(groups[::-1], axis=0)
        dh_carry[...] = groups[-1]
        a_carry[...] = a[:8, :]

        hp = jnp.where(i < nT - 1, hp_ref[...], 0.0)
        h_prev = _lookback(jnp.concatenate([hp, h], axis=0), 1)
        da = dH * h_prev
        dmult = dH * (ig * vv)
        div = dH * mult
        dv_ref[...] = div * ig
        dig = div * vv
        dlog_a = da * a - dmult * (a * a) * inv_mult
        dr = dlog_a * (-LRU_C * sp)
        dra = dr * r * (1.0 - r)
        dia = dig * ig * (1.0 - ig)
        dra_ref[...] = dra.astype(BF16)
        dia_ref[...] = dia.astype(BF16)
        dba_ref[...] += _colsum(dra)
        dbi_ref[...] += _colsum(dia)
        dlam_ref[...] += _colsum(dlog_a * (-LRU_C * r))

        @pl.when(i == nT - 1)
        def _():
            dlam_ref[...] = dlam_ref[...] * (-_sigmoid(-lamv))

    return pl.pallas_call(
        body, name=name, grid=(R // tc, nT),
        in_specs=[rblk(), rblk(), hprev, rblk(off), rblk(), rblk(), rblk(), _vec(1, tc), _vec(1, tc), _vec(1, tc)],
        out_specs=[rblk(), rblk(), rblk(), rblk(), _vec(1, tc), _vec(1, tc), _vec(1, tc)],
        out_shape=[SDS((T, R), BF16), SDS((T, R), BF16), SDS((T, R), F32), SDS((T, R), BF16),
                   SDS((1, R), F32), SDS((1, R), F32), SDS((1, R), F32)],
        scratch_shapes=[pltpu.VMEM((8, tc), F32), pltpu.VMEM((8, tc), F32)], compiler_params=_seq_params(),
    )(dy, hs, hs, proj, ra_pre, ia_pre, v, ba, bi, lam)


def _merge_fwd(name, proj, off_cols, b_gate, P, Q, tt, tc, side=None):
    T, D = P.shape
    o0 = off_cols // tc
    o1 = o0 + D // tc

    def body(g0_ref, g1_ref, b0_ref, b1_ref, p_ref, q_ref, m_ref):
        g0 = _sigmoid(g0_ref[...] + b0_ref[...])
        g1 = _sigmoid(g1_ref[...] + b1_ref[...])
        m_ref[...] = (g0 * p_ref[...] + g1 * q_ref[...]).astype(BF16)

    outs = _ride_call(
        side, body, name=name, grid=(D // tc, T // tt),
        in_specs=[_blk(tt, tc, o0), _blk(tt, tc, o1), _vec(1, tc), _vec(1, tc, D // tc), _blk(tt, tc), _blk(tt, tc)],
        args=[proj, proj, b_gate, b_gate, P, Q], out_specs=[_blk(tt, tc)], out_shape=[SDS((T, D), BF16)],
        scratch=[], semantics=("parallel", "arbitrary"))
    return outs if side is not None else outs[0]


def _merge_bwd(name, dm, proj, off_cols, b_gate, P, Q, tt, tc):
    T, D = P.shape
    o0 = off_cols // tc
    o1 = o0 + D // tc

    def body(dm_ref, g0_ref, g1_ref, b0_ref, b1_ref, p_ref, q_ref, dp_ref, dq_ref, dl0_ref, dl1_ref, db0_ref, db1_ref):
        i = pl.program_id(1)
        g0 = _sigmoid(g0_ref[...] + b0_ref[...])
        g1 = _sigmoid(g1_ref[...] + b1_ref[...])
        d = dm_ref[...].astype(F32)
        dp_ref[...] = (d * g0).astype(BF16)
        dq_ref[...] = (d * g1).astype(BF16)
        dl0 = d * p_ref[...] * g0 * (1.0 - g0)
        dl1 = d * q_ref[...] * g1 * (1.0 - g1)
        dl0_ref[...] = dl0.astype(BF16)
        dl1_ref[...] = dl1.astype(BF16)

        @pl.when(i == 0)
        def _():
            db0_ref[...] = jnp.zeros_like(db0_ref)
            db1_ref[...] = jnp.zeros_like(db1_ref)

        db0_ref[...] += _colsum(dl0)
        db1_ref[...] += _colsum(dl1)

    b = _blk(tt, tc)
    return pl.pallas_call(
        body, name=name, grid=(D // tc, T // tt),
        in_specs=[b, _blk(tt, tc, o0), _blk(tt, tc, o1), _vec(1, tc), _vec(1, tc, D // tc), b, b],
        out_specs=[b, b, b, b, _vec(1, tc), _vec(1, tc)],
        out_shape=[SDS((T, D), BF16)] * 4 + [SDS((1, D), F32)] * 2, compiler_params=_seq_params(),
    )(dm, proj, proj, b_gate, b_gate, P, Q)


def _ffn_act_fwd(name, up, w, b, tt, tc):
    T = up.shape[0]
    K, F = w.shape
    nF = F // tc

    def body(gp_ref, halo_ref, val_ref, w_ref, b_ref, z_ref):
        i = pl.program_id(1)
        gp = gp_ref[...].astype(F32)
        halo = jnp.where(i > 0, halo_ref[...].astype(F32), 0.0)
        ext = jnp.concatenate([halo, gp], axis=0)
        wv = w_ref[...]
        c = b_ref[...] + gp * wv[K - 1:K]
        for s in range(1, K):
            c = c + _lookback(ext, s) * wv[K - 1 - s:K - s]
        z_ref[...] = (_gelu(c) * val_ref[...].astype(F32)).astype(BF16)

    return pl.pallas_call(
        body, name=name, grid=(nF, T // tt),
        in_specs=[_blk(tt, tc), _prev_halo(tt, tc), _blk(tt, tc, nF), _vec(K, tc), _vec(1, tc)],
        out_specs=_blk(tt, tc), out_shape=SDS((T, F), BF16), compiler_params=_seq_params(),
    )(up, up, up, w, b)


def _ffn_down_fwd(name, up, w, b, w_down, x1, tm, tk, nsub):
    T = up.shape[0]
    K, F = w.shape
    D = w_down.shape[1]
    nF = F // tk
    sub = tk // nsub
    r16 = tm // HALO

    def body(gp_ref, halo_ref, val_ref, w_ref, b_ref, wd_ref, x1_ref, z_ref, o_ref, acc_ref):
        i, k = pl.program_id(0), pl.program_id(1)
        total = None
        for s in range(nsub):
            cs = pl.ds(s * sub, sub)
            gp = gp_ref[:, cs].astype(F32)
            halo = jnp.where(i > 0, halo_ref[:, cs].astype(F32), 0.0)
            ext = jnp.concatenate([halo, gp], axis=0)
            wv = w_ref[:, cs]
            c = b_ref[:, cs] + gp * wv[K - 1:K]
            for t in range(1, K):
                c = c + _lookback(ext, t) * wv[K - 1 - t:K - t]
            zs = (_gelu(c) * val_ref[:, cs].astype(F32)).astype(BF16)
            z_ref[:, cs] = zs
            p = jnp.dot(zs, wd_ref[cs, :], preferred_element_type=F32)
            total = p if total is None else total + p

        @pl.when(k == 0)
        def _():
            acc_ref[...] = total

        @pl.when(k > 0)
        def _():
            acc_ref[...] += total

        @pl.when(k == nF - 1)
        def _():
            o_ref[...] = acc_ref[...] + x1_ref[...]

    blk = lambda off: pl.BlockSpec((tm, tk), lambda i, k: (i, k + off))
    return pl.pallas_call(
        body, name=name, grid=(T // tm, nF),
        in_specs=[blk(0), pl.BlockSpec((HALO, tk), lambda i, k: (jnp.maximum(i * r16 - 1, 0), k)), blk(nF),
                  pl.BlockSpec((K, tk), lambda i, k: (0, k)), pl.BlockSpec((1, tk), lambda i, k: (0, k)),
                  pl.BlockSpec((tk, D), lambda i, k: (k, 0)), pl.BlockSpec((tm, D), lambda i, k: (i, 0))],
        out_specs=[blk(0), pl.BlockSpec((tm, D), lambda i, k: (i, 0))],
        out_shape=[SDS((T, F), BF16), SDS((T, D), F32)],
        scratch_shapes=[pltpu.VMEM((tm, D), F32)],
        compiler_params=_cparams(dimension_semantics=("parallel", "arbitrary")),
    )(up, up, up, w, b, w_down, x1)


def _ffn_act_bwd(name, up, dz, w, b, tt, tc):
    T = up.shape[0]
    K, F = w.shape
    nF = F // tc
    nT = T // tt

    def body(gp_ref, gph_ref, gpn_ref, val_ref, valn_ref, dz_ref, dzn_ref, w_ref, b_ref, dup_ref, dw_ref, db_ref):
        i = pl.program_id(1)
        wv = w_ref[...]
        gp = gp_ref[...].astype(F32)
        halo = jnp.where(i > 0, gph_ref[...].astype(F32), 0.0)
        full = jnp.concatenate([halo, gp, gpn_ref[...].astype(F32)], axis=0)
        c = b_ref[...] + full * wv[K - 1:K]
        for s in range(1, K):
            c = c + pltpu.roll(full, s, 0) * wv[K - 1 - s:K - s]
        c = c[HALO:]
        gate, dgate = _gelu_and_grad(c)
        dz_e = jnp.concatenate([dz_ref[...], dzn_ref[...]], axis=0).astype(F32)
        val_e = jnp.concatenate([val_ref[...], valn_ref[...]], axis=0).astype(F32)
        dc_e = dz_e * val_e * dgate
        row = lax.broadcasted_iota(jnp.int32, (tt + HALO, 1), 0)
        dc_e = jnp.where((row < tt) | (i < nT - 1), dc_e, 0.0)
        dc = dc_e[:tt]
        dgp = dc * wv[K - 1:K]
        uext = full[:tt + HALO]
        rows = [_colsum(dc * gp)]
        for s in range(1, K):
            dgp = dgp + _lookahead(dc_e, s, tt) * wv[K - 1 - s:K - s]
            rows.append(_colsum(dc * _lookback(uext, s)))
        dup_ref[0] = dgp.astype(BF16)
        dup_ref[1] = (dz_e[:tt] * gate[:tt]).astype(BF16)
        dw = jnp.concatenate(rows[::-1] + [jnp.zeros((8 - K, tc), F32)], axis=0)

        @pl.when(i == 0)
        def _():
            dw_ref[...] = jnp.zeros_like(dw_ref)
            db_ref[...] = jnp.zeros_like(db_ref)

        dw_ref[...] += dw
        db_ref[...] += _colsum(dc)

    return pl.pallas_call(
        body, name=name, grid=(nF, nT),
        in_specs=[_blk(tt, tc), _prev_halo(tt, tc), _next_halo(tt, tc, T), _blk(tt, tc, nF), _next_halo(tt, tc, T, nF),
                  _blk(tt, tc), _next_halo(tt, tc, T), _vec(K, tc), _vec(1, tc)],
        out_specs=[pl.BlockSpec((2, tt, tc), lambda j, i: (0, i, j)), _vec(8, tc), _vec(1, tc)],
        out_shape=[SDS((2, T, F), BF16), SDS((8, F), F32), SDS((1, F), F32)],
        compiler_params=_seq_params(),
    )(up, up, up, up, up, dz, dz, w, b)


def _adamw(name, w, g, m, v):
    R, C = w.shape
    tr = R
    if R * C * 4 > ADAMW_BLOCK_BYTES:
        tr = _pick(R, [t for t in (256, 128, 64, 32, 16, 8) if t * C * 4 <= ADAMW_BLOCK_BYTES])
    c1 = 1.0 - ADAM_B1 ** ADAM_STEP
    c2 = 1.0 - ADAM_B2 ** ADAM_STEP

    def body(w_ref, g_ref, m_ref, v_ref, d_ref, nm_ref, nv_ref):
        gv = g_ref[...]
        nm = ADAM_B1 * m_ref[...] + (1.0 - ADAM_B1) * gv
        nv = ADAM_B2 * v_ref[...] + (1.0 - ADAM_B2) * (gv * gv)
        nm_ref[...] = nm
        nv_ref[...] = nv
        d_ref[...] = -ADAM_LR * ((nm / c1) / (jnp.sqrt(nv / c2) + ADAM_EPS) + ADAM_WD * w_ref[...])

    spec = pl.BlockSpec((tr, C), lambda i: (i, 0))
    return pl.pallas_call(
        body, name=name, grid=(R // tr,), in_specs=[spec] * 4, out_specs=[spec] * 3,
        out_shape=[SDS((R, C), F32)] * 3, compiler_params=_cparams(dimension_semantics=("parallel",)),
    )(w, g, m, v)


def _add_halves(name, part, rcv, pos):
    S, R, C = part.shape
    h = R // 2
    tr = _pick(h, (512, 256, 128, 64, 32, 16))
    nb = h // tr

    def body(pos_ref, p_ref, r_ref, o_ref):
        o_ref[...] = (p_ref[...].astype(F32) + r_ref[...].astype(F32)).astype(BF16)

    return pl.pallas_call(
        body, name=name,
        grid_spec=pltpu.PrefetchScalarGridSpec(
            num_scalar_prefetch=1, grid=(S, nb),
            in_specs=[pl.BlockSpec((None, tr, C), lambda s, r, pos: (s, pos[0] * nb + r, 0)),
                      pl.BlockSpec((None, tr, C), lambda s, r, pos: (s, r, 0))],
            out_specs=pl.BlockSpec((None, tr, C), lambda s, r, pos: (s, r, 0))),
        out_shape=SDS((S, h, C), BF16), compiler_params=_cparams(dimension_semantics=("parallel", "parallel")),
    )(pos, part, rcv)


def _add_chips(name, chipsum, rcv, pos):
    S, h, C = chipsum.shape
    tr = _pick(h, (512, 256, 128, 64, 32, 16))

    def body(pos_ref, own_ref, r_ref, o_ref):
        o_ref[...] = ((own_ref[...].astype(F32) + r_ref[0].astype(F32)) + r_ref[1].astype(F32)) + r_ref[2].astype(F32)

    return pl.pallas_call(
        body, name=name,
        grid_spec=pltpu.PrefetchScalarGridSpec(
            num_scalar_prefetch=1, grid=(h // tr,),
            in_specs=[pl.BlockSpec((None, tr, C), lambda r, pos: (pos[1], r, 0)),
                      pl.BlockSpec((3, tr, C), lambda r, pos: (0, r, 0))],
            out_specs=pl.BlockSpec((None, tr, C), lambda r, pos: (pos[0], r, 0))),
        out_shape=SDS((2, h, C), F32), compiler_params=_cparams(dimension_semantics=("parallel",)),
    )(pos, chipsum, rcv)


def _place():
    x, y, c = lax.axis_index("x"), lax.axis_index("y"), lax.axis_index("c")
    chips = [(1 - x, y), (x, 1 - y), (1 - x, 1 - y)]
    slots = [2 * cx + cy for cx, cy in chips]
    return x, y, c, chips, slots


def _gather_side(shards, split):
    n = len(shards)

    def copies(ins, outs, sems):
        send_sems, recv_sems, local_sems = sems
        x, y, c, chips, slots = _place()
        me = 2 * x + y
        sibling = (x, y, 1 - c)

        def rows(a, half):
            hrows = ins[a].shape[0] // 2
            return pl.ds(half * hrows, hrows)

        def ici(a, j):
            if split[a]:
                src, dst = ins[a].at[rows(a, c)], outs[a].at[me, rows(a, c)]
            else:
                src, dst = ins[a], outs[a].at[me]
            return pltpu.make_async_remote_copy(src_ref=src, dst_ref=dst, send_sem=send_sems.at[a, j],
                                                recv_sem=recv_sems.at[a, j], device_id=(*chips[j], c), device_id_type=MESH)

        def landed(a, j):
            dst = outs[a].at[slots[j], rows(a, c)] if split[a] else outs[a].at[slots[j]]
            return pltpu.make_async_remote_copy(src_ref=dst, dst_ref=dst, send_sem=send_sems.at[a, j],
                                                recv_sem=recv_sems.at[a, j], device_id=(*chips[j], c), device_id_type=MESH)

        def d2d(a, j, half):
            blk = outs[a].at[slots[j], rows(a, half)]
            return pltpu.make_async_remote_copy(src_ref=blk, dst_ref=blk, send_sem=send_sems.at[a, 3 + j],
                                                recv_sem=recv_sems.at[a, 3 + j], device_id=sibling, device_id_type=MESH)

        local = [pltpu.make_async_copy(ins[a], outs[a].at[me], local_sems.at[a]) for a in range(n)]
        return c, ici, landed, d2d, local

    def start(ins, outs, sems):
        c, ici, landed, d2d, local = copies(ins, outs, sems)
        for cp in local:
            cp.start()
        for a in range(n):
            for j in range(3):
                ici(a, j).start()

    def mid(ins, outs, sems):
        c, ici, landed, d2d, local = copies(ins, outs, sems)
        for a in range(n):
            for j in range(3):
                landed(a, j).wait_recv()
                if split[a]:
                    d2d(a, j, c).start()

    def finish(ins, outs, sems):
        c, ici, landed, d2d, local = copies(ins, outs, sems)
        for a in range(n):
            if split[a]:
                for j in range(3):
                    d2d(a, j, 1 - c).wait_recv()
        for a in range(n):
            for j in range(3):
                ici(a, j).wait_send()
                if split[a]:
                    d2d(a, j, c).wait_send()
        for cp in local:
            cp.wait()

    return _Side(shards, [SDS((N_SLOTS,) + s.shape, s.dtype) for s in shards],
                 [pltpu.SemaphoreType.DMA((n, 6)), pltpu.SemaphoreType.DMA((n, 6)), pltpu.SemaphoreType.DMA((n,))],
                 start, mid, finish)


def _swap_partial_halves(name, parts):
    side = _swap_side(parts)
    n = len(parts)

    def body(*refs):
        ins, outs, sems = refs[:n], refs[n:2 * n], refs[2 * n:]
        side.start(ins, outs, sems)
        side.finish(ins, outs, sems)

    return pl.pallas_call(
        body, name=name, in_specs=[ANY] * n, out_specs=[ANY] * n, out_shape=side.out_shapes,
        scratch_shapes=side.sem_shapes, compiler_params=_cparams(has_side_effects=True),
    )(*parts)


def _swap_side(parts):
    n = len(parts)

    def copies(ins, outs, sems):
        send_sems, recv_sems = sems
        x, y, c, _, _ = _place()
        cps = []
        for a in range(n):
            h = ins[a].shape[1] // 2
            cps.append(pltpu.make_async_remote_copy(
                src_ref=ins[a].at[:, pl.ds((1 - c) * h, h)], dst_ref=outs[a], send_sem=send_sems.at[a],
                recv_sem=recv_sems.at[a], device_id=(x, y, 1 - c), device_id_type=MESH))
        return cps

    def start(ins, outs, sems):
        for cp in copies(ins, outs, sems):
            cp.start()

    def finish(ins, outs, sems):
        for cp in copies(ins, outs, sems):
            cp.wait()

    return _Side(parts, [SDS((p.shape[0], p.shape[1] // 2, p.shape[2]), p.dtype) for p in parts],
                 [pltpu.SemaphoreType.DMA((n,)), pltpu.SemaphoreType.DMA((n,))], start, None, finish)


def _exchange_side(sums):
    n = len(sums)

    def copies(ins, outs, sems):
        send_sems, recv_sems = sems
        x, y, c, chips, slots = _place()
        return [pltpu.make_async_remote_copy(
            src_ref=ins[a].at[slots[j]], dst_ref=outs[a].at[j], send_sem=send_sems.at[a, j],
            recv_sem=recv_sems.at[a, j], device_id=(*chips[j], c), device_id_type=MESH)
            for a in range(n) for j in range(3)]

    def start(ins, outs, sems):
        for cp in copies(ins, outs, sems):
            cp.start()

    def finish(ins, outs, sems):
        for cp in copies(ins, outs, sems):
            cp.wait()

    return _Side(sums, [SDS((3,) + s.shape[1:], s.dtype) for s in sums],
                 [pltpu.SemaphoreType.DMA((n, 3)), pltpu.SemaphoreType.DMA((n, 3))], start, None, finish)


def _swap_reduced_halves(totals):
    side = _reduced_swap_side(totals)
    n = len(totals)

    def body(*refs):
        ins, outs, sems = refs[:n], refs[n:2 * n], refs[2 * n:]
        side.start(ins, outs, sems)
        side.finish(ins, outs, sems)

    return pl.pallas_call(
        body, name="swap_reduced_halves", in_specs=[ANY] * n, out_specs=[ANY] * n, out_shape=side.out_shapes,
        input_output_aliases={a: a for a in range(n)}, scratch_shapes=side.sem_shapes,
        compiler_params=_cparams(has_side_effects=True),
    )(*totals)


def _reduced_swap_side(totals):
    n = len(totals)

    def sends(outs, sems):
        send_sems, recv_sems = sems
        x, y, c, _, _ = _place()
        return [pltpu.make_async_remote_copy(src_ref=outs[a].at[c], dst_ref=outs[a].at[c], send_sem=send_sems.at[a],
                                             recv_sem=recv_sems.at[a], device_id=(x, y, 1 - c), device_id_type=MESH)
                for a in range(n)]

    def start(ins, outs, sems):
        for cp in sends(outs, sems):
            cp.start()

    def finish(ins, outs, sems):
        send_sems, recv_sems = sems
        x, y, c, _, _ = _place()
        for a in range(n):
            got = outs[a].at[1 - c]
            pltpu.make_async_remote_copy(src_ref=got, dst_ref=got, send_sem=send_sems.at[a], recv_sem=recv_sems.at[a],
                                         device_id=(x, y, 1 - c), device_id_type=MESH).wait_recv()
        for cp in sends(outs, sems):
            cp.wait_send()

    return _Side(totals, [SDS(t.shape, t.dtype) for t in totals],
                 [pltpu.SemaphoreType.DMA((n,)), pltpu.SemaphoreType.DMA((n,))], start, None, finish, in_place=True)


def _allreduce_small(pack):
    rows = pack.shape[0]

    def body(in_ref, out_ref, buf, send_sems, recv_sems):
        x, y, c = lax.axis_index("x"), lax.axis_index("y"), lax.axis_index("c")
        me = 4 * x + 2 * y + c
        buf[me] = in_ref[...]
        cps = []
        for k in range(1, 8):
            kx, ky, kc = (k >> 2) & 1, (k >> 1) & 1, k & 1
            peer = (x ^ kx, y ^ ky, c ^ kc)
            cps.append(pltpu.make_async_remote_copy(src_ref=in_ref, dst_ref=buf.at[me], send_sem=send_sems.at[k - 1],
                                                    recv_sem=recv_sems.at[k - 1], device_id=peer, device_id_type=MESH))
        for cp in cps:
            cp.start()
        for k in range(1, 8):
            got = buf.at[me ^ k]
            pltpu.make_async_remote_copy(src_ref=got, dst_ref=got, send_sem=send_sems.at[k - 1],
                                         recv_sem=recv_sems.at[k - 1], device_id=(x, y, c), device_id_type=MESH).wait_recv()
        for cp in cps:
            cp.wait_send()
        acc = buf[0]
        for d in range(1, 8):
            acc = acc + buf[d]
        out_ref[...] = acc

    vm = pl.BlockSpec(memory_space=pltpu.VMEM)
    return pl.pallas_call(
        body, name="allreduce_small", in_specs=[vm], out_specs=vm, out_shape=SDS(pack.shape, F32),
        scratch_shapes=[pltpu.VMEM((8, rows, LANES), F32), pltpu.SemaphoreType.DMA((7,)), pltpu.SemaphoreType.DMA((7,))],
        compiler_params=_cparams(has_side_effects=True),
    )(pack)


def _rows2d(a):
    return a.reshape((-1, a.shape[-1]))


def _from_slots_blockdiag(g, H, w):
    q = w // N_SLOTS
    return g.reshape(N_SLOTS, H, q, w).transpose(1, 0, 2, 3).reshape(H, w, w)


def _to_slots_blockdiag(d, H, w):
    q = w // N_SLOTS
    return d.reshape(H, N_SLOTS, q, w).transpose(1, 0, 2, 3).reshape(N_SLOTS, H * q, w)


def kernel(x, g_mix, w_in, b_gate, w_pool, pool_scale, lru_conv_w, lru_conv_b, w_a, b_a, w_i, b_i, lru_lambda, w_pool_proj, w_lru_proj, w_out, g_mlp, w_up, ffn_conv_w, ffn_conv_b, w_down, g_final, loss_target, m_g_mix, m_w_in, m_b_gate, m_w_pool, m_pool_scale, m_lru_conv_w, m_lru_conv_b, m_w_a, m_b_a, m_w_i, m_b_i, m_lru_lambda, m_w_pool_proj, m_w_lru_proj, m_w_out, m_g_mlp, m_w_up, m_ffn_conv_w, m_ffn_conv_b, m_w_down, m_g_final, v_g_mix, v_w_in, v_b_gate, v_w_pool, v_pool_scale, v_lru_conv_w, v_lru_conv_b, v_w_a, v_b_a, v_w_i, v_b_i, v_lru_lambda, v_w_pool_proj, v_w_lru_proj, v_w_out, v_g_mlp, v_w_up, v_ffn_conv_w, v_ffn_conv_b, v_w_down, v_g_final):
    weights = dict(g_mix=g_mix, w_in=w_in, b_gate=b_gate, w_pool=w_pool, pool_scale=pool_scale, lru_conv_w=lru_conv_w,
                   lru_conv_b=lru_conv_b, w_a=w_a, b_a=b_a, w_i=w_i, b_i=b_i, lru_lambda=lru_lambda,
                   w_pool_proj=w_pool_proj, w_lru_proj=w_lru_proj, w_out=w_out, g_mlp=g_mlp, w_up=w_up,
                   ffn_conv_w=ffn_conv_w, ffn_conv_b=ffn_conv_b, w_down=w_down, g_final=g_final)
    mom_m = dict(g_mix=m_g_mix, w_in=m_w_in, b_gate=m_b_gate, w_pool=m_w_pool, pool_scale=m_pool_scale,
                 lru_conv_w=m_lru_conv_w, lru_conv_b=m_lru_conv_b, w_a=m_w_a, b_a=m_b_a, w_i=m_w_i, b_i=m_b_i,
                 lru_lambda=m_lru_lambda, w_pool_proj=m_w_pool_proj, w_lru_proj=m_w_lru_proj, w_out=m_w_out,
                 g_mlp=m_g_mlp, w_up=m_w_up, ffn_conv_w=m_ffn_conv_w, ffn_conv_b=m_ffn_conv_b, w_down=m_w_down,
                 g_final=m_g_final)
    mom_v = dict(g_mix=v_g_mix, w_in=v_w_in, b_gate=v_b_gate, w_pool=v_w_pool, pool_scale=v_pool_scale,
                 lru_conv_w=v_lru_conv_w, lru_conv_b=v_lru_conv_b, w_a=v_w_a, b_a=v_b_a, w_i=v_w_i, b_i=v_b_i,
                 lru_lambda=v_lru_lambda, w_pool_proj=v_w_pool_proj, w_lru_proj=v_w_lru_proj, w_out=v_w_out,
                 g_mlp=v_g_mlp, w_up=v_w_up, ffn_conv_w=v_ffn_conv_w, ffn_conv_b=v_ffn_conv_b, w_down=v_w_down,
                 g_final=v_g_final)
    order = list(weights)

    T, D = x.shape[1], x.shape[2]
    Cp = pool_scale.shape[1]
    G, gw = w_pool.shape[1], w_pool.shape[3]
    H, hw = w_a.shape[1], w_a.shape[3]
    R = b_a.shape[1]
    F = ffn_conv_b.shape[1]
    KC, KF = lru_conv_w.shape[1], ffn_conv_w.shape[1]
    p0, p1, p2 = Cp, Cp + R, Cp + 2 * R
    xs = x.reshape(T, D)
    tgt = loss_target.reshape(T, D)

    my_x, my_y, my_c = lax.axis_index("x"), lax.axis_index("y"), lax.axis_index("c")
    slot = 2 * my_x + my_y
    pos = jnp.stack([my_c, slot]).astype(jnp.int32)

    big = ["w_in", "w_pool", "w_a", "w_i", "w_pool_proj", "w_lru_proj", "w_out", "w_up", "w_down"]
    first = big[:4]
    shard16 = {n: _rows2d(weights[n][0]).astype(BF16) for n in big}
    first_side = _gather_side([shard16[n] for n in first] + [lru_conv_w[0], ffn_conv_w[0]], [True] * 4 + [False, False])
    n_in, n_up, n_pp = w_in.shape[2], w_up.shape[2], w_pool_proj.shape[2]

    tt_row = _pick(T, (256, 128))
    tt_seq = _pick(T, (512, 256, 128))
    tt_pool = _pick(T, (1024, 512, 256, 128))
    tc_seq = _pick(R, (512, 256, 128))
    tm1 = _pick(T, (1024, 512, 256))
    tm2 = _pick(T, (2048, 1024, 512, 256))
    tkT = _pick(T, (1024, 512, 256))
    tkT2 = _pick(T, (4096, 2048, 1024, 512, 256))
    tn_in = _pick(n_in, (768, 1152, 384, 128))
    tn_up = _pick(n_up, (1536, 1024, 768, 512, 128))
    tk_up = _pick(n_up, (1024, 768, 512, 128))
    tn_pp = _pick(n_pp, (512, 256, 128))
    tD1 = _pick(D, (1024, 512))
    tD2 = _pick(D, (2048, 1024, 512))
    tF = _pick(F, (1536, 1024, 512))
    tkF = _pick(F, (2048, 1536, 1024, 512))

    h, *gathered = _rms_fwd("rms1_fwd", xs, g_mix, tt_row, side=first_side)
    gw_ = dict(zip(first + ["lru_conv_w", "ffn_conv_w"], gathered))
    W_in = gw_["w_in"]
    W_pool = _from_slots_blockdiag(gw_["w_pool"], G, gw)
    W_a = _from_slots_blockdiag(gw_["w_a"], H, hw)
    W_i = _from_slots_blockdiag(gw_["w_i"], H, hw)
    cw_lru = gw_["lru_conv_w"].transpose(1, 0, 2).reshape(KC, R)
    cw_ffn = gw_["ffn_conv_w"].transpose(1, 0, 2).reshape(KF, F)
    on_proj = ["w_pool_proj", "w_lru_proj", "w_out", "w_down"]
    proj3, *late = _matmul("proj_fwd", "nn", h, W_in, out_dtype=F32, tm=tm2, tn=tn_in, tk=D,
                           side=_gather_side([shard16[n] for n in on_proj], [True] * len(on_proj)))
    proj = proj3[0]
    gw_.update(zip(on_proj, late))
    W_pp = gw_["w_pool_proj"]
    W_lp = gw_["w_lru_proj"].reshape(R, D)
    W_out = gw_["w_out"].reshape(D, D)
    W_down = gw_["w_down"].reshape(F, D)
    up16, eD = shard16["w_up"], D // 8
    d_pool = _pool_fwd("pool_fwd", proj, Cp, gw, tt_pool)
    z_pool, y_pool = _bd_fwd("pool_mix_fwd", d_pool, [W_pool], tm1, scale=pool_scale)
    v_f, v_b = _conv_fwd("lru_conv_fwd", proj, p0, cw_lru, lru_conv_b, tt_seq, tc_seq)
    ra_pre, ia_pre = _bd_fwd("lru_gate_fwd", v_b, [W_a, W_i], tm1)
    hs, y_lru, up_a = _lru_fwd("lru_scan_fwd", ra_pre, ia_pre, v_f, proj, p1, b_a, b_i, lru_lambda, tt_seq, tc_seq,
                               side=_gather_side([up16[:3 * eD]], [True]))
    P = _matmul("pool_proj_fwd", "nn", y_pool, W_pp, out_dtype=F32, tm=tm2, tn=tn_pp, tk=Cp)[0]
    Q3, up_b = _matmul("lru_proj_fwd", "nn", y_lru, W_lp, out_dtype=F32, tm=tm1, tn=tD1, tk=R,
                       side=_gather_side([up16[3 * eD:5 * eD]], [True]))
    Q = Q3[0]
    merged, up_c = _merge_fwd("merge_fwd", proj, p2, b_gate, P, Q, tt_seq, tc_seq,
                              side=_gather_side([up16[5 * eD:6 * eD]], [True]))
    x13, up_d = _matmul("out_fwd", "nn", merged, W_out, out_dtype=F32, tm=tm1, tn=tD1, tk=D, res=xs,
                        side=_gather_side([up16[6 * eD:]], [True]))
    x1 = x13[0]
    W_up = jnp.concatenate([up_a, up_b, up_c, up_d], axis=1)
    h2 = _rms_fwd("rms2_fwd", x1, g_mlp, tt_row)
    up = _matmul("up_fwd", "nn", h2, W_up, out_dtype=BF16, tm=tm1, tn=tn_up, tk=D)[0]
    z, x2 = _ffn_down_fwd("ffn_down_fwd", up, cw_ffn, ffn_conv_b, W_down, x1, _pick(T, (512, 256)),
                          _pick(F, (1024, 512)), 4)

    dx2, dx2b, d_g_final, lossvec = _final_loss_bwd("final_loss_bwd", x2, tgt, g_final.reshape(1, D), tt_row)
    dz = _matmul("down_bwd_x", "nt", dx2b, W_down, out_dtype=BF16, tm=tm1, tn=tF, tk=D)[0]
    dW_down = _matmul("down_bwd_w", "tn", z, dx2b, out_dtype=BF16, tm=_pick(F, (1024, 512)), tn=tD1, tk=tkT2)
    dup, d_cw_ffn, d_ffn_b = _ffn_act_bwd("ffn_act_bwd", up, dz, cw_ffn, ffn_conv_b, tt_seq, tc_seq)
    dh2 = _matmul("up_bwd_x", "nt", dup, W_up, out_dtype=F32, tm=tm1, tn=tD1, tk=n_up)[0]
    dW_up = _matmul("up_bwd_w", "tn", h2, dup, out_seg=N_SLOTS, out_dtype=BF16, tm=tD1, tn=tk_up, tk=tkT2)
    dx1, dx1b, d_g_mlp = _rms_bwd("rms2_bwd", dh2, x1, g_mlp, dx2, tt_row, True)
    dmerged = _matmul("out_bwd_x", "nt", dx1b, W_out, out_dtype=BF16, tm=tm1, tn=tD2, tk=D)[0]
    q_rows = lambda a, rows: a.reshape(N_SLOTS, rows // N_SLOTS, a.shape[-1])
    ffn_parts = [dW_up, q_rows(dW_down, F)]
    dW_out, *ffn_sib = _matmul("out_bwd_w", "tn", merged, dx1b, out_dtype=BF16, tm=tD1, tn=tD1, tk=tkT2,
                               side=_swap_side(ffn_parts))
    ffn_sums = [_add_halves(f"add_halves_{n}", p, r, pos) for n, p, r in zip(big[7:], ffn_parts, ffn_sib)]
    dP, dQ, dgl0, dgl1, d_bg0, d_bg1 = _merge_bwd("merge_bwd", dmerged, proj, p2, b_gate, P, Q, tt_seq, tc_seq)
    dy_pool = _matmul("pool_proj_bwd_x", "nt", dP, W_pp, out_dtype=F32, tm=tm2, tn=Cp, tk=tn_pp)[0]
    dW_pp = _matmul("pool_proj_bwd_w", "tn", y_pool, dP, out_seg=N_SLOTS, out_dtype=BF16, tm=Cp, tn=tn_pp,
                    tk=_pick(T, (2048, 1024, 512, 256)))
    dy_lru = _matmul("lru_proj_bwd_x", "nt", dQ, W_lp, out_dtype=F32, tm=tm1, tn=tD1, tk=D)[0]
    dW_lp = _matmul("lru_proj_bwd_w", "tn", y_lru, dQ, out_dtype=BF16, tm=_pick(R, (1024, 512)), tn=tD1, tk=tkT2)
    dra, dia, dv1, du_gelu, d_b_a, d_b_i, d_lam = _lru_bwd(
        "lru_scan_bwd", dy_lru, hs, proj, p1, ra_pre, ia_pre, v_f, b_a, b_i, lru_lambda, tt_seq, tc_seq)
    dv = _bd_bwd_x("lru_gate_bwd_x", [dra, dia], [W_a, W_i], tm1, res=dv1)
    dW_a, dW_i = _bd_bwd_w("lru_gate_bwd_w", v_b, [dra, dia], hw, tkT)
    du_lru, d_cw_lru, d_lru_b = _conv_bwd("lru_conv_bwd", dv, proj, p0, cw_lru, tt_seq, tc_seq)
    dzp, d_pool_scale = _pool_bwd_a("pool_scale_bwd", dy_pool, z_pool, pool_scale, tt_row)
    dd = _bd_bwd_x("pool_mix_bwd_x", [dzp], [W_pool], tm1)
    (dW_pool,) = _bd_bwd_w("pool_mix_bwd_w", d_pool, [dzp], gw, tkT)
    du_pool = _pool_bwd_b("pool_bwd", dd, gw, tt_pool)
    dproj = jnp.concatenate([du_pool, du_lru, du_gelu, dgl0, dgl1], axis=1)

    mix_parts = [_to_slots_blockdiag(dW_pool.astype(BF16), G, gw),
                 _to_slots_blockdiag(dW_a.astype(BF16), H, hw),
                 _to_slots_blockdiag(dW_i.astype(BF16), H, hw),
                 dW_pp, q_rows(dW_lp, R), q_rows(dW_out, D)]
    mix_sib = _swap_partial_halves("swap_partial_halves", mix_parts)
    mix_sums = [_add_halves(f"add_halves_{n}", p, r, pos) for n, p, r in zip(big[1:7], mix_parts, mix_sib)]
    dW_in, *ffn_chips = _matmul("proj_bwd_w", "tn", h, dproj, out_seg=N_SLOTS, out_dtype=BF16, tm=tD1, tn=tn_in,
                                tk=tkT2, side=_exchange_side(ffn_sums))
    (in_sib,) = _swap_partial_halves("swap_partial_halves_w_in", [dW_in])
    in_sum = _add_halves("add_halves_w_in", dW_in, in_sib, pos)
    ffn_totals = [_add_chips(f"add_chips_{n}", s, r, pos) for n, s, r in zip(big[7:], ffn_sums, ffn_chips)]
    dh3, *rest = _matmul("proj_bwd_x", "nt", dproj, W_in, out_dtype=F32, tm=tm1, tn=tD1, tk=n_in,
                         side=_both(_exchange_side([in_sum] + mix_sums), _reduced_swap_side(ffn_totals)))
    chips, ffn_full = rest[:7], rest[7:]
    grad_x, d_g_mix = _rms_bwd("rms1_bwd", dh3[0], xs, g_mix, dx1, tt_row, False)
    totals = [_add_chips(f"add_chips_{n}", s, r, pos) for n, s, r in zip(big[:7], [in_sum] + mix_sums, chips)]
    full = list(_swap_reduced_halves(totals)) + list(ffn_full)
    grads = {n: f.reshape(weights[n].shape) for n, f in zip(big, full)}

    small = ["g_mix", "b_gate", "pool_scale", "lru_conv_b", "b_a", "b_i", "lru_lambda", "g_mlp", "ffn_conv_b", "g_final"]
    small_g = [d_g_mix, jnp.concatenate([d_bg0, d_bg1], axis=1), d_pool_scale, d_lru_b, d_b_a, d_b_i, d_lam, d_g_mlp,
               d_ffn_b, d_g_final]
    pieces = [g.reshape(-1) for g in small_g] + [d_cw_lru[:KC].reshape(-1), d_cw_ffn[:KF].reshape(-1), lossvec.reshape(-1)]
    sizes = [p.shape[0] for p in pieces]
    total = sum(sizes)
    rows = -(-total // (8 * LANES)) * 8
    pack = jnp.concatenate(pieces + [jnp.zeros((rows * LANES - total,), F32)]).reshape(rows, LANES)
    summed = _allreduce_small(pack).reshape(-1)
    offs = [0]
    for s in sizes:
        offs.append(offs[-1] + s)
    for k, n in enumerate(small):
        grads[n] = summed[offs[k]:offs[k + 1]].reshape(weights[n].shape)
    ns = len(small)
    g_cw_lru = summed[offs[ns]:offs[ns + 1]].reshape(KC, R)
    g_cw_ffn = summed[offs[ns + 1]:offs[ns + 2]].reshape(KF, F)
    grads["lru_conv_w"] = lax.dynamic_slice_in_dim(g_cw_lru, slot * (R // N_SLOTS), R // N_SLOTS, axis=1)[None]
    grads["ffn_conv_w"] = lax.dynamic_slice_in_dim(g_cw_ffn, slot * (F // N_SLOTS), F // N_SLOTS, axis=1)[None]
    loss = jnp.sum(summed[offs[ns + 2]:offs[ns + 3]]) * (0.5 / D)

    delta, new_m, new_v = {}, {}, {}
    for n in big:
        d_, m_, v_ = _adamw(f"adamw_{n}", _rows2d(weights[n][0]), _rows2d(grads[n][0]), _rows2d(mom_m[n][0]),
                            _rows2d(mom_v[n][0]))
        delta[n], new_m[n], new_v[n] = (t.reshape(weights[n].shape) for t in (d_, m_, v_))
    rest = small + ["lru_conv_w", "ffn_conv_w"]
    rsizes = [weights[n].size for n in rest]
    rtotal = sum(rsizes)
    rrows = -(-rtotal // (8 * LANES)) * 8

    def packed(tree):
        flat = [tree[n].reshape(-1) for n in rest] + [jnp.zeros((rrows * LANES - rtotal,), F32)]
        return jnp.concatenate(flat).reshape(rrows, LANES)

    d_, m_, v_ = _adamw("adamw_small", packed(weights), packed(grads), packed(mom_m), packed(mom_v))
    o = 0
    for n, s in zip(rest, rsizes):
        for tree, flat in ((delta, d_), (new_m, m_), (new_v, v_)):
            tree[n] = flat.reshape(-1)[o:o + s].reshape(weights[n].shape)
        o += s

    return (loss, grad_x.reshape(x.shape), *[grads[n] for n in order], *[delta[n] for n in order],
            *[new_m[n] for n in order], *[new_v[n] for n in order])
```

```python
import functools

import jax
import jax.numpy as jnp
from jax import lax
from jax.experimental import pallas as pl
from jax.experimental.pallas import tpu as pltpu

F32 = jnp.float32
BF16 = jnp.bfloat16
MESH = pl.DeviceIdType.MESH

VMEM_LIMIT_V7X = 56 * 1024 * 1024
LANES = 128
HALO = 16
N_SLOTS = 4
ADAMW_BLOCK_BYTES = 3 * 512 * 1024

EPS = 1e-6
LRU_C = 8.0
ADAM_LR = 0.001
ADAM_B1 = 0.9
ADAM_B2 = 0.999
ADAM_EPS = 1e-08
ADAM_WD = 0.01
ADAM_STEP = 10
GELU_C = 0.7978845608028654
GELU_A = 0.044715

SDS = jax.ShapeDtypeStruct
ANY = pl.BlockSpec(memory_space=pl.ANY)


def _pick(n, cands):
    for c in cands:
        if c <= n and n % c == 0:
            return c
    raise ValueError(f"no tile for {n} in {cands}")


def _cparams(**kw):
    return pltpu.CompilerParams(vmem_limit_bytes=VMEM_LIMIT_V7X, **kw)


def _sigmoid(x):
    return 0.5 * jnp.tanh(0.5 * x) + 0.5


def _gelu(x):
    th = jnp.tanh(x * (GELU_C + (GELU_C * GELU_A) * (x * x)))
    hx = 0.5 * x
    return hx + hx * th


def _gelu_and_grad(x):
    x2 = x * x
    th = jnp.tanh(x * (GELU_C + (GELU_C * GELU_A) * x2))
    hx = 0.5 * x
    g = hx + hx * th
    dg = (0.5 + 0.5 * th) + (hx * (1.0 - th * th)) * (GELU_C + (3.0 * GELU_C * GELU_A) * x2)
    return g, dg


def _lookback(ext, k):
    return pltpu.roll(ext, k, 0)[HALO:]


def _lookahead(ext, k, tt):
    return pltpu.roll(ext, ext.shape[0] - k, 0)[:tt]


def _colsum(v):
    return jnp.sum(v, axis=0, keepdims=True)


def _seg3(a):
    return a if a.ndim == 3 else a.reshape((1,) + a.shape)


def _seg_spec(shape3, br, bc, rowf, colf):
    assert shape3[1] % br == 0 and shape3[2] % bc == 0, (shape3, br, bc)
    nb = shape3[2] // bc

    def imap(i, j, k):
        cb = colf(i, j, k)
        return (cb // nb, rowf(i, j, k), cb % nb)

    return pl.BlockSpec((None, br, bc), imap)


_I = lambda i, j, k: i
_J = lambda i, j, k: j
_K = lambda i, j, k: k


class _Side:
    def __init__(self, ins, out_shapes, sem_shapes, start, mid, finish, in_place=False):
        self.ins, self.out_shapes, self.sem_shapes = list(ins), list(out_shapes), list(sem_shapes)
        self.start, self.mid, self.finish = start, mid, finish
        self.alias = [(a, a) for a in range(len(self.ins))] if in_place else []


def _both(s1, s2):
    ni, no, ns = len(s1.ins), len(s1.out_shapes), len(s1.sem_shapes)

    def phase(f1, f2):
        if f1 is None and f2 is None:
            return None

        def run(ins, outs, sems):
            if f1 is not None:
                f1(ins[:ni], outs[:no], sems[:ns])
            if f2 is not None:
                f2(ins[ni:], outs[no:], sems[ns:])

        return run

    both = _Side(s1.ins + s2.ins, s1.out_shapes + s2.out_shapes, s1.sem_shapes + s2.sem_shapes,
                 phase(s1.start, s2.start), phase(s1.mid, s2.mid), phase(s1.finish, s2.finish))
    both.alias = s1.alias + [(ni + a, no + b) for a, b in s2.alias]
    return both


def _ride(side, body, n_in, n_out, n_scr, grid):
    n_sin, n_sout = len(side.ins), len(side.out_shapes)
    n_steps = 1
    for g in grid:
        n_steps *= g

    def wrapped(*refs):
        p = 0
        ins, s_in = refs[p:p + n_in], refs[p + n_in:p + n_in + n_sin]
        p += n_in + n_sin
        outs, s_out = refs[p:p + n_out], refs[p + n_out:p + n_out + n_sout]
        p += n_out + n_sout
        scr, s_sem = refs[p:p + n_scr], refs[p + n_scr:]
        step = 0
        for d, g in enumerate(grid):
            step = step * g + pl.program_id(d)

        @pl.when(step == 0)
        def _():
            side.start(s_in, s_out, s_sem)

        body(*ins, *outs, *scr)

        if side.mid is not None:
            @pl.when(step == (2 * n_steps) // 3)
            def _():
                side.mid(s_in, s_out, s_sem)

        @pl.when(step == n_steps - 1)
        def _():
            side.finish(s_in, s_out, s_sem)

    return wrapped


def _ride_call(side, body, *, name, grid, in_specs, args, out_specs, out_shape, scratch, semantics):
    out_specs, out_shape = list(out_specs), list(out_shape)
    aliases = {}
    if side is not None:
        aliases = {len(args) + a: len(out_shape) + b for a, b in side.alias}
        body = _ride(side, body, len(args), len(out_shape), len(scratch), grid)
        in_specs = list(in_specs) + [ANY] * len(side.ins)
        args = list(args) + side.ins
        out_specs = out_specs + [ANY] * len(side.out_shapes)
        out_shape = out_shape + side.out_shapes
        scratch = list(scratch) + side.sem_shapes
        semantics = ("arbitrary",) * len(grid)
    return pl.pallas_call(
        body, name=name, grid=grid, in_specs=in_specs, out_specs=out_specs, out_shape=out_shape,
        scratch_shapes=scratch, input_output_aliases=aliases,
        compiler_params=_cparams(dimension_semantics=semantics),
    )(*args)


def _matmul(name, mode, a, b, *, out_seg=1, out_dtype, tm, tn, tk, res=None, side=None):
    a3, b3 = _seg3(a), _seg3(b)
    if mode == "nn":
        M, K, N = a3.shape[1], a3.shape[0] * a3.shape[2], b3.shape[0] * b3.shape[2]
        assert b3.shape[1] == K
        a_spec = _seg_spec(a3.shape, tm, tk, _I, _K)
        b_spec = _seg_spec(b3.shape, tk, tn, _K, _J)
        dims = (((1,), (0,)), ((), ()))
    elif mode == "nt":
        M, K, N = a3.shape[1], a3.shape[0] * a3.shape[2], b3.shape[1]
        assert b3.shape[0] * b3.shape[2] == K
        a_spec = _seg_spec(a3.shape, tm, tk, _I, _K)
        b_spec = _seg_spec(b3.shape, tn, tk, _J, _K)
        dims = (((1,), (1,)), ((), ()))
    else:
        K, M, N = a3.shape[1], a3.shape[0] * a3.shape[2], b3.shape[0] * b3.shape[2]
        assert b3.shape[1] == K
        a_spec = _seg_spec(a3.shape, tk, tm, _K, _I)
        b_spec = _seg_spec(b3.shape, tk, tn, _K, _J)
        dims = (((0,), (0,)), ((), ()))
    out3 = (out_seg, M, N // out_seg)
    o_spec = _seg_spec(out3, tm, tn, _I, _J)
    grid = (M // tm, N // tn, K // tk)
    nk = grid[2]
    has_res = res is not None

    def body(*refs):
        a_ref, b_ref = refs[0], refs[1]
        r_ref = refs[2] if has_res else None
        o_ref = refs[3] if has_res else refs[2]
        acc_ref = refs[-1] if nk > 1 else None
        prod = lax.dot_general(a_ref[...], b_ref[...], dims, preferred_element_type=F32)

        def finish(val):
            if has_res:
                val = val + r_ref[...]
            o_ref[...] = val.astype(o_ref.dtype)

        if nk == 1:
            finish(prod)
        else:
            k = pl.program_id(2)

            @pl.when(k == 0)
            def _():
                acc_ref[...] = prod

            @pl.when(k > 0)
            def _():
                acc_ref[...] += prod

            @pl.when(k == nk - 1)
            def _():
                finish(acc_ref[...])

    in_specs = [a_spec, b_spec]
    args = [a3, b3]
    if has_res:
        in_specs.append(pl.BlockSpec((tm, tn), lambda i, j, k: (i, j)))
        args.append(res)
    outs = _ride_call(side, body, name=name, grid=grid, in_specs=in_specs, args=args, out_specs=[o_spec],
                      out_shape=[SDS(out3, out_dtype)], scratch=[pltpu.VMEM((tm, tn), F32)] if nk > 1 else [],
                      semantics=("parallel", "parallel", "arbitrary"))
    return outs if side is not None else outs[0]


def _bd_fwd(name, a, ws, tm, scale=None):
    T = a.shape[0]
    H, w, _ = ws[0].shape
    nw = len(ws)
    has_scale = scale is not None

    def body(*refs):
        a_ref = refs[0]
        w_refs = refs[1:1 + nw]
        pos = 1 + nw
        s_ref = refs[pos] if has_scale else None
        o_refs = refs[pos + (1 if has_scale else 0):]
        av = a_ref[...]
        for l in range(nw):
            r = jnp.dot(av, w_refs[l][...], preferred_element_type=F32)
            o_refs[l][...] = r
            if has_scale and l == 0:
                o_refs[nw][...] = (r * s_ref[...]).astype(BF16)

    blk = pl.BlockSpec((tm, w), lambda h, i: (i, h))
    wspec = pl.BlockSpec((None, w, w), lambda h, i: (h, 0, 0))
    in_specs = [blk] + [wspec] * nw
    args = [a] + list(ws)
    out_shape = [SDS((T, H * w), F32)] * nw
    out_specs = [blk] * nw
    if has_scale:
        in_specs.append(pl.BlockSpec((1, w), lambda h, i: (0, h)))
        args.append(scale)
        out_shape = out_shape + [SDS((T, H * w), BF16)]
        out_specs = out_specs + [blk]
    return pl.pallas_call(
        body, name=name, grid=(H, T // tm), in_specs=in_specs, out_specs=out_specs, out_shape=out_shape,
        compiler_params=_cparams(dimension_semantics=("parallel", "parallel")),
    )(*args)


def _bd_bwd_x(name, dys, ws, tm, res=None):
    T = dys[0].shape[0]
    H, w, _ = ws[0].shape
    nw = len(ws)
    has_res = res is not None

    def body(*refs):
        d_refs = refs[:nw]
        w_refs = refs[nw:2 * nw]
        r_ref = refs[2 * nw] if has_res else None
        o_ref = refs[-1]
        acc = None
        for l in range(nw):
            p = lax.dot_general(d_refs[l][...], w_refs[l][...], (((1,), (1,)), ((), ())),
                                preferred_element_type=F32)
            acc = p if acc is None else acc + p
        if has_res:
            acc = acc + r_ref[...]
        o_ref[...] = acc

    blk = pl.BlockSpec((tm, w), lambda h, i: (i, h))
    wspec = pl.BlockSpec((None, w, w), lambda h, i: (h, 0, 0))
    in_specs = [blk] * nw + [wspec] * nw + ([blk] if has_res else [])
    args = list(dys) + list(ws) + ([res] if has_res else [])
    return pl.pallas_call(
        body, name=name, grid=(H, T // tm), in_specs=in_specs, out_specs=blk, out_shape=SDS((T, H * w), F32),
        compiler_params=_cparams(dimension_semantics=("parallel", "parallel")),
    )(*args)


def _bd_bwd_w(name, a, dys, w, tk):
    T, HW = a.shape
    H = HW // w
    nw = len(dys)

    def body(*refs):
        a_ref = refs[0]
        d_refs = refs[1:1 + nw]
        o_refs = refs[1 + nw:]
        k = pl.program_id(1)
        av = a_ref[...]
        for l in range(nw):
            p = lax.dot_general(av, d_refs[l][...], (((0,), (0,)), ((), ())), preferred_element_type=F32)

            @pl.when(k == 0)
            def _(p=p, l=l):
                o_refs[l][...] = p

            @pl.when(k > 0)
            def _(p=p, l=l):
                o_refs[l][...] += p

    blk = pl.BlockSpec((tk, w), lambda h, k: (k, h))
    ospec = pl.BlockSpec((None, w, w), lambda h, k: (h, 0, 0))
    return pl.pallas_call(
        body, name=name, grid=(H, T // tk), in_specs=[blk] * (1 + nw), out_specs=[ospec] * nw,
        out_shape=[SDS((H, w, w), F32)] * nw,
        compiler_params=_cparams(dimension_semantics=("parallel", "arbitrary")),
    )(a, *dys)


def _rms_fwd(name, x, g, tt, side=None):
    T, D = x.shape

    def body(x_ref, g_ref, h_ref):
        xv = x_ref[...]
        rstd = lax.rsqrt(jnp.mean(xv * xv, axis=-1, keepdims=True) + EPS)
        h_ref[...] = (xv * rstd * g_ref[...]).astype(BF16)

    row = pl.BlockSpec((tt, D), lambda i: (i, 0))
    vec = pl.BlockSpec((1, D), lambda i: (0, 0))
    outs = _ride_call(side, body, name=name, grid=(T // tt,), in_specs=[row, vec], args=[x, g], out_specs=[row],
                      out_shape=[SDS((T, D), BF16)], scratch=[], semantics=("parallel",))
    return outs if side is not None else outs[0]


def _rms_bwd(name, dh, x, g, dres, tt, want_bf16):
    T, D = x.shape

    def body(dh_ref, x_ref, g_ref, dres_ref, *outs):
        i = pl.program_id(0)
        dx_ref = outs[0]
        dg_ref = outs[-1]
        xv = x_ref[...]
        rstd = lax.rsqrt(jnp.mean(xv * xv, axis=-1, keepdims=True) + EPS)
        xn = xv * rstd
        dhv = dh_ref[...]
        dyg = dhv * g_ref[...]
        dx = dres_ref[...] + rstd * (dyg - xn * jnp.mean(dyg * xn, axis=-1, keepdims=True))
        dx_ref[...] = dx
        if want_bf16:
            outs[1][...] = dx.astype(BF16)

        @pl.when(i == 0)
        def _():
            dg_ref[...] = jnp.zeros_like(dg_ref)

        dg_ref[...] += _colsum(dhv * xn)

    row = pl.BlockSpec((tt, D), lambda i: (i, 0))
    vec = pl.BlockSpec((1, D), lambda i: (0, 0))
    out_shape = [SDS((T, D), F32)] + ([SDS((T, D), BF16)] if want_bf16 else []) + [SDS((1, D), F32)]
    out_specs = [row] + ([row] if want_bf16 else []) + [vec]
    return pl.pallas_call(
        body, name=name, grid=(T // tt,), in_specs=[row, row, vec, row], out_specs=out_specs, out_shape=out_shape,
        compiler_params=_cparams(dimension_semantics=("arbitrary",)),
    )(dh, x, g, dres)


def _final_loss_bwd(name, x2, tgt, g, tt):
    T, D = x2.shape

    def body(x_ref, t_ref, g_ref, dx_ref, dxb_ref, dg_ref, lv_ref):
        i = pl.program_id(0)
        xv = x_ref[...]
        gv = g_ref[...]
        rstd = lax.rsqrt(jnp.mean(xv * xv, axis=-1, keepdims=True) + EPS)
        xn = xv * rstd
        e = xn * gv - t_ref[...]
        dy = e * (1.0 / D)
        dyg = dy * gv
        dx = rstd * (dyg - xn * jnp.mean(dyg * xn, axis=-1, keepdims=True))
        dx_ref[...] = dx
        dxb_ref[...] = dx.astype(BF16)

        @pl.when(i == 0)
        def _():
            dg_ref[...] = jnp.zeros_like(dg_ref)
            lv_ref[...] = jnp.zeros_like(lv_ref)

        dg_ref[...] += _colsum(dy * xn)
        lv_ref[...] += _colsum(e * e)

    row = pl.BlockSpec((tt, D), lambda i: (i, 0))
    vec = pl.BlockSpec((1, D), lambda i: (0, 0))
    return pl.pallas_call(
        body, name=name, grid=(T // tt,), in_specs=[row, row, vec], out_specs=[row, row, vec, vec],
        out_shape=[SDS((T, D), F32), SDS((T, D), BF16), SDS((1, D), F32), SDS((1, D), F32)],
        compiler_params=_cparams(dimension_semantics=("arbitrary",)),
    )(x2, tgt, g)


def _blk(tt, tc, off=0):
    return pl.BlockSpec((tt, tc), lambda j, i: (i, j + off))


def _vec(r, tc, off=0):
    return pl.BlockSpec((r, tc), lambda j, i: (0, j + off))


def _prev_halo(tt, tc, off=0):
    r = tt // HALO
    return pl.BlockSpec((HALO, tc), lambda j, i: (jnp.maximum(i * r - 1, 0), j + off))


def _next_halo(tt, tc, T, off=0):
    r = tt // HALO
    last = T // HALO - 1
    return pl.BlockSpec((HALO, tc), lambda j, i: (jnp.minimum((i + 1) * r, last), j + off))


def _seq_params():
    return _cparams(dimension_semantics=("parallel", "arbitrary"))


def _pool_fwd(name, proj, Cp, gw, tt):
    T = proj.shape[0]
    G = Cp // gw
    assert G == 4

    def body(u_ref, halo_ref, d_ref):
        j, i = pl.program_id(0), pl.program_id(1)
        u = u_ref[...]
        halo = jnp.where(i > 0, halo_ref[...], 0.0)
        ext = jnp.concatenate([halo, u], axis=0)
        e2 = ext + pltpu.roll(ext, 1, 0)
        e4 = e2 + pltpu.roll(e2, 2, 0)
        e8 = e4 + pltpu.roll(e4, 4, 0)
        e16 = e8 + pltpu.roll(e8, 8, 0)
        s = jnp.where(j == 0, e2, jnp.where(j == 1, e4, jnp.where(j == 2, e8, e16)))[HALO:]
        t1 = (i * tt + 1 + lax.broadcasted_iota(jnp.int32, (tt, 1), 0)).astype(F32)
        win = lax.shift_left(jnp.int32(2), j).astype(F32)
        d_ref[...] = (s * (1.0 / jnp.minimum(t1, win)) - u).astype(BF16)

    return pl.pallas_call(
        body, name=name, grid=(G, T // tt), in_specs=[_blk(tt, gw), _prev_halo(tt, gw)], out_specs=_blk(tt, gw),
        out_shape=SDS((T, Cp), BF16), compiler_params=_seq_params(),
    )(proj, proj)


def _pool_bwd_a(name, dy, z, scale, tt):
    T, Cp = dy.shape

    def body(dy_ref, z_ref, s_ref, dz_ref, ds_ref):
        i = pl.program_id(0)
        dyv = dy_ref[...]
        dz_ref[...] = (dyv * s_ref[...]).astype(BF16)

        @pl.when(i == 0)
        def _():
            ds_ref[...] = jnp.zeros_like(ds_ref)

        ds_ref[...] += _colsum(dyv * z_ref[...])

    row = pl.BlockSpec((tt, Cp), lambda i: (i, 0))
    vec = pl.BlockSpec((1, Cp), lambda i: (0, 0))
    return pl.pallas_call(
        body, name=name, grid=(T // tt,), in_specs=[row, row, vec], out_specs=[row, vec],
        out_shape=[SDS((T, Cp), BF16), SDS((1, Cp), F32)],
        compiler_params=_cparams(dimension_semantics=("arbitrary",)),
    )(dy, z, scale)


def _pool_bwd_b(name, dd, gw, tt):
    T, Cp = dd.shape
    G = Cp // gw
    nT = T // tt

    def body(d_ref, nxt_ref, du_ref):
        j, i = pl.program_id(0), pl.program_id(1)
        win = lax.shift_left(jnp.int32(2), j).astype(F32)
        dv = d_ref[...]
        t1 = (i * tt + 1 + lax.broadcasted_iota(jnp.int32, (tt + HALO, 1), 0)).astype(F32)
        nxt = jnp.where(i < nT - 1, nxt_ref[...], 0.0)
        ext = jnp.concatenate([dv, nxt], axis=0) * (1.0 / jnp.minimum(t1, win))
        n = tt + HALO
        f2 = ext + pltpu.roll(ext, n - 1, 0)
        f4 = f2 + pltpu.roll(f2, n - 2, 0)
        f8 = f4 + pltpu.roll(f4, n - 4, 0)
        f16 = f8 + pltpu.roll(f8, n - 8, 0)
        s = jnp.where(j == 0, f2, jnp.where(j == 1, f4, jnp.where(j == 2, f8, f16)))[:tt]
        du_ref[...] = (s - dv).astype(BF16)

    return pl.pallas_call(
        body, name=name, grid=(G, nT), in_specs=[_blk(tt, gw), _next_halo(tt, gw, T)], out_specs=_blk(tt, gw),
        out_shape=SDS((T, Cp), BF16), compiler_params=_seq_params(),
    )(dd, dd)


def _conv_fwd(name, src, off_cols, w, b, tt, tc):
    T = src.shape[0]
    K, C = w.shape
    off = off_cols // tc

    def body(u_ref, halo_ref, w_ref, b_ref, v_ref, vb_ref):
        i = pl.program_id(1)
        u = u_ref[...]
        halo = jnp.where(i > 0, halo_ref[...], 0.0)
        ext = jnp.concatenate([halo, u], axis=0)
        wv = w_ref[...]
        acc = b_ref[...] + u * wv[K - 1:K]
        for s in range(1, K):
            acc = acc + _lookback(ext, s) * wv[K - 1 - s:K - s]
        v_ref[...] = acc
        vb_ref[...] = acc.astype(BF16)

    return pl.pallas_call(
        body, name=name, grid=(C // tc, T // tt),
        in_specs=[_blk(tt, tc, off), _prev_halo(tt, tc, off), _vec(K, tc), _vec(1, tc)],
        out_specs=[_blk(tt, tc), _blk(tt, tc)], out_shape=[SDS((T, C), F32), SDS((T, C), BF16)],
        compiler_params=_seq_params(),
    )(src, src, w, b)


def _conv_bwd(name, dv, src, off_cols, w, tt, tc):
    T, C = dv.shape
    K = w.shape[0]
    off = off_cols // tc
    nT = T // tt

    def body(dv_ref, nxt_ref, u_ref, halo_ref, w_ref, du_ref, dw_ref, db_ref):
        i = pl.program_id(1)
        d = dv_ref[...]
        nxt = jnp.where(i < nT - 1, nxt_ref[...], 0.0)
        dext = jnp.concatenate([d, nxt], axis=0)
        u = u_ref[...]
        halo = jnp.where(i > 0, halo_ref[...], 0.0)
        uext = jnp.concatenate([halo, u], axis=0)
        wv = w_ref[...]
        du = d * wv[K - 1:K]
        rows = [_colsum(d * u)]
        for s in range(1, K):
            du = du + _lookahead(dext, s, tt) * wv[K - 1 - s:K - s]
            rows.append(_colsum(d * _lookback(uext, s)))
        du_ref[...] = du.astype(BF16)
        dw = jnp.concatenate(rows[::-1] + [jnp.zeros((8 - K, tc), F32)], axis=0)

        @pl.when(i == 0)
        def _():
            dw_ref[...] = jnp.zeros_like(dw_ref)
            db_ref[...] = jnp.zeros_like(db_ref)

        dw_ref[...] += dw
        db_ref[...] += _colsum(d)

    return pl.pallas_call(
        body, name=name, grid=(C // tc, nT),
        in_specs=[_blk(tt, tc), _next_halo(tt, tc, T), _blk(tt, tc, off), _prev_halo(tt, tc, off), _vec(K, tc)],
        out_specs=[_blk(tt, tc), _vec(8, tc), _vec(1, tc)],
        out_shape=[SDS((T, C), BF16), SDS((8, C), F32), SDS((1, C), F32)],
        compiler_params=_seq_params(),
    )(dv, dv, src, src, w)


def _lru_gates(ra, ia, v, ba, bi, lam):
    r = _sigmoid(ra + ba)
    ig = _sigmoid(ia + bi)
    nl = -lam
    ex = jnp.exp(-jnp.abs(nl))
    one_p = 1.0 + ex
    l1p = jnp.where(one_p == 1.0, ex, jnp.log(one_p) * ex / (one_p - 1.0))
    sp = jnp.maximum(nl, 0.0) + l1p
    a = jnp.exp(-LRU_C * r * sp)
    om = 1.0 - a * a
    rs = lax.rsqrt(jnp.maximum(om, 1e-30))
    return r, ig, sp, a, om * rs, rs


def _gate_pre(vv, w_ref, hw):
    vb = vv.astype(BF16)
    return jnp.concatenate([jnp.dot(vb[:, q * hw:(q + 1) * hw], w_ref[q], preferred_element_type=F32)
                            for q in range(vv.shape[1] // hw)], axis=1)


def _gate_pre_bwd(d, w_ref, hw):
    db = d.astype(BF16)
    return jnp.concatenate([lax.dot_general(db[:, q * hw:(q + 1) * hw], w_ref[q], (((1,), (1,)), ((), ())),
                                            preferred_element_type=F32)
                            for q in range(d.shape[1] // hw)], axis=1)


def _lru_fwd(name, w_a, w_i, v, proj, off_cols, ba, bi, lam, tt, tc, side=None):
    T, R = v.shape
    off = off_cols // tc
    hw = w_a.shape[-1]
    wspec = pl.BlockSpec((tc // hw, hw, hw), lambda j, i: (j, 0, 0))

    def body(wa_ref, wi_ref, v_ref, ug_ref, ba_ref, bi_ref, lam_ref, h_ref, y_ref, carry_ref):
        i = pl.program_id(1)

        @pl.when(i == 0)
        def _():
            carry_ref[...] = jnp.zeros_like(carry_ref)

        vv = v_ref[...]
        _, ig, _, a, mult, _ = _lru_gates(_gate_pre(vv, wa_ref, hw), _gate_pre(vv, wi_ref, hw), vv, ba_ref[...],
                                          bi_ref[...], lam_ref[...])
        A = a
        B = mult * (ig * vv)
        row8 = lax.broadcasted_iota(jnp.int32, (tt, 1), 0) % 8
        for s in (1, 2, 4):
            m = row8 >= s
            B = jnp.where(m, A * pltpu.roll(B, s, 0) + B, B)
            A = jnp.where(m, A * pltpu.roll(A, s, 0), A)
        carry = carry_ref[7:8, :]
        groups = []
        for g in range(tt // 8):
            hg = A[g * 8:(g + 1) * 8] * carry + B[g * 8:(g + 1) * 8]
            carry = hg[7:8]
            groups.append(hg)
        h = jnp.concatenate(groups, axis=0)
        h_ref[...] = h
        carry_ref[...] = groups[-1]
        y_ref[...] = (h * _gelu(ug_ref[...])).astype(BF16)

    return _ride_call(
        side, body, name=name, grid=(R // tc, T // tt),
        in_specs=[wspec, wspec, _blk(tt, tc), _blk(tt, tc, off), _vec(1, tc), _vec(1, tc), _vec(1, tc)],
        args=[w_a, w_i, v, proj, ba, bi, lam],
        out_specs=[_blk(tt, tc), _blk(tt, tc)], out_shape=[SDS((T, R), F32), SDS((T, R), BF16)],
        scratch=[pltpu.VMEM((8, tc), F32)], semantics=("parallel", "arbitrary"))


def _lru_bwd(name, dy, hs, proj, off_cols, w_a, w_i, v, ba, bi, lam, tt, tc):
    T, R = v.shape
    off = off_cols // tc
    nT = T // tt
    r16 = tt // HALO
    hw = w_a.shape[-1]
    wspec = pl.BlockSpec((tc // hw, hw, hw), lambda j, i: (j, 0, 0))

    def rblk(o=0):
        return pl.BlockSpec((tt, tc), lambda j, i: (nT - 1 - i, j + o))

    hprev = pl.BlockSpec((HALO, tc), lambda j, i: (jnp.maximum((nT - 1 - i) * r16 - 1, 0), j))

    def body(dy_ref, h_ref, hp_ref, ug_ref, wa_ref, wi_ref, v_ref, ba_ref, bi_ref, lam_ref,
             dra_ref, dia_ref, dv_ref, dug_ref, dba_ref, dbi_ref, dlam_ref, dh_carry, a_carry):
        i = pl.program_id(1)

        @pl.when(i == 0)
        def _():
            dh_carry[...] = jnp.zeros_like(dh_carry)
            a_carry[...] = jnp.zeros_like(a_carry)
            dba_ref[...] = jnp.zeros_like(dba_ref)
            dbi_ref[...] = jnp.zeros_like(dbi_ref)
            dlam_ref[...] = jnp.zeros_like(dlam_ref)

        vv = v_ref[...]
        lamv = lam_ref[...]
        r, ig, sp, a, mult, inv_mult = _lru_gates(_gate_pre(vv, wa_ref, hw), _gate_pre(vv, wi_ref, hw), vv,
                                                  ba_ref[...], bi_ref[...], lamv)
        h = h_ref[...]
        gel, dgel = _gelu_and_grad(ug_ref[...])
        dyv = dy_ref[...]
        dug_ref[...] = (dyv * h * dgel).astype(BF16)
        dhs = dyv * gel

        a_ext = jnp.concatenate([a, jnp.broadcast_to(a_carry[0:1, :], (HALO, tc))], axis=0)
        A = _lookahead(a_ext, 1, tt)
        B = dhs
        row8 = lax.broadcasted_iota(jnp.int32, (tt, 1), 0) % 8
        for s in (1, 2, 4):
            m = row8 < 8 - s
            B = jnp.where(m, B + A * pltpu.roll(B, tt - s, 0), B)
            A = jnp.where(m, A * pltpu.roll(A, tt - s, 0), A)
        carry = dh_carry[0:1, :]
        groups = []
        for g in reversed(range(tt // 8)):
            dg = B[g * 8:(g + 1) * 8] + A[g * 8:(g + 1) * 8] * carry
            carry = dg[0:1]
            groups.append(dg)
        dH = jnp.concatenate(groups[::-1], axis=0)
        dh_carry[...] = groups[-1]
        a_carry[...] = a[:8, :]

        hp = jnp.where(i < nT - 1, hp_ref[...], 0.0)
        h_prev = _lookback(jnp.concatenate([hp, h], axis=0), 1)
        da = dH * h_prev
        dmult = dH * (ig * vv)
        div = dH * mult
        dig = div * vv
        dlog_a = da * a - dmult * (a * a) * inv_mult
        dr = dlog_a * (-LRU_C * sp)
        dra = dr * r * (1.0 - r)
        dia = dig * ig * (1.0 - ig)
        dra_ref[...] = dra.astype(BF16)
        dia_ref[...] = dia.astype(BF16)
        dv_ref[...] = (_gate_pre_bwd(dra, wa_ref, hw) + _gate_pre_bwd(dia, wi_ref, hw)) + div * ig
        dba_ref[...] += _colsum(dra)
        dbi_ref[...] += _colsum(dia)
        dlam_ref[...] += _colsum(dlog_a * (-LRU_C * r))

        @pl.when(i == nT - 1)
        def _():
            dlam_ref[...] = dlam_ref[...] * (-_sigmoid(-lamv))

    return pl.pallas_call(
        body, name=name, grid=(R // tc, nT),
        in_specs=[rblk(), rblk(), hprev, rblk(off), wspec, wspec, rblk(), _vec(1, tc), _vec(1, tc), _vec(1, tc)],
        out_specs=[rblk(), rblk(), rblk(), rblk(), _vec(1, tc), _vec(1, tc), _vec(1, tc)],
        out_shape=[SDS((T, R), BF16), SDS((T, R), BF16), SDS((T, R), F32), SDS((T, R), BF16),
                   SDS((1, R), F32), SDS((1, R), F32), SDS((1, R), F32)],
        scratch_shapes=[pltpu.VMEM((8, tc), F32), pltpu.VMEM((8, tc), F32)], compiler_params=_seq_params(),
    )(dy, hs, hs, proj, w_a, w_i, v, ba, bi, lam)


def _merge_fwd(name, proj, off_cols, b_gate, P, Q, tt, tc, side=None):
    T, D = P.shape
    o0 = off_cols // tc
    o1 = o0 + D // tc

    def body(g0_ref, g1_ref, b0_ref, b1_ref, p_ref, q_ref, m_ref):
        g0 = _sigmoid(g0_ref[...] + b0_ref[...])
        g1 = _sigmoid(g1_ref[...] + b1_ref[...])
        m_ref[...] = (g0 * p_ref[...] + g1 * q_ref[...]).astype(BF16)

    outs = _ride_call(
        side, body, name=name, grid=(D // tc, T // tt),
        in_specs=[_blk(tt, tc, o0), _blk(tt, tc, o1), _vec(1, tc), _vec(1, tc, D // tc), _blk(tt, tc), _blk(tt, tc)],
        args=[proj, proj, b_gate, b_gate, P, Q], out_specs=[_blk(tt, tc)], out_shape=[SDS((T, D), BF16)],
        scratch=[], semantics=("parallel", "arbitrary"))
    return outs if side is not None else outs[0]


def _merge_bwd(name, dm, proj, off_cols, b_gate, P, Q, tt, tc):
    T, D = P.shape
    o0 = off_cols // tc
    o1 = o0 + D // tc

    def body(dm_ref, g0_ref, g1_ref, b0_ref, b1_ref, p_ref, q_ref, dp_ref, dq_ref, dl0_ref, dl1_ref, db0_ref, db1_ref):
        i = pl.program_id(1)
        g0 = _sigmoid(g0_ref[...] + b0_ref[...])
        g1 = _sigmoid(g1_ref[...] + b1_ref[...])
        d = dm_ref[...].astype(F32)
        dp_ref[...] = (d * g0).astype(BF16)
        dq_ref[...] = (d * g1).astype(BF16)
        dl0 = d * p_ref[...] * g0 * (1.0 - g0)
        dl1 = d * q_ref[...] * g1 * (1.0 - g1)
        dl0_ref[...] = dl0.astype(BF16)
        dl1_ref[...] = dl1.astype(BF16)

        @pl.when(i == 0)
        def _():
            db0_ref[...] = jnp.zeros_like(db0_ref)
            db1_ref[...] = jnp.zeros_like(db1_ref)

        db0_ref[...] += _colsum(dl0)
        db1_ref[...] += _colsum(dl1)

    b = _blk(tt, tc)
    return pl.pallas_call(
        body, name=name, grid=(D // tc, T // tt),
        in_specs=[b, _blk(tt, tc, o0), _blk(tt, tc, o1), _vec(1, tc), _vec(1, tc, D // tc), b, b],
        out_specs=[b, b, b, b, _vec(1, tc), _vec(1, tc)],
        out_shape=[SDS((T, D), BF16)] * 4 + [SDS((1, D), F32)] * 2, compiler_params=_seq_params(),
    )(dm, proj, proj, b_gate, b_gate, P, Q)


def _ffn_act_fwd(name, up, w, b, tt, tc):
    T = up.shape[0]
    K, F = w.shape
    nF = F // tc

    def body(gp_ref, halo_ref, val_ref, w_ref, b_ref, z_ref):
        i = pl.program_id(1)
        gp = gp_ref[...].astype(F32)
        halo = jnp.where(i > 0, halo_ref[...].astype(F32), 0.0)
        ext = jnp.concatenate([halo, gp], axis=0)
        wv = w_ref[...]
        c = b_ref[...] + gp * wv[K - 1:K]
        for s in range(1, K):
            c = c + _lookback(ext, s) * wv[K - 1 - s:K - s]
        z_ref[...] = (_gelu(c) * val_ref[...].astype(F32)).astype(BF16)

    return pl.pallas_call(
        body, name=name, grid=(nF, T // tt),
        in_specs=[_blk(tt, tc), _prev_halo(tt, tc), _blk(tt, tc, nF), _vec(K, tc), _vec(1, tc)],
        out_specs=_blk(tt, tc), out_shape=SDS((T, F), BF16), compiler_params=_seq_params(),
    )(up, up, up, w, b)


def _ffn_down_fwd(name, up, w, b, w_down, x1, tm, tk, nsub):
    T = up.shape[0]
    K, F = w.shape
    D = w_down.shape[1]
    nF = F // tk
    sub = tk // nsub
    r16 = tm // HALO

    def body(gp_ref, halo_ref, val_ref, w_ref, b_ref, wd_ref, x1_ref, z_ref, o_ref, acc_ref):
        i, k = pl.program_id(0), pl.program_id(1)
        total = None
        for s in range(nsub):
            cs = pl.ds(s * sub, sub)
            gp = gp_ref[:, cs].astype(F32)
            halo = jnp.where(i > 0, halo_ref[:, cs].astype(F32), 0.0)
            ext = jnp.concatenate([halo, gp], axis=0)
            wv = w_ref[:, cs]
            c = b_ref[:, cs] + gp * wv[K - 1:K]
            for t in range(1, K):
                c = c + _lookback(ext, t) * wv[K - 1 - t:K - t]
            zs = (_gelu(c) * val_ref[:, cs].astype(F32)).astype(BF16)
            z_ref[:, cs] = zs
            p = jnp.dot(zs, wd_ref[cs, :], preferred_element_type=F32)
            total = p if total is None else total + p

        @pl.when(k == 0)
        def _():
            acc_ref[...] = total

        @pl.when(k > 0)
        def _():
            acc_ref[...] += total

        @pl.when(k == nF - 1)
        def _():
            o_ref[...] = acc_ref[...] + x1_ref[...]

    blk = lambda off: pl.BlockSpec((tm, tk), lambda i, k: (i, k + off))
    return pl.pallas_call(
        body, name=name, grid=(T // tm, nF),
        in_specs=[blk(0), pl.BlockSpec((HALO, tk), lambda i, k: (jnp.maximum(i * r16 - 1, 0), k)), blk(nF),
                  pl.BlockSpec((K, tk), lambda i, k: (0, k)), pl.BlockSpec((1, tk), lambda i, k: (0, k)),
                  pl.BlockSpec((tk, D), lambda i, k: (k, 0)), pl.BlockSpec((tm, D), lambda i, k: (i, 0))],
        out_specs=[blk(0), pl.BlockSpec((tm, D), lambda i, k: (i, 0))],
        out_shape=[SDS((T, F), BF16), SDS((T, D), F32)],
        scratch_shapes=[pltpu.VMEM((tm, D), F32)],
        compiler_params=_cparams(dimension_semantics=("parallel", "arbitrary")),
    )(up, up, up, w, b, w_down, x1)


def _ffn_act_bwd(name, up, dz, w, b, tt, tc):
    T = up.shape[0]
    K, F = w.shape
    nF = F // tc
    nT = T // tt

    def body(gp_ref, gph_ref, gpn_ref, val_ref, valn_ref, dz_ref, dzn_ref, w_ref, b_ref, dup_ref, dw_ref, db_ref):
        i = pl.program_id(1)
        wv = w_ref[...]
        gp = gp_ref[...].astype(F32)
        halo = jnp.where(i > 0, gph_ref[...].astype(F32), 0.0)
        full = jnp.concatenate([halo, gp, gpn_ref[...].astype(F32)], axis=0)
        c = b_ref[...] + full * wv[K - 1:K]
        for s in range(1, K):
            c = c + pltpu.roll(full, s, 0) * wv[K - 1 - s:K - s]
        c = c[HALO:]
        gate, dgate = _gelu_and_grad(c)
        dz_e = jnp.concatenate([dz_ref[...], dzn_ref[...]], axis=0).astype(F32)
        val_e = jnp.concatenate([val_ref[...], valn_ref[...]], axis=0).astype(F32)
        dc_e = dz_e * val_e * dgate
        row = lax.broadcasted_iota(jnp.int32, (tt + HALO, 1), 0)
        dc_e = jnp.where((row < tt) | (i < nT - 1), dc_e, 0.0)
        dc = dc_e[:tt]
        dgp = dc * wv[K - 1:K]
        uext = full[:tt + HALO]
        rows = [_colsum(dc * gp)]
        for s in range(1, K):
            dgp = dgp + _lookahead(dc_e, s, tt) * wv[K - 1 - s:K - s]
            rows.append(_colsum(dc * _lookback(uext, s)))
        dup_ref[0] = dgp.astype(BF16)
        dup_ref[1] = (dz_e[:tt] * gate[:tt]).astype(BF16)
        dw = jnp.concatenate(rows[::-1] + [jnp.zeros((8 - K, tc), F32)], axis=0)

        @pl.when(i == 0)
        def _():
            dw_ref[...] = jnp.zeros_like(dw_ref)
            db_ref[...] = jnp.zeros_like(db_ref)

        dw_ref[...] += dw
        db_ref[...] += _colsum(dc)

    return pl.pallas_call(
        body, name=name, grid=(nF, nT),
        in_specs=[_blk(tt, tc), _prev_halo(tt, tc), _next_halo(tt, tc, T), _blk(tt, tc, nF), _next_halo(tt, tc, T, nF),
                  _blk(tt, tc), _next_halo(tt, tc, T), _vec(K, tc), _vec(1, tc)],
        out_specs=[pl.BlockSpec((2, tt, tc), lambda j, i: (0, i, j)), _vec(8, tc), _vec(1, tc)],
        out_shape=[SDS((2, T, F), BF16), SDS((8, F), F32), SDS((1, F), F32)],
        compiler_params=_seq_params(),
    )(up, up, up, up, up, dz, dz, w, b)


def _adamw(name, w, g, m, v):
    R, C = w.shape
    tr = R
    if R * C * 4 > ADAMW_BLOCK_BYTES:
        tr = _pick(R, [t for t in (256, 128, 64, 32, 16, 8) if t * C * 4 <= ADAMW_BLOCK_BYTES])
    c1 = 1.0 - ADAM_B1 ** ADAM_STEP
    c2 = 1.0 - ADAM_B2 ** ADAM_STEP

    def body(w_ref, g_ref, m_ref, v_ref, d_ref, nm_ref, nv_ref):
        gv = g_ref[...]
        nm = ADAM_B1 * m_ref[...] + (1.0 - ADAM_B1) * gv
        nv = ADAM_B2 * v_ref[...] + (1.0 - ADAM_B2) * (gv * gv)
        nm_ref[...] = nm
        nv_ref[...] = nv
        d_ref[...] = -ADAM_LR * ((nm / c1) / (jnp.sqrt(nv / c2) + ADAM_EPS) + ADAM_WD * w_ref[...])

    spec = pl.BlockSpec((tr, C), lambda i: (i, 0))
    return pl.pallas_call(
        body, name=name, grid=(R // tr,), in_specs=[spec] * 4, out_specs=[spec] * 3,
        out_shape=[SDS((R, C), F32)] * 3, compiler_params=_cparams(dimension_semantics=("parallel",)),
    )(w, g, m, v)


def _add_halves(name, part, rcv, pos):
    S, R, C = part.shape
    h = R // 2
    tr = _pick(h, (512, 256, 128, 64, 32, 16))
    nb = h // tr

    def body(pos_ref, p_ref, r_ref, o_ref):
        o_ref[...] = (p_ref[...].astype(F32) + r_ref[...].astype(F32)).astype(BF16)

    return pl.pallas_call(
        body, name=name,
        grid_spec=pltpu.PrefetchScalarGridSpec(
            num_scalar_prefetch=1, grid=(S, nb),
            in_specs=[pl.BlockSpec((None, tr, C), lambda s, r, pos: (s, pos[0] * nb + r, 0)),
                      pl.BlockSpec((None, tr, C), lambda s, r, pos: (s, r, 0))],
            out_specs=pl.BlockSpec((None, tr, C), lambda s, r, pos: (s, r, 0))),
        out_shape=SDS((S, h, C), BF16), compiler_params=_cparams(dimension_semantics=("parallel", "parallel")),
    )(pos, part, rcv)


def _add_chips(name, chipsum, rcv, pos):
    S, h, C = chipsum.shape
    tr = _pick(h, (512, 256, 128, 64, 32, 16))

    def body(pos_ref, own_ref, r_ref, o_ref):
        o_ref[...] = ((own_ref[...].astype(F32) + r_ref[0].astype(F32)) + r_ref[1].astype(F32)) + r_ref[2].astype(F32)

    return pl.pallas_call(
        body, name=name,
        grid_spec=pltpu.PrefetchScalarGridSpec(
            num_scalar_prefetch=1, grid=(h // tr,),
            in_specs=[pl.BlockSpec((None, tr, C), lambda r, pos: (pos[1], r, 0)),
                      pl.BlockSpec((3, tr, C), lambda r, pos: (0, r, 0))],
            out_specs=pl.BlockSpec((None, tr, C), lambda r, pos: (pos[0], r, 0))),
        out_shape=SDS((2, h, C), F32), compiler_params=_cparams(dimension_semantics=("parallel",)),
    )(pos, chipsum, rcv)


def _place():
    x, y, c = lax.axis_index("x"), lax.axis_index("y"), lax.axis_index("c")
    chips = [(1 - x, y), (x, 1 - y), (1 - x, 1 - y)]
    slots = [2 * cx + cy for cx, cy in chips]
    return x, y, c, chips, slots


def _gather_side(shards, split):
    n = len(shards)

    def copies(ins, outs, sems):
        send_sems, recv_sems, local_sems = sems
        x, y, c, chips, slots = _place()
        me = 2 * x + y
        sibling = (x, y, 1 - c)

        def rows(a, half):
            hrows = ins[a].shape[0] // 2
            return pl.ds(half * hrows, hrows)

        def ici(a, j):
            if split[a]:
                src, dst = ins[a].at[rows(a, c)], outs[a].at[me, rows(a, c)]
            else:
                src, dst = ins[a], outs[a].at[me]
            return pltpu.make_async_remote_copy(src_ref=src, dst_ref=dst, send_sem=send_sems.at[a, j],
                                                recv_sem=recv_sems.at[a, j], device_id=(*chips[j], c), device_id_type=MESH)

        def landed(a, j):
            dst = outs[a].at[slots[j], rows(a, c)] if split[a] else outs[a].at[slots[j]]
            return pltpu.make_async_remote_copy(src_ref=dst, dst_ref=dst, send_sem=send_sems.at[a, j],
                                                recv_sem=recv_sems.at[a, j], device_id=(*chips[j], c), device_id_type=MESH)

        def d2d(a, j, half):
            blk = outs[a].at[slots[j], rows(a, half)]
            return pltpu.make_async_remote_copy(src_ref=blk, dst_ref=blk, send_sem=send_sems.at[a, 3 + j],
                                                recv_sem=recv_sems.at[a, 3 + j], device_id=sibling, device_id_type=MESH)

        local = [pltpu.make_async_copy(ins[a], outs[a].at[me], local_sems.at[a]) for a in range(n)]
        return c, ici, landed, d2d, local

    def start(ins, outs, sems):
        c, ici, landed, d2d, local = copies(ins, outs, sems)
        for cp in local:
            cp.start()
        for a in range(n):
            for j in range(3):
                ici(a, j).start()

    def mid(ins, outs, sems):
        c, ici, landed, d2d, local = copies(ins, outs, sems)
        for a in range(n):
            for j in range(3):
                landed(a, j).wait_recv()
                if split[a]:
                    d2d(a, j, c).start()

    def finish(ins, outs, sems):
        c, ici, landed, d2d, local = copies(ins, outs, sems)
        for a in range(n):
            if split[a]:
                for j in range(3):
                    d2d(a, j, 1 - c).wait_recv()
        for a in range(n):
            for j in range(3):
                ici(a, j).wait_send()
                if split[a]:
                    d2d(a, j, c).wait_send()
        for cp in local:
            cp.wait()

    return _Side(shards, [SDS((N_SLOTS,) + s.shape, s.dtype) for s in shards],
                 [pltpu.SemaphoreType.DMA((n, 6)), pltpu.SemaphoreType.DMA((n, 6)), pltpu.SemaphoreType.DMA((n,))],
                 start, mid, finish)


def _swap_partial_halves(name, parts):
    side = _swap_side(parts)
    n = len(parts)

    def body(*refs):
        ins, outs, sems = refs[:n], refs[n:2 * n], refs[2 * n:]
        side.start(ins, outs, sems)
        side.finish(ins, outs, sems)

    return pl.pallas_call(
        body, name=name, in_specs=[ANY] * n, out_specs=[ANY] * n, out_shape=side.out_shapes,
        scratch_shapes=side.sem_shapes, compiler_params=_cparams(has_side_effects=True),
    )(*parts)


def _swap_side(parts):
    n = len(parts)

    def copies(ins, outs, sems):
        send_sems, recv_sems = sems
        x, y, c, _, _ = _place()
        cps = []
        for a in range(n):
            h = ins[a].shape[1] // 2
            cps.append(pltpu.make_async_remote_copy(
                src_ref=ins[a].at[:, pl.ds((1 - c) * h, h)], dst_ref=outs[a], send_sem=send_sems.at[a],
                recv_sem=recv_sems.at[a], device_id=(x, y, 1 - c), device_id_type=MESH))
        return cps

    def start(ins, outs, sems):
        for cp in copies(ins, outs, sems):
            cp.start()

    def finish(ins, outs, sems):
        for cp in copies(ins, outs, sems):
            cp.wait()

    return _Side(parts, [SDS((p.shape[0], p.shape[1] // 2, p.shape[2]), p.dtype) for p in parts],
                 [pltpu.SemaphoreType.DMA((n,)), pltpu.SemaphoreType.DMA((n,))], start, None, finish)


def _exchange_side(sums):
    n = len(sums)

    def copies(ins, outs, sems):
        send_sems, recv_sems = sems
        x, y, c, chips, slots = _place()
        return [pltpu.make_async_remote_copy(
            src_ref=ins[a].at[slots[j]], dst_ref=outs[a].at[j], send_sem=send_sems.at[a, j],
            recv_sem=recv_sems.at[a, j], device_id=(*chips[j], c), device_id_type=MESH)
            for a in range(n) for j in range(3)]

    def start(ins, outs, sems):
        for cp in copies(ins, outs, sems):
            cp.start()

    def finish(ins, outs, sems):
        for cp in copies(ins, outs, sems):
            cp.wait()

    return _Side(sums, [SDS((3,) + s.shape[1:], s.dtype) for s in sums],
                 [pltpu.SemaphoreType.DMA((n, 3)), pltpu.SemaphoreType.DMA((n, 3))], start, None, finish)


def _swap_reduced_halves(totals):
    side = _reduced_swap_side(totals)
    n = len(totals)

    def body(*refs):
        ins, outs, sems = refs[:n], refs[n:2 * n], refs[2 * n:]
        side.start(ins, outs, sems)
        side.finish(ins, outs, sems)

    return pl.pallas_call(
        body, name="swap_reduced_halves", in_specs=[ANY] * n, out_specs=[ANY] * n, out_shape=side.out_shapes,
        input_output_aliases={a: a for a in range(n)}, scratch_shapes=side.sem_shapes,
        compiler_params=_cparams(has_side_effects=True),
    )(*totals)


def _reduced_swap_side(totals):
    n = len(totals)

    def sends(outs, sems):
        send_sems, recv_sems = sems
        x, y, c, _, _ = _place()
        return [pltpu.make_async_remote_copy(src_ref=outs[a].at[c], dst_ref=outs[a].at[c], send_sem=send_sems.at[a],
                                             recv_sem=recv_sems.at[a], device_id=(x, y, 1 - c), device_id_type=MESH)
                for a in range(n)]

    def start(ins, outs, sems):
        for cp in sends(outs, sems):
            cp.start()

    def finish(ins, outs, sems):
        send_sems, recv_sems = sems
        x, y, c, _, _ = _place()
        for a in range(n):
            got = outs[a].at[1 - c]
            pltpu.make_async_remote_copy(src_ref=got, dst_ref=got, send_sem=send_sems.at[a], recv_sem=recv_sems.at[a],
                                         device_id=(x, y, 1 - c), device_id_type=MESH).wait_recv()
        for cp in sends(outs, sems):
            cp.wait_send()

    return _Side(totals, [SDS(t.shape, t.dtype) for t in totals],
                 [pltpu.SemaphoreType.DMA((n,)), pltpu.SemaphoreType.DMA((n,))], start, None, finish, in_place=True)


def _allreduce_small(pack):
    rows = pack.shape[0]

    def body(in_ref, out_ref, buf, send_sems, recv_sems):
        x, y, c = lax.axis_index("x"), lax.axis_index("y"), lax.axis_index("c")
        me = 4 * x + 2 * y + c
        buf[me] = in_ref[...]
        cps = []
        for k in range(1, 8):
            kx, ky, kc = (k >> 2) & 1, (k >> 1) & 1, k & 1
            peer = (x ^ kx, y ^ ky, c ^ kc)
            cps.append(pltpu.make_async_remote_copy(src_ref=in_ref, dst_ref=buf.at[me], send_sem=send_sems.at[k - 1],
                                                    recv_sem=recv_sems.at[k - 1], device_id=peer, device_id_type=MESH))
        for cp in cps:
            cp.start()
        for k in range(1, 8):
            got = buf.at[me ^ k]
            pltpu.make_async_remote_copy(src_ref=got, dst_ref=got, send_sem=send_sems.at[k - 1],
                                         recv_sem=recv_sems.at[k - 1], device_id=(x, y, c), device_id_type=MESH).wait_recv()
        for cp in cps:
            cp.wait_send()
        acc = buf[0]
        for d in range(1, 8):
            acc = acc + buf[d]
        out_ref[...] = acc

    vm = pl.BlockSpec(memory_space=pltpu.VMEM)
    return pl.pallas_call(
        body, name="allreduce_small", in_specs=[vm], out_specs=vm, out_shape=SDS(pack.shape, F32),
        scratch_shapes=[pltpu.VMEM((8, rows, LANES), F32), pltpu.SemaphoreType.DMA((7,)), pltpu.SemaphoreType.DMA((7,))],
        compiler_params=_cparams(has_side_effects=True),
    )(pack)


def _rows2d(a):
    return a.reshape((-1, a.shape[-1]))


def _from_slots_blockdiag(g, H, w):
    q = w // N_SLOTS
    return g.reshape(N_SLOTS, H, q, w).transpose(1, 0, 2, 3).reshape(H, w, w)


def _to_slots_blockdiag(d, H, w):
    q = w // N_SLOTS
    return d.reshape(H, N_SLOTS, q, w).transpose(1, 0, 2, 3).reshape(N_SLOTS, H * q, w)


def kernel(x, g_mix, w_in, b_gate, w_pool, pool_scale, lru_conv_w, lru_conv_b, w_a, b_a, w_i, b_i, lru_lambda, w_pool_proj, w_lru_proj, w_out, g_mlp, w_up, ffn_conv_w, ffn_conv_b, w_down, g_final, loss_target, m_g_mix, m_w_in, m_b_gate, m_w_pool, m_pool_scale, m_lru_conv_w, m_lru_conv_b, m_w_a, m_b_a, m_w_i, m_b_i, m_lru_lambda, m_w_pool_proj, m_w_lru_proj, m_w_out, m_g_mlp, m_w_up, m_ffn_conv_w, m_ffn_conv_b, m_w_down, m_g_final, v_g_mix, v_w_in, v_b_gate, v_w_pool, v_pool_scale, v_lru_conv_w, v_lru_conv_b, v_w_a, v_b_a, v_w_i, v_b_i, v_lru_lambda, v_w_pool_proj, v_w_lru_proj, v_w_out, v_g_mlp, v_w_up, v_ffn_conv_w, v_ffn_conv_b, v_w_down, v_g_final):
    weights = dict(g_mix=g_mix, w_in=w_in, b_gate=b_gate, w_pool=w_pool, pool_scale=pool_scale, lru_conv_w=lru_conv_w,
                   lru_conv_b=lru_conv_b, w_a=w_a, b_a=b_a, w_i=w_i, b_i=b_i, lru_lambda=lru_lambda,
                   w_pool_proj=w_pool_proj, w_lru_proj=w_lru_proj, w_out=w_out, g_mlp=g_mlp, w_up=w_up,
                   ffn_conv_w=ffn_conv_w, ffn_conv_b=ffn_conv_b, w_down=w_down, g_final=g_final)
    mom_m = dict(g_mix=m_g_mix, w_in=m_w_in, b_gate=m_b_gate, w_pool=m_w_pool, pool_scale=m_pool_scale,
                 lru_conv_w=m_lru_conv_w, lru_conv_b=m_lru_conv_b, w_a=m_w_a, b_a=m_b_a, w_i=m_w_i, b_i=m_b_i,
                 lru_lambda=m_lru_lambda, w_pool_proj=m_w_pool_proj, w_lru_proj=m_w_lru_proj, w_out=m_w_out,
                 g_mlp=m_g_mlp, w_up=m_w_up, ffn_conv_w=m_ffn_conv_w, ffn_conv_b=m_ffn_conv_b, w_down=m_w_down,
                 g_final=m_g_final)
    mom_v = dict(g_mix=v_g_mix, w_in=v_w_in, b_gate=v_b_gate, w_pool=v_w_pool, pool_scale=v_pool_scale,
                 lru_conv_w=v_lru_conv_w, lru_conv_b=v_lru_conv_b, w_a=v_w_a, b_a=v_b_a, w_i=v_w_i, b_i=v_b_i,
                 lru_lambda=v_lru_lambda, w_pool_proj=v_w_pool_proj, w_lru_proj=v_w_lru_proj, w_out=v_w_out,
                 g_mlp=v_g_mlp, w_up=v_w_up, ffn_conv_w=v_ffn_conv_w, ffn_conv_b=v_ffn_conv_b, w_down=v_w_down,
                 g_final=v_g_final)
    order = list(weights)

    T, D = x.shape[1], x.shape[2]
    Cp = pool_scale.shape[1]
    G, gw = w_pool.shape[1], w_pool.shape[3]
    H, hw = w_a.shape[1], w_a.shape[3]
    R = b_a.shape[1]
    F = ffn_conv_b.shape[1]
    KC, KF = lru_conv_w.shape[1], ffn_conv_w.shape[1]
    p0, p1, p2 = Cp, Cp + R, Cp + 2 * R
    xs = x.reshape(T, D)
    tgt = loss_target.reshape(T, D)

    my_x, my_y, my_c = lax.axis_index("x"), lax.axis_index("y"), lax.axis_index("c")
    slot = 2 * my_x + my_y
    pos = jnp.stack([my_c, slot]).astype(jnp.int32)

    big = ["w_in", "w_pool", "w_a", "w_i", "w_pool_proj", "w_lru_proj", "w_out", "w_up", "w_down"]
    first = big[:4]
    shard16 = {n: _rows2d(weights[n][0]).astype(BF16) for n in big}
    first_side = _gather_side([shard16[n] for n in first] + [lru_conv_w[0], ffn_conv_w[0]], [True] * 4 + [False, False])
    n_in, n_up, n_pp = w_in.shape[2], w_up.shape[2], w_pool_proj.shape[2]

    tt_row = _pick(T, (256, 128))
    tt_seq = _pick(T, (512, 256, 128))
    tt_pool = _pick(T, (1024, 512, 256, 128))
    tc_seq = _pick(R, (512, 256, 128))
    tm1 = _pick(T, (1024, 512, 256))
    tm2 = _pick(T, (2048, 1024, 512, 256))
    tkT = _pick(T, (1024, 512, 256))
    tkT2 = _pick(T, (4096, 2048, 1024, 512, 256))
    tn_in = _pick(n_in, (768, 1152, 384, 128))
    tn_up = _pick(n_up, (1536, 1024, 768, 512, 128))
    tk_up = _pick(n_up, (1024, 768, 512, 128))
    tn_pp = _pick(n_pp, (512, 256, 128))
    tD1 = _pick(D, (1024, 512))
    tD2 = _pick(D, (2048, 1024, 512))
    tF = _pick(F, (1536, 1024, 512))
    tkF = _pick(F, (2048, 1536, 1024, 512))

    h, *gathered = _rms_fwd("rms1_fwd", xs, g_mix, tt_row, side=first_side)
    gw_ = dict(zip(first + ["lru_conv_w", "ffn_conv_w"], gathered))
    W_in = gw_["w_in"]
    W_pool = _from_slots_blockdiag(gw_["w_pool"], G, gw)
    W_a = _from_slots_blockdiag(gw_["w_a"], H, hw)
    W_i = _from_slots_blockdiag(gw_["w_i"], H, hw)
    cw_lru = gw_["lru_conv_w"].transpose(1, 0, 2).reshape(KC, R)
    cw_ffn = gw_["ffn_conv_w"].transpose(1, 0, 2).reshape(KF, F)
    on_proj = ["w_pool_proj", "w_lru_proj", "w_out", "w_down"]
    proj3, *late = _matmul("proj_fwd", "nn", h, W_in, out_dtype=F32, tm=tm2, tn=tn_in, tk=D,
                           side=_gather_side([shard16[n] for n in on_proj], [True] * len(on_proj)))
    proj = proj3[0]
    gw_.update(zip(on_proj, late))
    W_pp = gw_["w_pool_proj"]
    W_lp = gw_["w_lru_proj"].reshape(R, D)
    W_out = gw_["w_out"].reshape(D, D)
    W_down = gw_["w_down"].reshape(F, D)
    up16, eD = shard16["w_up"], D // 8
    d_pool = _pool_fwd("pool_fwd", proj, Cp, gw, tt_pool)
    z_pool, y_pool = _bd_fwd("pool_mix_fwd", d_pool, [W_pool], tm1, scale=pool_scale)
    v_f, v_b = _conv_fwd("lru_conv_fwd", proj, p0, cw_lru, lru_conv_b, tt_seq, tc_seq)
    hs, y_lru, up_a = _lru_fwd("lru_scan_fwd", W_a, W_i, v_f, proj, p1, b_a, b_i, lru_lambda, tt_seq, tc_seq,
                               side=_gather_side([up16[:3 * eD]], [True]))
    P = _matmul("pool_proj_fwd", "nn", y_pool, W_pp, out_dtype=F32, tm=tm2, tn=tn_pp, tk=Cp)[0]
    Q3, up_b = _matmul("lru_proj_fwd", "nn", y_lru, W_lp, out_dtype=F32, tm=tm1, tn=tD1, tk=R,
                       side=_gather_side([up16[3 * eD:5 * eD]], [True]))
    Q = Q3[0]
    merged, up_c = _merge_fwd("merge_fwd", proj, p2, b_gate, P, Q, tt_seq, tc_seq,
                              side=_gather_side([up16[5 * eD:6 * eD]], [True]))
    x13, up_d = _matmul("out_fwd", "nn", merged, W_out, out_dtype=F32, tm=tm1, tn=tD1, tk=D, res=xs,
                        side=_gather_side([up16[6 * eD:]], [True]))
    x1 = x13[0]
    W_up = jnp.concatenate([up_a, up_b, up_c, up_d], axis=1)
    h2 = _rms_fwd("rms2_fwd", x1, g_mlp, tt_row)
    up = _matmul("up_fwd", "nn", h2, W_up, out_dtype=BF16, tm=tm1, tn=tn_up, tk=D)[0]
    z, x2 = _ffn_down_fwd("ffn_down_fwd", up, cw_ffn, ffn_conv_b, W_down, x1, _pick(T, (512, 256)),
                          _pick(F, (1024, 512)), 4)

    dx2, dx2b, d_g_final, lossvec = _final_loss_bwd("final_loss_bwd", x2, tgt, g_final.reshape(1, D), tt_row)
    dz = _matmul("down_bwd_x", "nt", dx2b, W_down, out_dtype=BF16, tm=tm1, tn=tF, tk=D)[0]
    dW_down = _matmul("down_bwd_w", "tn", z, dx2b, out_dtype=BF16, tm=_pick(F, (1024, 512)), tn=tD1, tk=tkT2)
    dup, d_cw_ffn, d_ffn_b = _ffn_act_bwd("ffn_act_bwd", up, dz, cw_ffn, ffn_conv_b, tt_seq, tc_seq)
    dh2 = _matmul("up_bwd_x", "nt", dup, W_up, out_dtype=F32, tm=tm1, tn=tD1, tk=n_up)[0]
    dW_up = _matmul("up_bwd_w", "tn", h2, dup, out_seg=N_SLOTS, out_dtype=BF16, tm=tD1, tn=tk_up, tk=tkT2)
    dx1, dx1b, d_g_mlp = _rms_bwd("rms2_bwd", dh2, x1, g_mlp, dx2, tt_row, True)
    dmerged = _matmul("out_bwd_x", "nt", dx1b, W_out, out_dtype=BF16, tm=tm1, tn=tD2, tk=D)[0]
    q_rows = lambda a, rows: a.reshape(N_SLOTS, rows // N_SLOTS, a.shape[-1])
    ffn_parts = [dW_up, q_rows(dW_down, F)]
    dW_out, *ffn_sib = _matmul("out_bwd_w", "tn", merged, dx1b, out_dtype=BF16, tm=tD1, tn=tD1, tk=tkT2,
                               side=_swap_side(ffn_parts))
    ffn_sums = [_add_halves(f"add_halves_{n}", p, r, pos) for n, p, r in zip(big[7:], ffn_parts, ffn_sib)]
    dP, dQ, dgl0, dgl1, d_bg0, d_bg1 = _merge_bwd("merge_bwd", dmerged, proj, p2, b_gate, P, Q, tt_seq, tc_seq)
    dy_pool = _matmul("pool_proj_bwd_x", "nt", dP, W_pp, out_dtype=F32, tm=tm2, tn=Cp, tk=tn_pp)[0]
    dW_pp = _matmul("pool_proj_bwd_w", "tn", y_pool, dP, out_seg=N_SLOTS, out_dtype=BF16, tm=Cp, tn=tn_pp,
                    tk=_pick(T, (2048, 1024, 512, 256)))
    dy_lru = _matmul("lru_proj_bwd_x", "nt", dQ, W_lp, out_dtype=F32, tm=tm1, tn=tD1, tk=D)[0]
    dW_lp = _matmul("lru_proj_bwd_w", "tn", y_lru, dQ, out_dtype=BF16, tm=_pick(R, (1024, 512)), tn=tD1, tk=tkT2)
    dra, dia, dv, du_gelu, d_b_a, d_b_i, d_lam = _lru_bwd(
        "lru_scan_bwd", dy_lru, hs, proj, p1, W_a, W_i, v_f, b_a, b_i, lru_lambda, tt_seq, tc_seq)
    dW_a, dW_i = _bd_bwd_w("lru_gate_bwd_w", v_b, [dra, dia], hw, tkT)
    du_lru, d_cw_lru, d_lru_b = _conv_bwd("lru_conv_bwd", dv, proj, p0, cw_lru, tt_seq, tc_seq)
    dzp, d_pool_scale = _pool_bwd_a("pool_scale_bwd", dy_pool, z_pool, pool_scale, tt_row)
    dd = _bd_bwd_x("pool_mix_bwd_x", [dzp], [W_pool], tm1)
    (dW_pool,) = _bd_bwd_w("pool_mix_bwd_w", d_pool, [dzp], gw, tkT)
    du_pool = _pool_bwd_b("pool_bwd", dd, gw, tt_pool)
    dproj = jnp.concatenate([du_pool, du_lru, du_gelu, dgl0, dgl1], axis=1)

    mix_parts = [_to_slots_blockdiag(dW_pool.astype(BF16), G, gw),
                 _to_slots_blockdiag(dW_a.astype(BF16), H, hw),
                 _to_slots_blockdiag(dW_i.astype(BF16), H, hw),
                 dW_pp, q_rows(dW_lp, R), q_rows(dW_out, D)]
    mix_sib = _swap_partial_halves("swap_partial_halves", mix_parts)
    mix_sums = [_add_halves(f"add_halves_{n}", p, r, pos) for n, p, r in zip(big[1:7], mix_parts, mix_sib)]
    dW_in, *ffn_chips = _matmul("proj_bwd_w", "tn", h, dproj, out_seg=N_SLOTS, out_dtype=BF16, tm=tD1, tn=tn_in,
                                tk=tkT2, side=_exchange_side(ffn_sums))
    (in_sib,) = _swap_partial_halves("swap_partial_halves_w_in", [dW_in])
    in_sum = _add_halves("add_halves_w_in", dW_in, in_sib, pos)
    ffn_totals = [_add_chips(f"add_chips_{n}", s, r, pos) for n, s, r in zip(big[7:], ffn_sums, ffn_chips)]
    dh3, *rest = _matmul("proj_bwd_x", "nt", dproj, W_in, out_dtype=F32, tm=tm1, tn=tD1, tk=n_in,
                         side=_both(_exchange_side([in_sum] + mix_sums), _reduced_swap_side(ffn_totals)))
    chips, ffn_full = rest[:7], rest[7:]
    grad_x, d_g_mix = _rms_bwd("rms1_bwd", dh3[0], xs, g_mix, dx1, tt_row, False)
    totals = [_add_chips(f"add_chips_{n}", s, r, pos) for n, s, r in zip(big[:7], [in_sum] + mix_sums, chips)]
    full = list(_swap_reduced_halves(totals)) + list(ffn_full)
    grads = {n: f.reshape(weights[n].shape) for n, f in zip(big, full)}

    small = ["g_mix", "b_gate", "pool_scale", "lru_conv_b", "b_a", "b_i", "lru_lambda", "g_mlp", "ffn_conv_b", "g_final"]
    small_g = [d_g_mix, jnp.concatenate([d_bg0, d_bg1], axis=1), d_pool_scale, d_lru_b, d_b_a, d_b_i, d_lam, d_g_mlp,
               d_ffn_b, d_g_final]
    pieces = [g.reshape(-1) for g in small_g] + [d_cw_lru[:KC].reshape(-1), d_cw_ffn[:KF].reshape(-1), lossvec.reshape(-1)]
    sizes = [p.shape[0] for p in pieces]
    total = sum(sizes)
    rows = -(-total // (8 * LANES)) * 8
    pack = jnp.concatenate(pieces + [jnp.zeros((rows * LANES - total,), F32)]).reshape(rows, LANES)
    summed = _allreduce_small(pack).reshape(-1)
    offs = [0]
    for s in sizes:
        offs.append(offs[-1] + s)
    for k, n in enumerate(small):
        grads[n] = summed[offs[k]:offs[k + 1]].reshape(weights[n].shape)
    ns = len(small)
    g_cw_lru = summed[offs[ns]:offs[ns + 1]].reshape(KC, R)
    g_cw_ffn = summed[offs[ns + 1]:offs[ns + 2]].reshape(KF, F)
    grads["lru_conv_w"] = lax.dynamic_slice_in_dim(g_cw_lru, slot * (R // N_SLOTS), R // N_SLOTS, axis=1)[None]
    grads["ffn_conv_w"] = lax.dynamic_slice_in_dim(g_cw_ffn, slot * (F // N_SLOTS), F // N_SLOTS, axis=1)[None]
    loss = jnp.sum(summed[offs[ns + 2]:offs[ns + 3]]) * (0.5 / D)

    delta, new_m, new_v = {}, {}, {}
    for n in big:
        d_, m_, v_ = _adamw(f"adamw_{n}", _rows2d(weights[n][0]), _rows2d(grads[n][0]), _rows2d(mom_m[n][0]),
                            _rows2d(mom_v[n][0]))
        delta[n], new_m[n], new_v[n] = (t.reshape(weights[n].shape) for t in (d_, m_, v_))
    rest = small + ["lru_conv_w", "ffn_conv_w"]
    rsizes = [weights[n].size for n in rest]
    rtotal = sum(rsizes)
    rrows = -(-rtotal // (8 * LANES)) * 8

    def packed(tree):
        flat = [tree[n].reshape(-1) for n in rest] + [jnp.zeros((rrows * LANES - rtotal,), F32)]
        return jnp.concatenate(flat).reshape(rrows, LANES)

    d_, m_, v_ = _adamw("adamw_small", packed(weights), packed(grads), packed(mom_m), packed(mom_v))
    o = 0
    for n, s in zip(rest, rsizes):
        for tree, flat in ((delta, d_), (new_m, m_), (new_v, v_)):
            tree[n] = flat.reshape(-1)[o:o + s].reshape(weights[n].shape)
        o += s

    return (loss, grad_x.reshape(x.shape), *[grads[n] for n in order], *[delta[n] for n in order],
            *[new_m[n] for n in order], *[new_v[n] for n in order])
```

```python
import functools

import jax
import jax.numpy as jnp
from jax import lax
from jax.experimental import pallas as pl
from jax.experimental.pallas import tpu as pltpu

F32 = jnp.float32
BF16 = jnp.bfloat16
MESH = pl.DeviceIdType.MESH

VMEM_LIMIT_V7X = 56 * 1024 * 1024
LANES = 128
HALO = 16
N_SLOTS = 4
ADAMW_BLOCK_BYTES = 3 * 512 * 1024

EPS = 1e-6
LRU_C = 8.0
ADAM_LR = 0.001
ADAM_B1 = 0.9
ADAM_B2 = 0.999
ADAM_EPS = 1e-08
ADAM_WD = 0.01
ADAM_STEP = 10
GELU_C = 0.7978845608028654
GELU_A = 0.044715

SDS = jax.ShapeDtypeStruct
ANY = pl.BlockSpec(memory_space=pl.ANY)


def _pick(n, cands):
    for c in cands:
        if c <= n and n % c == 0:
            return c
    raise ValueError(f"no tile for {n} in {cands}")


def _cparams(**kw):
    return pltpu.CompilerParams(vmem_limit_bytes=VMEM_LIMIT_V7X, **kw)


def _sigmoid(x):
    return 0.5 * jnp.tanh(0.5 * x) + 0.5


def _gelu(x):
    th = jnp.tanh(x * (GELU_C + (GELU_C * GELU_A) * (x * x)))
    hx = 0.5 * x
    return hx + hx * th


def _gelu_and_grad(x):
    x2 = x * x
    th = jnp.tanh(x * (GELU_C + (GELU_C * GELU_A) * x2))
    hx = 0.5 * x
    g = hx + hx * th
    dg = (0.5 + 0.5 * th) + (hx * (1.0 - th * th)) * (GELU_C + (3.0 * GELU_C * GELU_A) * x2)
    return g, dg


def _lookback(ext, k):
    return pltpu.roll(ext, k, 0)[HALO:]


def _lookahead(ext, k, tt):
    return pltpu.roll(ext, ext.shape[0] - k, 0)[:tt]


def _colsum(v):
    return jnp.sum(v, axis=0, keepdims=True)


def _seg3(a):
    return a if a.ndim == 3 else a.reshape((1,) + a.shape)


def _seg_spec(shape3, br, bc, rowf, colf):
    assert shape3[1] % br == 0 and shape3[2] % bc == 0, (shape3, br, bc)
    nb = shape3[2] // bc

    def imap(i, j, k):
        cb = colf(i, j, k)
        return (cb // nb, rowf(i, j, k), cb % nb)

    return pl.BlockSpec((None, br, bc), imap)


_I = lambda i, j, k: i
_J = lambda i, j, k: j
_K = lambda i, j, k: k


class _Side:
    def __init__(self, ins, out_shapes, sem_shapes, start, mid, finish, in_place=False):
        self.ins, self.out_shapes, self.sem_shapes = list(ins), list(out_shapes), list(sem_shapes)
        self.start, self.mid, self.finish = start, mid, finish
        self.alias = [(a, a) for a in range(len(self.ins))] if in_place else []


def _both(s1, s2):
    ni, no, ns = len(s1.ins), len(s1.out_shapes), len(s1.sem_shapes)

    def phase(f1, f2):
        if f1 is None and f2 is None:
            return None

        def run(ins, outs, sems):
            if f1 is not None:
                f1(ins[:ni], outs[:no], sems[:ns])
            if f2 is not None:
                f2(ins[ni:], outs[no:], sems[ns:])

        return run

    both = _Side(s1.ins + s2.ins, s1.out_shapes + s2.out_shapes, s1.sem_shapes + s2.sem_shapes,
                 phase(s1.start, s2.start), phase(s1.mid, s2.mid), phase(s1.finish, s2.finish))
    both.alias = s1.alias + [(ni + a, no + b) for a, b in s2.alias]
    return both


def _ride(side, body, n_in, n_out, n_scr, grid):
    n_sin, n_sout = len(side.ins), len(side.out_shapes)
    n_steps = 1
    for g in grid:
        n_steps *= g

    def wrapped(*refs):
        p = 0
        ins, s_in = refs[p:p + n_in], refs[p + n_in:p + n_in + n_sin]
        p += n_in + n_sin
        outs, s_out = refs[p:p + n_out], refs[p + n_out:p + n_out + n_sout]
        p += n_out + n_sout
        scr, s_sem = refs[p:p + n_scr], refs[p + n_scr:]
        step = 0
        for d, g in enumerate(grid):
            step = step * g + pl.program_id(d)

        @pl.when(step == 0)
        def _():
            side.start(s_in, s_out, s_sem)

        body(*ins, *outs, *scr)

        if side.mid is not None:
            @pl.when(step == (2 * n_steps) // 3)
            def _():
                side.mid(s_in, s_out, s_sem)

        @pl.when(step == n_steps - 1)
        def _():
            side.finish(s_in, s_out, s_sem)

    return wrapped


def _ride_call(side, body, *, name, grid, in_specs, args, out_specs, out_shape, scratch, semantics):
    out_specs, out_shape = list(out_specs), list(out_shape)
    aliases = {}
    if side is not None:
        aliases = {len(args) + a: len(out_shape) + b for a, b in side.alias}
        body = _ride(side, body, len(args), len(out_shape), len(scratch), grid)
        in_specs = list(in_specs) + [ANY] * len(side.ins)
        args = list(args) + side.ins
        out_specs = out_specs + [ANY] * len(side.out_shapes)
        out_shape = out_shape + side.out_shapes
        scratch = list(scratch) + side.sem_shapes
        semantics = ("arbitrary",) * len(grid)
    return pl.pallas_call(
        body, name=name, grid=grid, in_specs=in_specs, out_specs=out_specs, out_shape=out_shape,
        scratch_shapes=scratch, input_output_aliases=aliases,
        compiler_params=_cparams(dimension_semantics=semantics),
    )(*args)


def _matmul(name, mode, a, b, *, out_seg=1, out_dtype, tm, tn, tk, res=None, side=None):
    a3, b3 = _seg3(a), _seg3(b)
    if mode == "nn":
        M, K, N = a3.shape[1], a3.shape[0] * a3.shape[2], b3.shape[0] * b3.shape[2]
        assert b3.shape[1] == K
        a_spec = _seg_spec(a3.shape, tm, tk, _I, _K)
        b_spec = _seg_spec(b3.shape, tk, tn, _K, _J)
        dims = (((1,), (0,)), ((), ()))
    elif mode == "nt":
        M, K, N = a3.shape[1], a3.shape[0] * a3.shape[2], b3.shape[1]
        assert b3.shape[0] * b3.shape[2] == K
        a_spec = _seg_spec(a3.shape, tm, tk, _I, _K)
        b_spec = _seg_spec(b3.shape, tn, tk, _J, _K)
        dims = (((1,), (1,)), ((), ()))
    else:
        K, M, N = a3.shape[1], a3.shape[0] * a3.shape[2], b3.shape[0] * b3.shape[2]
        assert b3.shape[1] == K
        a_spec = _seg_spec(a3.shape, tk, tm, _K, _I)
        b_spec = _seg_spec(b3.shape, tk, tn, _K, _J)
        dims = (((0,), (0,)), ((), ()))
    out3 = (out_seg, M, N // out_seg)
    o_spec = _seg_spec(out3, tm, tn, _I, _J)
    grid = (M // tm, N // tn, K // tk)
    nk = grid[2]
    has_res = res is not None

    def body(*refs):
        a_ref, b_ref = refs[0], refs[1]
        r_ref = refs[2] if has_res else None
        o_ref = refs[3] if has_res else refs[2]
        acc_ref = refs[-1] if nk > 1 else None
        prod = lax.dot_general(a_ref[...], b_ref[...], dims, preferred_element_type=F32)

        def finish(val):
            if has_res:
                val = val + r_ref[...]
            o_ref[...] = val.astype(o_ref.dtype)

        if nk == 1:
            finish(prod)
        else:
            k = pl.program_id(2)

            @pl.when(k == 0)
            def _():
                acc_ref[...] = prod

            @pl.when(k > 0)
            def _():
                acc_ref[...] += prod

            @pl.when(k == nk - 1)
            def _():
                finish(acc_ref[...])

    in_specs = [a_spec, b_spec]
    args = [a3, b3]
    if has_res:
        in_specs.append(pl.BlockSpec((tm, tn), lambda i, j, k: (i, j)))
        args.append(res)
    outs = _ride_call(side, body, name=name, grid=grid, in_specs=in_specs, args=args, out_specs=[o_spec],
                      out_shape=[SDS(out3, out_dtype)], scratch=[pltpu.VMEM((tm, tn), F32)] if nk > 1 else [],
                      semantics=("parallel", "parallel", "arbitrary"))
    return outs if side is not None else outs[0]


def _bd_bwd_w(name, a, dys, w, tk):
    T, HW = a.shape
    H = HW // w
    nw = len(dys)

    def body(*refs):
        a_ref = refs[0]
        d_refs = refs[1:1 + nw]
        o_refs = refs[1 + nw:]
        k = pl.program_id(1)
        av = a_ref[...]
        for l in range(nw):
            p = lax.dot_general(av, d_refs[l][...], (((0,), (0,)), ((), ())), preferred_element_type=F32)

            @pl.when(k == 0)
            def _(p=p, l=l):
                o_refs[l][...] = p

            @pl.when(k > 0)
            def _(p=p, l=l):
                o_refs[l][...] += p

    blk = pl.BlockSpec((tk, w), lambda h, k: (k, h))
    ospec = pl.BlockSpec((None, w, w), lambda h, k: (h, 0, 0))
    return pl.pallas_call(
        body, name=name, grid=(H, T // tk), in_specs=[blk] * (1 + nw), out_specs=[ospec] * nw,
        out_shape=[SDS((H, w, w), F32)] * nw,
        compiler_params=_cparams(dimension_semantics=("parallel", "arbitrary")),
    )(a, *dys)


def _rms_fwd(name, x, g, tt, side=None):
    T, D = x.shape

    def body(x_ref, g_ref, h_ref):
        xv = x_ref[...]
        rstd = lax.rsqrt(jnp.mean(xv * xv, axis=-1, keepdims=True) + EPS)
        h_ref[...] = (xv * rstd * g_ref[...]).astype(BF16)

    row = pl.BlockSpec((tt, D), lambda i: (i, 0))
    vec = pl.BlockSpec((1, D), lambda i: (0, 0))
    outs = _ride_call(side, body, name=name, grid=(T // tt,), in_specs=[row, vec], args=[x, g], out_specs=[row],
                      out_shape=[SDS((T, D), BF16)], scratch=[], semantics=("parallel",))
    return outs if side is not None else outs[0]


def _rms_bwd(name, dh, x, g, dres, tt, want_bf16):
    T, D = x.shape

    def body(dh_ref, x_ref, g_ref, dres_ref, *outs):
        i = pl.program_id(0)
        dx_ref = outs[0]
        dg_ref = outs[-1]
        xv = x_ref[...]
        rstd = lax.rsqrt(jnp.mean(xv * xv, axis=-1, keepdims=True) + EPS)
        xn = xv * rstd
        dhv = dh_ref[...]
        dyg = dhv * g_ref[...]
        dx = dres_ref[...] + rstd * (dyg - xn * jnp.mean(dyg * xn, axis=-1, keepdims=True))
        dx_ref[...] = dx
        if want_bf16:
            outs[1][...] = dx.astype(BF16)

        @pl.when(i == 0)
        def _():
            dg_ref[...] = jnp.zeros_like(dg_ref)

        dg_ref[...] += _colsum(dhv * xn)

    row = pl.BlockSpec((tt, D), lambda i: (i, 0))
    vec = pl.BlockSpec((1, D), lambda i: (0, 0))
    out_shape = [SDS((T, D), F32)] + ([SDS((T, D), BF16)] if want_bf16 else []) + [SDS((1, D), F32)]
    out_specs = [row] + ([row] if want_bf16 else []) + [vec]
    return pl.pallas_call(
        body, name=name, grid=(T // tt,), in_specs=[row, row, vec, row], out_specs=out_specs, out_shape=out_shape,
        compiler_params=_cparams(dimension_semantics=("arbitrary",)),
    )(dh, x, g, dres)


def _final_loss_bwd(name, x2, tgt, g, tt):
    T, D = x2.shape

    def body(x_ref, t_ref, g_ref, dx_ref, dxb_ref, dg_ref, lv_ref):
        i = pl.program_id(0)
        xv = x_ref[...]
        gv = g_ref[...]
        rstd = lax.rsqrt(jnp.mean(xv * xv, axis=-1, keepdims=True) + EPS)
        xn = xv * rstd
        e = xn * gv - t_ref[...]
        dy = e * (1.0 / D)
        dyg = dy * gv
        dx = rstd * (dyg - xn * jnp.mean(dyg * xn, axis=-1, keepdims=True))
        dx_ref[...] = dx
        dxb_ref[...] = dx.astype(BF16)

        @pl.when(i == 0)
        def _():
            dg_ref[...] = jnp.zeros_like(dg_ref)
            lv_ref[...] = jnp.zeros_like(lv_ref)

        dg_ref[...] += _colsum(dy * xn)
        lv_ref[...] += _colsum(e * e)

    row = pl.BlockSpec((tt, D), lambda i: (i, 0))
    vec = pl.BlockSpec((1, D), lambda i: (0, 0))
    return pl.pallas_call(
        body, name=name, grid=(T // tt,), in_specs=[row, row, vec], out_specs=[row, row, vec, vec],
        out_shape=[SDS((T, D), F32), SDS((T, D), BF16), SDS((1, D), F32), SDS((1, D), F32)],
        compiler_params=_cparams(dimension_semantics=("arbitrary",)),
    )(x2, tgt, g)


def _blk(tt, tc, off=0):
    return pl.BlockSpec((tt, tc), lambda j, i: (i, j + off))


def _vec(r, tc, off=0):
    return pl.BlockSpec((r, tc), lambda j, i: (0, j + off))


def _prev_halo(tt, tc, off=0):
    r = tt // HALO
    return pl.BlockSpec((HALO, tc), lambda j, i: (jnp.maximum(i * r - 1, 0), j + off))


def _next_halo(tt, tc, T, off=0):
    r = tt // HALO
    last = T // HALO - 1
    return pl.BlockSpec((HALO, tc), lambda j, i: (jnp.minimum((i + 1) * r, last), j + off))


def _seq_params():
    return _cparams(dimension_semantics=("parallel", "arbitrary"))


def _pool_fwd(name, proj, Cp, gw, tt, w_pool, scale):
    T = proj.shape[0]
    G = Cp // gw
    assert G == 4

    def body(u_ref, halo_ref, w_ref, s_ref, d_ref, z_ref, y_ref):
        j, i = pl.program_id(0), pl.program_id(1)
        u = u_ref[...]
        halo = jnp.where(i > 0, halo_ref[...], 0.0)
        ext = jnp.concatenate([halo, u], axis=0)
        e2 = ext + pltpu.roll(ext, 1, 0)
        e4 = e2 + pltpu.roll(e2, 2, 0)
        e8 = e4 + pltpu.roll(e4, 4, 0)
        e16 = e8 + pltpu.roll(e8, 8, 0)
        s = jnp.where(j == 0, e2, jnp.where(j == 1, e4, jnp.where(j == 2, e8, e16)))[HALO:]
        t1 = (i * tt + 1 + lax.broadcasted_iota(jnp.int32, (tt, 1), 0)).astype(F32)
        win = lax.shift_left(jnp.int32(2), j).astype(F32)
        d = (s * (1.0 / jnp.minimum(t1, win)) - u).astype(BF16)
        d_ref[...] = d
        z = jnp.dot(d, w_ref[...], preferred_element_type=F32)
        z_ref[...] = z
        y_ref[...] = (z * s_ref[...]).astype(BF16)

    return pl.pallas_call(
        body, name=name, grid=(G, T // tt),
        in_specs=[_blk(tt, gw), _prev_halo(tt, gw), pl.BlockSpec((None, gw, gw), lambda j, i: (j, 0, 0)), _vec(1, gw)],
        out_specs=[_blk(tt, gw)] * 3, out_shape=[SDS((T, Cp), BF16), SDS((T, Cp), F32), SDS((T, Cp), BF16)],
        compiler_params=_seq_params(),
    )(proj, proj, w_pool, scale)


def _pool_bwd_a(name, dy, z, scale, w_pool, tt):
    T, Cp = dy.shape
    gw = w_pool.shape[-1]

    def body(dy_ref, z_ref, s_ref, w_ref, dz_ref, dd_ref, ds_ref):
        i = pl.program_id(0)
        dyv = dy_ref[...]
        dz = (dyv * s_ref[...]).astype(BF16)
        dz_ref[...] = dz
        dd_ref[...] = _gate_pre_bwd(dz, w_ref, gw)

        @pl.when(i == 0)
        def _():
            ds_ref[...] = jnp.zeros_like(ds_ref)

        ds_ref[...] += _colsum(dyv * z_ref[...])

    row = pl.BlockSpec((tt, Cp), lambda i: (i, 0))
    vec = pl.BlockSpec((1, Cp), lambda i: (0, 0))
    wspec = pl.BlockSpec(w_pool.shape, lambda i: (0, 0, 0))
    return pl.pallas_call(
        body, name=name, grid=(T // tt,), in_specs=[row, row, vec, wspec], out_specs=[row, row, vec],
        out_shape=[SDS((T, Cp), BF16), SDS((T, Cp), F32), SDS((1, Cp), F32)],
        compiler_params=_cparams(dimension_semantics=("arbitrary",)),
    )(dy, z, scale, w_pool)


def _pool_bwd_b(name, dd, gw, tt):
    T, Cp = dd.shape
    G = Cp // gw
    nT = T // tt

    def body(d_ref, nxt_ref, du_ref):
        j, i = pl.program_id(0), pl.program_id(1)
        win = lax.shift_left(jnp.int32(2), j).astype(F32)
        dv = d_ref[...]
        t1 = (i * tt + 1 + lax.broadcasted_iota(jnp.int32, (tt + HALO, 1), 0)).astype(F32)
        nxt = jnp.where(i < nT - 1, nxt_ref[...], 0.0)
        ext = jnp.concatenate([dv, nxt], axis=0) * (1.0 / jnp.minimum(t1, win))
        n = tt + HALO
        f2 = ext + pltpu.roll(ext, n - 1, 0)
        f4 = f2 + pltpu.roll(f2, n - 2, 0)
        f8 = f4 + pltpu.roll(f4, n - 4, 0)
        f16 = f8 + pltpu.roll(f8, n - 8, 0)
        s = jnp.where(j == 0, f2, jnp.where(j == 1, f4, jnp.where(j == 2, f8, f16)))[:tt]
        du_ref[...] = (s - dv).astype(BF16)

    return pl.pallas_call(
        body, name=name, grid=(G, nT), in_specs=[_blk(tt, gw), _next_halo(tt, gw, T)], out_specs=_blk(tt, gw),
        out_shape=SDS((T, Cp), BF16), compiler_params=_seq_params(),
    )(dd, dd)


def _conv_fwd(name, src, off_cols, w, b, tt, tc):
    T = src.shape[0]
    K, C = w.shape
    off = off_cols // tc

    def body(u_ref, halo_ref, w_ref, b_ref, v_ref, vb_ref):
        i = pl.program_id(1)
        u = u_ref[...]
        halo = jnp.where(i > 0, halo_ref[...], 0.0)
        ext = jnp.concatenate([halo, u], axis=0)
        wv = w_ref[...]
        acc = b_ref[...] + u * wv[K - 1:K]
        for s in range(1, K):
            acc = acc + _lookback(ext, s) * wv[K - 1 - s:K - s]
        v_ref[...] = acc
        vb_ref[...] = acc.astype(BF16)

    return pl.pallas_call(
        body, name=name, grid=(C // tc, T // tt),
        in_specs=[_blk(tt, tc, off), _prev_halo(tt, tc, off), _vec(K, tc), _vec(1, tc)],
        out_specs=[_blk(tt, tc), _blk(tt, tc)], out_shape=[SDS((T, C), F32), SDS((T, C), BF16)],
        compiler_params=_seq_params(),
    )(src, src, w, b)


def _conv_bwd(name, dv, src, off_cols, w, tt, tc):
    T, C = dv.shape
    K = w.shape[0]
    off = off_cols // tc
    nT = T // tt

    def body(dv_ref, nxt_ref, u_ref, halo_ref, w_ref, du_ref, dw_ref, db_ref):
        i = pl.program_id(1)
        d = dv_ref[...]
        nxt = jnp.where(i < nT - 1, nxt_ref[...], 0.0)
        dext = jnp.concatenate([d, nxt], axis=0)
        u = u_ref[...]
        halo = jnp.where(i > 0, halo_ref[...], 0.0)
        uext = jnp.concatenate([halo, u], axis=0)
        wv = w_ref[...]
        du = d * wv[K - 1:K]
        rows = [_colsum(d * u)]
        for s in range(1, K):
            du = du + _lookahead(dext, s, tt) * wv[K - 1 - s:K - s]
            rows.append(_colsum(d * _lookback(uext, s)))
        du_ref[...] = du.astype(BF16)
        dw = jnp.concatenate(rows[::-1] + [jnp.zeros((8 - K, tc), F32)], axis=0)

        @pl.when(i == 0)
        def _():
            dw_ref[...] = jnp.zeros_like(dw_ref)
            db_ref[...] = jnp.zeros_like(db_ref)

        dw_ref[...] += dw
        db_ref[...] += _colsum(d)

    return pl.pallas_call(
        body, name=name, grid=(C // tc, nT),
        in_specs=[_blk(tt, tc), _next_halo(tt, tc, T), _blk(tt, tc, off), _prev_halo(tt, tc, off), _vec(K, tc)],
        out_specs=[_blk(tt, tc), _vec(8, tc), _vec(1, tc)],
        out_shape=[SDS((T, C), BF16), SDS((8, C), F32), SDS((1, C), F32)],
        compiler_params=_seq_params(),
    )(dv, dv, src, src, w)


def _lru_gates(ra, ia, v, ba, bi, lam):
    r = _sigmoid(ra + ba)
    ig = _sigmoid(ia + bi)
    nl = -lam
    ex = jnp.exp(-jnp.abs(nl))
    one_p = 1.0 + ex
    l1p = jnp.where(one_p == 1.0, ex, jnp.log(one_p) * ex / (one_p - 1.0))
    sp = jnp.maximum(nl, 0.0) + l1p
    a = jnp.exp(-LRU_C * r * sp)
    om = 1.0 - a * a
    rs = lax.rsqrt(jnp.maximum(om, 1e-30))
    return r, ig, sp, a, om * rs, rs


def _gate_pre(vv, w_ref, hw):
    vb = vv.astype(BF16)
    return jnp.concatenate([jnp.dot(vb[:, q * hw:(q + 1) * hw], w_ref[q], preferred_element_type=F32)
                            for q in range(vv.shape[1] // hw)], axis=1)


def _gate_pre_bwd(d, w_ref, hw):
    db = d.astype(BF16)
    return jnp.concatenate([lax.dot_general(db[:, q * hw:(q + 1) * hw], w_ref[q], (((1,), (1,)), ((), ())),
                                            preferred_element_type=F32)
                            for q in range(d.shape[1] // hw)], axis=1)


def _lru_fwd(name, w_a, w_i, v, proj, off_cols, ba, bi, lam, tt, tc, side=None):
    T, R = v.shape
    off = off_cols // tc
    hw = w_a.shape[-1]
    wspec = pl.BlockSpec((tc // hw, hw, hw), lambda j, i: (j, 0, 0))

    def body(wa_ref, wi_ref, v_ref, ug_ref, ba_ref, bi_ref, lam_ref, h_ref, y_ref, carry_ref):
        i = pl.program_id(1)

        @pl.when(i == 0)
        def _():
            carry_ref[...] = jnp.zeros_like(carry_ref)

        vv = v_ref[...]
        _, ig, _, a, mult, _ = _lru_gates(_gate_pre(vv, wa_ref, hw), _gate_pre(vv, wi_ref, hw), vv, ba_ref[...],
                                          bi_ref[...], lam_ref[...])
        A = a
        B = mult * (ig * vv)
        row8 = lax.broadcasted_iota(jnp.int32, (tt, 1), 0) % 8
        for s in (1, 2, 4):
            m = row8 >= s
            B = jnp.where(m, A * pltpu.roll(B, s, 0) + B, B)
            A = jnp.where(m, A * pltpu.roll(A, s, 0), A)
        carry = carry_ref[7:8, :]
        groups = []
        for g in range(tt // 8):
            hg = A[g * 8:(g + 1) * 8] * carry + B[g * 8:(g + 1) * 8]
            carry = hg[7:8]
            groups.append(hg)
        h = jnp.concatenate(groups, axis=0)
        h_ref[...] = h
        carry_ref[...] = groups[-1]
        y_ref[...] = (h * _gelu(ug_ref[...])).astype(BF16)

    return _ride_call(
        side, body, name=name, grid=(R // tc, T // tt),
        in_specs=[wspec, wspec, _blk(tt, tc), _blk(tt, tc, off), _vec(1, tc), _vec(1, tc), _vec(1, tc)],
        args=[w_a, w_i, v, proj, ba, bi, lam],
        out_specs=[_blk(tt, tc), _blk(tt, tc)], out_shape=[SDS((T, R), F32), SDS((T, R), BF16)],
        scratch=[pltpu.VMEM((8, tc), F32)], semantics=("parallel", "arbitrary"))


def _lru_bwd(name, dy, hs, proj, off_cols, w_a, w_i, v, ba, bi, lam, tt, tc):
    T, R = v.shape
    off = off_cols // tc
    nT = T // tt
    r16 = tt // HALO
    hw = w_a.shape[-1]
    wspec = pl.BlockSpec((tc // hw, hw, hw), lambda j, i: (j, 0, 0))

    def rblk(o=0):
        return pl.BlockSpec((tt, tc), lambda j, i: (nT - 1 - i, j + o))

    hprev = pl.BlockSpec((HALO, tc), lambda j, i: (jnp.maximum((nT - 1 - i) * r16 - 1, 0), j))

    def body(dy_ref, h_ref, hp_ref, ug_ref, wa_ref, wi_ref, v_ref, ba_ref, bi_ref, lam_ref,
             dra_ref, dia_ref, dv_ref, dug_ref, dba_ref, dbi_ref, dlam_ref, dh_carry, a_carry):
        i = pl.program_id(1)

        @pl.when(i == 0)
        def _():
            dh_carry[...] = jnp.zeros_like(dh_carry)
            a_carry[...] = jnp.zeros_like(a_carry)
            dba_ref[...] = jnp.zeros_like(dba_ref)
            dbi_ref[...] = jnp.zeros_like(dbi_ref)
            dlam_ref[...] = jnp.zeros_like(dlam_ref)

        vv = v_ref[...]
        lamv = lam_ref[...]
        r, ig, sp, a, mult, inv_mult = _lru_gates(_gate_pre(vv, wa_ref, hw), _gate_pre(vv, wi_ref, hw), vv,
                                                  ba_ref[...], bi_ref[...], lamv)
        h = h_ref[...]
        gel, dgel = _gelu_and_grad(ug_ref[...])
        dyv = dy_ref[...]
        dug_ref[...] = (dyv * h * dgel).astype(BF16)
        dhs = dyv * gel

        a_ext = jnp.concatenate([a, jnp.broadcast_to(a_carry[0:1, :], (HALO, tc))], axis=0)
        A = _lookahead(a_ext, 1, tt)
        B = dhs
        row8 = lax.broadcasted_iota(jnp.int32, (tt, 1), 0) % 8
        for s in (1, 2, 4):
            m = row8 < 8 - s
            B = jnp.where(m, B + A * pltpu.roll(B, tt - s, 0), B)
            A = jnp.where(m, A * pltpu.roll(A, tt - s, 0), A)
        carry = dh_carry[0:1, :]
        groups = []
        for g in reversed(range(tt // 8)):
            dg = B[g * 8:(g + 1) * 8] + A[g * 8:(g + 1) * 8] * carry
            carry = dg[0:1]
            groups.append(dg)
        dH = jnp.concatenate(groups[::-1], axis=0)
        dh_carry[...] = groups[-1]
        a_carry[...] = a[:8, :]

        hp = jnp.where(i < nT - 1, hp_ref[...], 0.0)
        h_prev = _lookback(jnp.concatenate([hp, h], axis=0), 1)
        da = dH * h_prev
        dmult = dH * (ig * vv)
        div = dH * mult
        dig = div * vv
        dlog_a = da * a - dmult * (a * a) * inv_mult
        dr = dlog_a * (-LRU_C * sp)
        dra = dr * r * (1.0 - r)
        dia = dig * ig * (1.0 - ig)
        dra_ref[...] = dra.astype(BF16)
        dia_ref[...] = dia.astype(BF16)
        dv_ref[...] = (_gate_pre_bwd(dra, wa_ref, hw) + _gate_pre_bwd(dia, wi_ref, hw)) + div * ig
        dba_ref[...] += _colsum(dra)
        dbi_ref[...] += _colsum(dia)
        dlam_ref[...] += _colsum(dlog_a * (-LRU_C * r))

        @pl.when(i == nT - 1)
        def _():
            dlam_ref[...] = dlam_ref[...] * (-_sigmoid(-lamv))

    return pl.pallas_call(
        body, name=name, grid=(R // tc, nT),
        in_specs=[rblk(), rblk(), hprev, rblk(off), wspec, wspec, rblk(), _vec(1, tc), _vec(1, tc), _vec(1, tc)],
        out_specs=[rblk(), rblk(), rblk(), rblk(), _vec(1, tc), _vec(1, tc), _vec(1, tc)],
        out_shape=[SDS((T, R), BF16), SDS((T, R), BF16), SDS((T, R), F32), SDS((T, R), BF16),
                   SDS((1, R), F32), SDS((1, R), F32), SDS((1, R), F32)],
        scratch_shapes=[pltpu.VMEM((8, tc), F32), pltpu.VMEM((8, tc), F32)], compiler_params=_seq_params(),
    )(dy, hs, hs, proj, w_a, w_i, v, ba, bi, lam)


def _merge_fwd(name, proj, off_cols, b_gate, P, Q, tt, tc, side=None):
    T, D = P.shape
    o0 = off_cols // tc
    o1 = o0 + D // tc

    def body(g0_ref, g1_ref, b0_ref, b1_ref, p_ref, q_ref, m_ref):
        g0 = _sigmoid(g0_ref[...] + b0_ref[...])
        g1 = _sigmoid(g1_ref[...] + b1_ref[...])
        m_ref[...] = (g0 * p_ref[...].astype(F32) + g1 * q_ref[...].astype(F32)).astype(BF16)

    outs = _ride_call(
        side, body, name=name, grid=(D // tc, T // tt),
        in_specs=[_blk(tt, tc, o0), _blk(tt, tc, o1), _vec(1, tc), _vec(1, tc, D // tc), _blk(tt, tc), _blk(tt, tc)],
        args=[proj, proj, b_gate, b_gate, P, Q], out_specs=[_blk(tt, tc)], out_shape=[SDS((T, D), BF16)],
        scratch=[], semantics=("parallel", "arbitrary"))
    return outs if side is not None else outs[0]


def _merge_bwd(name, dm, proj, off_cols, b_gate, P, Q, tt, tc):
    T, D = P.shape
    o0 = off_cols // tc
    o1 = o0 + D // tc

    def body(dm_ref, g0_ref, g1_ref, b0_ref, b1_ref, p_ref, q_ref, dp_ref, dq_ref, dl0_ref, dl1_ref, db0_ref, db1_ref):
        i = pl.program_id(1)
        g0 = _sigmoid(g0_ref[...] + b0_ref[...])
        g1 = _sigmoid(g1_ref[...] + b1_ref[...])
        d = dm_ref[...].astype(F32)
        dp_ref[...] = (d * g0).astype(BF16)
        dq_ref[...] = (d * g1).astype(BF16)
        dl0 = d * p_ref[...].astype(F32) * g0 * (1.0 - g0)
        dl1 = d * q_ref[...].astype(F32) * g1 * (1.0 - g1)
        dl0_ref[...] = dl0.astype(BF16)
        dl1_ref[...] = dl1.astype(BF16)

        @pl.when(i == 0)
        def _():
            db0_ref[...] = jnp.zeros_like(db0_ref)
            db1_ref[...] = jnp.zeros_like(db1_ref)

        db0_ref[...] += _colsum(dl0)
        db1_ref[...] += _colsum(dl1)

    b = _blk(tt, tc)
    return pl.pallas_call(
        body, name=name, grid=(D // tc, T // tt),
        in_specs=[b, _blk(tt, tc, o0), _blk(tt, tc, o1), _vec(1, tc), _vec(1, tc, D // tc), b, b],
        out_specs=[b, b, b, b, _vec(1, tc), _vec(1, tc)],
        out_shape=[SDS((T, D), BF16)] * 4 + [SDS((1, D), F32)] * 2, compiler_params=_seq_params(),
    )(dm, proj, proj, b_gate, b_gate, P, Q)


def _ffn_act_fwd(name, up, w, b, tt, tc):
    T = up.shape[0]
    K, F = w.shape
    nF = F // tc

    def body(gp_ref, halo_ref, val_ref, w_ref, b_ref, z_ref):
        i = pl.program_id(1)
        gp = gp_ref[...].astype(F32)
        halo = jnp.where(i > 0, halo_ref[...].astype(F32), 0.0)
        ext = jnp.concatenate([halo, gp], axis=0)
        wv = w_ref[...]
        c = b_ref[...] + gp * wv[K - 1:K]
        for s in range(1, K):
            c = c + _lookback(ext, s) * wv[K - 1 - s:K - s]
        z_ref[...] = (_gelu(c) * val_ref[...].astype(F32)).astype(BF16)

    return pl.pallas_call(
        body, name=name, grid=(nF, T // tt),
        in_specs=[_blk(tt, tc), _prev_halo(tt, tc), _blk(tt, tc, nF), _vec(K, tc), _vec(1, tc)],
        out_specs=_blk(tt, tc), out_shape=SDS((T, F), BF16), compiler_params=_seq_params(),
    )(up, up, up, w, b)


def _ffn_down_fwd(name, up, w, b, w_down, x1, tm, tk, nsub):
    T = up.shape[0]
    K, F = w.shape
    D = w_down.shape[1]
    nF = F // tk
    sub = tk // nsub
    r16 = tm // HALO

    def body(gp_ref, halo_ref, val_ref, w_ref, b_ref, wd_ref, x1_ref, z_ref, o_ref, acc_ref):
        i, k = pl.program_id(0), pl.program_id(1)
        total = None
        for s in range(nsub):
            cs = pl.ds(s * sub, sub)
            gp = gp_ref[:, cs].astype(F32)
            halo = jnp.where(i > 0, halo_ref[:, cs].astype(F32), 0.0)
            ext = jnp.concatenate([halo, gp], axis=0)
            wv = w_ref[:, cs]
            c = b_ref[:, cs] + gp * wv[K - 1:K]
            for t in range(1, K):
                c = c + _lookback(ext, t) * wv[K - 1 - t:K - t]
            zs = (_gelu(c) * val_ref[:, cs].astype(F32)).astype(BF16)
            z_ref[:, cs] = zs
            p = jnp.dot(zs, wd_ref[cs, :], preferred_element_type=F32)
            total = p if total is None else total + p

        @pl.when(k == 0)
        def _():
            acc_ref[...] = total

        @pl.when(k > 0)
        def _():
            acc_ref[...] += total

        @pl.when(k == nF - 1)
        def _():
            o_ref[...] = acc_ref[...] + x1_ref[...]

    blk = lambda off: pl.BlockSpec((tm, tk), lambda i, k: (i, k + off))
    return pl.pallas_call(
        body, name=name, grid=(T // tm, nF),
        in_specs=[blk(0), pl.BlockSpec((HALO, tk), lambda i, k: (jnp.maximum(i * r16 - 1, 0), k)), blk(nF),
                  pl.BlockSpec((K, tk), lambda i, k: (0, k)), pl.BlockSpec((1, tk), lambda i, k: (0, k)),
                  pl.BlockSpec((tk, D), lambda i, k: (k, 0)), pl.BlockSpec((tm, D), lambda i, k: (i, 0))],
        out_specs=[blk(0), pl.BlockSpec((tm, D), lambda i, k: (i, 0))],
        out_shape=[SDS((T, F), BF16), SDS((T, D), F32)],
        scratch_shapes=[pltpu.VMEM((tm, D), F32)],
        compiler_params=_cparams(dimension_semantics=("parallel", "arbitrary")),
    )(up, up, up, w, b, w_down, x1)


def _ffn_act_bwd(name, up, dz, w, b, tt, tc):
    T = up.shape[0]
    K, F = w.shape
    nF = F // tc
    nT = T // tt

    def body(gp_ref, gph_ref, gpn_ref, val_ref, valn_ref, dz_ref, dzn_ref, w_ref, b_ref, dup_ref, dw_ref, db_ref):
        i = pl.program_id(1)
        wv = w_ref[...]
        gp = gp_ref[...].astype(F32)
        halo = jnp.where(i > 0, gph_ref[...].astype(F32), 0.0)
        full = jnp.concatenate([halo, gp, gpn_ref[...].astype(F32)], axis=0)
        c = b_ref[...] + full * wv[K - 1:K]
        for s in range(1, K):
            c = c + pltpu.roll(full, s, 0) * wv[K - 1 - s:K - s]
        c = c[HALO:]
        gate, dgate = _gelu_and_grad(c)
        dz_e = jnp.concatenate([dz_ref[...], dzn_ref[...]], axis=0).astype(F32)
        val_e = jnp.concatenate([val_ref[...], valn_ref[...]], axis=0).astype(F32)
        dc_e = dz_e * val_e * dgate
        row = lax.broadcasted_iota(jnp.int32, (tt + HALO, 1), 0)
        dc_e = jnp.where((row < tt) | (i < nT - 1), dc_e, 0.0)
        dc = dc_e[:tt]
        dgp = dc * wv[K - 1:K]
        uext = full[:tt + HALO]
        rows = [_colsum(dc * gp)]
        for s in range(1, K):
            dgp = dgp + _lookahead(dc_e, s, tt) * wv[K - 1 - s:K - s]
            rows.append(_colsum(dc * _lookback(uext, s)))
        dup_ref[0] = dgp.astype(BF16)
        dup_ref[1] = (dz_e[:tt] * gate[:tt]).astype(BF16)
        dw = jnp.concatenate(rows[::-1] + [jnp.zeros((8 - K, tc), F32)], axis=0)

        @pl.when(i == 0)
        def _():
            dw_ref[...] = jnp.zeros_like(dw_ref)
            db_ref[...] = jnp.zeros_like(db_ref)

        dw_ref[...] += dw
        db_ref[...] += _colsum(dc)

    return pl.pallas_call(
        body, name=name, grid=(nF, nT),
        in_specs=[_blk(tt, tc), _prev_halo(tt, tc), _next_halo(tt, tc, T), _blk(tt, tc, nF), _next_halo(tt, tc, T, nF),
                  _blk(tt, tc), _next_halo(tt, tc, T), _vec(K, tc), _vec(1, tc)],
        out_specs=[pl.BlockSpec((2, tt, tc), lambda j, i: (0, i, j)), _vec(8, tc), _vec(1, tc)],
        out_shape=[SDS((2, T, F), BF16), SDS((8, F), F32), SDS((1, F), F32)],
        compiler_params=_seq_params(),
    )(up, up, up, up, up, dz, dz, w, b)


def _adamw(name, w, g, m, v):
    R, C = w.shape
    tr = R
    if R * C * 4 > ADAMW_BLOCK_BYTES:
        tr = _pick(R, [t for t in (256, 128, 64, 32, 16, 8) if t * C * 4 <= ADAMW_BLOCK_BYTES])
    c1 = 1.0 - ADAM_B1 ** ADAM_STEP
    c2 = 1.0 - ADAM_B2 ** ADAM_STEP

    def body(w_ref, g_ref, m_ref, v_ref, d_ref, nm_ref, nv_ref):
        gv = g_ref[...]
        nm = ADAM_B1 * m_ref[...] + (1.0 - ADAM_B1) * gv
        nv = ADAM_B2 * v_ref[...] + (1.0 - ADAM_B2) * (gv * gv)
        nm_ref[...] = nm
        nv_ref[...] = nv
        d_ref[...] = -ADAM_LR * ((nm / c1) / (jnp.sqrt(nv / c2) + ADAM_EPS) + ADAM_WD * w_ref[...])

    spec = pl.BlockSpec((tr, C), lambda i: (i, 0))
    return pl.pallas_call(
        body, name=name, grid=(R // tr,), in_specs=[spec] * 4, out_specs=[spec] * 3,
        out_shape=[SDS((R, C), F32)] * 3, compiler_params=_cparams(dimension_semantics=("parallel",)),
    )(w, g, m, v)


def _add_halves(name, part, rcv, pos):
    S, R, C = part.shape
    h = R // 2
    tr = _pick(h, (512, 256, 128, 64, 32, 16))
    nb = h // tr

    def body(pos_ref, p_ref, r_ref, o_ref):
        o_ref[...] = (p_ref[...].astype(F32) + r_ref[...].astype(F32)).astype(BF16)

    return pl.pallas_call(
        body, name=name,
        grid_spec=pltpu.PrefetchScalarGridSpec(
            num_scalar_prefetch=1, grid=(S, nb),
            in_specs=[pl.BlockSpec((None, tr, C), lambda s, r, pos: (s, pos[0] * nb + r, 0)),
                      pl.BlockSpec((None, tr, C), lambda s, r, pos: (s, r, 0))],
            out_specs=pl.BlockSpec((None, tr, C), lambda s, r, pos: (s, r, 0))),
        out_shape=SDS((S, h, C), BF16), compiler_params=_cparams(dimension_semantics=("parallel", "parallel")),
    )(pos, part, rcv)


def _add_chips(name, chipsum, rcv, pos):
    S, h, C = chipsum.shape
    tr = _pick(h, (512, 256, 128, 64, 32, 16))

    def body(pos_ref, own_ref, r_ref, o_ref):
        o_ref[...] = ((own_ref[...].astype(F32) + r_ref[0].astype(F32)) + r_ref[1].astype(F32)) + r_ref[2].astype(F32)

    return pl.pallas_call(
        body, name=name,
        grid_spec=pltpu.PrefetchScalarGridSpec(
            num_scalar_prefetch=1, grid=(h // tr,),
            in_specs=[pl.BlockSpec((None, tr, C), lambda r, pos: (pos[1], r, 0)),
                      pl.BlockSpec((3, tr, C), lambda r, pos: (0, r, 0))],
            out_specs=pl.BlockSpec((None, tr, C), lambda r, pos: (pos[0], r, 0))),
        out_shape=SDS((2, h, C), F32), compiler_params=_cparams(dimension_semantics=("parallel",)),
    )(pos, chipsum, rcv)


def _place():
    x, y, c = lax.axis_index("x"), lax.axis_index("y"), lax.axis_index("c")
    chips = [(1 - x, y), (x, 1 - y), (1 - x, 1 - y)]
    slots = [2 * cx + cy for cx, cy in chips]
    return x, y, c, chips, slots


def _gather_side(shards, split):
    n = len(shards)

    def copies(ins, outs, sems):
        send_sems, recv_sems, local_sems = sems
        x, y, c, chips, slots = _place()
        me = 2 * x + y
        sibling = (x, y, 1 - c)

        def rows(a, half):
            hrows = ins[a].shape[0] // 2
            return pl.ds(half * hrows, hrows)

        def ici(a, j):
            if split[a]:
                src, dst = ins[a].at[rows(a, c)], outs[a].at[me, rows(a, c)]
            else:
                src, dst = ins[a], outs[a].at[me]
            return pltpu.make_async_remote_copy(src_ref=src, dst_ref=dst, send_sem=send_sems.at[a, j],
                                                recv_sem=recv_sems.at[a, j], device_id=(*chips[j], c), device_id_type=MESH)

        def landed(a, j):
            dst = outs[a].at[slots[j], rows(a, c)] if split[a] else outs[a].at[slots[j]]
            return pltpu.make_async_remote_copy(src_ref=dst, dst_ref=dst, send_sem=send_sems.at[a, j],
                                                recv_sem=recv_sems.at[a, j], device_id=(*chips[j], c), device_id_type=MESH)

        def d2d(a, j, half):
            blk = outs[a].at[slots[j], rows(a, half)]
            return pltpu.make_async_remote_copy(src_ref=blk, dst_ref=blk, send_sem=send_sems.at[a, 3 + j],
                                                recv_sem=recv_sems.at[a, 3 + j], device_id=sibling, device_id_type=MESH)

        local = [pltpu.make_async_copy(ins[a], outs[a].at[me], local_sems.at[a]) for a in range(n)]
        return c, ici, landed, d2d, local

    def start(ins, outs, sems):
        c, ici, landed, d2d, local = copies(ins, outs, sems)
        for cp in local:
            cp.start()
        for a in range(n):
            for j in range(3):
                ici(a, j).start()

    def mid(ins, outs, sems):
        c, ici, landed, d2d, local = copies(ins, outs, sems)
        for a in range(n):
            for j in range(3):
                landed(a, j).wait_recv()
                if split[a]:
                    d2d(a, j, c).start()

    def finish(ins, outs, sems):
        c, ici, landed, d2d, local = copies(ins, outs, sems)
        for a in range(n):
            if split[a]:
                for j in range(3):
                    d2d(a, j, 1 - c).wait_recv()
        for a in range(n):
            for j in range(3):
                ici(a, j).wait_send()
                if split[a]:
                    d2d(a, j, c).wait_send()
        for cp in local:
            cp.wait()

    return _Side(shards, [SDS((N_SLOTS,) + s.shape, s.dtype) for s in shards],
                 [pltpu.SemaphoreType.DMA((n, 6)), pltpu.SemaphoreType.DMA((n, 6)), pltpu.SemaphoreType.DMA((n,))],
                 start, mid, finish)


def _swap_partial_halves(name, parts):
    side = _swap_side(parts)
    n = len(parts)

    def body(*refs):
        ins, outs, sems = refs[:n], refs[n:2 * n], refs[2 * n:]
        side.start(ins, outs, sems)
        side.finish(ins, outs, sems)

    return pl.pallas_call(
        body, name=name, in_specs=[ANY] * n, out_specs=[ANY] * n, out_shape=side.out_shapes,
        scratch_shapes=side.sem_shapes, compiler_params=_cparams(has_side_effects=True),
    )(*parts)


def _swap_side(parts):
    n = len(parts)

    def copies(ins, outs, sems):
        send_sems, recv_sems = sems
        x, y, c, _, _ = _place()
        cps = []
        for a in range(n):
            h = ins[a].shape[1] // 2
            cps.append(pltpu.make_async_remote_copy(
                src_ref=ins[a].at[:, pl.ds((1 - c) * h, h)], dst_ref=outs[a], send_sem=send_sems.at[a],
                recv_sem=recv_sems.at[a], device_id=(x, y, 1 - c), device_id_type=MESH))
        return cps

    def start(ins, outs, sems):
        for cp in copies(ins, outs, sems):
            cp.start()

    def finish(ins, outs, sems):
        for cp in copies(ins, outs, sems):
            cp.wait()

    return _Side(parts, [SDS((p.shape[0], p.shape[1] // 2, p.shape[2]), p.dtype) for p in parts],
                 [pltpu.SemaphoreType.DMA((n,)), pltpu.SemaphoreType.DMA((n,))], start, None, finish)


def _exchange_side(sums):
    n = len(sums)

    def copies(ins, outs, sems):
        send_sems, recv_sems = sems
        x, y, c, chips, slots = _place()
        return [pltpu.make_async_remote_copy(
            src_ref=ins[a].at[slots[j]], dst_ref=outs[a].at[j], send_sem=send_sems.at[a, j],
            recv_sem=recv_sems.at[a, j], device_id=(*chips[j], c), device_id_type=MESH)
            for a in range(n) for j in range(3)]

    def start(ins, outs, sems):
        for cp in copies(ins, outs, sems):
            cp.start()

    def finish(ins, outs, sems):
        for cp in copies(ins, outs, sems):
            cp.wait()

    return _Side(sums, [SDS((3,) + s.shape[1:], s.dtype) for s in sums],
                 [pltpu.SemaphoreType.DMA((n, 3)), pltpu.SemaphoreType.DMA((n, 3))], start, None, finish)


def _swap_reduced_halves(totals):
    side = _reduced_swap_side(totals)
    n = len(totals)

    def body(*refs):
        ins, outs, sems = refs[:n], refs[n:2 * n], refs[2 * n:]
        side.start(ins, outs, sems)
        side.finish(ins, outs, sems)

    return pl.pallas_call(
        body, name="swap_reduced_halves", in_specs=[ANY] * n, out_specs=[ANY] * n, out_shape=side.out_shapes,
        input_output_aliases={a: a for a in range(n)}, scratch_shapes=side.sem_shapes,
        compiler_params=_cparams(has_side_effects=True),
    )(*totals)


def _reduced_swap_side(totals):
    n = len(totals)

    def sends(outs, sems):
        send_sems, recv_sems = sems
        x, y, c, _, _ = _place()
        return [pltpu.make_async_remote_copy(src_ref=outs[a].at[c], dst_ref=outs[a].at[c], send_sem=send_sems.at[a],
                                             recv_sem=recv_sems.at[a], device_id=(x, y, 1 - c), device_id_type=MESH)
                for a in range(n)]

    def start(ins, outs, sems):
        for cp in sends(outs, sems):
            cp.start()

    def finish(ins, outs, sems):
        send_sems, recv_sems = sems
        x, y, c, _, _ = _place()
        for a in range(n):
            got = outs[a].at[1 - c]
            pltpu.make_async_remote_copy(src_ref=got, dst_ref=got, send_sem=send_sems.at[a], recv_sem=recv_sems.at[a],
                                         device_id=(x, y, 1 - c), device_id_type=MESH).wait_recv()
        for cp in sends(outs, sems):
            cp.wait_send()

    return _Side(totals, [SDS(t.shape, t.dtype) for t in totals],
                 [pltpu.SemaphoreType.DMA((n,)), pltpu.SemaphoreType.DMA((n,))], start, None, finish, in_place=True)


def _allreduce_small(pack):
    rows = pack.shape[0]

    def body(in_ref, out_ref, buf, send_sems, recv_sems):
        x, y, c = lax.axis_index("x"), lax.axis_index("y"), lax.axis_index("c")
        me = 4 * x + 2 * y + c
        buf[me] = in_ref[...]
        cps = []
        for k in range(1, 8):
            kx, ky, kc = (k >> 2) & 1, (k >> 1) & 1, k & 1
            peer = (x ^ kx, y ^ ky, c ^ kc)
            cps.append(pltpu.make_async_remote_copy(src_ref=in_ref, dst_ref=buf.at[me], send_sem=send_sems.at[k - 1],
                                                    recv_sem=recv_sems.at[k - 1], device_id=peer, device_id_type=MESH))
        for cp in cps:
            cp.start()
        for k in range(1, 8):
            got = buf.at[me ^ k]
            pltpu.make_async_remote_copy(src_ref=got, dst_ref=got, send_sem=send_sems.at[k - 1],
                                         recv_sem=recv_sems.at[k - 1], device_id=(x, y, c), device_id_type=MESH).wait_recv()
        for cp in cps:
            cp.wait_send()
        acc = buf[0]
        for d in range(1, 8):
            acc = acc + buf[d]
        out_ref[...] = acc

    vm = pl.BlockSpec(memory_space=pltpu.VMEM)
    return pl.pallas_call(
        body, name="allreduce_small", in_specs=[vm], out_specs=vm, out_shape=SDS(pack.shape, F32),
        scratch_shapes=[pltpu.VMEM((8, rows, LANES), F32), pltpu.SemaphoreType.DMA((7,)), pltpu.SemaphoreType.DMA((7,))],
        compiler_params=_cparams(has_side_effects=True),
    )(pack)


def _rows2d(a):
    return a.reshape((-1, a.shape[-1]))


def _from_slots_blockdiag(g, H, w):
    q = w // N_SLOTS
    return g.reshape(N_SLOTS, H, q, w).transpose(1, 0, 2, 3).reshape(H, w, w)


def _to_slots_blockdiag(d, H, w):
    q = w // N_SLOTS
    return d.reshape(H, N_SLOTS, q, w).transpose(1, 0, 2, 3).reshape(N_SLOTS, H * q, w)


def kernel(x, g_mix, w_in, b_gate, w_pool, pool_scale, lru_conv_w, lru_conv_b, w_a, b_a, w_i, b_i, lru_lambda, w_pool_proj, w_lru_proj, w_out, g_mlp, w_up, ffn_conv_w, ffn_conv_b, w_down, g_final, loss_target, m_g_mix, m_w_in, m_b_gate, m_w_pool, m_pool_scale, m_lru_conv_w, m_lru_conv_b, m_w_a, m_b_a, m_w_i, m_b_i, m_lru_lambda, m_w_pool_proj, m_w_lru_proj, m_w_out, m_g_mlp, m_w_up, m_ffn_conv_w, m_ffn_conv_b, m_w_down, m_g_final, v_g_mix, v_w_in, v_b_gate, v_w_pool, v_pool_scale, v_lru_conv_w, v_lru_conv_b, v_w_a, v_b_a, v_w_i, v_b_i, v_lru_lambda, v_w_pool_proj, v_w_lru_proj, v_w_out, v_g_mlp, v_w_up, v_ffn_conv_w, v_ffn_conv_b, v_w_down, v_g_final):
    weights = dict(g_mix=g_mix, w_in=w_in, b_gate=b_gate, w_pool=w_pool, pool_scale=pool_scale, lru_conv_w=lru_conv_w,
                   lru_conv_b=lru_conv_b, w_a=w_a, b_a=b_a, w_i=w_i, b_i=b_i, lru_lambda=lru_lambda,
                   w_pool_proj=w_pool_proj, w_lru_proj=w_lru_proj, w_out=w_out, g_mlp=g_mlp, w_up=w_up,
                   ffn_conv_w=ffn_conv_w, ffn_conv_b=ffn_conv_b, w_down=w_down, g_final=g_final)
    mom_m = dict(g_mix=m_g_mix, w_in=m_w_in, b_gate=m_b_gate, w_pool=m_w_pool, pool_scale=m_pool_scale,
                 lru_conv_w=m_lru_conv_w, lru_conv_b=m_lru_conv_b, w_a=m_w_a, b_a=m_b_a, w_i=m_w_i, b_i=m_b_i,
                 lru_lambda=m_lru_lambda, w_pool_proj=m_w_pool_proj, w_lru_proj=m_w_lru_proj, w_out=m_w_out,
                 g_mlp=m_g_mlp, w_up=m_w_up, ffn_conv_w=m_ffn_conv_w, ffn_conv_b=m_ffn_conv_b, w_down=m_w_down,
                 g_final=m_g_final)
    mom_v = dict(g_mix=v_g_mix, w_in=v_w_in, b_gate=v_b_gate, w_pool=v_w_pool, pool_scale=v_pool_scale,
                 lru_conv_w=v_lru_conv_w, lru_conv_b=v_lru_conv_b, w_a=v_w_a, b_a=v_b_a, w_i=v_w_i, b_i=v_b_i,
                 lru_lambda=v_lru_lambda, w_pool_proj=v_w_pool_proj, w_lru_proj=v_w_lru_proj, w_out=v_w_out,
                 g_mlp=v_g_mlp, w_up=v_w_up, ffn_conv_w=v_ffn_conv_w, ffn_conv_b=v_ffn_conv_b, w_down=v_w_down,
                 g_final=v_g_final)
    order = list(weights)

    T, D = x.shape[1], x.shape[2]
    Cp = pool_scale.shape[1]
    G, gw = w_pool.shape[1], w_pool.shape[3]
    H, hw = w_a.shape[1], w_a.shape[3]
    R = b_a.shape[1]
    F = ffn_conv_b.shape[1]
    KC, KF = lru_conv_w.shape[1], ffn_conv_w.shape[1]
    p0, p1, p2 = Cp, Cp + R, Cp + 2 * R
    xs = x.reshape(T, D)
    tgt = loss_target.reshape(T, D)

    my_x, my_y, my_c = lax.axis_index("x"), lax.axis_index("y"), lax.axis_index("c")
    slot = 2 * my_x + my_y
    pos = jnp.stack([my_c, slot]).astype(jnp.int32)

    big = ["w_in", "w_pool", "w_a", "w_i", "w_pool_proj", "w_lru_proj", "w_out", "w_up", "w_down"]
    first = big[:4]
    shard16 = {n: _rows2d(weights[n][0]).astype(BF16) for n in big}
    first_side = _gather_side([shard16[n] for n in first] + [lru_conv_w[0], ffn_conv_w[0]], [True] * 4 + [False, False])
    n_in, n_up, n_pp = w_in.shape[2], w_up.shape[2], w_pool_proj.shape[2]

    tt_row = _pick(T, (256, 128))
    tt_seq = _pick(T, (512, 256, 128))
    tt_pool = _pick(T, (1024, 512, 256, 128))
    tc_seq = _pick(R, (512, 256, 128))
    tm1 = _pick(T, (1024, 512, 256))
    tm2 = _pick(T, (2048, 1024, 512, 256))
    tkT = _pick(T, (1024, 512, 256))
    tkT2 = _pick(T, (4096, 2048, 1024, 512, 256))
    tn_in = _pick(n_in, (768, 1152, 384, 128))
    tn_up = _pick(n_up, (1536, 1024, 768, 512, 128))
    tk_up = _pick(n_up, (1024, 768, 512, 128))
    tn_pp = _pick(n_pp, (512, 256, 128))
    tD1 = _pick(D, (1024, 512))
    tD2 = _pick(D, (2048, 1024, 512))
    tF = _pick(F, (1536, 1024, 512))
    tkF = _pick(F, (2048, 1536, 1024, 512))

    h, *gathered = _rms_fwd("rms1_fwd", xs, g_mix, tt_row, side=first_side)
    gw_ = dict(zip(first + ["lru_conv_w", "ffn_conv_w"], gathered))
    W_in = gw_["w_in"]
    W_pool = _from_slots_blockdiag(gw_["w_pool"], G, gw)
    W_a = _from_slots_blockdiag(gw_["w_a"], H, hw)
    W_i = _from_slots_blockdiag(gw_["w_i"], H, hw)
    cw_lru = gw_["lru_conv_w"].transpose(1, 0, 2).reshape(KC, R)
    cw_ffn = gw_["ffn_conv_w"].transpose(1, 0, 2).reshape(KF, F)
    on_proj = ["w_pool_proj", "w_lru_proj", "w_out", "w_down"]
    proj3, *late = _matmul("proj_fwd", "nn", h, W_in, out_dtype=F32, tm=tm2, tn=tn_in, tk=D,
                           side=_gather_side([shard16[n] for n in on_proj], [True] * len(on_proj)))
    proj = proj3[0]
    gw_.update(zip(on_proj, late))
    W_pp = gw_["w_pool_proj"]
    W_lp = gw_["w_lru_proj"].reshape(R, D)
    W_out = gw_["w_out"].reshape(D, D)
    W_down = gw_["w_down"].reshape(F, D)
    up16, eD = shard16["w_up"], D // 8
    d_pool, z_pool, y_pool = _pool_fwd("pool_fwd", proj, Cp, gw, tt_pool, W_pool, pool_scale)
    v_f, v_b = _conv_fwd("lru_conv_fwd", proj, p0, cw_lru, lru_conv_b, tt_seq, tc_seq)
    hs, y_lru, up_a = _lru_fwd("lru_scan_fwd", W_a, W_i, v_f, proj, p1, b_a, b_i, lru_lambda, tt_seq, tc_seq,
                               side=_gather_side([up16[:3 * eD]], [True]))
    P = _matmul("pool_proj_fwd", "nn", y_pool, W_pp, out_dtype=BF16, tm=tm2, tn=tn_pp, tk=Cp)[0]
    Q3, up_b = _matmul("lru_proj_fwd", "nn", y_lru, W_lp, out_dtype=BF16, tm=tm1, tn=tD1, tk=R,
                       side=_gather_side([up16[3 * eD:5 * eD]], [True]))
    Q = Q3[0]
    merged, up_c = _merge_fwd("merge_fwd", proj, p2, b_gate, P, Q, tt_seq, tc_seq,
                              side=_gather_side([up16[5 * eD:6 * eD]], [True]))
    x13, up_d = _matmul("out_fwd", "nn", merged, W_out, out_dtype=F32, tm=tm1, tn=tD1, tk=D, res=xs,
                        side=_gather_side([up16[6 * eD:]], [True]))
    x1 = x13[0]
    W_up = jnp.concatenate([up_a, up_b, up_c, up_d], axis=1)
    h2 = _rms_fwd("rms2_fwd", x1, g_mlp, tt_row)
    up = _matmul("up_fwd", "nn", h2, W_up, out_dtype=BF16, tm=tm1, tn=tn_up, tk=D)[0]
    z, x2 = _ffn_down_fwd("ffn_down_fwd", up, cw_ffn, ffn_conv_b, W_down, x1, _pick(T, (512, 256)),
                          _pick(F, (1024, 512)), 4)

    dx2, dx2b, d_g_final, lossvec = _final_loss_bwd("final_loss_bwd", x2, tgt, g_final.reshape(1, D), tt_row)
    dz = _matmul("down_bwd_x", "nt", dx2b, W_down, out_dtype=BF16, tm=tm1, tn=tF, tk=D)[0]
    dW_down = _matmul("down_bwd_w", "tn", z, dx2b, out_dtype=BF16, tm=_pick(F, (1024, 512)), tn=tD1, tk=tkT2)
    dup, d_cw_ffn, d_ffn_b = _ffn_act_bwd("ffn_act_bwd", up, dz, cw_ffn, ffn_conv_b, tt_seq, tc_seq)
    dh2 = _matmul("up_bwd_x", "nt", dup, W_up, out_dtype=F32, tm=tm1, tn=tD1, tk=n_up)[0]
    dW_up = _matmul("up_bwd_w", "tn", h2, dup, out_seg=N_SLOTS, out_dtype=BF16, tm=tD1, tn=tk_up, tk=tkT2)
    dx1, dx1b, d_g_mlp = _rms_bwd("rms2_bwd", dh2, x1, g_mlp, dx2, tt_row, True)
    dmerged = _matmul("out_bwd_x", "nt", dx1b, W_out, out_dtype=BF16, tm=tm1, tn=tD2, tk=D)[0]
    q_rows = lambda a, rows: a.reshape(N_SLOTS, rows // N_SLOTS, a.shape[-1])
    ffn_parts = [dW_up, q_rows(dW_down, F)]
    dW_out, *ffn_sib = _matmul("out_bwd_w", "tn", merged, dx1b, out_dtype=BF16, tm=tD1, tn=tD1, tk=tkT2,
                               side=_swap_side(ffn_parts))
    ffn_sums = [_add_halves(f"add_halves_{n}", p, r, pos) for n, p, r in zip(big[7:], ffn_parts, ffn_sib)]
    dP, dQ, dgl0, dgl1, d_bg0, d_bg1 = _merge_bwd("merge_bwd", dmerged, proj, p2, b_gate, P, Q, tt_seq, tc_seq)
    dy_pool = _matmul("pool_proj_bwd_x", "nt", dP, W_pp, out_dtype=F32, tm=tm2, tn=Cp, tk=tn_pp)[0]
    dW_pp = _matmul("pool_proj_bwd_w", "tn", y_pool, dP, out_seg=N_SLOTS, out_dtype=BF16, tm=Cp, tn=tn_pp,
                    tk=_pick(T, (2048, 1024, 512, 256)))
    dy_lru = _matmul("lru_proj_bwd_x", "nt", dQ, W_lp, out_dtype=F32, tm=tm1, tn=tD1, tk=D)[0]
    dW_lp = _matmul("lru_proj_bwd_w", "tn", y_lru, dQ, out_dtype=BF16, tm=_pick(R, (1024, 512)), tn=tD1, tk=tkT2)
    dra, dia, dv, du_gelu, d_b_a, d_b_i, d_lam = _lru_bwd(
        "lru_scan_bwd", dy_lru, hs, proj, p1, W_a, W_i, v_f, b_a, b_i, lru_lambda, tt_seq, tc_seq)
    dW_a, dW_i = _bd_bwd_w("lru_gate_bwd_w", v_b, [dra, dia], hw, tkT)
    du_lru, d_cw_lru, d_lru_b = _conv_bwd("lru_conv_bwd", dv, proj, p0, cw_lru, tt_seq, tc_seq)
    dzp, dd, d_pool_scale = _pool_bwd_a("pool_scale_bwd", dy_pool, z_pool, pool_scale, W_pool, tt_row)
    (dW_pool,) = _bd_bwd_w("pool_mix_bwd_w", d_pool, [dzp], gw, tkT)
    du_pool = _pool_bwd_b("pool_bwd", dd, gw, tt_pool)
    dproj = jnp.concatenate([du_pool, du_lru, du_gelu, dgl0, dgl1], axis=1)

    mix_parts = [_to_slots_blockdiag(dW_pool.astype(BF16), G, gw),
                 _to_slots_blockdiag(dW_a.astype(BF16), H, hw),
                 _to_slots_blockdiag(dW_i.astype(BF16), H, hw),
                 dW_pp, q_rows(dW_lp, R), q_rows(dW_out, D)]
    mix_sib = _swap_partial_halves("swap_partial_halves", mix_parts)
    mix_sums = [_add_halves(f"add_halves_{n}", p, r, pos) for n, p, r in zip(big[1:7], mix_parts, mix_sib)]
    dW_in, *ffn_chips = _matmul("proj_bwd_w", "tn", h, dproj, out_seg=N_SLOTS, out_dtype=BF16, tm=tD1, tn=tn_in,
                                tk=tkT2, side=_exchange_side(ffn_sums))
    (in_sib,) = _swap_partial_halves("swap_partial_halves_w_in", [dW_in])
    in_sum = _add_halves("add_halves_w_in", dW_in, in_sib, pos)
    ffn_totals = [_add_chips(f"add_chips_{n}", s, r, pos) for n, s, r in zip(big[7:], ffn_sums, ffn_chips)]
    dh3, *rest = _matmul("proj_bwd_x", "nt", dproj, W_in, out_dtype=F32, tm=tm1, tn=tD1, tk=n_in,
                         side=_both(_exchange_side([in_sum] + mix_sums), _reduced_swap_side(ffn_totals)))
    chips, ffn_full = rest[:7], rest[7:]
    grad_x, d_g_mix = _rms_bwd("rms1_bwd", dh3[0], xs, g_mix, dx1, tt_row, False)
    totals = [_add_chips(f"add_chips_{n}", s, r, pos) for n, s, r in zip(big[:7], [in_sum] + mix_sums, chips)]
    full = list(_swap_reduced_halves(totals)) + list(ffn_full)
    grads = {n: f.reshape(weights[n].shape) for n, f in zip(big, full)}

    small = ["g_mix", "b_gate", "pool_scale", "lru_conv_b", "b_a", "b_i", "lru_lambda", "g_mlp", "ffn_conv_b", "g_final"]
    small_g = [d_g_mix, jnp.concatenate([d_bg0, d_bg1], axis=1), d_pool_scale, d_lru_b, d_b_a, d_b_i, d_lam, d_g_mlp,
               d_ffn_b, d_g_final]
    pieces = [g.reshape(-1) for g in small_g] + [d_cw_lru[:KC].reshape(-1), d_cw_ffn[:KF].reshape(-1), lossvec.reshape(-1)]
    sizes = [p.shape[0] for p in pieces]
    total = sum(sizes)
    rows = -(-total // (8 * LANES)) * 8
    pack = jnp.concatenate(pieces + [jnp.zeros((rows * LANES - total,), F32)]).reshape(rows, LANES)
    summed = _allreduce_small(pack).reshape(-1)
    offs = [0]
    for s in sizes:
        offs.append(offs[-1] + s)
    for k, n in enumerate(small):
        grads[n] = summed[offs[k]:offs[k + 1]].reshape(weights[n].shape)
    ns = len(small)
    g_cw_lru = summed[offs[ns]:offs[ns + 1]].reshape(KC, R)
    g_cw_ffn = summed[offs[ns + 1]:offs[ns + 2]].reshape(KF, F)
    grads["lru_conv_w"] = lax.dynamic_slice_in_dim(g_cw_lru, slot * (R // N_SLOTS), R // N_SLOTS, axis=1)[None]
    grads["ffn_conv_w"] = lax.dynamic_slice_in_dim(g_cw_ffn, slot * (F // N_SLOTS), F // N_SLOTS, axis=1)[None]
    loss = jnp.sum(summed[offs[ns + 2]:offs[ns + 3]]) * (0.5 / D)

    delta, new_m, new_v = {}, {}, {}
    for n in big:
        d_, m_, v_ = _adamw(f"adamw_{n}", _rows2d(weights[n][0]), _rows2d(grads[n][0]), _rows2d(mom_m[n][0]),
                            _rows2d(mom_v[n][0]))
        delta[n], new_m[n], new_v[n] = (t.reshape(weights[n].shape) for t in (d_, m_, v_))
    rest = small + ["lru_conv_w", "ffn_conv_w"]
    rsizes = [weights[n].size for n in rest]
    rtotal = sum(rsizes)
    rrows = -(-rtotal // (8 * LANES)) * 8

    def packed(tree):
        flat = [tree[n].reshape(-1) for n in rest] + [jnp.zeros((rrows * LANES - rtotal,), F32)]
        return jnp.concatenate(flat).reshape(rrows, LANES)

    d_, m_, v_ = _adamw("adamw_small", packed(weights), packed(grads), packed(mom_m), packed(mom_v))
    o = 0
    for n, s in zip(rest, rsizes):
        for tree, flat in ((delta, d_), (new_m, m_), (new_v, v_)):
            tree[n] = flat.reshape(-1)[o:o + s].reshape(weights[n].shape)
        o += s

    return (loss, grad_x.reshape(x.shape), *[grads[n] for n in order], *[delta[n] for n in order],
            *[new_m[n] for n in order], *[new_v[n] for n in order])
```

```python
import functools

import jax
import jax.numpy as jnp
from jax import lax
from jax.experimental import pallas as pl
from jax.experimental.pallas import tpu as pltpu

F32 = jnp.float32
BF16 = jnp.bfloat16
MESH = pl.DeviceIdType.MESH

VMEM_LIMIT_V7X = 56 * 1024 * 1024
LANES = 128
HALO = 16
N_SLOTS = 4
ADAMW_BLOCK_BYTES = 3 * 512 * 1024

EPS = 1e-6
LRU_C = 8.0
ADAM_LR = 0.001
ADAM_B1 = 0.9
ADAM_B2 = 0.999
ADAM_EPS = 1e-08
ADAM_WD = 0.01
ADAM_STEP = 10
GELU_C = 0.7978845608028654
GELU_A = 0.044715

SDS = jax.ShapeDtypeStruct
ANY = pl.BlockSpec(memory_space=pl.ANY)


def _pick(n, cands):
    for c in cands:
        if c <= n and n % c == 0:
            return c
    raise ValueError(f"no tile for {n} in {cands}")


def _cparams(**kw):
    return pltpu.CompilerParams(vmem_limit_bytes=VMEM_LIMIT_V7X, **kw)


def _sigmoid(x):
    return 0.5 * jnp.tanh(0.5 * x) + 0.5


def _gelu(x):
    th = jnp.tanh(x * (GELU_C + (GELU_C * GELU_A) * (x * x)))
    hx = 0.5 * x
    return hx + hx * th


def _gelu_and_grad(x):
    x2 = x * x
    th = jnp.tanh(x * (GELU_C + (GELU_C * GELU_A) * x2))
    hx = 0.5 * x
    g = hx + hx * th
    dg = (0.5 + 0.5 * th) + (hx * (1.0 - th * th)) * (GELU_C + (3.0 * GELU_C * GELU_A) * x2)
    return g, dg


def _lookback(ext, k):
    return pltpu.roll(ext, k, 0)[HALO:]


def _lookahead(ext, k, tt):
    return pltpu.roll(ext, ext.shape[0] - k, 0)[:tt]


def _colsum(v):
    return jnp.sum(v, axis=0, keepdims=True)


def _seg3(a):
    return a if a.ndim == 3 else a.reshape((1,) + a.shape)


def _seg_spec(shape3, br, bc, rowf, colf):
    assert shape3[1] % br == 0 and shape3[2] % bc == 0, (shape3, br, bc)
    nb = shape3[2] // bc

    def imap(i, j, k):
        cb = colf(i, j, k)
        return (cb // nb, rowf(i, j, k), cb % nb)

    return pl.BlockSpec((None, br, bc), imap)


_I = lambda i, j, k: i
_J = lambda i, j, k: j
_K = lambda i, j, k: k


class _Side:
    def __init__(self, ins, out_shapes, sem_shapes, start, mid, finish, in_place=False):
        self.ins, self.out_shapes, self.sem_shapes = list(ins), list(out_shapes), list(sem_shapes)
        self.start, self.mid, self.finish = start, mid, finish
        self.alias = [(a, a) for a in range(len(self.ins))] if in_place else []


def _both(s1, s2):
    ni, no, ns = len(s1.ins), len(s1.out_shapes), len(s1.sem_shapes)

    def phase(f1, f2):
        if f1 is None and f2 is None:
            return None

        def run(ins, outs, sems):
            if f1 is not None:
                f1(ins[:ni], outs[:no], sems[:ns])
            if f2 is not None:
                f2(ins[ni:], outs[no:], sems[ns:])

        return run

    both = _Side(s1.ins + s2.ins, s1.out_shapes + s2.out_shapes, s1.sem_shapes + s2.sem_shapes,
                 phase(s1.start, s2.start), phase(s1.mid, s2.mid), phase(s1.finish, s2.finish))
    both.alias = s1.alias + [(ni + a, no + b) for a, b in s2.alias]
    return both


def _ride(side, body, n_in, n_out, n_scr, grid):
    n_sin, n_sout = len(side.ins), len(side.out_shapes)
    n_steps = 1
    for g in grid:
        n_steps *= g

    def wrapped(*refs):
        p = 0
        ins, s_in = refs[p:p + n_in], refs[p + n_in:p + n_in + n_sin]
        p += n_in + n_sin
        outs, s_out = refs[p:p + n_out], refs[p + n_out:p + n_out + n_sout]
        p += n_out + n_sout
        scr, s_sem = refs[p:p + n_scr], refs[p + n_scr:]
        step = 0
        for d, g in enumerate(grid):
            step = step * g + pl.program_id(d)

        @pl.when(step == 0)
        def _():
            side.start(s_in, s_out, s_sem)

        body(*ins, *outs, *scr)

        if side.mid is not None:
            @pl.when(step == (2 * n_steps) // 3)
            def _():
                side.mid(s_in, s_out, s_sem)

        @pl.when(step == n_steps - 1)
        def _():
            side.finish(s_in, s_out, s_sem)

    return wrapped


def _ride_call(side, body, *, name, grid, in_specs, args, out_specs, out_shape, scratch, semantics):
    out_specs, out_shape = list(out_specs), list(out_shape)
    aliases = {}
    if side is not None:
        aliases = {len(args) + a: len(out_shape) + b for a, b in side.alias}
        body = _ride(side, body, len(args), len(out_shape), len(scratch), grid)
        in_specs = list(in_specs) + [ANY] * len(side.ins)
        args = list(args) + side.ins
        out_specs = out_specs + [ANY] * len(side.out_shapes)
        out_shape = out_shape + side.out_shapes
        scratch = list(scratch) + side.sem_shapes
        semantics = ("arbitrary",) * len(grid)
    return pl.pallas_call(
        body, name=name, grid=grid, in_specs=in_specs, out_specs=out_specs, out_shape=out_shape,
        scratch_shapes=scratch, input_output_aliases=aliases,
        compiler_params=_cparams(dimension_semantics=semantics),
    )(*args)


def _matmul(name, mode, a, b, *, out_seg=1, out_dtype, tm, tn, tk, res=None, side=None):
    a3, b3 = _seg3(a), _seg3(b)
    if mode == "nn":
        M, K, N = a3.shape[1], a3.shape[0] * a3.shape[2], b3.shape[0] * b3.shape[2]
        assert b3.shape[1] == K
        a_spec = _seg_spec(a3.shape, tm, tk, _I, _K)
        b_spec = _seg_spec(b3.shape, tk, tn, _K, _J)
        dims = (((1,), (0,)), ((), ()))
    elif mode == "nt":
        M, K, N = a3.shape[1], a3.shape[0] * a3.shape[2], b3.shape[1]
        assert b3.shape[0] * b3.shape[2] == K
        a_spec = _seg_spec(a3.shape, tm, tk, _I, _K)
        b_spec = _seg_spec(b3.shape, tn, tk, _J, _K)
        dims = (((1,), (1,)), ((), ()))
    else:
        K, M, N = a3.shape[1], a3.shape[0] * a3.shape[2], b3.shape[0] * b3.shape[2]
        assert b3.shape[1] == K
        a_spec = _seg_spec(a3.shape, tk, tm, _K, _I)
        b_spec = _seg_spec(b3.shape, tk, tn, _K, _J)
        dims = (((0,), (0,)), ((), ()))
    out3 = (out_seg, M, N // out_seg)
    o_spec = _seg_spec(out3, tm, tn, _I, _J)
    grid = (M // tm, N // tn, K // tk)
    nk = grid[2]
    has_res = res is not None

    def body(*refs):
        a_ref, b_ref = refs[0], refs[1]
        r_ref = refs[2] if has_res else None
        o_ref = refs[3] if has_res else refs[2]
        acc_ref = refs[-1] if nk > 1 else None
        prod = lax.dot_general(a_ref[...], b_ref[...], dims, preferred_element_type=F32)

        def finish(val):
            if has_res:
                val = val + r_ref[...]
            o_ref[...] = val.astype(o_ref.dtype)

        if nk == 1:
            finish(prod)
        else:
            k = pl.program_id(2)

            @pl.when(k == 0)
            def _():
                acc_ref[...] = prod

            @pl.when(k > 0)
            def _():
                acc_ref[...] += prod

            @pl.when(k == nk - 1)
            def _():
                finish(acc_ref[...])

    in_specs = [a_spec, b_spec]
    args = [a3, b3]
    if has_res:
        in_specs.append(pl.BlockSpec((tm, tn), lambda i, j, k: (i, j)))
        args.append(res)
    outs = _ride_call(side, body, name=name, grid=grid, in_specs=in_specs, args=args, out_specs=[o_spec],
                      out_shape=[SDS(out3, out_dtype)], scratch=[pltpu.VMEM((tm, tn), F32)] if nk > 1 else [],
                      semantics=("parallel", "parallel", "arbitrary"))
    return outs if side is not None else outs[0]


def _bd_bwd_w(name, a, dys, w, tk):
    T, HW = a.shape
    H = HW // w
    nw = len(dys)

    def body(*refs):
        a_ref = refs[0]
        d_refs = refs[1:1 + nw]
        o_refs = refs[1 + nw:]
        k = pl.program_id(1)
        av = a_ref[...]
        for l in range(nw):
            p = lax.dot_general(av, d_refs[l][...], (((0,), (0,)), ((), ())), preferred_element_type=F32)

            @pl.when(k == 0)
            def _(p=p, l=l):
                o_refs[l][...] = p

            @pl.when(k > 0)
            def _(p=p, l=l):
                o_refs[l][...] += p

    blk = pl.BlockSpec((tk, w), lambda h, k: (k, h))
    ospec = pl.BlockSpec((None, w, w), lambda h, k: (h, 0, 0))
    return pl.pallas_call(
        body, name=name, grid=(H, T // tk), in_specs=[blk] * (1 + nw), out_specs=[ospec] * nw,
        out_shape=[SDS((H, w, w), F32)] * nw,
        compiler_params=_cparams(dimension_semantics=("parallel", "arbitrary")),
    )(a, *dys)


def _rms_fwd(name, x, g, tt, side=None):
    T, D = x.shape

    def body(x_ref, g_ref, h_ref):
        xv = x_ref[...]
        rstd = lax.rsqrt(jnp.mean(xv * xv, axis=-1, keepdims=True) + EPS)
        h_ref[...] = (xv * rstd * g_ref[...]).astype(BF16)

    row = pl.BlockSpec((tt, D), lambda i: (i, 0))
    vec = pl.BlockSpec((1, D), lambda i: (0, 0))
    outs = _ride_call(side, body, name=name, grid=(T // tt,), in_specs=[row, vec], args=[x, g], out_specs=[row],
                      out_shape=[SDS((T, D), BF16)], scratch=[], semantics=("parallel",))
    return outs if side is not None else outs[0]


def _rms_bwd(name, dh, x, g, dres, tt, want_bf16):
    T, D = x.shape

    def body(dh_ref, x_ref, g_ref, dres_ref, *outs):
        i = pl.program_id(0)
        dx_ref = outs[0]
        dg_ref = outs[-1]
        xv = x_ref[...]
        rstd = lax.rsqrt(jnp.mean(xv * xv, axis=-1, keepdims=True) + EPS)
        xn = xv * rstd
        dhv = dh_ref[...]
        dyg = dhv * g_ref[...]
        dx = dres_ref[...] + rstd * (dyg - xn * jnp.mean(dyg * xn, axis=-1, keepdims=True))
        dx_ref[...] = dx
        if want_bf16:
            outs[1][...] = dx.astype(BF16)

        @pl.when(i == 0)
        def _():
            dg_ref[...] = jnp.zeros_like(dg_ref)

        dg_ref[...] += _colsum(dhv * xn)

    row = pl.BlockSpec((tt, D), lambda i: (i, 0))
    vec = pl.BlockSpec((1, D), lambda i: (0, 0))
    out_shape = [SDS((T, D), F32)] + ([SDS((T, D), BF16)] if want_bf16 else []) + [SDS((1, D), F32)]
    out_specs = [row] + ([row] if want_bf16 else []) + [vec]
    return pl.pallas_call(
        body, name=name, grid=(T // tt,), in_specs=[row, row, vec, row], out_specs=out_specs, out_shape=out_shape,
        compiler_params=_cparams(dimension_semantics=("arbitrary",)),
    )(dh, x, g, dres)


def _final_loss_bwd(name, x2, tgt, g, tt):
    T, D = x2.shape

    def body(x_ref, t_ref, g_ref, dx_ref, dxb_ref, dg_ref, lv_ref):
        i = pl.program_id(0)
        xv = x_ref[...]
        gv = g_ref[...]
        rstd = lax.rsqrt(jnp.mean(xv * xv, axis=-1, keepdims=True) + EPS)
        xn = xv * rstd
        e = xn * gv - t_ref[...]
        dy = e * (1.0 / D)
        dyg = dy * gv
        dx = rstd * (dyg - xn * jnp.mean(dyg * xn, axis=-1, keepdims=True))
        dx_ref[...] = dx
        dxb_ref[...] = dx.astype(BF16)

        @pl.when(i == 0)
        def _():
            dg_ref[...] = jnp.zeros_like(dg_ref)
            lv_ref[...] = jnp.zeros_like(lv_ref)

        dg_ref[...] += _colsum(dy * xn)
        lv_ref[...] += _colsum(e * e)

    row = pl.BlockSpec((tt, D), lambda i: (i, 0))
    vec = pl.BlockSpec((1, D), lambda i: (0, 0))
    return pl.pallas_call(
        body, name=name, grid=(T // tt,), in_specs=[row, row, vec], out_specs=[row, row, vec, vec],
        out_shape=[SDS((T, D), F32), SDS((T, D), BF16), SDS((1, D), F32), SDS((1, D), F32)],
        compiler_params=_cparams(dimension_semantics=("arbitrary",)),
    )(x2, tgt, g)


def _blk(tt, tc, off=0):
    return pl.BlockSpec((tt, tc), lambda j, i: (i, j + off))


def _vec(r, tc, off=0):
    return pl.BlockSpec((r, tc), lambda j, i: (0, j + off))


def _prev_halo(tt, tc, off=0):
    r = tt // HALO
    return pl.BlockSpec((HALO, tc), lambda j, i: (jnp.maximum(i * r - 1, 0), j + off))


def _next_halo(tt, tc, T, off=0):
    r = tt // HALO
    last = T // HALO - 1
    return pl.BlockSpec((HALO, tc), lambda j, i: (jnp.minimum((i + 1) * r, last), j + off))


def _seq_params():
    return _cparams(dimension_semantics=("parallel", "arbitrary"))


def _pool_fwd(name, proj, Cp, gw, tt, w_pool, scale):
    T = proj.shape[0]
    G = Cp // gw
    assert G == 4

    def body(u_ref, halo_ref, w_ref, s_ref, d_ref, z_ref, y_ref):
        j, i = pl.program_id(0), pl.program_id(1)
        u = u_ref[...]
        halo = jnp.where(i > 0, halo_ref[...], 0.0)
        ext = jnp.concatenate([halo, u], axis=0)
        e2 = ext + pltpu.roll(ext, 1, 0)
        e4 = e2 + pltpu.roll(e2, 2, 0)
        e8 = e4 + pltpu.roll(e4, 4, 0)
        e16 = e8 + pltpu.roll(e8, 8, 0)
        s = jnp.where(j == 0, e2, jnp.where(j == 1, e4, jnp.where(j == 2, e8, e16)))[HALO:]
        t1 = (i * tt + 1 + lax.broadcasted_iota(jnp.int32, (tt, 1), 0)).astype(F32)
        win = lax.shift_left(jnp.int32(2), j).astype(F32)
        d = (s * (1.0 / jnp.minimum(t1, win)) - u).astype(BF16)
        d_ref[...] = d
        z = jnp.dot(d, w_ref[...], preferred_element_type=F32)
        z_ref[...] = z
        y_ref[...] = (z * s_ref[...]).astype(BF16)

    return pl.pallas_call(
        body, name=name, grid=(G, T // tt),
        in_specs=[_blk(tt, gw), _prev_halo(tt, gw), pl.BlockSpec((None, gw, gw), lambda j, i: (j, 0, 0)), _vec(1, gw)],
        out_specs=[_blk(tt, gw)] * 3, out_shape=[SDS((T, Cp), BF16), SDS((T, Cp), F32), SDS((T, Cp), BF16)],
        compiler_params=_seq_params(),
    )(proj, proj, w_pool, scale)


def _pool_bwd_a(name, dy, z, scale, w_pool, tt):
    T, Cp = dy.shape
    gw = w_pool.shape[-1]

    def body(dy_ref, z_ref, s_ref, w_ref, dz_ref, dd_ref, ds_ref):
        i = pl.program_id(0)
        dyv = dy_ref[...]
        dz = (dyv * s_ref[...]).astype(BF16)
        dz_ref[...] = dz
        dd_ref[...] = _gate_pre_bwd(dz, w_ref, gw)

        @pl.when(i == 0)
        def _():
            ds_ref[...] = jnp.zeros_like(ds_ref)

        ds_ref[...] += _colsum(dyv * z_ref[...])

    row = pl.BlockSpec((tt, Cp), lambda i: (i, 0))
    vec = pl.BlockSpec((1, Cp), lambda i: (0, 0))
    wspec = pl.BlockSpec(w_pool.shape, lambda i: (0, 0, 0))
    return pl.pallas_call(
        body, name=name, grid=(T // tt,), in_specs=[row, row, vec, wspec], out_specs=[row, row, vec],
        out_shape=[SDS((T, Cp), BF16), SDS((T, Cp), F32), SDS((1, Cp), F32)],
        compiler_params=_cparams(dimension_semantics=("arbitrary",)),
    )(dy, z, scale, w_pool)


def _pool_bwd_b(name, dd, gw, tt):
    T, Cp = dd.shape
    G = Cp // gw
    nT = T // tt

    def body(d_ref, nxt_ref, du_ref):
        j, i = pl.program_id(0), pl.program_id(1)
        win = lax.shift_left(jnp.int32(2), j).astype(F32)
        dv = d_ref[...]
        t1 = (i * tt + 1 + lax.broadcasted_iota(jnp.int32, (tt + HALO, 1), 0)).astype(F32)
        nxt = jnp.where(i < nT - 1, nxt_ref[...], 0.0)
        ext = jnp.concatenate([dv, nxt], axis=0) * (1.0 / jnp.minimum(t1, win))
        n = tt + HALO
        f2 = ext + pltpu.roll(ext, n - 1, 0)
        f4 = f2 + pltpu.roll(f2, n - 2, 0)
        f8 = f4 + pltpu.roll(f4, n - 4, 0)
        f16 = f8 + pltpu.roll(f8, n - 8, 0)
        s = jnp.where(j == 0, f2, jnp.where(j == 1, f4, jnp.where(j == 2, f8, f16)))[:tt]
        du_ref[...] = (s - dv).astype(BF16)

    return pl.pallas_call(
        body, name=name, grid=(G, nT), in_specs=[_blk(tt, gw), _next_halo(tt, gw, T)], out_specs=_blk(tt, gw),
        out_shape=SDS((T, Cp), BF16), compiler_params=_seq_params(),
    )(dd, dd)


def _conv_fwd(name, src, off_cols, w, b, tt, tc, side=None):
    T = src.shape[0]
    K, C = w.shape
    off = off_cols // tc

    def body(u_ref, halo_ref, w_ref, b_ref, v_ref, vb_ref):
        i = pl.program_id(1)
        u = u_ref[...]
        halo = jnp.where(i > 0, halo_ref[...], 0.0)
        ext = jnp.concatenate([halo, u], axis=0)
        wv = w_ref[...]
        acc = b_ref[...] + u * wv[K - 1:K]
        for s in range(1, K):
            acc = acc + _lookback(ext, s) * wv[K - 1 - s:K - s]
        v_ref[...] = acc
        vb_ref[...] = acc.astype(BF16)

    return _ride_call(
        side, body, name=name, grid=(C // tc, T // tt),
        in_specs=[_blk(tt, tc, off), _prev_halo(tt, tc, off), _vec(K, tc), _vec(1, tc)], args=[src, src, w, b],
        out_specs=[_blk(tt, tc), _blk(tt, tc)], out_shape=[SDS((T, C), F32), SDS((T, C), BF16)],
        scratch=[], semantics=("parallel", "arbitrary"))


def _conv_bwd(name, dv, src, off_cols, w, tt, tc):
    T, C = dv.shape
    K = w.shape[0]
    off = off_cols // tc
    nT = T // tt

    def body(dv_ref, nxt_ref, u_ref, halo_ref, w_ref, du_ref, dw_ref, db_ref):
        i = pl.program_id(1)
        d = dv_ref[...]
        nxt = jnp.where(i < nT - 1, nxt_ref[...], 0.0)
        dext = jnp.concatenate([d, nxt], axis=0)
        u = u_ref[...]
        halo = jnp.where(i > 0, halo_ref[...], 0.0)
        uext = jnp.concatenate([halo, u], axis=0)
        wv = w_ref[...]
        du = d * wv[K - 1:K]
        rows = [_colsum(d * u)]
        for s in range(1, K):
            du = du + _lookahead(dext, s, tt) * wv[K - 1 - s:K - s]
            rows.append(_colsum(d * _lookback(uext, s)))
        du_ref[...] = du.astype(BF16)
        dw = jnp.concatenate(rows[::-1] + [jnp.zeros((8 - K, tc), F32)], axis=0)

        @pl.when(i == 0)
        def _():
            dw_ref[...] = jnp.zeros_like(dw_ref)
            db_ref[...] = jnp.zeros_like(db_ref)

        dw_ref[...] += dw
        db_ref[...] += _colsum(d)

    return pl.pallas_call(
        body, name=name, grid=(C // tc, nT),
        in_specs=[_blk(tt, tc), _next_halo(tt, tc, T), _blk(tt, tc, off), _prev_halo(tt, tc, off), _vec(K, tc)],
        out_specs=[_blk(tt, tc), _vec(8, tc), _vec(1, tc)],
        out_shape=[SDS((T, C), BF16), SDS((8, C), F32), SDS((1, C), F32)],
        compiler_params=_seq_params(),
    )(dv, dv, src, src, w)


def _lru_gates(ra, ia, v, ba, bi, lam):
    r = _sigmoid(ra + ba)
    ig = _sigmoid(ia + bi)
    nl = -lam
    ex = jnp.exp(-jnp.abs(nl))
    one_p = 1.0 + ex
    l1p = jnp.where(one_p == 1.0, ex, jnp.log(one_p) * ex / (one_p - 1.0))
    sp = jnp.maximum(nl, 0.0) + l1p
    a = jnp.exp(-LRU_C * r * sp)
    om = 1.0 - a * a
    rs = lax.rsqrt(jnp.maximum(om, 1e-30))
    return r, ig, sp, a, om * rs, rs


def _gate_pre(vv, w_ref, hw):
    vb = vv.astype(BF16)
    return jnp.concatenate([jnp.dot(vb[:, q * hw:(q + 1) * hw], w_ref[q], preferred_element_type=F32)
                            for q in range(vv.shape[1] // hw)], axis=1)


def _gate_pre_bwd(d, w_ref, hw):
    db = d.astype(BF16)
    return jnp.concatenate([lax.dot_general(db[:, q * hw:(q + 1) * hw], w_ref[q], (((1,), (1,)), ((), ())),
                                            preferred_element_type=F32)
                            for q in range(d.shape[1] // hw)], axis=1)


def _lru_fwd(name, w_a, w_i, v, proj, off_cols, ba, bi, lam, tt, tc, side=None):
    T, R = v.shape
    off = off_cols // tc
    hw = w_a.shape[-1]
    wspec = pl.BlockSpec((tc // hw, hw, hw), lambda j, i: (j, 0, 0))

    def body(wa_ref, wi_ref, v_ref, ug_ref, ba_ref, bi_ref, lam_ref, h_ref, y_ref, carry_ref):
        i = pl.program_id(1)

        @pl.when(i == 0)
        def _():
            carry_ref[...] = jnp.zeros_like(carry_ref)

        vv = v_ref[...]
        _, ig, _, a, mult, _ = _lru_gates(_gate_pre(vv, wa_ref, hw), _gate_pre(vv, wi_ref, hw), vv, ba_ref[...],
                                          bi_ref[...], lam_ref[...])
        A = a
        B = mult * (ig * vv)
        row8 = lax.broadcasted_iota(jnp.int32, (tt, 1), 0) % 8
        for s in (1, 2, 4):
            m = row8 >= s
            B = jnp.where(m, A * pltpu.roll(B, s, 0) + B, B)
            A = jnp.where(m, A * pltpu.roll(A, s, 0), A)
        carry = carry_ref[7:8, :]
        groups = []
        for g in range(tt // 8):
            hg = A[g * 8:(g + 1) * 8] * carry + B[g * 8:(g + 1) * 8]
            carry = hg[7:8]
            groups.append(hg)
        h = jnp.concatenate(groups, axis=0)
        h_ref[...] = h
        carry_ref[...] = groups[-1]
        y_ref[...] = (h * _gelu(ug_ref[...])).astype(BF16)

    return _ride_call(
        side, body, name=name, grid=(R // tc, T // tt),
        in_specs=[wspec, wspec, _blk(tt, tc), _blk(tt, tc, off), _vec(1, tc), _vec(1, tc), _vec(1, tc)],
        args=[w_a, w_i, v, proj, ba, bi, lam],
        out_specs=[_blk(tt, tc), _blk(tt, tc)], out_shape=[SDS((T, R), F32), SDS((T, R), BF16)],
        scratch=[pltpu.VMEM((8, tc), F32)], semantics=("parallel", "arbitrary"))


def _lru_bwd(name, dy, hs, proj, off_cols, w_a, w_i, v, ba, bi, lam, tt, tc):
    T, R = v.shape
    off = off_cols // tc
    nT = T // tt
    r16 = tt // HALO
    hw = w_a.shape[-1]
    wspec = pl.BlockSpec((tc // hw, hw, hw), lambda j, i: (j, 0, 0))

    def rblk(o=0):
        return pl.BlockSpec((tt, tc), lambda j, i: (nT - 1 - i, j + o))

    hprev = pl.BlockSpec((HALO, tc), lambda j, i: (jnp.maximum((nT - 1 - i) * r16 - 1, 0), j))

    def body(dy_ref, h_ref, hp_ref, ug_ref, wa_ref, wi_ref, v_ref, ba_ref, bi_ref, lam_ref,
             dra_ref, dia_ref, dv_ref, dug_ref, dba_ref, dbi_ref, dlam_ref, dh_carry, a_carry):
        i = pl.program_id(1)

        @pl.when(i == 0)
        def _():
            dh_carry[...] = jnp.zeros_like(dh_carry)
            a_carry[...] = jnp.zeros_like(a_carry)
            dba_ref[...] = jnp.zeros_like(dba_ref)
            dbi_ref[...] = jnp.zeros_like(dbi_ref)
            dlam_ref[...] = jnp.zeros_like(dlam_ref)

        vv = v_ref[...]
        lamv = lam_ref[...]
        r, ig, sp, a, mult, inv_mult = _lru_gates(_gate_pre(vv, wa_ref, hw), _gate_pre(vv, wi_ref, hw), vv,
                                                  ba_ref[...], bi_ref[...], lamv)
        h = h_ref[...]
        gel, dgel = _gelu_and_grad(ug_ref[...])
        dyv = dy_ref[...]
        dug_ref[...] = (dyv * h * dgel).astype(BF16)
        dhs = dyv * gel

        a_ext = jnp.concatenate([a, jnp.broadcast_to(a_carry[0:1, :], (HALO, tc))], axis=0)
        A = _lookahead(a_ext, 1, tt)
        B = dhs
        row8 = lax.broadcasted_iota(jnp.int32, (tt, 1), 0) % 8
        for s in (1, 2, 4):
            m = row8 < 8 - s
            B = jnp.where(m, B + A * pltpu.roll(B, tt - s, 0), B)
            A = jnp.where(m, A * pltpu.roll(A, tt - s, 0), A)
        carry = dh_carry[0:1, :]
        groups = []
        for g in reversed(range(tt // 8)):
            dg = B[g * 8:(g + 1) * 8] + A[g * 8:(g + 1) * 8] * carry
            carry = dg[0:1]
            groups.append(dg)
        dH = jnp.concatenate(groups[::-1], axis=0)
        dh_carry[...] = groups[-1]
        a_carry[...] = a[:8, :]

        hp = jnp.where(i < nT - 1, hp_ref[...], 0.0)
        h_prev = _lookback(jnp.concatenate([hp, h], axis=0), 1)
        da = dH * h_prev
        dmult = dH * (ig * vv)
        div = dH * mult
        dig = div * vv
        dlog_a = da * a - dmult * (a * a) * inv_mult
        dr = dlog_a * (-LRU_C * sp)
        dra = dr * r * (1.0 - r)
        dia = dig * ig * (1.0 - ig)
        dra_ref[...] = dra.astype(BF16)
        dia_ref[...] = dia.astype(BF16)
        dv_ref[...] = (_gate_pre_bwd(dra, wa_ref, hw) + _gate_pre_bwd(dia, wi_ref, hw)) + div * ig
        dba_ref[...] += _colsum(dra)
        dbi_ref[...] += _colsum(dia)
        dlam_ref[...] += _colsum(dlog_a * (-LRU_C * r))

        @pl.when(i == nT - 1)
        def _():
            dlam_ref[...] = dlam_ref[...] * (-_sigmoid(-lamv))

    return pl.pallas_call(
        body, name=name, grid=(R // tc, nT),
        in_specs=[rblk(), rblk(), hprev, rblk(off), wspec, wspec, rblk(), _vec(1, tc), _vec(1, tc), _vec(1, tc)],
        out_specs=[rblk(), rblk(), rblk(), rblk(), _vec(1, tc), _vec(1, tc), _vec(1, tc)],
        out_shape=[SDS((T, R), BF16), SDS((T, R), BF16), SDS((T, R), F32), SDS((T, R), BF16),
                   SDS((1, R), F32), SDS((1, R), F32), SDS((1, R), F32)],
        scratch_shapes=[pltpu.VMEM((8, tc), F32), pltpu.VMEM((8, tc), F32)], compiler_params=_seq_params(),
    )(dy, hs, hs, proj, w_a, w_i, v, ba, bi, lam)


def _merge_fwd(name, proj, off_cols, b_gate, P, Q, tt, tc, side=None):
    T, D = P.shape
    o0 = off_cols // tc
    o1 = o0 + D // tc

    def body(g0_ref, g1_ref, b0_ref, b1_ref, p_ref, q_ref, m_ref):
        g0 = _sigmoid(g0_ref[...] + b0_ref[...])
        g1 = _sigmoid(g1_ref[...] + b1_ref[...])
        m_ref[...] = (g0 * p_ref[...].astype(F32) + g1 * q_ref[...].astype(F32)).astype(BF16)

    outs = _ride_call(
        side, body, name=name, grid=(D // tc, T // tt),
        in_specs=[_blk(tt, tc, o0), _blk(tt, tc, o1), _vec(1, tc), _vec(1, tc, D // tc), _blk(tt, tc), _blk(tt, tc)],
        args=[proj, proj, b_gate, b_gate, P, Q], out_specs=[_blk(tt, tc)], out_shape=[SDS((T, D), BF16)],
        scratch=[], semantics=("parallel", "arbitrary"))
    return outs if side is not None else outs[0]


def _merge_bwd(name, dm, proj, off_cols, b_gate, P, Q, tt, tc):
    T, D = P.shape
    o0 = off_cols // tc
    o1 = o0 + D // tc

    def body(dm_ref, g0_ref, g1_ref, b0_ref, b1_ref, p_ref, q_ref, dp_ref, dq_ref, dl0_ref, dl1_ref, db0_ref, db1_ref):
        i = pl.program_id(1)
        g0 = _sigmoid(g0_ref[...] + b0_ref[...])
        g1 = _sigmoid(g1_ref[...] + b1_ref[...])
        d = dm_ref[...].astype(F32)
        dp_ref[...] = (d * g0).astype(BF16)
        dq_ref[...] = (d * g1).astype(BF16)
        dl0 = d * p_ref[...].astype(F32) * g0 * (1.0 - g0)
        dl1 = d * q_ref[...].astype(F32) * g1 * (1.0 - g1)
        dl0_ref[...] = dl0.astype(BF16)
        dl1_ref[...] = dl1.astype(BF16)

        @pl.when(i == 0)
        def _():
            db0_ref[...] = jnp.zeros_like(db0_ref)
            db1_ref[...] = jnp.zeros_like(db1_ref)

        db0_ref[...] += _colsum(dl0)
        db1_ref[...] += _colsum(dl1)

    b = _blk(tt, tc)
    return pl.pallas_call(
        body, name=name, grid=(D // tc, T // tt),
        in_specs=[b, _blk(tt, tc, o0), _blk(tt, tc, o1), _vec(1, tc), _vec(1, tc, D // tc), b, b],
        out_specs=[b, b, b, b, _vec(1, tc), _vec(1, tc)],
        out_shape=[SDS((T, D), BF16)] * 4 + [SDS((1, D), F32)] * 2, compiler_params=_seq_params(),
    )(dm, proj, proj, b_gate, b_gate, P, Q)


def _ffn_act_fwd(name, up, w, b, tt, tc):
    T = up.shape[0]
    K, F = w.shape
    nF = F // tc

    def body(gp_ref, halo_ref, val_ref, w_ref, b_ref, z_ref):
        i = pl.program_id(1)
        gp = gp_ref[...].astype(F32)
        halo = jnp.where(i > 0, halo_ref[...].astype(F32), 0.0)
        ext = jnp.concatenate([halo, gp], axis=0)
        wv = w_ref[...]
        c = b_ref[...] + gp * wv[K - 1:K]
        for s in range(1, K):
            c = c + _lookback(ext, s) * wv[K - 1 - s:K - s]
        z_ref[...] = (_gelu(c) * val_ref[...].astype(F32)).astype(BF16)

    return pl.pallas_call(
        body, name=name, grid=(nF, T // tt),
        in_specs=[_blk(tt, tc), _prev_halo(tt, tc), _blk(tt, tc, nF), _vec(K, tc), _vec(1, tc)],
        out_specs=_blk(tt, tc), out_shape=SDS((T, F), BF16), compiler_params=_seq_params(),
    )(up, up, up, w, b)


def _ffn_down_fwd(name, up, w, b, w_down, x1, tm, tk, nsub):
    T = up.shape[0]
    K, F = w.shape
    D = w_down.shape[1]
    nF = F // tk
    sub = tk // nsub
    r16 = tm // HALO

    def body(gp_ref, halo_ref, val_ref, w_ref, b_ref, wd_ref, x1_ref, z_ref, o_ref, acc_ref):
        i, k = pl.program_id(0), pl.program_id(1)
        total = None
        for s in range(nsub):
            cs = pl.ds(s * sub, sub)
            gp = gp_ref[:, cs].astype(F32)
            halo = jnp.where(i > 0, halo_ref[:, cs].astype(F32), 0.0)
            ext = jnp.concatenate([halo, gp], axis=0)
            wv = w_ref[:, cs]
            c = b_ref[:, cs] + gp * wv[K - 1:K]
            for t in range(1, K):
                c = c + _lookback(ext, t) * wv[K - 1 - t:K - t]
            zs = (_gelu(c) * val_ref[:, cs].astype(F32)).astype(BF16)
            z_ref[:, cs] = zs
            p = jnp.dot(zs, wd_ref[cs, :], preferred_element_type=F32)
            total = p if total is None else total + p

        @pl.when(k == 0)
        def _():
            acc_ref[...] = total

        @pl.when(k > 0)
        def _():
            acc_ref[...] += total

        @pl.when(k == nF - 1)
        def _():
            o_ref[...] = acc_ref[...] + x1_ref[...]

    blk = lambda off: pl.BlockSpec((tm, tk), lambda i, k: (i, k + off))
    return pl.pallas_call(
        body, name=name, grid=(T // tm, nF),
        in_specs=[blk(0), pl.BlockSpec((HALO, tk), lambda i, k: (jnp.maximum(i * r16 - 1, 0), k)), blk(nF),
                  pl.BlockSpec((K, tk), lambda i, k: (0, k)), pl.BlockSpec((1, tk), lambda i, k: (0, k)),
                  pl.BlockSpec((tk, D), lambda i, k: (k, 0)), pl.BlockSpec((tm, D), lambda i, k: (i, 0))],
        out_specs=[blk(0), pl.BlockSpec((tm, D), lambda i, k: (i, 0))],
        out_shape=[SDS((T, F), BF16), SDS((T, D), F32)],
        scratch_shapes=[pltpu.VMEM((tm, D), F32)],
        compiler_params=_cparams(dimension_semantics=("parallel", "arbitrary")),
    )(up, up, up, w, b, w_down, x1)


def _ffn_act_bwd(name, up, dz, w, b, tt, tc):
    T = up.shape[0]
    K, F = w.shape
    nF = F // tc
    nT = T // tt

    def body(gp_ref, gph_ref, gpn_ref, val_ref, valn_ref, dz_ref, dzn_ref, w_ref, b_ref, dup_ref, dw_ref, db_ref):
        i = pl.program_id(1)
        wv = w_ref[...]
        gp = gp_ref[...].astype(F32)
        halo = jnp.where(i > 0, gph_ref[...].astype(F32), 0.0)
        full = jnp.concatenate([halo, gp, gpn_ref[...].astype(F32)], axis=0)
        c = b_ref[...] + full * wv[K - 1:K]
        for s in range(1, K):
            c = c + pltpu.roll(full, s, 0) * wv[K - 1 - s:K - s]
        c = c[HALO:]
        gate, dgate = _gelu_and_grad(c)
        dz_e = jnp.concatenate([dz_ref[...], dzn_ref[...]], axis=0).astype(F32)
        val_e = jnp.concatenate([val_ref[...], valn_ref[...]], axis=0).astype(F32)
        dc_e = dz_e * val_e * dgate
        row = lax.broadcasted_iota(jnp.int32, (tt + HALO, 1), 0)
        dc_e = jnp.where((row < tt) | (i < nT - 1), dc_e, 0.0)
        dc = dc_e[:tt]
        dgp = dc * wv[K - 1:K]
        uext = full[:tt + HALO]
        rows = [_colsum(dc * gp)]
        for s in range(1, K):
            dgp = dgp + _lookahead(dc_e, s, tt) * wv[K - 1 - s:K - s]
            rows.append(_colsum(dc * _lookback(uext, s)))
        dup_ref[0] = dgp.astype(BF16)
        dup_ref[1] = (dz_e[:tt] * gate[:tt]).astype(BF16)
        dw = jnp.concatenate(rows[::-1] + [jnp.zeros((8 - K, tc), F32)], axis=0)

        @pl.when(i == 0)
        def _():
            dw_ref[...] = jnp.zeros_like(dw_ref)
            db_ref[...] = jnp.zeros_like(db_ref)

        dw_ref[...] += dw
        db_ref[...] += _colsum(dc)

    return pl.pallas_call(
        body, name=name, grid=(nF, nT),
        in_specs=[_blk(tt, tc), _prev_halo(tt, tc), _next_halo(tt, tc, T), _blk(tt, tc, nF), _next_halo(tt, tc, T, nF),
                  _blk(tt, tc), _next_halo(tt, tc, T), _vec(K, tc), _vec(1, tc)],
        out_specs=[pl.BlockSpec((2, tt, tc), lambda j, i: (0, i, j)), _vec(8, tc), _vec(1, tc)],
        out_shape=[SDS((2, T, F), BF16), SDS((8, F), F32), SDS((1, F), F32)],
        compiler_params=_seq_params(),
    )(up, up, up, up, up, dz, dz, w, b)


def _adamw(name, w, g, m, v):
    R, C = w.shape
    tr = R
    if R * C * 4 > ADAMW_BLOCK_BYTES:
        tr = _pick(R, [t for t in (256, 128, 64, 32, 16, 8) if t * C * 4 <= ADAMW_BLOCK_BYTES])
    c1 = 1.0 - ADAM_B1 ** ADAM_STEP
    c2 = 1.0 - ADAM_B2 ** ADAM_STEP

    def body(w_ref, g_ref, m_ref, v_ref, d_ref, nm_ref, nv_ref):
        gv = g_ref[...]
        nm = ADAM_B1 * m_ref[...] + (1.0 - ADAM_B1) * gv
        nv = ADAM_B2 * v_ref[...] + (1.0 - ADAM_B2) * (gv * gv)
        nm_ref[...] = nm
        nv_ref[...] = nv
        d_ref[...] = -ADAM_LR * ((nm / c1) / (jnp.sqrt(nv / c2) + ADAM_EPS) + ADAM_WD * w_ref[...])

    spec = pl.BlockSpec((tr, C), lambda i: (i, 0))
    return pl.pallas_call(
        body, name=name, grid=(R // tr,), in_specs=[spec] * 4, out_specs=[spec] * 3,
        out_shape=[SDS((R, C), F32)] * 3, compiler_params=_cparams(dimension_semantics=("parallel",)),
    )(w, g, m, v)


def _add_halves(name, part, rcv, pos):
    S, R, C = part.shape
    h = R // 2
    tr = _pick(h, (512, 256, 128, 64, 32, 16))
    nb = h // tr

    def body(pos_ref, p_ref, r_ref, o_ref):
        o_ref[...] = (p_ref[...].astype(F32) + r_ref[...].astype(F32)).astype(BF16)

    return pl.pallas_call(
        body, name=name,
        grid_spec=pltpu.PrefetchScalarGridSpec(
            num_scalar_prefetch=1, grid=(S, nb),
            in_specs=[pl.BlockSpec((None, tr, C), lambda s, r, pos: (s, pos[0] * nb + r, 0)),
                      pl.BlockSpec((None, tr, C), lambda s, r, pos: (s, r, 0))],
            out_specs=pl.BlockSpec((None, tr, C), lambda s, r, pos: (s, r, 0))),
        out_shape=SDS((S, h, C), BF16), compiler_params=_cparams(dimension_semantics=("parallel", "parallel")),
    )(pos, part, rcv)


def _add_chips(name, chipsum, rcv, pos):
    S, h, C = chipsum.shape
    tr = _pick(h, (512, 256, 128, 64, 32, 16))

    def body(pos_ref, own_ref, r_ref, o_ref):
        o_ref[...] = ((own_ref[...].astype(F32) + r_ref[0].astype(F32)) + r_ref[1].astype(F32)) + r_ref[2].astype(F32)

    return pl.pallas_call(
        body, name=name,
        grid_spec=pltpu.PrefetchScalarGridSpec(
            num_scalar_prefetch=1, grid=(h // tr,),
            in_specs=[pl.BlockSpec((None, tr, C), lambda r, pos: (pos[1], r, 0)),
                      pl.BlockSpec((3, tr, C), lambda r, pos: (0, r, 0))],
            out_specs=pl.BlockSpec((None, tr, C), lambda r, pos: (pos[0], r, 0))),
        out_shape=SDS((2, h, C), F32), compiler_params=_cparams(dimension_semantics=("parallel",)),
    )(pos, chipsum, rcv)


def _place():
    x, y, c = lax.axis_index("x"), lax.axis_index("y"), lax.axis_index("c")
    chips = [(1 - x, y), (x, 1 - y), (1 - x, 1 - y)]
    slots = [2 * cx + cy for cx, cy in chips]
    return x, y, c, chips, slots


def _gather_side(shards, split):
    n = len(shards)

    def copies(ins, outs, sems):
        send_sems, recv_sems, local_sems = sems
        x, y, c, chips, slots = _place()
        me = 2 * x + y
        sibling = (x, y, 1 - c)

        def rows(a, half):
            hrows = ins[a].shape[0] // 2
            return pl.ds(half * hrows, hrows)

        def ici(a, j):
            if split[a]:
                src, dst = ins[a].at[rows(a, c)], outs[a].at[me, rows(a, c)]
            else:
                src, dst = ins[a], outs[a].at[me]
            return pltpu.make_async_remote_copy(src_ref=src, dst_ref=dst, send_sem=send_sems.at[a, j],
                                                recv_sem=recv_sems.at[a, j], device_id=(*chips[j], c), device_id_type=MESH)

        def landed(a, j):
            dst = outs[a].at[slots[j], rows(a, c)] if split[a] else outs[a].at[slots[j]]
            return pltpu.make_async_remote_copy(src_ref=dst, dst_ref=dst, send_sem=send_sems.at[a, j],
                                                recv_sem=recv_sems.at[a, j], device_id=(*chips[j], c), device_id_type=MESH)

        def d2d(a, j, half):
            blk = outs[a].at[slots[j], rows(a, half)]
            return pltpu.make_async_remote_copy(src_ref=blk, dst_ref=blk, send_sem=send_sems.at[a, 3 + j],
                                                recv_sem=recv_sems.at[a, 3 + j], device_id=sibling, device_id_type=MESH)

        local = [pltpu.make_async_copy(ins[a], outs[a].at[me], local_sems.at[a]) for a in range(n)]
        return c, ici, landed, d2d, local

    def start(ins, outs, sems):
        c, ici, landed, d2d, local = copies(ins, outs, sems)
        for cp in local:
            cp.start()
        for a in range(n):
            for j in range(3):
                ici(a, j).start()

    def mid(ins, outs, sems):
        c, ici, landed, d2d, local = copies(ins, outs, sems)
        for a in range(n):
            for j in range(3):
                landed(a, j).wait_recv()
                if split[a]:
                    d2d(a, j, c).start()

    def finish(ins, outs, sems):
        c, ici, landed, d2d, local = copies(ins, outs, sems)
        for a in range(n):
            if split[a]:
                for j in range(3):
                    d2d(a, j, 1 - c).wait_recv()
        for a in range(n):
            for j in range(3):
                ici(a, j).wait_send()
                if split[a]:
                    d2d(a, j, c).wait_send()
        for cp in local:
            cp.wait()

    return _Side(shards, [SDS((N_SLOTS,) + s.shape, s.dtype) for s in shards],
                 [pltpu.SemaphoreType.DMA((n, 6)), pltpu.SemaphoreType.DMA((n, 6)), pltpu.SemaphoreType.DMA((n,))],
                 start, mid, finish)


def _swap_partial_halves(name, parts):
    side = _swap_side(parts)
    n = len(parts)

    def body(*refs):
        ins, outs, sems = refs[:n], refs[n:2 * n], refs[2 * n:]
        side.start(ins, outs, sems)
        side.finish(ins, outs, sems)

    return pl.pallas_call(
        body, name=name, in_specs=[ANY] * n, out_specs=[ANY] * n, out_shape=side.out_shapes,
        scratch_shapes=side.sem_shapes, compiler_params=_cparams(has_side_effects=True),
    )(*parts)


def _swap_side(parts):
    n = len(parts)

    def copies(ins, outs, sems):
        send_sems, recv_sems = sems
        x, y, c, _, _ = _place()
        cps = []
        for a in range(n):
            h = ins[a].shape[1] // 2
            cps.append(pltpu.make_async_remote_copy(
                src_ref=ins[a].at[:, pl.ds((1 - c) * h, h)], dst_ref=outs[a], send_sem=send_sems.at[a],
                recv_sem=recv_sems.at[a], device_id=(x, y, 1 - c), device_id_type=MESH))
        return cps

    def start(ins, outs, sems):
        for cp in copies(ins, outs, sems):
            cp.start()

    def finish(ins, outs, sems):
        for cp in copies(ins, outs, sems):
            cp.wait()

    return _Side(parts, [SDS((p.shape[0], p.shape[1] // 2, p.shape[2]), p.dtype) for p in parts],
                 [pltpu.SemaphoreType.DMA((n,)), pltpu.SemaphoreType.DMA((n,))], start, None, finish)


def _exchange_side(sums):
    n = len(sums)

    def copies(ins, outs, sems):
        send_sems, recv_sems = sems
        x, y, c, chips, slots = _place()
        return [pltpu.make_async_remote_copy(
            src_ref=ins[a].at[slots[j]], dst_ref=outs[a].at[j], send_sem=send_sems.at[a, j],
            recv_sem=recv_sems.at[a, j], device_id=(*chips[j], c), device_id_type=MESH)
            for a in range(n) for j in range(3)]

    def start(ins, outs, sems):
        for cp in copies(ins, outs, sems):
            cp.start()

    def finish(ins, outs, sems):
        for cp in copies(ins, outs, sems):
            cp.wait()

    return _Side(sums, [SDS((3,) + s.shape[1:], s.dtype) for s in sums],
                 [pltpu.SemaphoreType.DMA((n, 3)), pltpu.SemaphoreType.DMA((n, 3))], start, None, finish)


def _swap_reduced_halves(totals):
    side = _reduced_swap_side(totals)
    n = len(totals)

    def body(*refs):
        ins, outs, sems = refs[:n], refs[n:2 * n], refs[2 * n:]
        side.start(ins, outs, sems)
        side.finish(ins, outs, sems)

    return pl.pallas_call(
        body, name="swap_reduced_halves", in_specs=[ANY] * n, out_specs=[ANY] * n, out_shape=side.out_shapes,
        input_output_aliases={a: a for a in range(n)}, scratch_shapes=side.sem_shapes,
        compiler_params=_cparams(has_side_effects=True),
    )(*totals)


def _reduced_swap_side(totals):
    n = len(totals)

    def sends(outs, sems):
        send_sems, recv_sems = sems
        x, y, c, _, _ = _place()
        return [pltpu.make_async_remote_copy(src_ref=outs[a].at[c], dst_ref=outs[a].at[c], send_sem=send_sems.at[a],
                                             recv_sem=recv_sems.at[a], device_id=(x, y, 1 - c), device_id_type=MESH)
                for a in range(n)]

    def start(ins, outs, sems):
        for cp in sends(outs, sems):
            cp.start()

    def finish(ins, outs, sems):
        send_sems, recv_sems = sems
        x, y, c, _, _ = _place()
        for a in range(n):
            got = outs[a].at[1 - c]
            pltpu.make_async_remote_copy(src_ref=got, dst_ref=got, send_sem=send_sems.at[a], recv_sem=recv_sems.at[a],
                                         device_id=(x, y, 1 - c), device_id_type=MESH).wait_recv()
        for cp in sends(outs, sems):
            cp.wait_send()

    return _Side(totals, [SDS(t.shape, t.dtype) for t in totals],
                 [pltpu.SemaphoreType.DMA((n,)), pltpu.SemaphoreType.DMA((n,))], start, None, finish, in_place=True)


def _allreduce_small(pack):
    rows = pack.shape[0]

    def body(in_ref, out_ref, buf, send_sems, recv_sems):
        x, y, c = lax.axis_index("x"), lax.axis_index("y"), lax.axis_index("c")
        me = 4 * x + 2 * y + c
        buf[me] = in_ref[...]
        cps = []
        for k in range(1, 8):
            kx, ky, kc = (k >> 2) & 1, (k >> 1) & 1, k & 1
            peer = (x ^ kx, y ^ ky, c ^ kc)
            cps.append(pltpu.make_async_remote_copy(src_ref=in_ref, dst_ref=buf.at[me], send_sem=send_sems.at[k - 1],
                                                    recv_sem=recv_sems.at[k - 1], device_id=peer, device_id_type=MESH))
        for cp in cps:
            cp.start()
        for k in range(1, 8):
            got = buf.at[me ^ k]
            pltpu.make_async_remote_copy(src_ref=got, dst_ref=got, send_sem=send_sems.at[k - 1],
                                         recv_sem=recv_sems.at[k - 1], device_id=(x, y, c), device_id_type=MESH).wait_recv()
        for cp in cps:
            cp.wait_send()
        acc = buf[0]
        for d in range(1, 8):
            acc = acc + buf[d]
        out_ref[...] = acc

    vm = pl.BlockSpec(memory_space=pltpu.VMEM)
    return pl.pallas_call(
        body, name="allreduce_small", in_specs=[vm], out_specs=vm, out_shape=SDS(pack.shape, F32),
        scratch_shapes=[pltpu.VMEM((8, rows, LANES), F32), pltpu.SemaphoreType.DMA((7,)), pltpu.SemaphoreType.DMA((7,))],
        compiler_params=_cparams(has_side_effects=True),
    )(pack)


def _rows2d(a):
    return a.reshape((-1, a.shape[-1]))


def _from_slots_blockdiag(g, H, w):
    q = w // N_SLOTS
    return g.reshape(N_SLOTS, H, q, w).transpose(1, 0, 2, 3).reshape(H, w, w)


def _to_slots_blockdiag(d, H, w):
    q = w // N_SLOTS
    return d.reshape(H, N_SLOTS, q, w).transpose(1, 0, 2, 3).reshape(N_SLOTS, H * q, w)


def kernel(x, g_mix, w_in, b_gate, w_pool, pool_scale, lru_conv_w, lru_conv_b, w_a, b_a, w_i, b_i, lru_lambda, w_pool_proj, w_lru_proj, w_out, g_mlp, w_up, ffn_conv_w, ffn_conv_b, w_down, g_final, loss_target, m_g_mix, m_w_in, m_b_gate, m_w_pool, m_pool_scale, m_lru_conv_w, m_lru_conv_b, m_w_a, m_b_a, m_w_i, m_b_i, m_lru_lambda, m_w_pool_proj, m_w_lru_proj, m_w_out, m_g_mlp, m_w_up, m_ffn_conv_w, m_ffn_conv_b, m_w_down, m_g_final, v_g_mix, v_w_in, v_b_gate, v_w_pool, v_pool_scale, v_lru_conv_w, v_lru_conv_b, v_w_a, v_b_a, v_w_i, v_b_i, v_lru_lambda, v_w_pool_proj, v_w_lru_proj, v_w_out, v_g_mlp, v_w_up, v_ffn_conv_w, v_ffn_conv_b, v_w_down, v_g_final):
    weights = dict(g_mix=g_mix, w_in=w_in, b_gate=b_gate, w_pool=w_pool, pool_scale=pool_scale, lru_conv_w=lru_conv_w,
                   lru_conv_b=lru_conv_b, w_a=w_a, b_a=b_a, w_i=w_i, b_i=b_i, lru_lambda=lru_lambda,
                   w_pool_proj=w_pool_proj, w_lru_proj=w_lru_proj, w_out=w_out, g_mlp=g_mlp, w_up=w_up,
                   ffn_conv_w=ffn_conv_w, ffn_conv_b=ffn_conv_b, w_down=w_down, g_final=g_final)
    mom_m = dict(g_mix=m_g_mix, w_in=m_w_in, b_gate=m_b_gate, w_pool=m_w_pool, pool_scale=m_pool_scale,
                 lru_conv_w=m_lru_conv_w, lru_conv_b=m_lru_conv_b, w_a=m_w_a, b_a=m_b_a, w_i=m_w_i, b_i=m_b_i,
                 lru_lambda=m_lru_lambda, w_pool_proj=m_w_pool_proj, w_lru_proj=m_w_lru_proj, w_out=m_w_out,
                 g_mlp=m_g_mlp, w_up=m_w_up, ffn_conv_w=m_ffn_conv_w, ffn_conv_b=m_ffn_conv_b, w_down=m_w_down,
                 g_final=m_g_final)
    mom_v = dict(g_mix=v_g_mix, w_in=v_w_in, b_gate=v_b_gate, w_pool=v_w_pool, pool_scale=v_pool_scale,
                 lru_conv_w=v_lru_conv_w, lru_conv_b=v_lru_conv_b, w_a=v_w_a, b_a=v_b_a, w_i=v_w_i, b_i=v_b_i,
                 lru_lambda=v_lru_lambda, w_pool_proj=v_w_pool_proj, w_lru_proj=v_w_lru_proj, w_out=v_w_out,
                 g_mlp=v_g_mlp, w_up=v_w_up, ffn_conv_w=v_ffn_conv_w, ffn_conv_b=v_ffn_conv_b, w_down=v_w_down,
                 g_final=v_g_final)
    order = list(weights)

    T, D = x.shape[1], x.shape[2]
    Cp = pool_scale.shape[1]
    G, gw = w_pool.shape[1], w_pool.shape[3]
    H, hw = w_a.shape[1], w_a.shape[3]
    R = b_a.shape[1]
    F = ffn_conv_b.shape[1]
    KC, KF = lru_conv_w.shape[1], ffn_conv_w.shape[1]
    p0, p1, p2 = Cp, Cp + R, Cp + 2 * R
    xs = x.reshape(T, D)
    tgt = loss_target.reshape(T, D)

    my_x, my_y, my_c = lax.axis_index("x"), lax.axis_index("y"), lax.axis_index("c")
    slot = 2 * my_x + my_y
    pos = jnp.stack([my_c, slot]).astype(jnp.int32)

    big = ["w_in", "w_pool", "w_a", "w_i", "w_pool_proj", "w_lru_proj", "w_out", "w_up", "w_down"]
    first = big[:4]
    shard16 = {n: _rows2d(weights[n][0]).astype(BF16) for n in big}
    first_side = _gather_side([shard16[n] for n in first] + [lru_conv_w[0], ffn_conv_w[0]], [True] * 4 + [False, False])
    n_in, n_up, n_pp = w_in.shape[2], w_up.shape[2], w_pool_proj.shape[2]

    tt_row = _pick(T, (256, 128))
    tt_seq = _pick(T, (512, 256, 128))
    tt_pool = _pick(T, (1024, 512, 256, 128))
    tc_seq = _pick(R, (512, 256, 128))
    tm1 = _pick(T, (1024, 512, 256))
    tm2 = _pick(T, (2048, 1024, 512, 256))
    tkT = _pick(T, (1024, 512, 256))
    tkT2 = _pick(T, (4096, 2048, 1024, 512, 256))
    tn_in = _pick(n_in, (768, 1152, 384, 128))
    tn_up = _pick(n_up, (1536, 1024, 768, 512, 128))
    tk_up = _pick(n_up, (1024, 768, 512, 128))
    tn_pp = _pick(n_pp, (512, 256, 128))
    tD1 = _pick(D, (1024, 512))
    tD2 = _pick(D, (2048, 1024, 512))
    tF = _pick(F, (1536, 1024, 512))
    tkF = _pick(F, (2048, 1536, 1024, 512))

    h, *gathered = _rms_fwd("rms1_fwd", xs, g_mix, tt_row, side=first_side)
    gw_ = dict(zip(first + ["lru_conv_w", "ffn_conv_w"], gathered))
    W_in = gw_["w_in"]
    W_pool = _from_slots_blockdiag(gw_["w_pool"], G, gw)
    W_a = _from_slots_blockdiag(gw_["w_a"], H, hw)
    W_i = _from_slots_blockdiag(gw_["w_i"], H, hw)
    cw_lru = gw_["lru_conv_w"].transpose(1, 0, 2).reshape(KC, R)
    cw_ffn = gw_["ffn_conv_w"].transpose(1, 0, 2).reshape(KF, F)
    on_proj = ["w_pool_proj", "w_lru_proj", "w_out", "w_down"]
    proj3, *late = _matmul("proj_fwd", "nn", h, W_in, out_dtype=F32, tm=tm2, tn=tn_in, tk=D,
                           side=_gather_side([shard16[n] for n in on_proj], [True] * len(on_proj)))
    proj = proj3[0]
    gw_.update(zip(on_proj, late))
    W_pp = gw_["w_pool_proj"]
    W_lp = gw_["w_lru_proj"].reshape(R, D)
    W_out = gw_["w_out"].reshape(D, D)
    W_down = gw_["w_down"].reshape(F, D)
    up16, eD = shard16["w_up"], D // 8
    d_pool, z_pool, y_pool = _pool_fwd("pool_fwd", proj, Cp, gw, tt_pool, W_pool, pool_scale)
    v_f, v_b, up_a = _conv_fwd("lru_conv_fwd", proj, p0, cw_lru, lru_conv_b, tt_seq, tc_seq,
                               side=_gather_side([up16[:eD]], [True]))
    hs, y_lru, up_b = _lru_fwd("lru_scan_fwd", W_a, W_i, v_f, proj, p1, b_a, b_i, lru_lambda, tt_seq, tc_seq,
                               side=_gather_side([up16[eD:4 * eD]], [True]))
    P = _matmul("pool_proj_fwd", "nn", y_pool, W_pp, out_dtype=BF16, tm=tm2, tn=tn_pp, tk=Cp)[0]
    Q3, up_c = _matmul("lru_proj_fwd", "nn", y_lru, W_lp, out_dtype=BF16, tm=tm1, tn=tD1, tk=R,
                       side=_gather_side([up16[4 * eD:5 * eD]], [True]))
    Q = Q3[0]
    merged, up_d = _merge_fwd("merge_fwd", proj, p2, b_gate, P, Q, tt_seq, tc_seq,
                              side=_gather_side([up16[5 * eD:6 * eD]], [True]))
    x13, up_e = _matmul("out_fwd", "nn", merged, W_out, out_dtype=F32, tm=tm1, tn=tD1, tk=D, res=xs,
                        side=_gather_side([up16[6 * eD:]], [True]))
    x1 = x13[0]
    W_up = jnp.concatenate([up_a, up_b, up_c, up_d, up_e], axis=1)
    h2 = _rms_fwd("rms2_fwd", x1, g_mlp, tt_row)
    up = _matmul("up_fwd", "nn", h2, W_up, out_dtype=BF16, tm=tm1, tn=tn_up, tk=D)[0]
    z, x2 = _ffn_down_fwd("ffn_down_fwd", up, cw_ffn, ffn_conv_b, W_down, x1, _pick(T, (512, 256)),
                          _pick(F, (1024, 512)), 4)

    dx2, dx2b, d_g_final, lossvec = _final_loss_bwd("final_loss_bwd", x2, tgt, g_final.reshape(1, D), tt_row)
    dz = _matmul("down_bwd_x", "nt", dx2b, W_down, out_dtype=BF16, tm=tm1, tn=tF, tk=D)[0]
    dW_down = _matmul("down_bwd_w", "tn", z, dx2b, out_dtype=BF16, tm=_pick(F, (1024, 512)), tn=tD1, tk=tkT2)
    dup, d_cw_ffn, d_ffn_b = _ffn_act_bwd("ffn_act_bwd", up, dz, cw_ffn, ffn_conv_b, tt_seq, tc_seq)
    dh2 = _matmul("up_bwd_x", "nt", dup, W_up, out_dtype=F32, tm=tm1, tn=tD1, tk=n_up)[0]
    dW_up = _matmul("up_bwd_w", "tn", h2, dup, out_seg=N_SLOTS, out_dtype=BF16, tm=tD1, tn=tk_up, tk=tkT2)
    dx1, dx1b, d_g_mlp = _rms_bwd("rms2_bwd", dh2, x1, g_mlp, dx2, tt_row, True)
    dmerged = _matmul("out_bwd_x", "nt", dx1b, W_out, out_dtype=BF16, tm=tm1, tn=tD2, tk=D)[0]
    q_rows = lambda a, rows: a.reshape(N_SLOTS, rows // N_SLOTS, a.shape[-1])
    ffn_parts = [dW_up, q_rows(dW_down, F)]
    dW_out, *ffn_sib = _matmul("out_bwd_w", "tn", merged, dx1b, out_dtype=BF16, tm=tD1, tn=tD1, tk=tkT2,
                               side=_swap_side(ffn_parts))
    ffn_sums = [_add_halves(f"add_halves_{n}", p, r, pos) for n, p, r in zip(big[7:], ffn_parts, ffn_sib)]
    dP, dQ, dgl0, dgl1, d_bg0, d_bg1 = _merge_bwd("merge_bwd", dmerged, proj, p2, b_gate, P, Q, tt_seq, tc_seq)
    dy_pool = _matmul("pool_proj_bwd_x", "nt", dP, W_pp, out_dtype=F32, tm=tm2, tn=Cp, tk=tn_pp)[0]
    dW_pp = _matmul("pool_proj_bwd_w", "tn", y_pool, dP, out_seg=N_SLOTS, out_dtype=BF16, tm=Cp, tn=tn_pp,
                    tk=_pick(T, (2048, 1024, 512, 256)))
    dy_lru = _matmul("lru_proj_bwd_x", "nt", dQ, W_lp, out_dtype=F32, tm=tm1, tn=tD1, tk=D)[0]
    dW_lp = _matmul("lru_proj_bwd_w", "tn", y_lru, dQ, out_dtype=BF16, tm=_pick(R, (1024, 512)), tn=tD1, tk=tkT2)
    dra, dia, dv, du_gelu, d_b_a, d_b_i, d_lam = _lru_bwd(
        "lru_scan_bwd", dy_lru, hs, proj, p1, W_a, W_i, v_f, b_a, b_i, lru_lambda, tt_seq, tc_seq)
    dW_a, dW_i = _bd_bwd_w("lru_gate_bwd_w", v_b, [dra, dia], hw, tkT)
    du_lru, d_cw_lru, d_lru_b = _conv_bwd("lru_conv_bwd", dv, proj, p0, cw_lru, tt_seq, tc_seq)
    dzp, dd, d_pool_scale = _pool_bwd_a("pool_scale_bwd", dy_pool, z_pool, pool_scale, W_pool, tt_row)
    (dW_pool,) = _bd_bwd_w("pool_mix_bwd_w", d_pool, [dzp], gw, tkT)
    du_pool = _pool_bwd_b("pool_bwd", dd, gw, tt_pool)
    dproj = jnp.concatenate([du_pool, du_lru, du_gelu, dgl0, dgl1], axis=1)

    mix_parts = [_to_slots_blockdiag(dW_pool.astype(BF16), G, gw),
                 _to_slots_blockdiag(dW_a.astype(BF16), H, hw),
                 _to_slots_blockdiag(dW_i.astype(BF16), H, hw),
                 dW_pp, q_rows(dW_lp, R), q_rows(dW_out, D)]
    dW_in, *riders = _matmul("proj_bwd_w", "tn", h, dproj, out_seg=N_SLOTS, out_dtype=BF16, tm=tD1, tn=tn_in,
                             tk=tkT2, side=_both(_exchange_side(ffn_sums), _swap_side(mix_parts)))
    ffn_chips, mix_sib = riders[:2], riders[2:]
    mix_sums = [_add_halves(f"add_halves_{n}", p, r, pos) for n, p, r in zip(big[1:7], mix_parts, mix_sib)]
    (in_sib,) = _swap_partial_halves("swap_partial_halves_w_in", [dW_in])
    in_sum = _add_halves("add_halves_w_in", dW_in, in_sib, pos)
    ffn_totals = [_add_chips(f"add_chips_{n}", s, r, pos) for n, s, r in zip(big[7:], ffn_sums, ffn_chips)]
    dh3, *rest = _matmul("proj_bwd_x", "nt", dproj, W_in, out_dtype=F32, tm=tm1, tn=tD1, tk=n_in,
                         side=_both(_exchange_side([in_sum] + mix_sums), _reduced_swap_side(ffn_totals)))
    chips, ffn_full = rest[:7], rest[7:]
    grad_x, d_g_mix = _rms_bwd("rms1_bwd", dh3[0], xs, g_mix, dx1, tt_row, False)
    totals = [_add_chips(f"add_chips_{n}", s, r, pos) for n, s, r in zip(big[:7], [in_sum] + mix_sums, chips)]
    full = list(_swap_reduced_halves(totals)) + list(ffn_full)
    grads = {n: f.reshape(weights[n].shape) for n, f in zip(big, full)}

    small = ["g_mix", "b_gate", "pool_scale", "lru_conv_b", "b_a", "b_i", "lru_lambda", "g_mlp", "ffn_conv_b", "g_final"]
    small_g = [d_g_mix, jnp.concatenate([d_bg0, d_bg1], axis=1), d_pool_scale, d_lru_b, d_b_a, d_b_i, d_lam, d_g_mlp,
               d_ffn_b, d_g_final]
    pieces = [g.reshape(-1) for g in small_g] + [d_cw_lru[:KC].reshape(-1), d_cw_ffn[:KF].reshape(-1), lossvec.reshape(-1)]
    sizes = [p.shape[0] for p in pieces]
    total = sum(sizes)
    rows = -(-total // (8 * LANES)) * 8
    pack = jnp.concatenate(pieces + [jnp.zeros((rows * LANES - total,), F32)]).reshape(rows, LANES)
    summed = _allreduce_small(pack).reshape(-1)
    offs = [0]
    for s in sizes:
        offs.append(offs[-1] + s)
    for k, n in enumerate(small):
        grads[n] = summed[offs[k]:offs[k + 1]].reshape(weights[n].shape)
    ns = len(small)
    g_cw_lru = summed[offs[ns]:offs[ns + 1]].reshape(KC, R)
    g_cw_ffn = summed[offs[ns + 1]:offs[ns + 2]].reshape(KF, F)
    grads["lru_conv_w"] = lax.dynamic_slice_in_dim(g_cw_lru, slot * (R // N_SLOTS), R // N_SLOTS, axis=1)[None]
    grads["ffn_conv_w"] = lax.dynamic_slice_in_dim(g_cw_ffn, slot * (F // N_SLOTS), F // N_SLOTS, axis=1)[None]
    loss = jnp.sum(summed[offs[ns + 2]:offs[ns + 3]]) * (0.5 / D)

    delta, new_m, new_v = {}, {}, {}
    for n in big:
        d_, m_, v_ = _adamw(f"adamw_{n}", _rows2d(weights[n][0]), _rows2d(grads[n][0]), _rows2d(mom_m[n][0]),
                            _rows2d(mom_v[n][0]))
        delta[n], new_m[n], new_v[n] = (t.reshape(weights[n].shape) for t in (d_, m_, v_))
    rest = small + ["lru_conv_w", "ffn_conv_w"]
    rsizes = [weights[n].size for n in rest]
    rtotal = sum(rsizes)
    rrows = -(-rtotal // (8 * LANES)) * 8

    def packed(tree):
        flat = [tree[n].reshape(-1) for n in rest] + [jnp.zeros((rrows * LANES - rtotal,), F32)]
        return jnp.concatenate(flat).reshape(rrows, LANES)

    d_, m_, v_ = _adamw("adamw_small", packed(weights), packed(grads), packed(mom_m), packed(mom_v))
    o = 0
    for n, s in zip(rest, rsizes):
        for tree, flat in ((delta, d_), (new_m, m_), (new_v, v_)):
            tree[n] = flat.reshape(-1)[o:o + s].reshape(weights[n].shape)
        o += s

    return (loss, grad_x.reshape(x.shape), *[grads[n] for n in order], *[delta[n] for n in order],
            *[new_m[n] for n in order], *[new_v[n] for n in order])
```

```python
import functools

import jax
import jax.numpy as jnp
from jax import lax
from jax.experimental import pallas as pl
from jax.experimental.pallas import tpu as pltpu

F32 = jnp.float32
BF16 = jnp.bfloat16
MESH = pl.DeviceIdType.MESH

VMEM_LIMIT_V7X = 56 * 1024 * 1024
LANES = 128
HALO = 16
N_SLOTS = 4
ADAMW_BLOCK_BYTES = 3 * 512 * 1024

EPS = 1e-6
LRU_C = 8.0
ADAM_LR = 0.001
ADAM_B1 = 0.9
ADAM_B2 = 0.999
ADAM_EPS = 1e-08
ADAM_WD = 0.01
ADAM_STEP = 10
GELU_C = 0.7978845608028654
GELU_A = 0.044715

SDS = jax.ShapeDtypeStruct
ANY = pl.BlockSpec(memory_space=pl.ANY)


def _pick(n, cands):
    for c in cands:
        if c <= n and n % c == 0:
            return c
    raise ValueError(f"no tile for {n} in {cands}")


def _cparams(**kw):
    return pltpu.CompilerParams(vmem_limit_bytes=VMEM_LIMIT_V7X, **kw)


def _sigmoid(x):
    return 0.5 * jnp.tanh(0.5 * x) + 0.5


def _gelu(x):
    th = jnp.tanh(x * (GELU_C + (GELU_C * GELU_A) * (x * x)))
    hx = 0.5 * x
    return hx + hx * th


def _gelu_and_grad(x):
    x2 = x * x
    th = jnp.tanh(x * (GELU_C + (GELU_C * GELU_A) * x2))
    hx = 0.5 * x
    g = hx + hx * th
    dg = (0.5 + 0.5 * th) + (hx * (1.0 - th * th)) * (GELU_C + (3.0 * GELU_C * GELU_A) * x2)
    return g, dg


def _lookback(ext, k):
    return pltpu.roll(ext, k, 0)[HALO:]


def _lookahead(ext, k, tt):
    return pltpu.roll(ext, ext.shape[0] - k, 0)[:tt]


def _colsum(v):
    return jnp.sum(v, axis=0, keepdims=True)


def _seg3(a):
    return a if a.ndim == 3 else a.reshape((1,) + a.shape)


def _seg_spec(shape3, br, bc, rowf, colf):
    assert shape3[1] % br == 0 and shape3[2] % bc == 0, (shape3, br, bc)
    nb = shape3[2] // bc

    def imap(i, j, k):
        cb = colf(i, j, k)
        return (cb // nb, rowf(i, j, k), cb % nb)

    return pl.BlockSpec((None, br, bc), imap)


_I = lambda i, j, k: i
_J = lambda i, j, k: j
_K = lambda i, j, k: k


class _Side:
    def __init__(self, ins, out_shapes, sem_shapes, start, mid, finish, in_place=False):
        self.ins, self.out_shapes, self.sem_shapes = list(ins), list(out_shapes), list(sem_shapes)
        self.start, self.mid, self.finish = start, mid, finish
        self.alias = [(a, a) for a in range(len(self.ins))] if in_place else []


def _both(s1, s2):
    ni, no, ns = len(s1.ins), len(s1.out_shapes), len(s1.sem_shapes)

    def phase(f1, f2):
        if f1 is None and f2 is None:
            return None

        def run(ins, outs, sems):
            if f1 is not None:
                f1(ins[:ni], outs[:no], sems[:ns])
            if f2 is not None:
                f2(ins[ni:], outs[no:], sems[ns:])

        return run

    both = _Side(s1.ins + s2.ins, s1.out_shapes + s2.out_shapes, s1.sem_shapes + s2.sem_shapes,
                 phase(s1.start, s2.start), phase(s1.mid, s2.mid), phase(s1.finish, s2.finish))
    both.alias = s1.alias + [(ni + a, no + b) for a, b in s2.alias]
    return both


def _ride(side, body, n_in, n_out, n_scr, grid):
    n_sin, n_sout = len(side.ins), len(side.out_shapes)
    n_steps = 1
    for g in grid:
        n_steps *= g

    def wrapped(*refs):
        p = 0
        ins, s_in = refs[p:p + n_in], refs[p + n_in:p + n_in + n_sin]
        p += n_in + n_sin
        outs, s_out = refs[p:p + n_out], refs[p + n_out:p + n_out + n_sout]
        p += n_out + n_sout
        scr, s_sem = refs[p:p + n_scr], refs[p + n_scr:]
        step = 0
        for d, g in enumerate(grid):
            step = step * g + pl.program_id(d)

        @pl.when(step == 0)
        def _():
            side.start(s_in, s_out, s_sem)

        body(*ins, *outs, *scr)

        if side.mid is not None:
            @pl.when(step == (2 * n_steps) // 3)
            def _():
                side.mid(s_in, s_out, s_sem)

        @pl.when(step == n_steps - 1)
        def _():
            side.finish(s_in, s_out, s_sem)

    return wrapped


def _ride_call(side, body, *, name, grid, in_specs, args, out_specs, out_shape, scratch, semantics):
    out_specs, out_shape = list(out_specs), list(out_shape)
    aliases = {}
    if side is not None:
        aliases = {len(args) + a: len(out_shape) + b for a, b in side.alias}
        body = _ride(side, body, len(args), len(out_shape), len(scratch), grid)
        in_specs = list(in_specs) + [ANY] * len(side.ins)
        args = list(args) + side.ins
        out_specs = out_specs + [ANY] * len(side.out_shapes)
        out_shape = out_shape + side.out_shapes
        scratch = list(scratch) + side.sem_shapes
        semantics = ("arbitrary",) * len(grid)
    return pl.pallas_call(
        body, name=name, grid=grid, in_specs=in_specs, out_specs=out_specs, out_shape=out_shape,
        scratch_shapes=scratch, input_output_aliases=aliases,
        compiler_params=_cparams(dimension_semantics=semantics),
    )(*args)


def _matmul(name, mode, a, b, *, out_seg=1, out_dtype, tm, tn, tk, res=None, side=None):
    a3, b3 = _seg3(a), _seg3(b)
    if mode == "nn":
        M, K, N = a3.shape[1], a3.shape[0] * a3.shape[2], b3.shape[0] * b3.shape[2]
        assert b3.shape[1] == K
        a_spec = _seg_spec(a3.shape, tm, tk, _I, _K)
        b_spec = _seg_spec(b3.shape, tk, tn, _K, _J)
        dims = (((1,), (0,)), ((), ()))
    elif mode == "nt":
        M, K, N = a3.shape[1], a3.shape[0] * a3.shape[2], b3.shape[1]
        assert b3.shape[0] * b3.shape[2] == K
        a_spec = _seg_spec(a3.shape, tm, tk, _I, _K)
        b_spec = _seg_spec(b3.shape, tn, tk, _J, _K)
        dims = (((1,), (1,)), ((), ()))
    else:
        K, M, N = a3.shape[1], a3.shape[0] * a3.shape[2], b3.shape[0] * b3.shape[2]
        assert b3.shape[1] == K
        a_spec = _seg_spec(a3.shape, tk, tm, _K, _I)
        b_spec = _seg_spec(b3.shape, tk, tn, _K, _J)
        dims = (((0,), (0,)), ((), ()))
    out3 = (out_seg, M, N // out_seg)
    o_spec = _seg_spec(out3, tm, tn, _I, _J)
    grid = (M // tm, N // tn, K // tk)
    nk = grid[2]
    has_res = res is not None

    def body(*refs):
        a_ref, b_ref = refs[0], refs[1]
        r_ref = refs[2] if has_res else None
        o_ref = refs[3] if has_res else refs[2]
        acc_ref = refs[-1] if nk > 1 else None
        prod = lax.dot_general(a_ref[...], b_ref[...], dims, preferred_element_type=F32)

        def finish(val):
            if has_res:
                val = val + r_ref[...]
            o_ref[...] = val.astype(o_ref.dtype)

        if nk == 1:
            finish(prod)
        else:
            k = pl.program_id(2)

            @pl.when(k == 0)
            def _():
                acc_ref[...] = prod

            @pl.when(k > 0)
            def _():
                acc_ref[...] += prod

            @pl.when(k == nk - 1)
            def _():
                finish(acc_ref[...])

    in_specs = [a_spec, b_spec]
    args = [a3, b3]
    if has_res:
        in_specs.append(pl.BlockSpec((tm, tn), lambda i, j, k: (i, j)))
        args.append(res)
    outs = _ride_call(side, body, name=name, grid=grid, in_specs=in_specs, args=args, out_specs=[o_spec],
                      out_shape=[SDS(out3, out_dtype)], scratch=[pltpu.VMEM((tm, tn), F32)] if nk > 1 else [],
                      semantics=("parallel", "parallel", "arbitrary"))
    return outs if side is not None else outs[0]


def _bd_bwd_w(name, a, dys, w, tk):
    T, HW = a.shape
    H = HW // w
    nw = len(dys)

    def body(*refs):
        a_ref = refs[0]
        d_refs = refs[1:1 + nw]
        o_refs = refs[1 + nw:]
        k = pl.program_id(1)
        av = a_ref[...]
        for l in range(nw):
            p = lax.dot_general(av, d_refs[l][...], (((0,), (0,)), ((), ())), preferred_element_type=F32)

            @pl.when(k == 0)
            def _(p=p, l=l):
                o_refs[l][...] = p

            @pl.when(k > 0)
            def _(p=p, l=l):
                o_refs[l][...] += p

    blk = pl.BlockSpec((tk, w), lambda h, k: (k, h))
    ospec = pl.BlockSpec((None, w, w), lambda h, k: (h, 0, 0))
    return pl.pallas_call(
        body, name=name, grid=(H, T // tk), in_specs=[blk] * (1 + nw), out_specs=[ospec] * nw,
        out_shape=[SDS((H, w, w), F32)] * nw,
        compiler_params=_cparams(dimension_semantics=("parallel", "arbitrary")),
    )(a, *dys)


def _rms_fwd(name, x, g, tt, side=None):
    T, D = x.shape

    def body(x_ref, g_ref, h_ref):
        xv = x_ref[...]
        rstd = lax.rsqrt(jnp.mean(xv * xv, axis=-1, keepdims=True) + EPS)
        h_ref[...] = (xv * rstd * g_ref[...]).astype(BF16)

    row = pl.BlockSpec((tt, D), lambda i: (i, 0))
    vec = pl.BlockSpec((1, D), lambda i: (0, 0))
    outs = _ride_call(side, body, name=name, grid=(T // tt,), in_specs=[row, vec], args=[x, g], out_specs=[row],
                      out_shape=[SDS((T, D), BF16)], scratch=[], semantics=("parallel",))
    return outs if side is not None else outs[0]


def _rms_bwd(name, dh, x, g, dres, tt, want_bf16, side=None):
    T, D = x.shape

    def body(dh_ref, x_ref, g_ref, dres_ref, *outs):
        i = pl.program_id(0)
        dx_ref = outs[0]
        dg_ref = outs[-1]
        xv = x_ref[...]
        rstd = lax.rsqrt(jnp.mean(xv * xv, axis=-1, keepdims=True) + EPS)
        xn = xv * rstd
        dhv = dh_ref[...]
        dyg = dhv * g_ref[...]
        dx = dres_ref[...] + rstd * (dyg - xn * jnp.mean(dyg * xn, axis=-1, keepdims=True))
        dx_ref[...] = dx
        if want_bf16:
            outs[1][...] = dx.astype(BF16)

        @pl.when(i == 0)
        def _():
            dg_ref[...] = jnp.zeros_like(dg_ref)

        dg_ref[...] += _colsum(dhv * xn)

    row = pl.BlockSpec((tt, D), lambda i: (i, 0))
    vec = pl.BlockSpec((1, D), lambda i: (0, 0))
    out_shape = [SDS((T, D), F32)] + ([SDS((T, D), BF16)] if want_bf16 else []) + [SDS((1, D), F32)]
    out_specs = [row] + ([row] if want_bf16 else []) + [vec]
    return _ride_call(side, body, name=name, grid=(T // tt,), in_specs=[row, row, vec, row], args=[dh, x, g, dres],
                      out_specs=out_specs, out_shape=out_shape, scratch=[], semantics=("arbitrary",))


def _final_loss_bwd(name, x2, tgt, g, tt):
    T, D = x2.shape

    def body(x_ref, t_ref, g_ref, dx_ref, dxb_ref, dg_ref, lv_ref):
        i = pl.program_id(0)
        xv = x_ref[...]
        gv = g_ref[...]
        rstd = lax.rsqrt(jnp.mean(xv * xv, axis=-1, keepdims=True) + EPS)
        xn = xv * rstd
        e = xn * gv - t_ref[...]
        dy = e * (1.0 / D)
        dyg = dy * gv
        dx = rstd * (dyg - xn * jnp.mean(dyg * xn, axis=-1, keepdims=True))
        dx_ref[...] = dx
        dxb_ref[...] = dx.astype(BF16)

        @pl.when(i == 0)
        def _():
            dg_ref[...] = jnp.zeros_like(dg_ref)
            lv_ref[...] = jnp.zeros_like(lv_ref)

        dg_ref[...] += _colsum(dy * xn)
        lv_ref[...] += _colsum(e * e)

    row = pl.BlockSpec((tt, D), lambda i: (i, 0))
    vec = pl.BlockSpec((1, D), lambda i: (0, 0))
    return pl.pallas_call(
        body, name=name, grid=(T // tt,), in_specs=[row, row, vec], out_specs=[row, row, vec, vec],
        out_shape=[SDS((T, D), F32), SDS((T, D), BF16), SDS((1, D), F32), SDS((1, D), F32)],
        compiler_params=_cparams(dimension_semantics=("arbitrary",)),
    )(x2, tgt, g)


def _blk(tt, tc, off=0):
    return pl.BlockSpec((tt, tc), lambda j, i: (i, j + off))


def _vec(r, tc, off=0):
    return pl.BlockSpec((r, tc), lambda j, i: (0, j + off))


def _prev_halo(tt, tc, off=0):
    r = tt // HALO
    return pl.BlockSpec((HALO, tc), lambda j, i: (jnp.maximum(i * r - 1, 0), j + off))


def _next_halo(tt, tc, T, off=0):
    r = tt // HALO
    last = T // HALO - 1
    return pl.BlockSpec((HALO, tc), lambda j, i: (jnp.minimum((i + 1) * r, last), j + off))


def _seq_params():
    return _cparams(dimension_semantics=("parallel", "arbitrary"))


def _pool_fwd(name, proj, Cp, gw, tt, w_pool, scale):
    T = proj.shape[0]
    G = Cp // gw
    assert G == 4

    def body(u_ref, halo_ref, w_ref, s_ref, d_ref, z_ref, y_ref):
        j, i = pl.program_id(0), pl.program_id(1)
        u = u_ref[...]
        halo = jnp.where(i > 0, halo_ref[...], 0.0)
        ext = jnp.concatenate([halo, u], axis=0)
        e2 = ext + pltpu.roll(ext, 1, 0)
        e4 = e2 + pltpu.roll(e2, 2, 0)
        e8 = e4 + pltpu.roll(e4, 4, 0)
        e16 = e8 + pltpu.roll(e8, 8, 0)
        s = jnp.where(j == 0, e2, jnp.where(j == 1, e4, jnp.where(j == 2, e8, e16)))[HALO:]
        t1 = (i * tt + 1 + lax.broadcasted_iota(jnp.int32, (tt, 1), 0)).astype(F32)
        win = lax.shift_left(jnp.int32(2), j).astype(F32)
        d = (s * (1.0 / jnp.minimum(t1, win)) - u).astype(BF16)
        d_ref[...] = d
        z = jnp.dot(d, w_ref[...], preferred_element_type=F32)
        z_ref[...] = z
        y_ref[...] = (z * s_ref[...]).astype(BF16)

    return pl.pallas_call(
        body, name=name, grid=(G, T // tt),
        in_specs=[_blk(tt, gw), _prev_halo(tt, gw), pl.BlockSpec((None, gw, gw), lambda j, i: (j, 0, 0)), _vec(1, gw)],
        out_specs=[_blk(tt, gw)] * 3, out_shape=[SDS((T, Cp), BF16), SDS((T, Cp), F32), SDS((T, Cp), BF16)],
        compiler_params=_seq_params(),
    )(proj, proj, w_pool, scale)


def _pool_bwd_a(name, dy, z, scale, w_pool, tt):
    T, Cp = dy.shape
    gw = w_pool.shape[-1]

    def body(dy_ref, z_ref, s_ref, w_ref, dz_ref, dd_ref, ds_ref):
        i = pl.program_id(0)
        dyv = dy_ref[...]
        dz = (dyv * s_ref[...]).astype(BF16)
        dz_ref[...] = dz
        dd_ref[...] = _gate_pre_bwd(dz, w_ref, gw)

        @pl.when(i == 0)
        def _():
            ds_ref[...] = jnp.zeros_like(ds_ref)

        ds_ref[...] += _colsum(dyv * z_ref[...])

    row = pl.BlockSpec((tt, Cp), lambda i: (i, 0))
    vec = pl.BlockSpec((1, Cp), lambda i: (0, 0))
    wspec = pl.BlockSpec(w_pool.shape, lambda i: (0, 0, 0))
    return pl.pallas_call(
        body, name=name, grid=(T // tt,), in_specs=[row, row, vec, wspec], out_specs=[row, row, vec],
        out_shape=[SDS((T, Cp), BF16), SDS((T, Cp), F32), SDS((1, Cp), F32)],
        compiler_params=_cparams(dimension_semantics=("arbitrary",)),
    )(dy, z, scale, w_pool)


def _pool_bwd_b(name, dd, gw, tt):
    T, Cp = dd.shape
    G = Cp // gw
    nT = T // tt

    def body(d_ref, nxt_ref, du_ref):
        j, i = pl.program_id(0), pl.program_id(1)
        win = lax.shift_left(jnp.int32(2), j).astype(F32)
        dv = d_ref[...]
        t1 = (i * tt + 1 + lax.broadcasted_iota(jnp.int32, (tt + HALO, 1), 0)).astype(F32)
        nxt = jnp.where(i < nT - 1, nxt_ref[...], 0.0)
        ext = jnp.concatenate([dv, nxt], axis=0) * (1.0 / jnp.minimum(t1, win))
        n = tt + HALO
        f2 = ext + pltpu.roll(ext, n - 1, 0)
        f4 = f2 + pltpu.roll(f2, n - 2, 0)
        f8 = f4 + pltpu.roll(f4, n - 4, 0)
        f16 = f8 + pltpu.roll(f8, n - 8, 0)
        s = jnp.where(j == 0, f2, jnp.where(j == 1, f4, jnp.where(j == 2, f8, f16)))[:tt]
        du_ref[...] = (s - dv).astype(BF16)

    return pl.pallas_call(
        body, name=name, grid=(G, nT), in_specs=[_blk(tt, gw), _next_halo(tt, gw, T)], out_specs=_blk(tt, gw),
        out_shape=SDS((T, Cp), BF16), compiler_params=_seq_params(),
    )(dd, dd)


def _conv_fwd(name, src, off_cols, w, b, tt, tc, side=None):
    T = src.shape[0]
    K, C = w.shape
    off = off_cols // tc

    def body(u_ref, halo_ref, w_ref, b_ref, v_ref, vb_ref):
        i = pl.program_id(1)
        u = u_ref[...]
        halo = jnp.where(i > 0, halo_ref[...], 0.0)
        ext = jnp.concatenate([halo, u], axis=0)
        wv = w_ref[...]
        acc = b_ref[...] + u * wv[K - 1:K]
        for s in range(1, K):
            acc = acc + _lookback(ext, s) * wv[K - 1 - s:K - s]
        v_ref[...] = acc
        vb_ref[...] = acc.astype(BF16)

    return _ride_call(
        side, body, name=name, grid=(C // tc, T // tt),
        in_specs=[_blk(tt, tc, off), _prev_halo(tt, tc, off), _vec(K, tc), _vec(1, tc)], args=[src, src, w, b],
        out_specs=[_blk(tt, tc), _blk(tt, tc)], out_shape=[SDS((T, C), F32), SDS((T, C), BF16)],
        scratch=[], semantics=("parallel", "arbitrary"))


def _conv_bwd(name, dv, src, off_cols, w, tt, tc):
    T, C = dv.shape
    K = w.shape[0]
    off = off_cols // tc
    nT = T // tt

    def body(dv_ref, nxt_ref, u_ref, halo_ref, w_ref, du_ref, dw_ref, db_ref):
        i = pl.program_id(1)
        d = dv_ref[...]
        nxt = jnp.where(i < nT - 1, nxt_ref[...], 0.0)
        dext = jnp.concatenate([d, nxt], axis=0)
        u = u_ref[...]
        halo = jnp.where(i > 0, halo_ref[...], 0.0)
        uext = jnp.concatenate([halo, u], axis=0)
        wv = w_ref[...]
        du = d * wv[K - 1:K]
        rows = [_colsum(d * u)]
        for s in range(1, K):
            du = du + _lookahead(dext, s, tt) * wv[K - 1 - s:K - s]
            rows.append(_colsum(d * _lookback(uext, s)))
        du_ref[...] = du.astype(BF16)
        dw = jnp.concatenate(rows[::-1] + [jnp.zeros((8 - K, tc), F32)], axis=0)

        @pl.when(i == 0)
        def _():
            dw_ref[...] = jnp.zeros_like(dw_ref)
            db_ref[...] = jnp.zeros_like(db_ref)

        dw_ref[...] += dw
        db_ref[...] += _colsum(d)

    return pl.pallas_call(
        body, name=name, grid=(C // tc, nT),
        in_specs=[_blk(tt, tc), _next_halo(tt, tc, T), _blk(tt, tc, off), _prev_halo(tt, tc, off), _vec(K, tc)],
        out_specs=[_blk(tt, tc), _vec(8, tc), _vec(1, tc)],
        out_shape=[SDS((T, C), BF16), SDS((8, C), F32), SDS((1, C), F32)],
        compiler_params=_seq_params(),
    )(dv, dv, src, src, w)


def _lru_gates(ra, ia, v, ba, bi, lam):
    r = _sigmoid(ra + ba)
    ig = _sigmoid(ia + bi)
    nl = -lam
    ex = jnp.exp(-jnp.abs(nl))
    one_p = 1.0 + ex
    l1p = jnp.where(one_p == 1.0, ex, jnp.log(one_p) * ex / (one_p - 1.0))
    sp = jnp.maximum(nl, 0.0) + l1p
    a = jnp.exp(-LRU_C * r * sp)
    om = 1.0 - a * a
    rs = lax.rsqrt(jnp.maximum(om, 1e-30))
    return r, ig, sp, a, om * rs, rs


def _gate_pre(vv, w_ref, hw):
    vb = vv.astype(BF16)
    return jnp.concatenate([jnp.dot(vb[:, q * hw:(q + 1) * hw], w_ref[q], preferred_element_type=F32)
                            for q in range(vv.shape[1] // hw)], axis=1)


def _gate_pre_bwd(d, w_ref, hw):
    db = d.astype(BF16)
    return jnp.concatenate([lax.dot_general(db[:, q * hw:(q + 1) * hw], w_ref[q], (((1,), (1,)), ((), ())),
                                            preferred_element_type=F32)
                            for q in range(d.shape[1] // hw)], axis=1)


def _lru_fwd(name, w_a, w_i, v, proj, off_cols, ba, bi, lam, tt, tc, side=None):
    T, R = v.shape
    off = off_cols // tc
    hw = w_a.shape[-1]
    wspec = pl.BlockSpec((tc // hw, hw, hw), lambda j, i: (j, 0, 0))

    def body(wa_ref, wi_ref, v_ref, ug_ref, ba_ref, bi_ref, lam_ref, h_ref, y_ref, carry_ref):
        i = pl.program_id(1)

        @pl.when(i == 0)
        def _():
            carry_ref[...] = jnp.zeros_like(carry_ref)

        vv = v_ref[...]
        _, ig, _, a, mult, _ = _lru_gates(_gate_pre(vv, wa_ref, hw), _gate_pre(vv, wi_ref, hw), vv, ba_ref[...],
                                          bi_ref[...], lam_ref[...])
        A = a
        B = mult * (ig * vv)
        row8 = lax.broadcasted_iota(jnp.int32, (tt, 1), 0) % 8
        for s in (1, 2, 4):
            m = row8 >= s
            B = jnp.where(m, A * pltpu.roll(B, s, 0) + B, B)
            A = jnp.where(m, A * pltpu.roll(A, s, 0), A)
        carry = carry_ref[7:8, :]
        groups = []
        for g in range(tt // 8):
            hg = A[g * 8:(g + 1) * 8] * carry + B[g * 8:(g + 1) * 8]
            carry = hg[7:8]
            groups.append(hg)
        h = jnp.concatenate(groups, axis=0)
        h_ref[...] = h
        carry_ref[...] = groups[-1]
        y_ref[...] = (h * _gelu(ug_ref[...])).astype(BF16)

    return _ride_call(
        side, body, name=name, grid=(R // tc, T // tt),
        in_specs=[wspec, wspec, _blk(tt, tc), _blk(tt, tc, off), _vec(1, tc), _vec(1, tc), _vec(1, tc)],
        args=[w_a, w_i, v, proj, ba, bi, lam],
        out_specs=[_blk(tt, tc), _blk(tt, tc)], out_shape=[SDS((T, R), F32), SDS((T, R), BF16)],
        scratch=[pltpu.VMEM((8, tc), F32)], semantics=("parallel", "arbitrary"))


def _lru_bwd(name, dy, hs, proj, off_cols, w_a, w_i, v, ba, bi, lam, tt, tc):
    T, R = v.shape
    off = off_cols // tc
    nT = T // tt
    r16 = tt // HALO
    hw = w_a.shape[-1]
    wspec = pl.BlockSpec((tc // hw, hw, hw), lambda j, i: (j, 0, 0))

    def rblk(o=0):
        return pl.BlockSpec((tt, tc), lambda j, i: (nT - 1 - i, j + o))

    hprev = pl.BlockSpec((HALO, tc), lambda j, i: (jnp.maximum((nT - 1 - i) * r16 - 1, 0), j))

    def body(dy_ref, h_ref, hp_ref, ug_ref, wa_ref, wi_ref, v_ref, ba_ref, bi_ref, lam_ref,
             dra_ref, dia_ref, dv_ref, dug_ref, dba_ref, dbi_ref, dlam_ref, dh_carry, a_carry):
        i = pl.program_id(1)

        @pl.when(i == 0)
        def _():
            dh_carry[...] = jnp.zeros_like(dh_carry)
            a_carry[...] = jnp.zeros_like(a_carry)
            dba_ref[...] = jnp.zeros_like(dba_ref)
            dbi_ref[...] = jnp.zeros_like(dbi_ref)
            dlam_ref[...] = jnp.zeros_like(dlam_ref)

        vv = v_ref[...]
        lamv = lam_ref[...]
        r, ig, sp, a, mult, inv_mult = _lru_gates(_gate_pre(vv, wa_ref, hw), _gate_pre(vv, wi_ref, hw), vv,
                                                  ba_ref[...], bi_ref[...], lamv)
        h = h_ref[...]
        gel, dgel = _gelu_and_grad(ug_ref[...])
        dyv = dy_ref[...]
        dug_ref[...] = (dyv * h * dgel).astype(BF16)
        dhs = dyv * gel

        a_ext = jnp.concatenate([a, jnp.broadcast_to(a_carry[0:1, :], (HALO, tc))], axis=0)
        A = _lookahead(a_ext, 1, tt)
        B = dhs
        row8 = lax.broadcasted_iota(jnp.int32, (tt, 1), 0) % 8
        for s in (1, 2, 4):
            m = row8 < 8 - s
            B = jnp.where(m, B + A * pltpu.roll(B, tt - s, 0), B)
            A = jnp.where(m, A * pltpu.roll(A, tt - s, 0), A)
        carry = dh_carry[0:1, :]
        groups = []
        for g in reversed(range(tt // 8)):
            dg = B[g * 8:(g + 1) * 8] + A[g * 8:(g + 1) * 8] * carry
            carry = dg[0:1]
            groups.append(dg)
        dH = jnp.concatenate(groups[::-1], axis=0)
        dh_carry[...] = groups[-1]
        a_carry[...] = a[:8, :]

        hp = jnp.where(i < nT - 1, hp_ref[...], 0.0)
        h_prev = _lookback(jnp.concatenate([hp, h], axis=0), 1)
        da = dH * h_prev
        dmult = dH * (ig * vv)
        div = dH * mult
        dig = div * vv
        dlog_a = da * a - dmult * (a * a) * inv_mult
        dr = dlog_a * (-LRU_C * sp)
        dra = dr * r * (1.0 - r)
        dia = dig * ig * (1.0 - ig)
        dra_ref[...] = dra.astype(BF16)
        dia_ref[...] = dia.astype(BF16)
        dv_ref[...] = (_gate_pre_bwd(dra, wa_ref, hw) + _gate_pre_bwd(dia, wi_ref, hw)) + div * ig
        dba_ref[...] += _colsum(dra)
        dbi_ref[...] += _colsum(dia)
        dlam_ref[...] += _colsum(dlog_a * (-LRU_C * r))

        @pl.when(i == nT - 1)
        def _():
            dlam_ref[...] = dlam_ref[...] * (-_sigmoid(-lamv))

    return pl.pallas_call(
        body, name=name, grid=(R // tc, nT),
        in_specs=[rblk(), rblk(), hprev, rblk(off), wspec, wspec, rblk(), _vec(1, tc), _vec(1, tc), _vec(1, tc)],
        out_specs=[rblk(), rblk(), rblk(), rblk(), _vec(1, tc), _vec(1, tc), _vec(1, tc)],
        out_shape=[SDS((T, R), BF16), SDS((T, R), BF16), SDS((T, R), F32), SDS((T, R), BF16),
                   SDS((1, R), F32), SDS((1, R), F32), SDS((1, R), F32)],
        scratch_shapes=[pltpu.VMEM((8, tc), F32), pltpu.VMEM((8, tc), F32)], compiler_params=_seq_params(),
    )(dy, hs, hs, proj, w_a, w_i, v, ba, bi, lam)


def _merge_fwd(name, proj, off_cols, b_gate, P, Q, tt, tc, side=None):
    T, D = P.shape
    o0 = off_cols // tc
    o1 = o0 + D // tc

    def body(g0_ref, g1_ref, b0_ref, b1_ref, p_ref, q_ref, m_ref):
        g0 = _sigmoid(g0_ref[...] + b0_ref[...])
        g1 = _sigmoid(g1_ref[...] + b1_ref[...])
        m_ref[...] = (g0 * p_ref[...].astype(F32) + g1 * q_ref[...].astype(F32)).astype(BF16)

    outs = _ride_call(
        side, body, name=name, grid=(D // tc, T // tt),
        in_specs=[_blk(tt, tc, o0), _blk(tt, tc, o1), _vec(1, tc), _vec(1, tc, D // tc), _blk(tt, tc), _blk(tt, tc)],
        args=[proj, proj, b_gate, b_gate, P, Q], out_specs=[_blk(tt, tc)], out_shape=[SDS((T, D), BF16)],
        scratch=[], semantics=("parallel", "arbitrary"))
    return outs if side is not None else outs[0]


def _merge_bwd(name, dm, proj, off_cols, b_gate, P, Q, tt, tc):
    T, D = P.shape
    o0 = off_cols // tc
    o1 = o0 + D // tc

    def body(dm_ref, g0_ref, g1_ref, b0_ref, b1_ref, p_ref, q_ref, dp_ref, dq_ref, dl0_ref, dl1_ref, db0_ref, db1_ref):
        i = pl.program_id(1)
        g0 = _sigmoid(g0_ref[...] + b0_ref[...])
        g1 = _sigmoid(g1_ref[...] + b1_ref[...])
        d = dm_ref[...].astype(F32)
        dp_ref[...] = (d * g0).astype(BF16)
        dq_ref[...] = (d * g1).astype(BF16)
        dl0 = d * p_ref[...].astype(F32) * g0 * (1.0 - g0)
        dl1 = d * q_ref[...].astype(F32) * g1 * (1.0 - g1)
        dl0_ref[...] = dl0.astype(BF16)
        dl1_ref[...] = dl1.astype(BF16)

        @pl.when(i == 0)
        def _():
            db0_ref[...] = jnp.zeros_like(db0_ref)
            db1_ref[...] = jnp.zeros_like(db1_ref)

        db0_ref[...] += _colsum(dl0)
        db1_ref[...] += _colsum(dl1)

    b = _blk(tt, tc)
    return pl.pallas_call(
        body, name=name, grid=(D // tc, T // tt),
        in_specs=[b, _blk(tt, tc, o0), _blk(tt, tc, o1), _vec(1, tc), _vec(1, tc, D // tc), b, b],
        out_specs=[b, b, b, b, _vec(1, tc), _vec(1, tc)],
        out_shape=[SDS((T, D), BF16)] * 4 + [SDS((1, D), F32)] * 2, compiler_params=_seq_params(),
    )(dm, proj, proj, b_gate, b_gate, P, Q)


def _ffn_act_fwd(name, up, w, b, tt, tc):
    T = up.shape[0]
    K, F = w.shape
    nF = F // tc

    def body(gp_ref, halo_ref, val_ref, w_ref, b_ref, z_ref):
        i = pl.program_id(1)
        gp = gp_ref[...].astype(F32)
        halo = jnp.where(i > 0, halo_ref[...].astype(F32), 0.0)
        ext = jnp.concatenate([halo, gp], axis=0)
        wv = w_ref[...]
        c = b_ref[...] + gp * wv[K - 1:K]
        for s in range(1, K):
            c = c + _lookback(ext, s) * wv[K - 1 - s:K - s]
        z_ref[...] = (_gelu(c) * val_ref[...].astype(F32)).astype(BF16)

    return pl.pallas_call(
        body, name=name, grid=(nF, T // tt),
        in_specs=[_blk(tt, tc), _prev_halo(tt, tc), _blk(tt, tc, nF), _vec(K, tc), _vec(1, tc)],
        out_specs=_blk(tt, tc), out_shape=SDS((T, F), BF16), compiler_params=_seq_params(),
    )(up, up, up, w, b)


def _ffn_down_fwd(name, up, w, b, w_down, x1, tm, tk, nsub):
    T = up.shape[0]
    K, F = w.shape
    D = w_down.shape[1]
    nF = F // tk
    sub = tk // nsub
    r16 = tm // HALO

    def body(gp_ref, halo_ref, val_ref, w_ref, b_ref, wd_ref, x1_ref, z_ref, o_ref, acc_ref):
        i, k = pl.program_id(0), pl.program_id(1)
        total = None
        for s in range(nsub):
            cs = pl.ds(s * sub, sub)
            gp = gp_ref[:, cs].astype(F32)
            halo = jnp.where(i > 0, halo_ref[:, cs].astype(F32), 0.0)
            ext = jnp.concatenate([halo, gp], axis=0)
            wv = w_ref[:, cs]
            c = b_ref[:, cs] + gp * wv[K - 1:K]
            for t in range(1, K):
                c = c + _lookback(ext, t) * wv[K - 1 - t:K - t]
            zs = (_gelu(c) * val_ref[:, cs].astype(F32)).astype(BF16)
            z_ref[:, cs] = zs
            p = jnp.dot(zs, wd_ref[cs, :], preferred_element_type=F32)
            total = p if total is None else total + p

        @pl.when(k == 0)
        def _():
            acc_ref[...] = total

        @pl.when(k > 0)
        def _():
            acc_ref[...] += total

        @pl.when(k == nF - 1)
        def _():
            o_ref[...] = acc_ref[...] + x1_ref[...]

    blk = lambda off: pl.BlockSpec((tm, tk), lambda i, k: (i, k + off))
    return pl.pallas_call(
        body, name=name, grid=(T // tm, nF),
        in_specs=[blk(0), pl.BlockSpec((HALO, tk), lambda i, k: (jnp.maximum(i * r16 - 1, 0), k)), blk(nF),
                  pl.BlockSpec((K, tk), lambda i, k: (0, k)), pl.BlockSpec((1, tk), lambda i, k: (0, k)),
                  pl.BlockSpec((tk, D), lambda i, k: (k, 0)), pl.BlockSpec((tm, D), lambda i, k: (i, 0))],
        out_specs=[blk(0), pl.BlockSpec((tm, D), lambda i, k: (i, 0))],
        out_shape=[SDS((T, F), BF16), SDS((T, D), F32)],
        scratch_shapes=[pltpu.VMEM((tm, D), F32)],
        compiler_params=_cparams(dimension_semantics=("parallel", "arbitrary")),
    )(up, up, up, w, b, w_down, x1)


def _ffn_act_bwd(name, up, dz, w, b, tt, tc):
    T = up.shape[0]
    K, F = w.shape
    nF = F // tc
    nT = T // tt

    def body(gp_ref, gph_ref, gpn_ref, val_ref, valn_ref, dz_ref, dzn_ref, w_ref, b_ref, dup_ref, dw_ref, db_ref):
        i = pl.program_id(1)
        wv = w_ref[...]
        gp = gp_ref[...].astype(F32)
        halo = jnp.where(i > 0, gph_ref[...].astype(F32), 0.0)
        full = jnp.concatenate([halo, gp, gpn_ref[...].astype(F32)], axis=0)
        c = b_ref[...] + full * wv[K - 1:K]
        for s in range(1, K):
            c = c + pltpu.roll(full, s, 0) * wv[K - 1 - s:K - s]
        c = c[HALO:]
        gate, dgate = _gelu_and_grad(c)
        dz_e = jnp.concatenate([dz_ref[...], dzn_ref[...]], axis=0).astype(F32)
        val_e = jnp.concatenate([val_ref[...], valn_ref[...]], axis=0).astype(F32)
        dc_e = dz_e * val_e * dgate
        row = lax.broadcasted_iota(jnp.int32, (tt + HALO, 1), 0)
        dc_e = jnp.where((row < tt) | (i < nT - 1), dc_e, 0.0)
        dc = dc_e[:tt]
        dgp = dc * wv[K - 1:K]
        uext = full[:tt + HALO]
        rows = [_colsum(dc * gp)]
        for s in range(1, K):
            dgp = dgp + _lookahead(dc_e, s, tt) * wv[K - 1 - s:K - s]
            rows.append(_colsum(dc * _lookback(uext, s)))
        dup_ref[0] = dgp.astype(BF16)
        dup_ref[1] = (dz_e[:tt] * gate[:tt]).astype(BF16)
        dw = jnp.concatenate(rows[::-1] + [jnp.zeros((8 - K, tc), F32)], axis=0)

        @pl.when(i == 0)
        def _():
            dw_ref[...] = jnp.zeros_like(dw_ref)
            db_ref[...] = jnp.zeros_like(db_ref)

        dw_ref[...] += dw
        db_ref[...] += _colsum(dc)

    return pl.pallas_call(
        body, name=name, grid=(nF, nT),
        in_specs=[_blk(tt, tc), _prev_halo(tt, tc), _next_halo(tt, tc, T), _blk(tt, tc, nF), _next_halo(tt, tc, T, nF),
                  _blk(tt, tc), _next_halo(tt, tc, T), _vec(K, tc), _vec(1, tc)],
        out_specs=[pl.BlockSpec((2, tt, tc), lambda j, i: (0, i, j)), _vec(8, tc), _vec(1, tc)],
        out_shape=[SDS((2, T, F), BF16), SDS((8, F), F32), SDS((1, F), F32)],
        compiler_params=_seq_params(),
    )(up, up, up, up, up, dz, dz, w, b)


def _adamw(name, w, g, m, v):
    R, C = w.shape
    tr = R
    if R * C * 4 > ADAMW_BLOCK_BYTES:
        tr = _pick(R, [t for t in (256, 128, 64, 32, 16, 8) if t * C * 4 <= ADAMW_BLOCK_BYTES])
    c1 = 1.0 - ADAM_B1 ** ADAM_STEP
    c2 = 1.0 - ADAM_B2 ** ADAM_STEP

    def body(w_ref, g_ref, m_ref, v_ref, d_ref, nm_ref, nv_ref):
        gv = g_ref[...]
        nm = ADAM_B1 * m_ref[...] + (1.0 - ADAM_B1) * gv
        nv = ADAM_B2 * v_ref[...] + (1.0 - ADAM_B2) * (gv * gv)
        nm_ref[...] = nm
        nv_ref[...] = nv
        d_ref[...] = -ADAM_LR * ((nm / c1) / (jnp.sqrt(nv / c2) + ADAM_EPS) + ADAM_WD * w_ref[...])

    spec = pl.BlockSpec((tr, C), lambda i: (i, 0))
    return pl.pallas_call(
        body, name=name, grid=(R // tr,), in_specs=[spec] * 4, out_specs=[spec] * 3,
        out_shape=[SDS((R, C), F32)] * 3, compiler_params=_cparams(dimension_semantics=("parallel",)),
    )(w, g, m, v)


def _add_halves(name, part, rcv, pos):
    S, R, C = part.shape
    h = R // 2
    tr = _pick(h, (512, 256, 128, 64, 32, 16))
    nb = h // tr

    def body(pos_ref, p_ref, r_ref, o_ref):
        o_ref[...] = (p_ref[...].astype(F32) + r_ref[...].astype(F32)).astype(BF16)

    return pl.pallas_call(
        body, name=name,
        grid_spec=pltpu.PrefetchScalarGridSpec(
            num_scalar_prefetch=1, grid=(S, nb),
            in_specs=[pl.BlockSpec((None, tr, C), lambda s, r, pos: (s, pos[0] * nb + r, 0)),
                      pl.BlockSpec((None, tr, C), lambda s, r, pos: (s, r, 0))],
            out_specs=pl.BlockSpec((None, tr, C), lambda s, r, pos: (s, r, 0))),
        out_shape=SDS((S, h, C), BF16), compiler_params=_cparams(dimension_semantics=("parallel", "parallel")),
    )(pos, part, rcv)


def _add_chips(name, chipsum, rcv, pos):
    S, h, C = chipsum.shape
    tr = _pick(h, (512, 256, 128, 64, 32, 16))

    def body(pos_ref, own_ref, r_ref, o_ref):
        o_ref[...] = ((own_ref[...].astype(F32) + r_ref[0].astype(F32)) + r_ref[1].astype(F32)) + r_ref[2].astype(F32)

    return pl.pallas_call(
        body, name=name,
        grid_spec=pltpu.PrefetchScalarGridSpec(
            num_scalar_prefetch=1, grid=(h // tr,),
            in_specs=[pl.BlockSpec((None, tr, C), lambda r, pos: (pos[1], r, 0)),
                      pl.BlockSpec((3, tr, C), lambda r, pos: (0, r, 0))],
            out_specs=pl.BlockSpec((None, tr, C), lambda r, pos: (pos[0], r, 0))),
        out_shape=SDS((2, h, C), F32), compiler_params=_cparams(dimension_semantics=("parallel",)),
    )(pos, chipsum, rcv)


def _place():
    x, y, c = lax.axis_index("x"), lax.axis_index("y"), lax.axis_index("c")
    chips = [(1 - x, y), (x, 1 - y), (1 - x, 1 - y)]
    slots = [2 * cx + cy for cx, cy in chips]
    return x, y, c, chips, slots


def _gather_side(shards, split):
    n = len(shards)

    def copies(ins, outs, sems):
        send_sems, recv_sems, local_sems = sems
        x, y, c, chips, slots = _place()
        me = 2 * x + y
        sibling = (x, y, 1 - c)

        def rows(a, half):
            hrows = ins[a].shape[0] // 2
            return pl.ds(half * hrows, hrows)

        def ici(a, j):
            if split[a]:
                src, dst = ins[a].at[rows(a, c)], outs[a].at[me, rows(a, c)]
            else:
                src, dst = ins[a], outs[a].at[me]
            return pltpu.make_async_remote_copy(src_ref=src, dst_ref=dst, send_sem=send_sems.at[a, j],
                                                recv_sem=recv_sems.at[a, j], device_id=(*chips[j], c), device_id_type=MESH)

        def landed(a, j):
            dst = outs[a].at[slots[j], rows(a, c)] if split[a] else outs[a].at[slots[j]]
            return pltpu.make_async_remote_copy(src_ref=dst, dst_ref=dst, send_sem=send_sems.at[a, j],
                                                recv_sem=recv_sems.at[a, j], device_id=(*chips[j], c), device_id_type=MESH)

        def d2d(a, j, half):
            blk = outs[a].at[slots[j], rows(a, half)]
            return pltpu.make_async_remote_copy(src_ref=blk, dst_ref=blk, send_sem=send_sems.at[a, 3 + j],
                                                recv_sem=recv_sems.at[a, 3 + j], device_id=sibling, device_id_type=MESH)

        local = [pltpu.make_async_copy(ins[a], outs[a].at[me], local_sems.at[a]) for a in range(n)]
        return c, ici, landed, d2d, local

    def start(ins, outs, sems):
        c, ici, landed, d2d, local = copies(ins, outs, sems)
        for cp in local:
            cp.start()
        for a in range(n):
            for j in range(3):
                ici(a, j).start()

    def mid(ins, outs, sems):
        c, ici, landed, d2d, local = copies(ins, outs, sems)
        for a in range(n):
            for j in range(3):
                landed(a, j).wait_recv()
                if split[a]:
                    d2d(a, j, c).start()

    def finish(ins, outs, sems):
        c, ici, landed, d2d, local = copies(ins, outs, sems)
        for a in range(n):
            if split[a]:
                for j in range(3):
                    d2d(a, j, 1 - c).wait_recv()
        for a in range(n):
            for j in range(3):
                ici(a, j).wait_send()
                if split[a]:
                    d2d(a, j, c).wait_send()
        for cp in local:
            cp.wait()

    return _Side(shards, [SDS((N_SLOTS,) + s.shape, s.dtype) for s in shards],
                 [pltpu.SemaphoreType.DMA((n, 6)), pltpu.SemaphoreType.DMA((n, 6)), pltpu.SemaphoreType.DMA((n,))],
                 start, mid, finish)


def _swap_partial_halves(name, parts):
    side = _swap_side(parts)
    n = len(parts)

    def body(*refs):
        ins, outs, sems = refs[:n], refs[n:2 * n], refs[2 * n:]
        side.start(ins, outs, sems)
        side.finish(ins, outs, sems)

    return pl.pallas_call(
        body, name=name, in_specs=[ANY] * n, out_specs=[ANY] * n, out_shape=side.out_shapes,
        scratch_shapes=side.sem_shapes, compiler_params=_cparams(has_side_effects=True),
    )(*parts)


def _swap_side(parts):
    n = len(parts)

    def copies(ins, outs, sems):
        send_sems, recv_sems = sems
        x, y, c, _, _ = _place()
        cps = []
        for a in range(n):
            h = ins[a].shape[1] // 2
            cps.append(pltpu.make_async_remote_copy(
                src_ref=ins[a].at[:, pl.ds((1 - c) * h, h)], dst_ref=outs[a], send_sem=send_sems.at[a],
                recv_sem=recv_sems.at[a], device_id=(x, y, 1 - c), device_id_type=MESH))
        return cps

    def start(ins, outs, sems):
        for cp in copies(ins, outs, sems):
            cp.start()

    def finish(ins, outs, sems):
        for cp in copies(ins, outs, sems):
            cp.wait()

    return _Side(parts, [SDS((p.shape[0], p.shape[1] // 2, p.shape[2]), p.dtype) for p in parts],
                 [pltpu.SemaphoreType.DMA((n,)), pltpu.SemaphoreType.DMA((n,))], start, None, finish)


def _exchange_side(sums):
    n = len(sums)

    def copies(ins, outs, sems):
        send_sems, recv_sems = sems
        x, y, c, chips, slots = _place()
        return [pltpu.make_async_remote_copy(
            src_ref=ins[a].at[slots[j]], dst_ref=outs[a].at[j], send_sem=send_sems.at[a, j],
            recv_sem=recv_sems.at[a, j], device_id=(*chips[j], c), device_id_type=MESH)
            for a in range(n) for j in range(3)]

    def start(ins, outs, sems):
        for cp in copies(ins, outs, sems):
            cp.start()

    def finish(ins, outs, sems):
        for cp in copies(ins, outs, sems):
            cp.wait()

    return _Side(sums, [SDS((3,) + s.shape[1:], s.dtype) for s in sums],
                 [pltpu.SemaphoreType.DMA((n, 3)), pltpu.SemaphoreType.DMA((n, 3))], start, None, finish)


def _swap_reduced_halves(totals):
    side = _reduced_swap_side(totals)
    n = len(totals)

    def body(*refs):
        ins, outs, sems = refs[:n], refs[n:2 * n], refs[2 * n:]
        side.start(ins, outs, sems)
        side.finish(ins, outs, sems)

    return pl.pallas_call(
        body, name="swap_reduced_halves", in_specs=[ANY] * n, out_specs=[ANY] * n, out_shape=side.out_shapes,
        input_output_aliases={a: a for a in range(n)}, scratch_shapes=side.sem_shapes,
        compiler_params=_cparams(has_side_effects=True),
    )(*totals)


def _reduced_swap_side(totals):
    n = len(totals)

    def sends(outs, sems):
        send_sems, recv_sems = sems
        x, y, c, _, _ = _place()
        return [pltpu.make_async_remote_copy(src_ref=outs[a].at[c], dst_ref=outs[a].at[c], send_sem=send_sems.at[a],
                                             recv_sem=recv_sems.at[a], device_id=(x, y, 1 - c), device_id_type=MESH)
                for a in range(n)]

    def start(ins, outs, sems):
        for cp in sends(outs, sems):
            cp.start()

    def finish(ins, outs, sems):
        send_sems, recv_sems = sems
        x, y, c, _, _ = _place()
        for a in range(n):
            got = outs[a].at[1 - c]
            pltpu.make_async_remote_copy(src_ref=got, dst_ref=got, send_sem=send_sems.at[a], recv_sem=recv_sems.at[a],
                                         device_id=(x, y, 1 - c), device_id_type=MESH).wait_recv()
        for cp in sends(outs, sems):
            cp.wait_send()

    return _Side(totals, [SDS(t.shape, t.dtype) for t in totals],
                 [pltpu.SemaphoreType.DMA((n,)), pltpu.SemaphoreType.DMA((n,))], start, None, finish, in_place=True)


def _allreduce_small(pack):
    rows = pack.shape[0]

    def body(in_ref, out_ref, buf, send_sems, recv_sems):
        x, y, c = lax.axis_index("x"), lax.axis_index("y"), lax.axis_index("c")
        me = 4 * x + 2 * y + c
        buf[me] = in_ref[...]
        cps = []
        for k in range(1, 8):
            kx, ky, kc = (k >> 2) & 1, (k >> 1) & 1, k & 1
            peer = (x ^ kx, y ^ ky, c ^ kc)
            cps.append(pltpu.make_async_remote_copy(src_ref=in_ref, dst_ref=buf.at[me], send_sem=send_sems.at[k - 1],
                                                    recv_sem=recv_sems.at[k - 1], device_id=peer, device_id_type=MESH))
        for cp in cps:
            cp.start()
        for k in range(1, 8):
            got = buf.at[me ^ k]
            pltpu.make_async_remote_copy(src_ref=got, dst_ref=got, send_sem=send_sems.at[k - 1],
                                         recv_sem=recv_sems.at[k - 1], device_id=(x, y, c), device_id_type=MESH).wait_recv()
        for cp in cps:
            cp.wait_send()
        acc = buf[0]
        for d in range(1, 8):
            acc = acc + buf[d]
        out_ref[...] = acc

    vm = pl.BlockSpec(memory_space=pltpu.VMEM)
    return pl.pallas_call(
        body, name="allreduce_small", in_specs=[vm], out_specs=vm, out_shape=SDS(pack.shape, F32),
        scratch_shapes=[pltpu.VMEM((8, rows, LANES), F32), pltpu.SemaphoreType.DMA((7,)), pltpu.SemaphoreType.DMA((7,))],
        compiler_params=_cparams(has_side_effects=True),
    )(pack)


def _rows2d(a):
    return a.reshape((-1, a.shape[-1]))


def _from_slots_blockdiag(g, H, w):
    q = w // N_SLOTS
    return g.reshape(N_SLOTS, H, q, w).transpose(1, 0, 2, 3).reshape(H, w, w)


def _to_slots_blockdiag(d, H, w):
    q = w // N_SLOTS
    return d.reshape(H, N_SLOTS, q, w).transpose(1, 0, 2, 3).reshape(N_SLOTS, H * q, w)


def kernel(x, g_mix, w_in, b_gate, w_pool, pool_scale, lru_conv_w, lru_conv_b, w_a, b_a, w_i, b_i, lru_lambda, w_pool_proj, w_lru_proj, w_out, g_mlp, w_up, ffn_conv_w, ffn_conv_b, w_down, g_final, loss_target, m_g_mix, m_w_in, m_b_gate, m_w_pool, m_pool_scale, m_lru_conv_w, m_lru_conv_b, m_w_a, m_b_a, m_w_i, m_b_i, m_lru_lambda, m_w_pool_proj, m_w_lru_proj, m_w_out, m_g_mlp, m_w_up, m_ffn_conv_w, m_ffn_conv_b, m_w_down, m_g_final, v_g_mix, v_w_in, v_b_gate, v_w_pool, v_pool_scale, v_lru_conv_w, v_lru_conv_b, v_w_a, v_b_a, v_w_i, v_b_i, v_lru_lambda, v_w_pool_proj, v_w_lru_proj, v_w_out, v_g_mlp, v_w_up, v_ffn_conv_w, v_ffn_conv_b, v_w_down, v_g_final):
    weights = dict(g_mix=g_mix, w_in=w_in, b_gate=b_gate, w_pool=w_pool, pool_scale=pool_scale, lru_conv_w=lru_conv_w,
                   lru_conv_b=lru_conv_b, w_a=w_a, b_a=b_a, w_i=w_i, b_i=b_i, lru_lambda=lru_lambda,
                   w_pool_proj=w_pool_proj, w_lru_proj=w_lru_proj, w_out=w_out, g_mlp=g_mlp, w_up=w_up,
                   ffn_conv_w=ffn_conv_w, ffn_conv_b=ffn_conv_b, w_down=w_down, g_final=g_final)
    mom_m = dict(g_mix=m_g_mix, w_in=m_w_in, b_gate=m_b_gate, w_pool=m_w_pool, pool_scale=m_pool_scale,
                 lru_conv_w=m_lru_conv_w, lru_conv_b=m_lru_conv_b, w_a=m_w_a, b_a=m_b_a, w_i=m_w_i, b_i=m_b_i,
                 lru_lambda=m_lru_lambda, w_pool_proj=m_w_pool_proj, w_lru_proj=m_w_lru_proj, w_out=m_w_out,
                 g_mlp=m_g_mlp, w_up=m_w_up, ffn_conv_w=m_ffn_conv_w, ffn_conv_b=m_ffn_conv_b, w_down=m_w_down,
                 g_final=m_g_final)
    mom_v = dict(g_mix=v_g_mix, w_in=v_w_in, b_gate=v_b_gate, w_pool=v_w_pool, pool_scale=v_pool_scale,
                 lru_conv_w=v_lru_conv_w, lru_conv_b=v_lru_conv_b, w_a=v_w_a, b_a=v_b_a, w_i=v_w_i, b_i=v_b_i,
                 lru_lambda=v_lru_lambda, w_pool_proj=v_w_pool_proj, w_lru_proj=v_w_lru_proj, w_out=v_w_out,
                 g_mlp=v_g_mlp, w_up=v_w_up, ffn_conv_w=v_ffn_conv_w, ffn_conv_b=v_ffn_conv_b, w_down=v_w_down,
                 g_final=v_g_final)
    order = list(weights)

    T, D = x.shape[1], x.shape[2]
    Cp = pool_scale.shape[1]
    G, gw = w_pool.shape[1], w_pool.shape[3]
    H, hw = w_a.shape[1], w_a.shape[3]
    R = b_a.shape[1]
    F = ffn_conv_b.shape[1]
    KC, KF = lru_conv_w.shape[1], ffn_conv_w.shape[1]
    p0, p1, p2 = Cp, Cp + R, Cp + 2 * R
    xs = x.reshape(T, D)
    tgt = loss_target.reshape(T, D)

    my_x, my_y, my_c = lax.axis_index("x"), lax.axis_index("y"), lax.axis_index("c")
    slot = 2 * my_x + my_y
    pos = jnp.stack([my_c, slot]).astype(jnp.int32)

    big = ["w_in", "w_pool", "w_a", "w_i", "w_pool_proj", "w_lru_proj", "w_out", "w_up", "w_down"]
    first = big[:4]
    shard16 = {n: _rows2d(weights[n][0]).astype(BF16) for n in big}
    first_side = _gather_side([shard16[n] for n in first] + [lru_conv_w[0], ffn_conv_w[0]], [True] * 4 + [False, False])
    n_in, n_up, n_pp = w_in.shape[2], w_up.shape[2], w_pool_proj.shape[2]

    tt_row = _pick(T, (256, 128))
    tt_seq = _pick(T, (512, 256, 128))
    tt_pool = _pick(T, (1024, 512, 256, 128))
    tc_seq = _pick(R, (512, 256, 128))
    tm1 = _pick(T, (1024, 512, 256))
    tm2 = _pick(T, (2048, 1024, 512, 256))
    tkT = _pick(T, (1024, 512, 256))
    tkT2 = _pick(T, (4096, 2048, 1024, 512, 256))
    tn_in = _pick(n_in, (768, 1152, 384, 128))
    tn_up = _pick(n_up, (1536, 1024, 768, 512, 128))
    tk_up = _pick(n_up, (1024, 768, 512, 128))
    tn_pp = _pick(n_pp, (512, 256, 128))
    tD1 = _pick(D, (1024, 512))
    tD2 = _pick(D, (2048, 1024, 512))
    tF = _pick(F, (1536, 1024, 512))
    tkF = _pick(F, (2048, 1536, 1024, 512))

    h, *gathered = _rms_fwd("rms1_fwd", xs, g_mix, tt_row, side=first_side)
    gw_ = dict(zip(first + ["lru_conv_w", "ffn_conv_w"], gathered))
    W_in = gw_["w_in"]
    W_pool = _from_slots_blockdiag(gw_["w_pool"], G, gw)
    W_a = _from_slots_blockdiag(gw_["w_a"], H, hw)
    W_i = _from_slots_blockdiag(gw_["w_i"], H, hw)
    cw_lru = gw_["lru_conv_w"].transpose(1, 0, 2).reshape(KC, R)
    cw_ffn = gw_["ffn_conv_w"].transpose(1, 0, 2).reshape(KF, F)
    on_proj = ["w_pool_proj", "w_lru_proj", "w_out", "w_down"]
    proj3, *late = _matmul("proj_fwd", "nn", h, W_in, out_dtype=F32, tm=tm2, tn=tn_in, tk=D,
                           side=_gather_side([shard16[n] for n in on_proj], [True] * len(on_proj)))
    proj = proj3[0]
    gw_.update(zip(on_proj, late))
    W_pp = gw_["w_pool_proj"]
    W_lp = gw_["w_lru_proj"].reshape(R, D)
    W_out = gw_["w_out"].reshape(D, D)
    W_down = gw_["w_down"].reshape(F, D)
    up16, eD = shard16["w_up"], D // 8
    d_pool, z_pool, y_pool = _pool_fwd("pool_fwd", proj, Cp, gw, tt_pool, W_pool, pool_scale)
    v_f, v_b, up_a = _conv_fwd("lru_conv_fwd", proj, p0, cw_lru, lru_conv_b, tt_seq, tc_seq,
                               side=_gather_side([up16[:eD]], [True]))
    hs, y_lru, up_b = _lru_fwd("lru_scan_fwd", W_a, W_i, v_f, proj, p1, b_a, b_i, lru_lambda, tt_seq, tc_seq,
                               side=_gather_side([up16[eD:4 * eD]], [True]))
    P = _matmul("pool_proj_fwd", "nn", y_pool, W_pp, out_dtype=BF16, tm=tm2, tn=tn_pp, tk=Cp)[0]
    Q3, up_c = _matmul("lru_proj_fwd", "nn", y_lru, W_lp, out_dtype=BF16, tm=tm1, tn=tD1, tk=R,
                       side=_gather_side([up16[4 * eD:5 * eD]], [True]))
    Q = Q3[0]
    merged, up_d = _merge_fwd("merge_fwd", proj, p2, b_gate, P, Q, tt_seq, tc_seq,
                              side=_gather_side([up16[5 * eD:6 * eD]], [True]))
    x13, up_e = _matmul("out_fwd", "nn", merged, W_out, out_dtype=F32, tm=tm1, tn=tD1, tk=D, res=xs,
                        side=_gather_side([up16[6 * eD:]], [True]))
    x1 = x13[0]
    W_up = jnp.concatenate([up_a, up_b, up_c, up_d, up_e], axis=1)
    h2 = _rms_fwd("rms2_fwd", x1, g_mlp, tt_row)
    up = _matmul("up_fwd", "nn", h2, W_up, out_dtype=BF16, tm=tm1, tn=tn_up, tk=D)[0]
    z, x2 = _ffn_down_fwd("ffn_down_fwd", up, cw_ffn, ffn_conv_b, W_down, x1, _pick(T, (512, 256)),
                          _pick(F, (1024, 512)), 4)

    dx2, dx2b, d_g_final, lossvec = _final_loss_bwd("final_loss_bwd", x2, tgt, g_final.reshape(1, D), tt_row)
    dz = _matmul("down_bwd_x", "nt", dx2b, W_down, out_dtype=BF16, tm=tm1, tn=tF, tk=D)[0]
    dW_down = _matmul("down_bwd_w", "tn", z, dx2b, out_dtype=BF16, tm=_pick(F, (1024, 512)), tn=tD1, tk=tkT2)
    dup, d_cw_ffn, d_ffn_b = _ffn_act_bwd("ffn_act_bwd", up, dz, cw_ffn, ffn_conv_b, tt_seq, tc_seq)
    dh2 = _matmul("up_bwd_x", "nt", dup, W_up, out_dtype=F32, tm=tm1, tn=tD1, tk=n_up)[0]
    dW_up = _matmul("up_bwd_w", "tn", h2, dup, out_seg=N_SLOTS, out_dtype=BF16, tm=tD1, tn=tk_up, tk=tkT2)
    dx1, dx1b, d_g_mlp = _rms_bwd("rms2_bwd", dh2, x1, g_mlp, dx2, tt_row, True)
    dmerged = _matmul("out_bwd_x", "nt", dx1b, W_out, out_dtype=BF16, tm=tm1, tn=tD2, tk=D)[0]
    q_rows = lambda a, rows: a.reshape(N_SLOTS, rows // N_SLOTS, a.shape[-1])
    ffn_parts = [dW_up, q_rows(dW_down, F)]
    dW_out, *ffn_sib = _matmul("out_bwd_w", "tn", merged, dx1b, out_dtype=BF16, tm=tD1, tn=tD1, tk=tkT2,
                               side=_swap_side(ffn_parts))
    ffn_sums = [_add_halves(f"add_halves_{n}", p, r, pos) for n, p, r in zip(big[7:], ffn_parts, ffn_sib)]
    dP, dQ, dgl0, dgl1, d_bg0, d_bg1 = _merge_bwd("merge_bwd", dmerged, proj, p2, b_gate, P, Q, tt_seq, tc_seq)
    dy_pool = _matmul("pool_proj_bwd_x", "nt", dP, W_pp, out_dtype=F32, tm=tm2, tn=Cp, tk=tn_pp)[0]
    dW_pp = _matmul("pool_proj_bwd_w", "tn", y_pool, dP, out_seg=N_SLOTS, out_dtype=BF16, tm=Cp, tn=tn_pp,
                    tk=_pick(T, (2048, 1024, 512, 256)))
    dy_lru = _matmul("lru_proj_bwd_x", "nt", dQ, W_lp, out_dtype=F32, tm=tm1, tn=tD1, tk=D)[0]
    dW_lp = _matmul("lru_proj_bwd_w", "tn", y_lru, dQ, out_dtype=BF16, tm=_pick(R, (1024, 512)), tn=tD1, tk=tkT2)
    dra, dia, dv, du_gelu, d_b_a, d_b_i, d_lam = _lru_bwd(
        "lru_scan_bwd", dy_lru, hs, proj, p1, W_a, W_i, v_f, b_a, b_i, lru_lambda, tt_seq, tc_seq)
    dW_a, dW_i = _bd_bwd_w("lru_gate_bwd_w", v_b, [dra, dia], hw, tkT)
    du_lru, d_cw_lru, d_lru_b = _conv_bwd("lru_conv_bwd", dv, proj, p0, cw_lru, tt_seq, tc_seq)
    dzp, dd, d_pool_scale = _pool_bwd_a("pool_scale_bwd", dy_pool, z_pool, pool_scale, W_pool, tt_row)
    (dW_pool,) = _bd_bwd_w("pool_mix_bwd_w", d_pool, [dzp], gw, tkT)
    du_pool = _pool_bwd_b("pool_bwd", dd, gw, tt_pool)
    dproj = jnp.concatenate([du_pool, du_lru, du_gelu, dgl0, dgl1], axis=1)

    mix_parts = [_to_slots_blockdiag(dW_pool.astype(BF16), G, gw),
                 _to_slots_blockdiag(dW_a.astype(BF16), H, hw),
                 _to_slots_blockdiag(dW_i.astype(BF16), H, hw),
                 dW_pp, q_rows(dW_lp, R), q_rows(dW_out, D)]
    dW_in, *riders = _matmul("proj_bwd_w", "tn", h, dproj, out_seg=N_SLOTS, out_dtype=BF16, tm=tD1, tn=tn_in,
                             tk=tkT2, side=_both(_exchange_side(ffn_sums), _swap_side(mix_parts)))
    ffn_chips, mix_sib = riders[:2], riders[2:]
    mix_sums = [_add_halves(f"add_halves_{n}", p, r, pos) for n, p, r in zip(big[1:7], mix_parts, mix_sib)]
    (in_sib,) = _swap_partial_halves("swap_partial_halves_w_in", [dW_in])
    in_sum = _add_halves("add_halves_w_in", dW_in, in_sib, pos)
    ffn_totals = [_add_chips(f"add_chips_{n}", s, r, pos) for n, s, r in zip(big[7:], ffn_sums, ffn_chips)]
    dh3, *rest = _matmul("proj_bwd_x", "nt", dproj, W_in, out_dtype=F32, tm=tm1, tn=tD1, tk=n_in,
                         side=_both(_exchange_side([in_sum] + mix_sums), _reduced_swap_side(ffn_totals)))
    chips, ffn_full = rest[:7], rest[7:]
    totals = [_add_chips(f"add_chips_{n}", s, r, pos) for n, s, r in zip(big[:7], [in_sum] + mix_sums, chips)]
    grad_x, d_g_mix, *swapped = _rms_bwd("rms1_bwd", dh3[0], xs, g_mix, dx1, tt_row, False,
                                         side=_reduced_swap_side(totals))
    full = list(swapped) + list(ffn_full)
    grads = {n: f.reshape(weights[n].shape) for n, f in zip(big, full)}

    small = ["g_mix", "b_gate", "pool_scale", "lru_conv_b", "b_a", "b_i", "lru_lambda", "g_mlp", "ffn_conv_b", "g_final"]
    small_g = [d_g_mix, jnp.concatenate([d_bg0, d_bg1], axis=1), d_pool_scale, d_lru_b, d_b_a, d_b_i, d_lam, d_g_mlp,
               d_ffn_b, d_g_final]
    pieces = [g.reshape(-1) for g in small_g] + [d_cw_lru[:KC].reshape(-1), d_cw_ffn[:KF].reshape(-1), lossvec.reshape(-1)]
    sizes = [p.shape[0] for p in pieces]
    total = sum(sizes)
    rows = -(-total // (8 * LANES)) * 8
    pack = jnp.concatenate(pieces + [jnp.zeros((rows * LANES - total,), F32)]).reshape(rows, LANES)
    summed = _allreduce_small(pack).reshape(-1)
    offs = [0]
    for s in sizes:
        offs.append(offs[-1] + s)
    for k, n in enumerate(small):
        grads[n] = summed[offs[k]:offs[k + 1]].reshape(weights[n].shape)
    ns = len(small)
    g_cw_lru = summed[offs[ns]:offs[ns + 1]].reshape(KC, R)
    g_cw_ffn = summed[offs[ns + 1]:offs[ns + 2]].reshape(KF, F)
    grads["lru_conv_w"] = lax.dynamic_slice_in_dim(g_cw_lru, slot * (R // N_SLOTS), R // N_SLOTS, axis=1)[None]
    grads["ffn_conv_w"] = lax.dynamic_slice_in_dim(g_cw_ffn, slot * (F // N_SLOTS), F // N_SLOTS, axis=1)[None]
    loss = jnp.sum(summed[offs[ns + 2]:offs[ns + 3]]) * (0.5 / D)

    delta, new_m, new_v = {}, {}, {}
    for n in big:
        d_, m_, v_ = _adamw(f"adamw_{n}", _rows2d(weights[n][0]), _rows2d(grads[n][0]), _rows2d(mom_m[n][0]),
                            _rows2d(mom_v[n][0]))
        delta[n], new_m[n], new_v[n] = (t.reshape(weights[n].shape) for t in (d_, m_, v_))
    rest = small + ["lru_conv_w", "ffn_conv_w"]
    rsizes = [weights[n].size for n in rest]
    rtotal = sum(rsizes)
    rrows = -(-rtotal // (8 * LANES)) * 8

    def packed(tree):
        flat = [tree[n].reshape(-1) for n in rest] + [jnp.zeros((rrows * LANES - rtotal,), F32)]
        return jnp.concatenate(flat).reshape(rrows, LANES)

    d_, m_, v_ = _adamw("adamw_small", packed(weights), packed(grads), packed(mom_m), packed(mom_v))
    o = 0
    for n, s in zip(rest, rsizes):
        for tree, flat in ((delta, d_), (new_m, m_), (new_v, v_)):
            tree[n] = flat.reshape(-1)[o:o + s].reshape(weights[n].shape)
        o += s

    return (loss, grad_x.reshape(x.shape), *[grads[n] for n in order], *[delta[n] for n in order],
            *[new_m[n] for n in order], *[new_v[n] for n in order])
```
